```python
import jax
import jax.numpy as jnp
from jax import lax
import numpy as np

D_MODEL = 1024
BATCH = 16
SEQ = 2048
DEPTH = 2

GRID_W = 64
CTX_LEN = 256
HEAD_DIM = 64
ATT_HEADS = 8
ATT_KV_HEADS = 2
ATT_GROUPS = ATT_HEADS // ATT_KV_HEADS
ATT_WIDTH = ATT_HEADS * HEAD_DIM
KV_WIDTH = ATT_KV_HEADS * HEAD_DIM
WINDOW = 128
BLOCK = 128
ATT_SCALE = HEAD_DIM ** -0.5
ROPE_THETA = 10000.0
ROPE_FREQS = HEAD_DIM // 4
POOL_WINDOWS = (2, 4, 8, 16)
POOL_GROUPS = len(POOL_WINDOWS)
POOL_WIDTH = D_MODEL // 2
POOL_GROUP_W = POOL_WIDTH // POOL_GROUPS
MIX_AB_IN = ATT_WIDTH + 2 * KV_WIDTH + POOL_WIDTH
MIX_AB_OUT = ATT_WIDTH + POOL_WIDTH
LRU_WIDTH = D_MODEL
LRU_BLOCKS = 8
LRU_BLOCK_W = LRU_WIDTH // LRU_BLOCKS
LRU_C = 8.0
CONV_W = 4
CONV_LEFT = (CONV_W - 1) // 2
D_FF = 2816
N_MOD = 9
LN_EPS = 1e-5
NEG_INF = -1e30
DEEPNORM_ALPHA = (2 * DEPTH) ** 0.25
DEEPNORM_BETA = (8 * DEPTH) ** -0.25
N_EVEN = (DEPTH + 1) // 2
N_ODD = DEPTH // 2

kernel_name = 'hybrid_window_attn_pool_rglru_macaron_dit'


def layer_norm(x, g, b):
    xf = x.astype(jnp.float32)
    mu = jnp.mean(xf, axis=-1, keepdims=True)
    var = jnp.mean(jnp.square(xf - mu), axis=-1, keepdims=True)
    return ((xf - mu) * lax.rsqrt(var + LN_EPS)).astype(x.dtype) * g + b


def residual_post_norm(x, y, g, b):
    return layer_norm(DEEPNORM_ALPHA * x + y, g, b)


def modulate(x, shift, scale):
    return x * (1.0 + scale) + shift


def swiglu(x, w_gate, w_up, w_down):
    return (jax.nn.silu(x @ w_gate) * (x @ w_up)) @ w_down


def axial_rope(rows):
    row = jnp.repeat(jnp.arange(rows, dtype=jnp.float32), GRID_W)
    col = jnp.tile(jnp.arange(GRID_W, dtype=jnp.float32), rows)
    inv = ROPE_THETA ** (-jnp.arange(ROPE_FREQS, dtype=jnp.float32) / ROPE_FREQS)
    ang = jnp.concatenate([row[:, None] * inv, col[:, None] * inv], axis=-1)
    return jnp.cos(ang), jnp.sin(ang)


def apply_rope(x, cos, sin):
    half = HEAD_DIM // 2
    cs = cos[None, :, None, :].astype(x.dtype)
    sn = sin[None, :, None, :].astype(x.dtype)
    x1, x2 = x[..., :half], x[..., half:]
    return jnp.concatenate([x1 * cs - x2 * sn, x2 * cs + x1 * sn], axis=-1)


def windowed_sink_attention(q, k, v, k_ctx, v_ctx, sink):
    bsz, n = q.shape[0], q.shape[1]
    nb = n // BLOCK
    qb = q.reshape(bsz, nb, BLOCK, ATT_KV_HEADS, ATT_GROUPS, HEAD_DIM)

    def band(t):
        tp = jnp.pad(t, ((0, 0), (BLOCK, BLOCK), (0, 0), (0, 0)))
        tp = tp.reshape(bsz, nb + 2, BLOCK, ATT_KV_HEADS, HEAD_DIM)
        return jnp.concatenate([tp[:, :-2], tp[:, 1:-1], tp[:, 2:]], axis=2)

    kb, vb = band(k), band(v)
    s_win = jnp.einsum('bnqkgd,bnskd->bnkgqs', qb, kb, preferred_element_type=jnp.float32) * ATT_SCALE
    qpos = jnp.arange(n).reshape(nb, BLOCK)
    kpos = (jnp.arange(nb)[:, None] - 1) * BLOCK + jnp.arange(3 * BLOCK)[None, :]
    valid = ((jnp.abs(qpos[:, :, None] - kpos[:, None, :]) <= WINDOW)
             & (kpos[:, None, :] >= 0) & (kpos[:, None, :] < n))
    s_win = jnp.where(valid[None, :, None, None], s_win, NEG_INF)
    s_ctx = jnp.einsum('bnqkgd,bckd->bnkgqc', qb, k_ctx, preferred_element_type=jnp.float32) * ATT_SCALE
    sink_l = sink.astype(jnp.float32).reshape(ATT_KV_HEADS, ATT_GROUPS)[None, None, :, :, None, None]
    m = jnp.maximum(jnp.maximum(s_win.max(-1, keepdims=True), s_ctx.max(-1, keepdims=True)), sink_l)
    e_win = jnp.exp(s_win - m)
    e_ctx = jnp.exp(s_ctx - m)
    denom = e_win.sum(-1, keepdims=True) + e_ctx.sum(-1, keepdims=True) + jnp.exp(sink_l - m)
    o = (jnp.einsum('bnkgqs,bnskd->bnqkgd', (e_win / denom).astype(v.dtype), vb)
         + jnp.einsum('bnkgqc,bckd->bnqkgd', (e_ctx / denom).astype(v.dtype), v_ctx))
    return o.reshape(bsz, n, ATT_WIDTH)


def context_sink_attention(q_ctx, k_ctx, v_ctx, sink):
    bsz, n_c = q_ctx.shape[0], q_ctx.shape[1]
    qg = q_ctx.reshape(bsz, n_c, ATT_KV_HEADS, ATT_GROUPS, HEAD_DIM)
    s = jnp.einsum('bqkgd,bskd->bkgqs', qg, k_ctx, preferred_element_type=jnp.float32) * ATT_SCALE
    sink_c = jnp.broadcast_to(sink.astype(jnp.float32).reshape(ATT_KV_HEADS, ATT_GROUPS)[None, :, :, None, None],
                              s.shape[:-1] + (1,))
    p = jax.nn.softmax(jnp.concatenate([s, sink_c], axis=-1), axis=-1)[..., :n_c]
    o = jnp.einsum('bkgqs,bskd->bqkgd', p.astype(v_ctx.dtype), v_ctx)
    return o.reshape(bsz, n_c, ATT_WIDTH)


def multiscale_pool(u, w_pool, pool_scale):
    bsz, n = u.shape[0], u.shape[1]
    uf = u.astype(jnp.float32)
    cs = jnp.pad(jnp.cumsum(uf, axis=1), ((0, 0), (1, 0), (0, 0)))
    t = jnp.arange(n)
    diffs = []
    for g, w in enumerate(POOL_WINDOWS):
        r = w // 2
        lo = jnp.maximum(t - r, 0)
        hi = jnp.minimum(t + r, n - 1) + 1
        sl = slice(g * POOL_GROUP_W, (g + 1) * POOL_GROUP_W)
        seg = cs[:, :, sl]
        mean = (seg[:, hi] - seg[:, lo]) / (hi - lo).astype(jnp.float32)[None, :, None]
        diffs.append(mean - uf[:, :, sl])
    d = jnp.stack(diffs, axis=2).astype(u.dtype)
    y = jnp.einsum('blgc,gce->blge', d, w_pool).reshape(bsz, n, POOL_WIDTH)
    return y * pool_scale


def attention_pool_mixer(h, hc, cos, sin, w_in, sink, w_pool, pool_scale, w_out, need_ctx_out):
    bsz, n = h.shape[0], h.shape[1]
    n_c = hc.shape[1]
    splits = [ATT_WIDTH, ATT_WIDTH + KV_WIDTH, ATT_WIDTH + 2 * KV_WIDTH]
    q, k, v, u = jnp.split(h @ w_in, splits, axis=-1)
    if need_ctx_out:
        q_c, k_c, v_c, u_c = jnp.split(hc @ w_in, splits, axis=-1)
    else:
        k_c, v_c = jnp.split(hc @ w_in[:, ATT_WIDTH:ATT_WIDTH + 2 * KV_WIDTH], 2, axis=-1)
    q = apply_rope(q.reshape(bsz, n, ATT_HEADS, HEAD_DIM), cos, sin)
    k = apply_rope(k.reshape(bsz, n, ATT_KV_HEADS, HEAD_DIM), cos, sin)
    v = v.reshape(bsz, n, ATT_KV_HEADS, HEAD_DIM)
    k_c = k_c.reshape(bsz, n_c, ATT_KV_HEADS, HEAD_DIM)
    v_c = v_c.reshape(bsz, n_c, ATT_KV_HEADS, HEAD_DIM)
    att = windowed_sink_attention(q, k, v, k_c, v_c, sink)
    pool = multiscale_pool(u, w_pool, pool_scale)
    out = jnp.concatenate([att, pool], axis=-1) @ w_out
    if not need_ctx_out:
        return out, None
    att_c = context_sink_attention(q_c, k_c, v_c, sink)
    pool_c = multiscale_pool(u_c, w_pool, pool_scale)
    out_c = jnp.concatenate([att_c, pool_c], axis=-1) @ w_out
    return out, out_c


def centred_depthwise_conv(x, w, b):
    n = x.shape[1]
    xp = jnp.pad(x, ((0, 0), (CONV_LEFT, CONV_W - 1 - CONV_LEFT), (0, 0)))
    y = b
    for tap in range(CONV_W):
        y = y + xp[:, tap:tap + n] * w[tap]
    return y


def rglru_coeffs(u, wa, ba, wx, bx, lam):
    bsz, n = u.shape[0], u.shape[1]
    ub = u.reshape(bsz, n, LRU_BLOCKS, LRU_BLOCK_W)
    r = jax.nn.sigmoid(jnp.einsum('blhi,hij->blhj', ub, wa).reshape(bsz, n, LRU_WIDTH) + ba)
    gi = jax.nn.sigmoid(jnp.einsum('blhi,hij->blhj', ub, wx).reshape(bsz, n, LRU_WIDTH) + bx)
    log_a = -LRU_C * r.astype(jnp.float32) * jax.nn.softplus(-lam.astype(jnp.float32))
    a = jnp.exp(log_a)
    b = jnp.sqrt(-jnp.expm1(2.0 * log_a)) * (gi * u).astype(jnp.float32)
    return a, b


def linear_scan(a, b, h0):
    b = b.at[:, 0].add(a[:, 0] * h0)

    def combine(left, right):
        return left[0] * right[0], right[0] * left[1] + right[1]

    _, h = lax.associative_scan(combine, (a, b), axis=1)
    return h


def recurrent_mixer(h, hc, w_in, conv_w, conv_b, wa, ba, wx, bx, lam, w_out, need_ctx_out):
    gate, u = jnp.split(h @ w_in, 2, axis=-1)
    if need_ctx_out:
        gate_c, u_c = jnp.split(hc @ w_in, 2, axis=-1)
    else:
        u_c = hc @ w_in[:, LRU_WIDTH:]
    u = centred_depthwise_conv(u, conv_w, conv_b)
    u_c = centred_depthwise_conv(u_c, conv_w, conv_b)
    h0 = jnp.zeros((h.shape[0], LRU_WIDTH), jnp.float32)
    ys, ys_c = [], []
    for direction in range(2):
        a, b = rglru_coeffs(u, wa[direction], ba[direction], wx[direction], bx[direction], lam[direction])
        a_c, b_c = rglru_coeffs(u_c, wa[direction], ba[direction], wx[direction], bx[direction], lam[direction])
        if direction == 1:
            a, b, a_c, b_c = (jnp.flip(a, 1), jnp.flip(b, 1), jnp.flip(a_c, 1), jnp.flip(b_c, 1))
        s_c = linear_scan(a_c, b_c, h0)
        s = linear_scan(a, b, s_c[:, -1])
        if direction == 1:
            s, s_c = jnp.flip(s, 1), jnp.flip(s_c, 1)
        ys.append(s)
        ys_c.append(s_c)
    y = (ys[0] + ys[1]).astype(h.dtype)
    out = (jax.nn.gelu(gate) * y) @ w_out
    if not need_ctx_out:
        return out, None
    y_c = (ys_c[0] + ys_c[1]).astype(hc.dtype)
    out_c = (jax.nn.gelu(gate_c) * y_c) @ w_out
    return out, out_c


def setup_inputs(seed: int = 0) -> dict:
    key = jax.random.key(seed)
    ks = jax.random.split(key, 26)

    def nrm(i, shape, scale):
        return jax.random.normal(ks[i], shape, jnp.float32) * scale

    lam_u = jax.random.uniform(ks[23], (N_ODD, 2, LRU_WIDTH), jnp.float32, 0.9, 0.999)
    return {
        'x': nrm(0, (BATCH, SEQ, D_MODEL), 1.0),
        'c': nrm(1, (BATCH, D_MODEL), 1.0),
        'ctx': nrm(2, (BATCH, CTX_LEN, D_MODEL), 1.0),
        'c_ctx': nrm(3, (D_MODEL,), 1.0),
        'w_mod': nrm(4, (DEPTH, D_MODEL, N_MOD * D_MODEL), 0.5 * D_MODEL ** -0.5),
        'b_mod': nrm(5, (DEPTH, N_MOD * D_MODEL), 0.02),
        'ln_g': 1.0 + nrm(6, (DEPTH, 3, D_MODEL), 0.02),
        'ln_b': nrm(7, (DEPTH, 3, D_MODEL), 0.02),
        'ffn_w_gate': nrm(8, (DEPTH, 2, D_MODEL, D_FF), D_MODEL ** -0.5),
        'ffn_w_up': nrm(9, (DEPTH, 2, D_MODEL, D_FF), D_MODEL ** -0.5),
        'ffn_w_down': nrm(10, (DEPTH, 2, D_FF, D_MODEL), DEEPNORM_BETA * D_FF ** -0.5),
        'mix_ab_w_in': nrm(11, (N_EVEN, D_MODEL, MIX_AB_IN), D_MODEL ** -0.5),
        'attn_sink': nrm(12, (N_EVEN, ATT_HEADS), 0.5),
        'pool_w': nrm(13, (N_EVEN, POOL_GROUPS, POOL_GROUP_W, POOL_GROUP_W), POOL_GROUP_W ** -0.5),
        'pool_scale': 1.0 + nrm(14, (N_EVEN, POOL_WIDTH), 0.1),
        'mix_ab_w_out': nrm(15, (N_EVEN, MIX_AB_OUT, D_MODEL), DEEPNORM_BETA * MIX_AB_OUT ** -0.5),
        'lru_w_in': nrm(16, (N_ODD, D_MODEL, 2 * LRU_WIDTH), D_MODEL ** -0.5),
        'lru_conv_w': nrm(17, (N_ODD, CONV_W, LRU_WIDTH), CONV_W ** -0.5),
        'lru_conv_b': nrm(18, (N_ODD, LRU_WIDTH), 0.02),
        'lru_wa': nrm(19, (N_ODD, 2, LRU_BLOCKS, LRU_BLOCK_W, LRU_BLOCK_W), LRU_BLOCK_W ** -0.5),
        'lru_ba': nrm(20, (N_ODD, 2, LRU_WIDTH), 0.02),
        'lru_wx': nrm(21, (N_ODD, 2, LRU_BLOCKS, LRU_BLOCK_W, LRU_BLOCK_W), LRU_BLOCK_W ** -0.5),
        'lru_bx': nrm(22, (N_ODD, 2, LRU_WIDTH), 0.02),
        'lru_lambda': jnp.log(lam_u) - jnp.log1p(-lam_u),
        'lru_w_out': nrm(24, (N_ODD, LRU_WIDTH, D_MODEL), DEEPNORM_BETA * LRU_WIDTH ** -0.5),
    }


def reference(x, c, ctx, c_ctx, w_mod, b_mod, ln_g, ln_b, ffn_w_gate, ffn_w_up, ffn_w_down,
              mix_ab_w_in, attn_sink, pool_w, pool_scale, mix_ab_w_out,
              lru_w_in, lru_conv_w, lru_conv_b, lru_wa, lru_ba, lru_wx, lru_bx, lru_lambda, lru_w_out):
    rows = x.shape[1] // GRID_W
    cos, sin = axial_rope(rows)
    h, hc = x, ctx
    for layer in range(DEPTH):
        ctx_out = layer < DEPTH - 1
        m = jnp.split((jax.nn.silu(c) @ w_mod[layer] + b_mod[layer])[:, None, :], N_MOD, axis=-1)
        mc = jnp.split((jax.nn.silu(c_ctx) @ w_mod[layer] + b_mod[layer])[None, None, :], N_MOD, axis=-1)
        ffn1 = (ffn_w_gate[layer, 0], ffn_w_up[layer, 0], ffn_w_down[layer, 0])
        ffn2 = (ffn_w_gate[layer, 1], ffn_w_up[layer, 1], ffn_w_down[layer, 1])
        h = residual_post_norm(h, 0.5 * m[2] * swiglu(modulate(h, m[0], m[1]), *ffn1), ln_g[layer, 0], ln_b[layer, 0])
        hc = residual_post_norm(hc, 0.5 * mc[2] * swiglu(modulate(hc, mc[0], mc[1]), *ffn1), ln_g[layer, 0], ln_b[layer, 0])
        h_in = modulate(h, m[3], m[4])
        hc_in = modulate(hc, mc[3], mc[4])
        idx = layer // 2
        if layer % 2 == 0:
            y, y_c = attention_pool_mixer(h_in, hc_in, cos, sin, mix_ab_w_in[idx], attn_sink[idx],
                                          pool_w[idx], pool_scale[idx], mix_ab_w_out[idx], ctx_out)
        else:
            y, y_c = recurrent_mixer(h_in, hc_in, lru_w_in[idx], lru_conv_w[idx], lru_conv_b[idx],
                                     lru_wa[idx], lru_ba[idx], lru_wx[idx], lru_bx[idx], lru_lambda[idx],
                                     lru_w_out[idx], ctx_out)
        h = residual_post_norm(h, m[5] * y, ln_g[layer, 1], ln_b[layer, 1])
        h = residual_post_norm(h, 0.5 * m[8] * swiglu(modulate(h, m[6], m[7]), *ffn2), ln_g[layer, 2], ln_b[layer, 2])
        if ctx_out:
            hc = residual_post_norm(hc, mc[5] * y_c, ln_g[layer, 1], ln_b[layer, 1])
            hc = residual_post_norm(hc, 0.5 * mc[8] * swiglu(modulate(hc, mc[6], mc[7]), *ffn2), ln_g[layer, 2], ln_b[layer, 2])
    return h
```

```python
import functools

import jax
import jax.numpy as jnp
from jax import lax
from jax.experimental import pallas as pl
from jax.experimental.pallas import tpu as pltpu

D_MODEL = 1024
BATCH = 16
SEQ = 2048
DEPTH = 2
GRID_W = 64
CTX_LEN = 256
HEAD_DIM = 64
ATT_HEADS = 8
ATT_KV_HEADS = 2
ATT_GROUPS = ATT_HEADS // ATT_KV_HEADS
ATT_WIDTH = ATT_HEADS * HEAD_DIM
KV_WIDTH = ATT_KV_HEADS * HEAD_DIM
WINDOW = 128
BLOCK = 128
ATT_SCALE = HEAD_DIM ** -0.5
ROPE_THETA = 10000.0
ROPE_FREQS = HEAD_DIM // 4
POOL_WINDOWS = (2, 4, 8, 16)
POOL_WIDTH = D_MODEL // 2
POOL_GROUP_W = POOL_WIDTH // len(POOL_WINDOWS)
MIX_AB_IN = ATT_WIDTH + 2 * KV_WIDTH + POOL_WIDTH
LRU_WIDTH = D_MODEL
LRU_BLOCKS = 8
LRU_BLOCK_W = LRU_WIDTH // LRU_BLOCKS
LRU_C = 8.0
CONV_W = 4
CONV_LEFT = (CONV_W - 1) // 2
D_FF = 2816
N_MOD = 9
LN_EPS = 1e-5
NEG_INF = -1e30
DEEPNORM_ALPHA = (2 * DEPTH) ** 0.25

V7X_LANES = 128
V7X_SUBLANES = 8
V7X_VMEM_BYTES = 64 * 1024 * 1024
V7X_VMEM_USABLE_BYTES = 60000 * 1024

F32 = jnp.float32
BF16 = jnp.bfloat16

POOL_HALO = V7X_SUBLANES
PERM_STEPS = 16
MOD_ROWS = 24


def _nbytes(shape, dtype):
    n = 1
    for s in shape:
        n *= s
    return n * jnp.dtype(dtype).itemsize


def _vmem_limit(pipelined, resident, temporaries):
    est = 2 * sum(pipelined) + sum(resident) + temporaries
    return int(min(V7X_VMEM_USABLE_BYTES, max(est * 5 // 4, 16 * 1024 * 1024)))


def _const_spec(shape):
    nd = len(shape)
    return pl.BlockSpec(shape, lambda *_: (0,) * nd, pipeline_mode=pl.Buffered(1))


def _tok_spec(tm, width, nt):
    return pl.BlockSpec((tm, width), lambda b, t: (b * nt + t, 0))


def _mod_spec():
    return pl.BlockSpec((1, N_MOD, D_MODEL), lambda b, t: (b, 0, 0))


def _layer_norm(z, g, b):
    mu = jnp.mean(z, axis=-1, keepdims=True)
    zc = z - mu
    var = jnp.mean(zc * zc, axis=-1, keepdims=True)
    return zc * lax.rsqrt(var + LN_EPS) * g + b


def _gelu_tanh(x):
    return 0.5 * x * (1.0 + jnp.tanh(0.7978845608028654 * (x + 0.044715 * (x * x * x))))


def _mod_kernel(c_ref, w_ref, b_ref, o_ref):
    c = c_ref[...]
    a = c * jax.nn.sigmoid(c)
    o_ref[0] = jnp.dot(a, w_ref[0], preferred_element_type=F32,
                       precision=lax.Precision.HIGHEST) + b_ref[0]


def _modulation(c_all, w_mod, b_mod):
    tn = 1024
    n_out = N_MOD * D_MODEL
    blocks = [_nbytes((1, D_MODEL, tn), F32), _nbytes((1, MOD_ROWS, tn), F32)]
    return pl.pallas_call(
        _mod_kernel,
        grid=(DEPTH, n_out // tn),
        in_specs=[
            pl.BlockSpec((MOD_ROWS, D_MODEL), lambda l, j: (0, 0)),
            pl.BlockSpec((1, D_MODEL, tn), lambda l, j: (l, 0, j)),
            pl.BlockSpec((1, 1, tn), lambda l, j: (l, 0, j)),
        ],
        out_specs=pl.BlockSpec((1, MOD_ROWS, tn), lambda l, j: (l, 0, j)),
        out_shape=jax.ShapeDtypeStruct((DEPTH, MOD_ROWS, n_out), F32),
        compiler_params=pltpu.CompilerParams(
            dimension_semantics=("arbitrary", "arbitrary"),
            vmem_limit_bytes=_vmem_limit(blocks, [], 4 * blocks[0])),
        name="modulation",
    )(c_all, w_mod, b_mod.reshape(DEPTH, 1, n_out))


def _ffn_kernel(x_ref, mod_ref, wg_ref, wu_ref, wd_ref, lng_ref, lnb_ref, o_ref, *, j0):
    x = x_ref[...]
    shift = mod_ref[0, j0:j0 + 1, :]
    scale = mod_ref[0, j0 + 1:j0 + 2, :]
    gate = mod_ref[0, j0 + 2:j0 + 3, :]
    xin = (x * (1.0 + scale) + shift).astype(BF16)
    g = jnp.dot(xin, wg_ref[...], preferred_element_type=F32)
    u = jnp.dot(xin, wu_ref[...], preferred_element_type=F32)
    a = (g * jax.nn.sigmoid(g) * u).astype(BF16)
    y = jnp.dot(a, wd_ref[...], preferred_element_type=F32)
    z = DEEPNORM_ALPHA * x + (0.5 * gate) * y
    o_ref[...] = _layer_norm(z, lng_ref[...], lnb_ref[...])


def _ffn(x, mod, j0, wg, wu, wd, ln_g, ln_b, *, nb, length, tm=256):
    nt = length // tm
    tok = _nbytes((tm, D_MODEL), F32)
    weights = [_nbytes(w.shape, BF16) for w in (wg, wu, wd)]
    temps = 3 * _nbytes((tm, D_FF), F32) + 4 * tok
    return pl.pallas_call(
        functools.partial(_ffn_kernel, j0=j0),
        grid=(nb, nt),
        in_specs=[
            _tok_spec(tm, D_MODEL, nt),
            _mod_spec(),
            _const_spec(wg.shape), _const_spec(wu.shape), _const_spec(wd.shape),
            _const_spec((1, D_MODEL)), _const_spec((1, D_MODEL)),
        ],
        out_specs=_tok_spec(tm, D_MODEL, nt),
        out_shape=jax.ShapeDtypeStruct((nb * length, D_MODEL), F32),
        compiler_params=pltpu.CompilerParams(
            dimension_semantics=("arbitrary", "arbitrary"),
            vmem_limit_bytes=_vmem_limit([tok, tok], weights, temps)),
        name="ffn",
    )(x, mod, wg, wu, wd, ln_g, ln_b)


def _dup_halves(z, lane):
    zr = pltpu.roll(z, HEAD_DIM, 1)
    lo = lane < HEAD_DIM
    return jnp.where(lo, z, zr), jnp.where(lo, zr, z)


def _attn_inproj_kernel(*refs, rope):
    if rope:
        x_ref, mod_ref, w_ref, cos_ref, sin_ref, q_ref, k_ref, v_ref, u_ref = refs
    else:
        x_ref, mod_ref, w_ref, q_ref, k_ref, v_ref, u_ref = refs
    x = x_ref[...]
    xin = (x * (1.0 + mod_ref[0, 4:5, :]) + mod_ref[0, 3:4, :]).astype(BF16)
    p = jnp.dot(xin, w_ref[...], preferred_element_type=F32)
    lane = lax.broadcasted_iota(jnp.int32, (x.shape[0], V7X_LANES), 1)
    first_half = (lane & (HEAD_DIM - 1)) < HEAD_DIM // 2

    def rot(z):
        if not rope:
            return z
        zr = jnp.where(first_half, pltpu.roll(z, V7X_LANES - HEAD_DIM // 2, 1),
                       pltpu.roll(z, HEAD_DIM // 2, 1))
        return z * cos_ref[...] + zr * sin_ref[...]

    for c in range(ATT_WIDTH // V7X_LANES):
        sl = slice(c * V7X_LANES, (c + 1) * V7X_LANES)
        q_ref[:, sl] = (rot(p[:, sl]) * ATT_SCALE).astype(BF16)
    k0, k1 = _dup_halves(rot(p[:, ATT_WIDTH:ATT_WIDTH + KV_WIDTH]), lane)
    k_ref[:, 0:V7X_LANES] = k0.astype(BF16)
    k_ref[:, V7X_LANES:] = k1.astype(BF16)
    v0, v1 = _dup_halves(p[:, ATT_WIDTH + KV_WIDTH:ATT_WIDTH + 2 * KV_WIDTH], lane)
    v_ref[:, 0:V7X_LANES] = v0.astype(BF16)
    v_ref[:, V7X_LANES:] = v1.astype(BF16)
    u_ref[...] = p[:, ATT_WIDTH + 2 * KV_WIDTH:]


def _attn_inproj(x, mod, w_in, cos, sin, *, nb, length, tm=256):
    nt = length // tm
    rope = cos is not None
    tok = _nbytes((tm, D_MODEL), F32)
    outs = [_nbytes((tm, ATT_WIDTH), BF16), 2 * _nbytes((tm, 2 * KV_WIDTH), BF16),
            _nbytes((tm, POOL_WIDTH), F32)]
    in_specs = [_tok_spec(tm, D_MODEL, nt), _mod_spec(), _const_spec(w_in.shape)]
    args = [x, mod, w_in]
    if rope:
        in_specs += [pl.BlockSpec((tm, V7X_LANES), lambda b, t: (t, 0))] * 2
        args += [cos, sin]
    rows = nb * length
    return pl.pallas_call(
        functools.partial(_attn_inproj_kernel, rope=rope),
        grid=(nb, nt),
        in_specs=in_specs,
        out_specs=[
            _tok_spec(tm, ATT_WIDTH, nt),
            _tok_spec(tm, 2 * KV_WIDTH, nt),
            _tok_spec(tm, 2 * KV_WIDTH, nt),
            _tok_spec(tm, POOL_WIDTH, nt),
        ],
        out_shape=[
            jax.ShapeDtypeStruct((rows, ATT_WIDTH), BF16),
            jax.ShapeDtypeStruct((rows, 2 * KV_WIDTH), BF16),
            jax.ShapeDtypeStruct((rows, 2 * KV_WIDTH), BF16),
            jax.ShapeDtypeStruct((rows, POOL_WIDTH), F32),
        ],
        compiler_params=pltpu.CompilerParams(
            dimension_semantics=("arbitrary", "arbitrary"),
            vmem_limit_bytes=_vmem_limit([tok] + outs, [_nbytes(w_in.shape, BF16)],
                                         3 * _nbytes((tm, MIX_AB_IN), F32))),
        name="attn_inproj",
    )(*args)


def _attend_block(q_blk, keys, values, biases, sink_ref, kh):
    lane = lax.broadcasted_iota(jnp.int32, (BLOCK, V7X_LANES), 1)
    lo = lane < HEAD_DIM
    zero = jnp.zeros((BLOCK, V7X_LANES), BF16)
    parts = []
    for g in range(ATT_GROUPS):
        c = (kh * ATT_GROUPS + g) // 2
        qc = q_blk[:, c * V7X_LANES:(c + 1) * V7X_LANES]
        parts.append(jnp.where(lo if g % 2 == 0 else jnp.logical_not(lo), qc, zero))
    qs = jnp.concatenate(parts, axis=0)
    nt_dims = (((1,), (1,)), ((), ()))
    scores = [lax.dot_general(qs, k, nt_dims, preferred_element_type=F32) for k in keys]
    probs = [[] for _ in keys]
    rdens = []
    for g in range(ATT_GROUPS):
        rows = slice(g * BLOCK, (g + 1) * BLOCK)
        sg = [s[rows] if b is None else s[rows] + b for s, b in zip(scores, biases)]
        sink = sink_ref[kh * ATT_GROUPS + g]
        m = sg[0].max(axis=-1, keepdims=True)
        for s in sg[1:]:
            m = jnp.maximum(m, s.max(axis=-1, keepdims=True))
        m = jnp.maximum(m, sink)
        den = jnp.exp(sink - m)
        for i, s in enumerate(sg):
            e = jnp.exp(s - m)
            den = den + e.sum(axis=-1, keepdims=True)
            probs[i].append(e.astype(BF16))
        rdens.append(1.0 / den)
    o2 = None
    for p, v in zip(probs, values):
        t = jnp.dot(jnp.concatenate(p, axis=0), v, preferred_element_type=F32)
        o2 = t if o2 is None else o2 + t
    og = [o2[g * BLOCK:(g + 1) * BLOCK] * rdens[g] for g in range(ATT_GROUPS)]
    return jnp.where(lo, og[0], og[1]), jnp.where(lo, og[2], og[3])


def _pool_mix(ubuf_ref, tq, pos0, length, wpool_ref, pscale_ref):
    pos = pos0 + lax.broadcasted_iota(jnp.int32, (tq, 1), 0)
    outs = []
    for gi, w in enumerate(POOL_WINDOWS):
        r = w // 2
        cols = slice(gi * POOL_GROUP_W, (gi + 1) * POOL_GROUP_W)
        acc = ubuf_ref[POOL_HALO - r:POOL_HALO - r + tq, cols]
        for o in range(-r + 1, r + 1):
            acc = acc + ubuf_ref[POOL_HALO + o:POOL_HALO + o + tq, cols]
        cnt = (jnp.minimum(pos + r, length - 1) - jnp.maximum(pos - r, 0) + 1).astype(F32)
        d = acc / cnt - ubuf_ref[POOL_HALO:POOL_HALO + tq, cols]
        y = jnp.dot(d.astype(BF16), wpool_ref[gi], preferred_element_type=F32)
        outs.append(y * pscale_ref[:, cols])
    return jnp.concatenate(outs, axis=-1)


def _attn_out(att, pool, wout_ref, h_ref, gate, lng_ref, lnb_ref, o_ref):
    mix = jnp.concatenate([att, pool], axis=-1).astype(BF16)
    y = jnp.dot(mix, wout_ref[...], preferred_element_type=F32)
    z = DEEPNORM_ALPHA * h_ref[...] + gate * y
    o_ref[...] = _layer_norm(z, lng_ref[...], lnb_ref[...])


def _attn_lat_kernel(sink_ref, q_ref, kc_ref, kp_ref, kn_ref, vc_ref, vp_ref, vn_ref,
                     kctx_ref, vctx_ref, uc_ref, up_ref, un_ref, h_ref, mod_ref, bias_ref,
                     wpool_ref, pscale_ref, wout_ref, lng_ref, lnb_ref, o_ref,
                     kbuf, vbuf, ubuf, att_scr, *, tq, length):
    t = pl.program_id(1)
    last = pl.num_programs(1) - 1
    nblk = tq // BLOCK
    kbuf[0:BLOCK] = kp_ref[...]
    kbuf[BLOCK:BLOCK + tq] = kc_ref[...]
    kbuf[BLOCK + tq:] = kn_ref[...]
    vbuf[0:BLOCK] = vp_ref[...]
    vbuf[BLOCK:BLOCK + tq] = vc_ref[...]
    vbuf[BLOCK + tq:] = vn_ref[...]
    for j in range(nblk):
        blk = t * nblk + j
        bidx = jnp.where(blk == 0, 0, jnp.where(blk == length // BLOCK - 1, 2, 1))
        bias = bias_ref[bidx]
        q_blk = q_ref[j * BLOCK:(j + 1) * BLOCK, :]
        for kh in range(ATT_KV_HEADS):
            cols = slice(kh * V7X_LANES, (kh + 1) * V7X_LANES)
            win = slice(j * BLOCK, j * BLOCK + 3 * BLOCK)
            c0, c1 = _attend_block(q_blk, [kbuf[win, cols], kctx_ref[:, cols]],
                                   [vbuf[win, cols], vctx_ref[:, cols]], [bias, None], sink_ref, kh)
            rows = slice(j * BLOCK, (j + 1) * BLOCK)
            att_scr[rows, (2 * kh) * V7X_LANES:(2 * kh + 1) * V7X_LANES] = c0
            att_scr[rows, (2 * kh + 1) * V7X_LANES:(2 * kh + 2) * V7X_LANES] = c1
    ubuf[0:POOL_HALO] = jnp.where(t == 0, 0.0, up_ref[...])
    ubuf[POOL_HALO:POOL_HALO + tq] = uc_ref[...]
    ubuf[POOL_HALO + tq:] = jnp.where(t == last, 0.0, un_ref[...])
    pool = _pool_mix(ubuf, tq, t * tq, length, wpool_ref, pscale_ref)
    _attn_out(att_scr[...], pool, wout_ref, h_ref, mod_ref[0, 5:6, :], lng_ref, lnb_ref, o_ref)


def _attn_ctx_kernel(sink_ref, q_ref, kctx_ref, vctx_ref, uc_ref, h_ref, mod_ref,
                     wpool_ref, pscale_ref, wout_ref, lng_ref, lnb_ref, o_ref,
                     ubuf, att_scr, *, tq):
    for j in range(tq // BLOCK):
        q_blk = q_ref[j * BLOCK:(j + 1) * BLOCK, :]
        for kh in range(ATT_KV_HEADS):
            cols = slice(kh * V7X_LANES, (kh + 1) * V7X_LANES)
            c0, c1 = _attend_block(q_blk, [kctx_ref[:, cols]], [vctx_ref[:, cols]], [None],
                                   sink_ref, kh)
            rows = slice(j * BLOCK, (j + 1) * BLOCK)
            att_scr[rows, (2 * kh) * V7X_LANES:(2 * kh + 1) * V7X_LANES] = c0
            att_scr[rows, (2 * kh + 1) * V7X_LANES:(2 * kh + 2) * V7X_LANES] = c1
    ubuf[0:POOL_HALO] = jnp.zeros((POOL_HALO, POOL_WIDTH), F32)
    ubuf[POOL_HALO:POOL_HALO + tq] = uc_ref[...]
    ubuf[POOL_HALO + tq:] = jnp.zeros((POOL_HALO, POOL_WIDTH), F32)
    pool = _pool_mix(ubuf, tq, 0, tq, wpool_ref, pscale_ref)
    _attn_out(att_scr[...], pool, wout_ref, h_ref, mod_ref[0, 5:6, :], lng_ref, lnb_ref, o_ref)


def _window_bias():
    qi = jnp.arange(BLOCK)[:, None]
    kc = jnp.arange(3 * BLOCK)[None, :]
    band = jnp.abs(qi + BLOCK - kc) <= WINDOW
    first = band & (kc >= BLOCK)
    final = band & (kc < 2 * BLOCK)
    return jnp.where(jnp.stack([first, band, final]), 0.0, NEG_INF).astype(F32)


def _smem_spec():
    return pl.BlockSpec(memory_space=pltpu.SMEM)


def _attn_lat(sink, q, k2, v2, kctx, vctx, u, h, mod, wpool, pscale, wout, ln_g, ln_b, *, tq=512):
    nt = SEQ // tq
    bpt = tq // BLOCK
    bps = SEQ // BLOCK
    hpt = tq // POOL_HALO
    hps = SEQ // POOL_HALO
    kvw = 2 * KV_WIDTH
    cur = lambda w: pl.BlockSpec((tq, w), lambda b, t: (b * nt + t, 0))
    prev_blk = lambda rows, w, per_tile, per_samp: pl.BlockSpec(
        (rows, w), lambda b, t: (b * per_samp + jnp.maximum(t * per_tile - 1, 0), 0))
    next_blk = lambda rows, w, per_tile, per_samp: pl.BlockSpec(
        (rows, w), lambda b, t: (b * per_samp + jnp.minimum((t + 1) * per_tile, per_samp - 1), 0))
    ctx_blk = pl.BlockSpec((CTX_LEN, kvw), lambda b, t: (b, 0))
    bias = _window_bias()
    tok = _nbytes((tq, D_MODEL), F32)
    pipelined = [_nbytes((tq, ATT_WIDTH), BF16), 4 * _nbytes((tq + 2 * BLOCK, kvw), BF16),
                 _nbytes((tq + 2 * POOL_HALO, POOL_WIDTH), F32), 2 * tok]
    resident = [_nbytes(bias.shape, F32), _nbytes(wpool.shape, BF16), _nbytes(wout.shape, BF16)]
    scratch = [_nbytes((tq + 2 * BLOCK, kvw), BF16)] * 2 + [
        _nbytes((tq + 2 * POOL_HALO, POOL_WIDTH), F32), _nbytes((tq, ATT_WIDTH), F32)]
    return pl.pallas_call(
        functools.partial(_attn_lat_kernel, tq=tq, length=SEQ),
        grid=(BATCH, nt),
        in_specs=[
            _smem_spec(),
            cur(ATT_WIDTH),
            cur(kvw), prev_blk(BLOCK, kvw, bpt, bps), next_blk(BLOCK, kvw, bpt, bps),
            cur(kvw), prev_blk(BLOCK, kvw, bpt, bps), next_blk(BLOCK, kvw, bpt, bps),
            ctx_blk, ctx_blk,
            cur(POOL_WIDTH), prev_blk(POOL_HALO, POOL_WIDTH, hpt, hps),
            next_blk(POOL_HALO, POOL_WIDTH, hpt, hps),
            cur(D_MODEL), _mod_spec(),
            _const_spec(bias.shape), _const_spec(wpool.shape), _const_spec((1, POOL_WIDTH)),
            _const_spec(wout.shape), _const_spec((1, D_MODEL)), _const_spec((1, D_MODEL)),
        ],
        out_specs=cur(D_MODEL),
        out_shape=jax.ShapeDtypeStruct((BATCH * SEQ, D_MODEL), F32),
        scratch_shapes=[
            pltpu.VMEM((tq + 2 * BLOCK, kvw), BF16), pltpu.VMEM((tq + 2 * BLOCK, kvw), BF16),
            pltpu.VMEM((tq + 2 * POOL_HALO, POOL_WIDTH), F32), pltpu.VMEM((tq, ATT_WIDTH), F32),
        ],
        compiler_params=pltpu.CompilerParams(
            dimension_semantics=("arbitrary", "arbitrary"),
            vmem_limit_bytes=_vmem_limit(pipelined, resident + scratch, 6 * tok)),
        name="attn_latent",
    )(sink, q, k2, k2, k2, v2, v2, v2, kctx, vctx, u, u, u, h, mod, bias, wpool, pscale, wout,
      ln_g, ln_b)


def _attn_ctx(sink, q, kctx, vctx, u, h, mod, wpool, pscale, wout, ln_g, ln_b):
    tq = CTX_LEN
    kvw = 2 * KV_WIDTH
    blk = lambda w: pl.BlockSpec((tq, w), lambda b, t: (b, 0))
    tok = _nbytes((tq, D_MODEL), F32)
    pipelined = [_nbytes((tq, ATT_WIDTH), BF16), 2 * _nbytes((tq, kvw), BF16),
                 _nbytes((tq, POOL_WIDTH), F32), 2 * tok]
    resident = [_nbytes(wpool.shape, BF16), _nbytes(wout.shape, BF16)]
    scratch = [_nbytes((tq + 2 * POOL_HALO, POOL_WIDTH), F32), _nbytes((tq, ATT_WIDTH), F32)]
    return pl.pallas_call(
        functools.partial(_attn_ctx_kernel, tq=tq),
        grid=(BATCH, 1),
        in_specs=[
            _smem_spec(),
            blk(ATT_WIDTH), blk(kvw), blk(kvw), blk(POOL_WIDTH), blk(D_MODEL),
            pl.BlockSpec((1, N_MOD, D_MODEL), lambda b, t: (0, 0, 0)),
            _const_spec(wpool.shape), _const_spec((1, POOL_WIDTH)),
            _const_spec(wout.shape), _const_spec((1, D_MODEL)), _const_spec((1, D_MODEL)),
        ],
        out_specs=blk(D_MODEL),
        out_shape=jax.ShapeDtypeStruct((BATCH * CTX_LEN, D_MODEL), F32),
        scratch_shapes=[
            pltpu.VMEM((tq + 2 * POOL_HALO, POOL_WIDTH), F32), pltpu.VMEM((tq, ATT_WIDTH), F32),
        ],
        compiler_params=pltpu.CompilerParams(
            dimension_semantics=("arbitrary", "arbitrary"),
            vmem_limit_bytes=_vmem_limit(pipelined, resident + scratch, 6 * tok)),
        name="attn_context",
    )(sink, q, kctx, vctx, u, h, mod, wpool, pscale, wout, ln_g, ln_b)


def _row_perm(to_time_major):
    n = BATCH * PERM_STEPS
    r = lax.broadcasted_iota(jnp.int32, (n, n), 0)
    c = lax.broadcasted_iota(jnp.int32, (n, n), 1)
    if to_time_major:
        hit = ((r // BATCH) == (c % PERM_STEPS)) & ((r % BATCH) == (c // PERM_STEPS))
    else:
        hit = ((r // PERM_STEPS) == (c % BATCH)) & ((r % PERM_STEPS) == (c // BATCH))
    return jnp.where(hit, 1.0, 0.0).astype(BF16)


def _lru_inproj_kernel(*refs, tmt, with_gate):
    if with_gate:
        x_ref, scale_ref, shift_ref, w_ref, gate_ref, u_ref = refs
    else:
        x_ref, scale_ref, shift_ref, w_ref, u_ref = refs
    xin = (x_ref[...] * (1.0 + scale_ref[...]) + shift_ref[...]).astype(BF16)
    perm = _row_perm(to_time_major=True)
    parts = []
    for tau in range(tmt // PERM_STEPS):
        sub = xin[:, tau * PERM_STEPS:(tau + 1) * PERM_STEPS, :].reshape(BATCH * PERM_STEPS, D_MODEL)
        parts.append(jnp.dot(perm, sub, preferred_element_type=F32).astype(BF16))
    xt = jnp.concatenate(parts, axis=0)
    p = jnp.dot(xt, w_ref[...], preferred_element_type=F32)
    if with_gate:
        gate_ref[...] = p[:, :LRU_WIDTH].astype(BF16).reshape(tmt, BATCH, LRU_WIDTH)
        u_ref[...] = p[:, LRU_WIDTH:].reshape(tmt, BATCH, LRU_WIDTH)
    else:
        u_ref[...] = p.reshape(tmt, BATCH, LRU_WIDTH)


def _lru_inproj(x3, scale, shift, w, *, with_gate, tmt=32):
    length = x3.shape[1]
    blk = _nbytes((BATCH, tmt, D_MODEL), F32)
    out_spec = lambda: pl.BlockSpec((tmt, BATCH, LRU_WIDTH), lambda i: (i, 0, 0))
    out_specs = [out_spec()]
    out_shape = [jax.ShapeDtypeStruct((length, BATCH, LRU_WIDTH), F32)]
    outs = [blk]
    if with_gate:
        out_specs = [out_spec()] + out_specs
        out_shape = [jax.ShapeDtypeStruct((length, BATCH, LRU_WIDTH), BF16)] + out_shape
        outs.append(blk // 2)
    res = pl.pallas_call(
        functools.partial(_lru_inproj_kernel, tmt=tmt, with_gate=with_gate),
        grid=(length // tmt,),
        in_specs=[pl.BlockSpec((BATCH, tmt, D_MODEL), lambda i: (0, i, 0)),
                  _const_spec(scale.shape), _const_spec(shift.shape), _const_spec(w.shape)],
        out_specs=out_specs,
        out_shape=out_shape,
        compiler_params=pltpu.CompilerParams(
            dimension_semantics=("arbitrary",),
            vmem_limit_bytes=_vmem_limit([blk] + outs, [_nbytes(w.shape, BF16)], 5 * blk)),
        name="lru_inproj",
    )(x3, scale, shift, w)
    return res if with_gate else (None, res[0])


def _lru_scan_kernel(uc_ref, up_ref, un_ref, h0_ref, cw_ref, cb_ref, wrg_ref, ba_ref, bx_ref,
                     lam_ref, s_ref, hfin_ref, h_scr, a_scr, b_scr, *, tt, reverse):
    i = pl.program_id(0)
    nsteps = pl.num_programs(0)
    ti = nsteps - 1 - i if reverse else i

    @pl.when(i == 0)
    def _():
        h_scr[...] = h0_ref[...]

    prev = jnp.where(ti == 0, 0.0, up_ref[...])
    nxt = jnp.where(ti == nsteps - 1, 0.0, un_ref[...])
    ext = jnp.concatenate([prev, uc_ref[...], nxt], axis=0)
    u = cb_ref[...].reshape(1, 1, LRU_WIDTH)
    for tap in range(CONV_W):
        u = u + ext[tap:tap + tt] * cw_ref[tap:tap + 1, :].reshape(1, 1, LRU_WIDTH)
    u2 = u.reshape(tt * BATCH, LRU_WIDTH)
    lam = lam_ref[...]
    decay = -LRU_C * (jnp.maximum(-lam, 0.0) + jnp.log1p(jnp.exp(-jnp.abs(lam))))
    for blk in range(LRU_BLOCKS):
        cols = slice(blk * LRU_BLOCK_W, (blk + 1) * LRU_BLOCK_W)
        ub = u2[:, cols]
        rg = jnp.dot(ub.astype(BF16), wrg_ref[blk], preferred_element_type=F32)
        r = jax.nn.sigmoid(rg[:, :LRU_BLOCK_W] + ba_ref[:, cols])
        gi = jax.nn.sigmoid(rg[:, LRU_BLOCK_W:] + bx_ref[:, cols])
        a = jnp.exp(decay[:, cols] * r)
        b = jnp.sqrt(1.0 - a * a) * (gi * ub)
        a_scr[:, :, cols] = a.reshape(tt, BATCH, LRU_BLOCK_W)
        b_scr[:, :, cols] = b.reshape(tt, BATCH, LRU_BLOCK_W)

    def step(k, h):
        idx = tt - 1 - k if reverse else k
        h = a_scr[idx] * h + b_scr[idx]
        s_ref[idx] = h
        return h

    h = lax.fori_loop(0, tt, step, h_scr[...], unroll=4)
    h_scr[...] = h
    hfin_ref[...] = h


def _lru_scan(u_t, h0, conv_w, conv_b, wrg, ba, bx, lam, *, reverse, tt=32):
    length = u_t.shape[0]
    nsteps = length // tt
    tile = lambda i: (nsteps - 1 - i) if reverse else i
    blk = _nbytes((tt, BATCH, LRU_WIDTH), F32)
    return pl.pallas_call(
        functools.partial(_lru_scan_kernel, tt=tt, reverse=reverse),
        grid=(nsteps,),
        in_specs=[
            pl.BlockSpec((tt, BATCH, LRU_WIDTH), lambda i: (tile(i), 0, 0)),
            pl.BlockSpec((1, BATCH, LRU_WIDTH), lambda i: (jnp.maximum(tile(i) * tt - 1, 0), 0, 0)),
            pl.BlockSpec((2, BATCH, LRU_WIDTH),
                         lambda i: (jnp.minimum((tile(i) + 1) * (tt // 2), length // 2 - 1), 0, 0)),
            _const_spec((BATCH, LRU_WIDTH)),
            _const_spec((CONV_W, LRU_WIDTH)), _const_spec((1, LRU_WIDTH)),
            _const_spec(wrg.shape),
            _const_spec((1, LRU_WIDTH)), _const_spec((1, LRU_WIDTH)), _const_spec((1, LRU_WIDTH)),
        ],
        out_specs=[
            pl.BlockSpec((tt, BATCH, LRU_WIDTH), lambda i: (tile(i), 0, 0)),
            pl.BlockSpec((BATCH, LRU_WIDTH), lambda i: (0, 0)),
        ],
        out_shape=[
            jax.ShapeDtypeStruct((length, BATCH, LRU_WIDTH), F32),
            jax.ShapeDtypeStruct((BATCH, LRU_WIDTH), F32),
        ],
        scratch_shapes=[
            pltpu.VMEM((BATCH, LRU_WIDTH), F32),
            pltpu.VMEM((tt, BATCH, LRU_WIDTH), F32), pltpu.VMEM((tt, BATCH, LRU_WIDTH), F32),
        ],
        compiler_params=pltpu.CompilerParams(
            dimension_semantics=("arbitrary",),
            vmem_limit_bytes=_vmem_limit([blk, blk], [2 * blk, _nbytes(wrg.shape, BF16)], 6 * blk)),
        name="lru_scan_bwd" if reverse else "lru_scan_fwd",
    )(u_t, u_t, u_t, h0, conv_w, conv_b, wrg, ba, bx, lam)


def _lru_out_kernel(sf_ref, sb_ref, gate_ref, h_ref, g2_ref, w_ref, lng_ref, lnb_ref, o_ref, *, tt):
    y = sf_ref[...] + sb_ref[...]
    z = (_gelu_tanh(gate_ref[...].astype(F32)) * y).astype(BF16)
    perm = _row_perm(to_time_major=False)
    parts = []
    for tau in range(tt // PERM_STEPS):
        sub = z[tau * PERM_STEPS:(tau + 1) * PERM_STEPS].reshape(PERM_STEPS * BATCH, LRU_WIDTH)
        zb = jnp.dot(perm, sub, preferred_element_type=F32).astype(BF16)
        parts.append(zb.reshape(BATCH, PERM_STEPS, LRU_WIDTH))
    zb = jnp.concatenate(parts, axis=1).reshape(BATCH * tt, LRU_WIDTH)
    out = jnp.dot(zb, w_ref[...], preferred_element_type=F32).reshape(BATCH, tt, D_MODEL)
    res = DEEPNORM_ALPHA * h_ref[...] + g2_ref[...] * out
    o_ref[...] = _layer_norm(res, lng_ref[...].reshape(1, 1, D_MODEL), lnb_ref[...].reshape(1, 1, D_MODEL))


def _lru_out(s_f, s_b, gate_t, h3, g2, w_out, ln_g, ln_b, *, tt=32):
    tm_blk = lambda: pl.BlockSpec((tt, BATCH, LRU_WIDTH), lambda i: (i, 0, 0))
    bm_blk = lambda: pl.BlockSpec((BATCH, tt, D_MODEL), lambda i: (0, i, 0))
    blk = _nbytes((tt, BATCH, LRU_WIDTH), F32)
    return pl.pallas_call(
        functools.partial(_lru_out_kernel, tt=tt),
        grid=(SEQ // tt,),
        in_specs=[tm_blk(), tm_blk(), tm_blk(), bm_blk(),
                  _const_spec((BATCH, 1, D_MODEL)), _const_spec(w_out.shape),
                  _const_spec((1, D_MODEL)), _const_spec((1, D_MODEL))],
        out_specs=bm_blk(),
        out_shape=jax.ShapeDtypeStruct((BATCH, SEQ, D_MODEL), F32),
        compiler_params=pltpu.CompilerParams(
            dimension_semantics=("arbitrary",),
            vmem_limit_bytes=_vmem_limit([blk] * 5, [_nbytes(w_out.shape, BF16)], 5 * blk)),
        name="lru_out",
    )(s_f, s_b, gate_t, h3, g2, w_out, ln_g, ln_b)


def _rope_tables():
    rows = SEQ // GRID_W
    row = jnp.repeat(jnp.arange(rows, dtype=F32), GRID_W)
    col = jnp.tile(jnp.arange(GRID_W, dtype=F32), rows)
    inv = ROPE_THETA ** (-jnp.arange(ROPE_FREQS, dtype=F32) / ROPE_FREQS)
    ang = jnp.concatenate([row[:, None] * inv, col[:, None] * inv], axis=-1)
    cos, sin = jnp.cos(ang), jnp.sin(ang)
    cos_t = jnp.tile(cos, (1, V7X_LANES // (HEAD_DIM // 2)))
    sin_t = jnp.tile(jnp.concatenate([-sin, sin], axis=-1), (1, V7X_LANES // HEAD_DIM))
    return cos_t, sin_t


def kernel(x, c, ctx, c_ctx, w_mod, b_mod, ln_g, ln_b, ffn_w_gate, ffn_w_up, ffn_w_down, mix_ab_w_in, attn_sink, pool_w, pool_scale, mix_ab_w_out, lru_w_in, lru_conv_w, lru_conv_b, lru_wa, lru_ba, lru_wx, lru_bx, lru_lambda, lru_w_out):
    assert x.shape == (BATCH, SEQ, D_MODEL) and ctx.shape == (BATCH, CTX_LEN, D_MODEL)
    c_all = jnp.concatenate(
        [c, c_ctx[None, :], jnp.zeros((MOD_ROWS - BATCH - 1, D_MODEL), F32)], axis=0)
    mod_all = _modulation(c_all, w_mod, b_mod)
    mod_lat = mod_all[:, :BATCH].reshape(DEPTH, BATCH, N_MOD, D_MODEL)
    mod_ctx = mod_all[:, BATCH:BATCH + 1].reshape(DEPTH, 1, N_MOD, D_MODEL)

    wg = ffn_w_gate.astype(BF16)
    wu = ffn_w_up.astype(BF16)
    wd = ffn_w_down.astype(BF16)
    row = lambda v: v.reshape(1, -1)
    n_ctx = BATCH * CTX_LEN

    h = x.reshape(BATCH * SEQ, D_MODEL)
    hc = ctx.reshape(n_ctx, D_MODEL)

    l = 0
    ml, mc = mod_lat[l], mod_ctx[l]
    ffn1 = (wg[l, 0], wu[l, 0], wd[l, 0], row(ln_g[l, 0]), row(ln_b[l, 0]))
    ffn2 = (wg[l, 1], wu[l, 1], wd[l, 1], row(ln_g[l, 2]), row(ln_b[l, 2]))
    h = _ffn(h, ml, 0, *ffn1, nb=BATCH, length=SEQ)
    hc = _ffn(hc, mc, 0, *ffn1, nb=1, length=n_ctx)
    w_in = mix_ab_w_in[0].astype(BF16)
    cos_t, sin_t = _rope_tables()
    q, k2, v2, u = _attn_inproj(h, ml, w_in, cos_t, sin_t, nb=BATCH, length=SEQ)
    q_c, k2_c, v2_c, u_c = _attn_inproj(hc, mc, w_in, None, None, nb=1, length=n_ctx)
    mix_args = (pool_w[0].astype(BF16), row(pool_scale[0]), mix_ab_w_out[0].astype(BF16),
                row(ln_g[l, 1]), row(ln_b[l, 1]))
    h = _attn_lat(attn_sink[0], q, k2, v2, k2_c, v2_c, u, h, ml, *mix_args)
    hc = _attn_ctx(attn_sink[0], q_c, k2_c, v2_c, u_c, hc, mc, *mix_args)
    h = _ffn(h, ml, 6, *ffn2, nb=BATCH, length=SEQ)
    hc = _ffn(hc, mc, 6, *ffn2, nb=1, length=n_ctx)

    l = 1
    ml, mc = mod_lat[l], mod_ctx[l]
    ffn1 = (wg[l, 0], wu[l, 0], wd[l, 0], row(ln_g[l, 0]), row(ln_b[l, 0]))
    ffn2 = (wg[l, 1], wu[l, 1], wd[l, 1], row(ln_g[l, 2]), row(ln_b[l, 2]))
    h = _ffn(h, ml, 0, *ffn1, nb=BATCH, length=SEQ)
    hc = _ffn(hc, mc, 0, *ffn1, nb=1, length=n_ctx)
    w_in = lru_w_in[0].astype(BF16)
    h3 = h.reshape(BATCH, SEQ, D_MODEL)
    gate_t, u_t = _lru_inproj(h3, ml[:, 4:5, :], ml[:, 3:4, :], w_in, with_gate=True)
    _, uc_t = _lru_inproj(hc.reshape(BATCH, CTX_LEN, D_MODEL), mc[:, 4:5, :], mc[:, 3:4, :],
                          w_in[:, LRU_WIDTH:], with_gate=False)
    wrg = jnp.concatenate([lru_wa[0], lru_wx[0]], axis=-1).astype(BF16)
    conv_w, conv_b = lru_conv_w[0], row(lru_conv_b[0])
    zeros = jnp.zeros((BATCH, LRU_WIDTH), F32)
    states = []
    for d in range(2):
        args = (conv_w, conv_b, wrg[d], row(lru_ba[0, d]), row(lru_bx[0, d]), row(lru_lambda[0, d]))
        _, h_ctx = _lru_scan(uc_t, zeros, *args, reverse=bool(d))
        s, _ = _lru_scan(u_t, h_ctx, *args, reverse=bool(d))
        states.append(s)
    h3 = _lru_out(states[0], states[1], gate_t, h3, ml[:, 5:6, :], lru_w_out[0].astype(BF16),
                  row(ln_g[l, 1]), row(ln_b[l, 1]))
    h = _ffn(h3.reshape(BATCH * SEQ, D_MODEL), ml, 6, *ffn2, nb=BATCH, length=SEQ)
    return h.reshape(BATCH, SEQ, D_MODEL)
```

```python
import functools

import jax
import jax.numpy as jnp
from jax import lax
from jax.experimental import pallas as pl
from jax.experimental.pallas import tpu as pltpu

D_MODEL = 1024
BATCH = 16
SEQ = 2048
DEPTH = 2
GRID_W = 64
CTX_LEN = 256
HEAD_DIM = 64
ATT_HEADS = 8
ATT_KV_HEADS = 2
ATT_GROUPS = ATT_HEADS // ATT_KV_HEADS
ATT_WIDTH = ATT_HEADS * HEAD_DIM
KV_WIDTH = ATT_KV_HEADS * HEAD_DIM
WINDOW = 128
BLOCK = 128
ATT_SCALE = HEAD_DIM ** -0.5
LOG2E = 1.4426950408889634
ROPE_THETA = 10000.0
ROPE_FREQS = HEAD_DIM // 4
POOL_WINDOWS = (2, 4, 8, 16)
POOL_WIDTH = D_MODEL // 2
POOL_GROUP_W = POOL_WIDTH // len(POOL_WINDOWS)
MIX_AB_IN = ATT_WIDTH + 2 * KV_WIDTH + POOL_WIDTH
LRU_WIDTH = D_MODEL
LRU_BLOCKS = 8
LRU_BLOCK_W = LRU_WIDTH // LRU_BLOCKS
LRU_C = 8.0
CONV_W = 4
CONV_LEFT = (CONV_W - 1) // 2
D_FF = 2816
N_MOD = 9
LN_EPS = 1e-5
NEG_INF = -1e30
DEEPNORM_ALPHA = (2 * DEPTH) ** 0.25

V7X_LANES = 128
V7X_SUBLANES = 8
V7X_VMEM_BYTES = 64 * 1024 * 1024
V7X_VMEM_USABLE_BYTES = 60000 * 1024

F32 = jnp.float32
BF16 = jnp.bfloat16

POOL_HALO = V7X_SUBLANES
PERM_STEPS = 16
MOD_ROWS = 24


def _nbytes(shape, dtype):
    n = 1
    for s in shape:
        n *= s
    return n * jnp.dtype(dtype).itemsize


def _vmem_limit(pipelined, resident, temporaries):
    est = 2 * sum(pipelined) + sum(resident) + temporaries
    return int(min(V7X_VMEM_USABLE_BYTES, max(est * 5 // 4, 16 * 1024 * 1024)))


def _const_spec(shape):
    nd = len(shape)
    return pl.BlockSpec(shape, lambda *_: (0,) * nd, pipeline_mode=pl.Buffered(1))


def _tok_spec(tm, width, nt):
    return pl.BlockSpec((tm, width), lambda b, t: (b * nt + t, 0))


def _mod_spec():
    return pl.BlockSpec((1, N_MOD, D_MODEL), lambda b, t: (b, 0, 0))


def _layer_norm(z, g, b):
    mu = jnp.mean(z, axis=-1, keepdims=True)
    zc = z - mu
    var = jnp.mean(zc * zc, axis=-1, keepdims=True)
    return zc * lax.rsqrt(var + LN_EPS) * g + b


def _gelu_tanh(x):
    return 0.5 * x * (1.0 + jnp.tanh(0.7978845608028654 * (x + 0.044715 * (x * x * x))))


def _mod_kernel(c_ref, w_ref, b_ref, o_ref):
    c = c_ref[...]
    a = c * jax.nn.sigmoid(c)
    o_ref[0] = jnp.dot(a, w_ref[0], preferred_element_type=F32,
                       precision=lax.Precision.HIGHEST) + b_ref[0]


def _modulation(c_all, w_mod, b_mod):
    tn = 1024
    n_out = N_MOD * D_MODEL
    blocks = [_nbytes((1, D_MODEL, tn), F32), _nbytes((1, MOD_ROWS, tn), F32)]
    return pl.pallas_call(
        _mod_kernel,
        grid=(DEPTH, n_out // tn),
        in_specs=[
            pl.BlockSpec((MOD_ROWS, D_MODEL), lambda l, j: (0, 0)),
            pl.BlockSpec((1, D_MODEL, tn), lambda l, j: (l, 0, j)),
            pl.BlockSpec((1, 1, tn), lambda l, j: (l, 0, j)),
        ],
        out_specs=pl.BlockSpec((1, MOD_ROWS, tn), lambda l, j: (l, 0, j)),
        out_shape=jax.ShapeDtypeStruct((DEPTH, MOD_ROWS, n_out), F32),
        compiler_params=pltpu.CompilerParams(
            dimension_semantics=("arbitrary", "arbitrary"),
            vmem_limit_bytes=_vmem_limit(blocks, [], 4 * blocks[0])),
        name="modulation",
    )(c_all, w_mod, b_mod.reshape(DEPTH, 1, n_out))


def _ffn_kernel(x_ref, mod_ref, wg_ref, wu_ref, wd_ref, lng_ref, lnb_ref, o_ref, *, j0, sub):
    shift = mod_ref[0, j0:j0 + 1, :]
    scale = mod_ref[0, j0 + 1:j0 + 2, :]
    gate = mod_ref[0, j0 + 2:j0 + 3, :]
    for s in range(x_ref.shape[0] // sub):
        rows = slice(s * sub, (s + 1) * sub)
        x = x_ref[rows, :]
        xin = (x * (1.0 + scale) + shift).astype(BF16)
        g = jnp.dot(xin, wg_ref[...], preferred_element_type=F32)
        u = jnp.dot(xin, wu_ref[...], preferred_element_type=F32)
        a = (g * jax.nn.sigmoid(g) * u).astype(BF16)
        y = jnp.dot(a, wd_ref[...], preferred_element_type=F32)
        z = DEEPNORM_ALPHA * x + (0.5 * gate) * y
        o_ref[rows, :] = _layer_norm(z, lng_ref[...], lnb_ref[...])


def _ffn(x, mod, j0, which, wg, wu, wd, ln_g, ln_b, *, nb, length, tm=512, sub=256):
    nt = length // tm
    tok = _nbytes((tm, D_MODEL), F32)
    weights = [_nbytes(w.shape[2:], BF16) for w in (wg, wu, wd)]
    temps = 3 * _nbytes((tm, D_FF), F32) + 4 * tok
    wspec = lambda w: pl.BlockSpec((None, None) + w.shape[2:], lambda b, t: which + (0, 0),
                                   pipeline_mode=pl.Buffered(1))
    return pl.pallas_call(
        functools.partial(_ffn_kernel, j0=j0, sub=sub),
        grid=(nb, nt),
        in_specs=[
            _tok_spec(tm, D_MODEL, nt),
            _mod_spec(),
            wspec(wg), wspec(wu), wspec(wd),
            _const_spec((1, D_MODEL)), _const_spec((1, D_MODEL)),
        ],
        out_specs=_tok_spec(tm, D_MODEL, nt),
        out_shape=jax.ShapeDtypeStruct((nb * length, D_MODEL), F32),
        compiler_params=pltpu.CompilerParams(
            dimension_semantics=("arbitrary", "arbitrary"),
            vmem_limit_bytes=_vmem_limit([tok, tok], weights, temps)),
        name="ffn",
    )(x, mod, wg, wu, wd, ln_g, ln_b)


def _dup_halves(z, lane):
    zr = pltpu.roll(z, HEAD_DIM, 1)
    lo = lane < HEAD_DIM
    return jnp.where(lo, z, zr), jnp.where(lo, zr, z)


def _attn_inproj_kernel(*refs, rope):
    if rope:
        x_ref, mod_ref, w_ref, cos_ref, sin_ref, q_ref, k_ref, v_ref, u_ref = refs
    else:
        x_ref, mod_ref, w_ref, q_ref, k_ref, v_ref, u_ref = refs
    x = x_ref[...]
    xin = (x * (1.0 + mod_ref[0, 4:5, :]) + mod_ref[0, 3:4, :]).astype(BF16)
    p = jnp.dot(xin, w_ref[...], preferred_element_type=F32)
    lane = lax.broadcasted_iota(jnp.int32, (x.shape[0], V7X_LANES), 1)
    first_half = (lane & (HEAD_DIM - 1)) < HEAD_DIM // 2

    def rot(z):
        if not rope:
            return z
        zr = jnp.where(first_half, pltpu.roll(z, V7X_LANES - HEAD_DIM // 2, 1),
                       pltpu.roll(z, HEAD_DIM // 2, 1))
        return z * cos_ref[...] + zr * sin_ref[...]

    for c in range(ATT_WIDTH // V7X_LANES):
        sl = slice(c * V7X_LANES, (c + 1) * V7X_LANES)
        q_ref[:, sl] = (rot(p[:, sl]) * (ATT_SCALE * LOG2E)).astype(BF16)
    k0, k1 = _dup_halves(rot(p[:, ATT_WIDTH:ATT_WIDTH + KV_WIDTH]), lane)
    k_ref[:, 0:V7X_LANES] = k0.astype(BF16)
    k_ref[:, V7X_LANES:] = k1.astype(BF16)
    v0, v1 = _dup_halves(p[:, ATT_WIDTH + KV_WIDTH:ATT_WIDTH + 2 * KV_WIDTH], lane)
    v_ref[:, 0:V7X_LANES] = v0.astype(BF16)
    v_ref[:, V7X_LANES:] = v1.astype(BF16)
    u_ref[...] = p[:, ATT_WIDTH + 2 * KV_WIDTH:]


def _attn_inproj(x, mod, w_in, cos, sin, *, nb, length, tm=512):
    nt = length // tm
    rope = cos is not None
    tok = _nbytes((tm, D_MODEL), F32)
    outs = [_nbytes((tm, ATT_WIDTH), BF16), 2 * _nbytes((tm, 2 * KV_WIDTH), BF16),
            _nbytes((tm, POOL_WIDTH), F32)]
    in_specs = [_tok_spec(tm, D_MODEL, nt), _mod_spec(), _const_spec(w_in.shape)]
    args = [x, mod, w_in]
    if rope:
        in_specs += [pl.BlockSpec((tm, V7X_LANES), lambda b, t: (t, 0))] * 2
        args += [cos, sin]
    rows = nb * length
    return pl.pallas_call(
        functools.partial(_attn_inproj_kernel, rope=rope),
        grid=(nb, nt),
        in_specs=in_specs,
        out_specs=[
            _tok_spec(tm, ATT_WIDTH, nt),
            _tok_spec(tm, 2 * KV_WIDTH, nt),
            _tok_spec(tm, 2 * KV_WIDTH, nt),
            _tok_spec(tm, POOL_WIDTH, nt),
        ],
        out_shape=[
            jax.ShapeDtypeStruct((rows, ATT_WIDTH), BF16),
            jax.ShapeDtypeStruct((rows, 2 * KV_WIDTH), BF16),
            jax.ShapeDtypeStruct((rows, 2 * KV_WIDTH), BF16),
            jax.ShapeDtypeStruct((rows, POOL_WIDTH), F32),
        ],
        compiler_params=pltpu.CompilerParams(
            dimension_semantics=("arbitrary", "arbitrary"),
            vmem_limit_bytes=_vmem_limit([tok] + outs, [_nbytes(w_in.shape, BF16)],
                                         3 * _nbytes((tm, MIX_AB_IN), F32))),
        name="attn_inproj",
    )(*args)


def _attend_block(q_blk, keys, values, biases, sink_ref, kh):
    lane = lax.broadcasted_iota(jnp.int32, (BLOCK, V7X_LANES), 1)
    lo = lane < HEAD_DIM
    zero = jnp.zeros((BLOCK, V7X_LANES), BF16)
    parts = []
    for g in range(ATT_GROUPS):
        c = (kh * ATT_GROUPS + g) // 2
        qc = q_blk[:, c * V7X_LANES:(c + 1) * V7X_LANES]
        parts.append(jnp.where(lo if g % 2 == 0 else jnp.logical_not(lo), qc, zero))
    qs = jnp.concatenate(parts, axis=0)
    nt_dims = (((1,), (1,)), ((), ()))
    scores = [lax.dot_general(qs, k, nt_dims, preferred_element_type=F32) for k in keys]
    probs = [[] for _ in keys]
    rdens = []
    for g in range(ATT_GROUPS):
        rows = slice(g * BLOCK, (g + 1) * BLOCK)
        sg = [s[rows] if b is None else s[rows] + b for s, b in zip(scores, biases)]
        sink = sink_ref[kh * ATT_GROUPS + g] * LOG2E
        m = sg[0].max(axis=-1, keepdims=True)
        for s in sg[1:]:
            m = jnp.maximum(m, s.max(axis=-1, keepdims=True))
        m = jnp.maximum(m, sink)
        den = jnp.exp2(sink - m)
        for i, s in enumerate(sg):
            e = jnp.exp2(s - m)
            den = den + e.sum(axis=-1, keepdims=True)
            probs[i].append(e.astype(BF16))
        rdens.append(1.0 / den)
    o2 = None
    for p, v in zip(probs, values):
        t = jnp.dot(jnp.concatenate(p, axis=0), v, preferred_element_type=F32)
        o2 = t if o2 is None else o2 + t
    og = [o2[g * BLOCK:(g + 1) * BLOCK] * rdens[g] for g in range(ATT_GROUPS)]
    return jnp.where(lo, og[0], og[1]), jnp.where(lo, og[2], og[3])


def _pool_mix(ubuf_ref, tq, pos0, length, wpool_ref, pscale_ref):
    pos = pos0 + lax.broadcasted_iota(jnp.int32, (tq, 1), 0)
    outs = []
    for gi, w in enumerate(POOL_WINDOWS):
        r = w // 2
        cols = slice(gi * POOL_GROUP_W, (gi + 1) * POOL_GROUP_W)
        acc = ubuf_ref[POOL_HALO - r:POOL_HALO - r + tq, cols]
        for o in range(-r + 1, r + 1):
            acc = acc + ubuf_ref[POOL_HALO + o:POOL_HALO + o + tq, cols]
        cnt = (jnp.minimum(pos + r, length - 1) - jnp.maximum(pos - r, 0) + 1).astype(F32)
        d = acc / cnt - ubuf_ref[POOL_HALO:POOL_HALO + tq, cols]
        y = jnp.dot(d.astype(BF16), wpool_ref[gi], preferred_element_type=F32)
        outs.append(y * pscale_ref[:, cols])
    return jnp.concatenate(outs, axis=-1)


def _attn_out(att, pool, wout_ref, h_ref, gate, lng_ref, lnb_ref, o_ref):
    mix = jnp.concatenate([att, pool], axis=-1).astype(BF16)
    y = jnp.dot(mix, wout_ref[...], preferred_element_type=F32)
    z = DEEPNORM_ALPHA * h_ref[...] + gate * y
    o_ref[...] = _layer_norm(z, lng_ref[...], lnb_ref[...])


def _attn_lat_kernel(sink_ref, q_ref, kc_ref, kp_ref, kn_ref, vc_ref, vp_ref, vn_ref,
                     kctx_ref, vctx_ref, uc_ref, up_ref, un_ref, h_ref, mod_ref, bias_ref,
                     wpool_ref, pscale_ref, wout_ref, lng_ref, lnb_ref, o_ref,
                     kbuf, vbuf, ubuf, att_scr, *, tq, length):
    t = pl.program_id(1)
    last = pl.num_programs(1) - 1
    nblk = tq // BLOCK
    kbuf[0:BLOCK] = kp_ref[...]
    kbuf[BLOCK:BLOCK + tq] = kc_ref[...]
    kbuf[BLOCK + tq:] = kn_ref[...]
    vbuf[0:BLOCK] = vp_ref[...]
    vbuf[BLOCK:BLOCK + tq] = vc_ref[...]
    vbuf[BLOCK + tq:] = vn_ref[...]
    for j in range(nblk):
        blk = t * nblk + j
        bidx = jnp.where(blk == 0, 0, jnp.where(blk == length // BLOCK - 1, 2, 1))
        bias = bias_ref[bidx]
        q_blk = q_ref[j * BLOCK:(j + 1) * BLOCK, :]
        for kh in range(ATT_KV_HEADS):
            cols = slice(kh * V7X_LANES, (kh + 1) * V7X_LANES)
            win = slice(j * BLOCK, j * BLOCK + 3 * BLOCK)
            c0, c1 = _attend_block(q_blk, [kbuf[win, cols], kctx_ref[:, cols]],
                                   [vbuf[win, cols], vctx_ref[:, cols]], [bias, None], sink_ref, kh)
            rows = slice(j * BLOCK, (j + 1) * BLOCK)
            att_scr[rows, (2 * kh) * V7X_LANES:(2 * kh + 1) * V7X_LANES] = c0
            att_scr[rows, (2 * kh + 1) * V7X_LANES:(2 * kh + 2) * V7X_LANES] = c1
    ubuf[0:POOL_HALO] = jnp.where(t == 0, 0.0, up_ref[...])
    ubuf[POOL_HALO:POOL_HALO + tq] = uc_ref[...]
    ubuf[POOL_HALO + tq:] = jnp.where(t == last, 0.0, un_ref[...])
    pool = _pool_mix(ubuf, tq, t * tq, length, wpool_ref, pscale_ref)
    _attn_out(att_scr[...], pool, wout_ref, h_ref, mod_ref[0, 5:6, :], lng_ref, lnb_ref, o_ref)


def _attn_ctx_kernel(sink_ref, q_ref, kctx_ref, vctx_ref, uc_ref, h_ref, mod_ref,
                     wpool_ref, pscale_ref, wout_ref, lng_ref, lnb_ref, o_ref,
                     ubuf, att_scr, *, tq):
    for j in range(tq // BLOCK):
        q_blk = q_ref[j * BLOCK:(j + 1) * BLOCK, :]
        for kh in range(ATT_KV_HEADS):
            cols = slice(kh * V7X_LANES, (kh + 1) * V7X_LANES)
            c0, c1 = _attend_block(q_blk, [kctx_ref[:, cols]], [vctx_ref[:, cols]], [None],
                                   sink_ref, kh)
            rows = slice(j * BLOCK, (j + 1) * BLOCK)
            att_scr[rows, (2 * kh) * V7X_LANES:(2 * kh + 1) * V7X_LANES] = c0
            att_scr[rows, (2 * kh + 1) * V7X_LANES:(2 * kh + 2) * V7X_LANES] = c1
    ubuf[0:POOL_HALO] = jnp.zeros((POOL_HALO, POOL_WIDTH), F32)
    ubuf[POOL_HALO:POOL_HALO + tq] = uc_ref[...]
    ubuf[POOL_HALO + tq:] = jnp.zeros((POOL_HALO, POOL_WIDTH), F32)
    pool = _pool_mix(ubuf, tq, 0, tq, wpool_ref, pscale_ref)
    _attn_out(att_scr[...], pool, wout_ref, h_ref, mod_ref[0, 5:6, :], lng_ref, lnb_ref, o_ref)


def _window_bias():
    qi = jnp.arange(BLOCK)[:, None]
    kc = jnp.arange(3 * BLOCK)[None, :]
    band = jnp.abs(qi + BLOCK - kc) <= WINDOW
    first = band & (kc >= BLOCK)
    final = band & (kc < 2 * BLOCK)
    return jnp.where(jnp.stack([first, band, final]), 0.0, NEG_INF).astype(F32)


def _smem_spec():
    return pl.BlockSpec(memory_space=pltpu.SMEM)


def _attn_lat(sink, q, k2, v2, kctx, vctx, u, h, mod, wpool, pscale, wout, ln_g, ln_b, *, tq=512):
    nt = SEQ // tq
    bpt = tq // BLOCK
    bps = SEQ // BLOCK
    hpt = tq // POOL_HALO
    hps = SEQ // POOL_HALO
    kvw = 2 * KV_WIDTH
    cur = lambda w: pl.BlockSpec((tq, w), lambda b, t: (b * nt + t, 0))
    prev_blk = lambda rows, w, per_tile, per_samp: pl.BlockSpec(
        (rows, w), lambda b, t: (b * per_samp + jnp.maximum(t * per_tile - 1, 0), 0))
    next_blk = lambda rows, w, per_tile, per_samp: pl.BlockSpec(
        (rows, w), lambda b, t: (b * per_samp + jnp.minimum((t + 1) * per_tile, per_samp - 1), 0))
    ctx_blk = pl.BlockSpec((CTX_LEN, kvw), lambda b, t: (b, 0))
    bias = _window_bias()
    tok = _nbytes((tq, D_MODEL), F32)
    pipelined = [_nbytes((tq, ATT_WIDTH), BF16), 4 * _nbytes((tq + 2 * BLOCK, kvw), BF16),
                 _nbytes((tq + 2 * POOL_HALO, POOL_WIDTH), F32), 2 * tok]
    resident = [_nbytes(bias.shape, F32), _nbytes(wpool.shape, BF16), _nbytes(wout.shape, BF16)]
    scratch = [_nbytes((tq + 2 * BLOCK, kvw), BF16)] * 2 + [
        _nbytes((tq + 2 * POOL_HALO, POOL_WIDTH), F32), _nbytes((tq, ATT_WIDTH), F32)]
    return pl.pallas_call(
        functools.partial(_attn_lat_kernel, tq=tq, length=SEQ),
        grid=(BATCH, nt),
        in_specs=[
            _smem_spec(),
            cur(ATT_WIDTH),
            cur(kvw), prev_blk(BLOCK, kvw, bpt, bps), next_blk(BLOCK, kvw, bpt, bps),
            cur(kvw), prev_blk(BLOCK, kvw, bpt, bps), next_blk(BLOCK, kvw, bpt, bps),
            ctx_blk, ctx_blk,
            cur(POOL_WIDTH), prev_blk(POOL_HALO, POOL_WIDTH, hpt, hps),
            next_blk(POOL_HALO, POOL_WIDTH, hpt, hps),
            cur(D_MODEL), _mod_spec(),
            _const_spec(bias.shape), _const_spec(wpool.shape), _const_spec((1, POOL_WIDTH)),
            _const_spec(wout.shape), _const_spec((1, D_MODEL)), _const_spec((1, D_MODEL)),
        ],
        out_specs=cur(D_MODEL),
        out_shape=jax.ShapeDtypeStruct((BATCH * SEQ, D_MODEL), F32),
        scratch_shapes=[
            pltpu.VMEM((tq + 2 * BLOCK, kvw), BF16), pltpu.VMEM((tq + 2 * BLOCK, kvw), BF16),
            pltpu.VMEM((tq + 2 * POOL_HALO, POOL_WIDTH), F32), pltpu.VMEM((tq, ATT_WIDTH), F32),
        ],
        compiler_params=pltpu.CompilerParams(
            dimension_semantics=("arbitrary", "arbitrary"),
            vmem_limit_bytes=_vmem_limit(pipelined, resident + scratch, 6 * tok)),
        name="attn_latent",
    )(sink, q, k2, k2, k2, v2, v2, v2, kctx, vctx, u, u, u, h, mod, bias, wpool, pscale, wout,
      ln_g, ln_b)


def _attn_ctx(sink, q, kctx, vctx, u, h, mod, wpool, pscale, wout, ln_g, ln_b):
    tq = CTX_LEN
    kvw = 2 * KV_WIDTH
    blk = lambda w: pl.BlockSpec((tq, w), lambda b, t: (b, 0))
    tok = _nbytes((tq, D_MODEL), F32)
    pipelined = [_nbytes((tq, ATT_WIDTH), BF16), 2 * _nbytes((tq, kvw), BF16),
                 _nbytes((tq, POOL_WIDTH), F32), 2 * tok]
    resident = [_nbytes(wpool.shape, BF16), _nbytes(wout.shape, BF16)]
    scratch = [_nbytes((tq + 2 * POOL_HALO, POOL_WIDTH), F32), _nbytes((tq, ATT_WIDTH), F32)]
    return pl.pallas_call(
        functools.partial(_attn_ctx_kernel, tq=tq),
        grid=(BATCH, 1),
        in_specs=[
            _smem_spec(),
            blk(ATT_WIDTH), blk(kvw), blk(kvw), blk(POOL_WIDTH), blk(D_MODEL),
            pl.BlockSpec((1, N_MOD, D_MODEL), lambda b, t: (0, 0, 0)),
            _const_spec(wpool.shape), _const_spec((1, POOL_WIDTH)),
            _const_spec(wout.shape), _const_spec((1, D_MODEL)), _const_spec((1, D_MODEL)),
        ],
        out_specs=blk(D_MODEL),
        out_shape=jax.ShapeDtypeStruct((BATCH * CTX_LEN, D_MODEL), F32),
        scratch_shapes=[
            pltpu.VMEM((tq + 2 * POOL_HALO, POOL_WIDTH), F32), pltpu.VMEM((tq, ATT_WIDTH), F32),
        ],
        compiler_params=pltpu.CompilerParams(
            dimension_semantics=("arbitrary", "arbitrary"),
            vmem_limit_bytes=_vmem_limit(pipelined, resident + scratch, 6 * tok)),
        name="attn_context",
    )(sink, q, kctx, vctx, u, h, mod, wpool, pscale, wout, ln_g, ln_b)


def _row_perm(to_time_major):
    n = BATCH * PERM_STEPS
    r = lax.broadcasted_iota(jnp.int32, (n, n), 0)
    c = lax.broadcasted_iota(jnp.int32, (n, n), 1)
    if to_time_major:
        hit = ((r // BATCH) == (c % PERM_STEPS)) & ((r % BATCH) == (c // PERM_STEPS))
    else:
        hit = ((r // PERM_STEPS) == (c % BATCH)) & ((r % PERM_STEPS) == (c // BATCH))
    return jnp.where(hit, 1.0, 0.0).astype(BF16)


def _lru_inproj_kernel(*refs, tmt, with_gate):
    if with_gate:
        x_ref, scale_ref, shift_ref, w_ref, gate_ref, u_ref = refs
    else:
        x_ref, scale_ref, shift_ref, w_ref, u_ref = refs
    xin = (x_ref[...] * (1.0 + scale_ref[...]) + shift_ref[...]).astype(BF16)
    perm = _row_perm(to_time_major=True)
    parts = []
    for tau in range(tmt // PERM_STEPS):
        sub = xin[:, tau * PERM_STEPS:(tau + 1) * PERM_STEPS, :].reshape(BATCH * PERM_STEPS, D_MODEL)
        parts.append(jnp.dot(perm, sub, preferred_element_type=F32).astype(BF16))
    xt = jnp.concatenate(parts, axis=0)
    p = jnp.dot(xt, w_ref[...], preferred_element_type=F32)
    if with_gate:
        gate_ref[...] = p[:, :LRU_WIDTH].astype(BF16).reshape(tmt, BATCH, LRU_WIDTH)
        u_ref[...] = p[:, LRU_WIDTH:].reshape(tmt, BATCH, LRU_WIDTH)
    else:
        u_ref[...] = p.reshape(tmt, BATCH, LRU_WIDTH)


def _lru_inproj(x3, scale, shift, w, *, with_gate, tmt=32):
    length = x3.shape[1]
    blk = _nbytes((BATCH, tmt, D_MODEL), F32)
    out_spec = lambda: pl.BlockSpec((tmt, BATCH, LRU_WIDTH), lambda i: (i, 0, 0))
    out_specs = [out_spec()]
    out_shape = [jax.ShapeDtypeStruct((length, BATCH, LRU_WIDTH), F32)]
    outs = [blk]
    if with_gate:
        out_specs = [out_spec()] + out_specs
        out_shape = [jax.ShapeDtypeStruct((length, BATCH, LRU_WIDTH), BF16)] + out_shape
        outs.append(blk // 2)
    res = pl.pallas_call(
        functools.partial(_lru_inproj_kernel, tmt=tmt, with_gate=with_gate),
        grid=(length // tmt,),
        in_specs=[pl.BlockSpec((BATCH, tmt, D_MODEL), lambda i: (0, i, 0)),
                  _const_spec(scale.shape), _const_spec(shift.shape), _const_spec(w.shape)],
        out_specs=out_specs,
        out_shape=out_shape,
        compiler_params=pltpu.CompilerParams(
            dimension_semantics=("arbitrary",),
            vmem_limit_bytes=_vmem_limit([blk] + outs, [_nbytes(w.shape, BF16)], 5 * blk)),
        name="lru_inproj",
    )(x3, scale, shift, w)
    return res if with_gate else (None, res[0])


def _lru_coeffs(uc_ref, up_ref, un_ref, cw_ref, cb_ref, wrg_ref, ba_ref, bx_ref, lam_ref,
                a_scr, b_scr, ti, nsteps, tt):
    prev = jnp.where(ti == 0, 0.0, up_ref[...])
    nxt = jnp.where(ti == nsteps - 1, 0.0, un_ref[...])
    ext = jnp.concatenate([prev, uc_ref[...], nxt], axis=0)
    u = cb_ref[...].reshape(1, 1, LRU_WIDTH)
    for tap in range(CONV_W):
        u = u + ext[tap:tap + tt] * cw_ref[tap:tap + 1, :].reshape(1, 1, LRU_WIDTH)
    u2 = u.reshape(tt * BATCH, LRU_WIDTH)
    lam = lam_ref[...]
    decay2 = (-LRU_C * LOG2E) * (jnp.maximum(-lam, 0.0) + jnp.log1p(jnp.exp(-jnp.abs(lam))))
    for blk in range(LRU_BLOCKS):
        cols = slice(blk * LRU_BLOCK_W, (blk + 1) * LRU_BLOCK_W)
        ub = u2[:, cols]
        rg = jnp.dot(ub.astype(BF16), wrg_ref[blk], preferred_element_type=F32)
        r = jax.nn.sigmoid(rg[:, :LRU_BLOCK_W] + ba_ref[:, cols])
        gi = jax.nn.sigmoid(rg[:, LRU_BLOCK_W:] + bx_ref[:, cols])
        a = jnp.exp2(decay2[:, cols] * r)
        x = 1.0 - a * a
        b = jnp.where(x > 0.0, x * lax.rsqrt(x), 0.0) * (gi * ub)
        a_scr[:, :, cols] = a.reshape(tt, BATCH, LRU_BLOCK_W)
        b_scr[:, :, cols] = b.reshape(tt, BATCH, LRU_BLOCK_W)


def _lru_scan_kernel(uc_ref, up_ref, un_ref, h0_ref, cw_ref, cb_ref, wrg_ref, ba_ref, bx_ref,
                     lam_ref, s_ref, hfin_ref, h_scr, a_scr, b_scr, *, tt, reverse):
    i = pl.program_id(0)
    nsteps = pl.num_programs(0)
    ti = nsteps - 1 - i if reverse else i

    @pl.when(i == 0)
    def _():
        h_scr[...] = h0_ref[...]

    _lru_coeffs(uc_ref, up_ref, un_ref, cw_ref, cb_ref, wrg_ref, ba_ref, bx_ref, lam_ref,
                a_scr, b_scr, ti, nsteps, tt)

    def step(k, h):
        idx = tt - 1 - k if reverse else k
        h = a_scr[idx] * h + b_scr[idx]
        s_ref[idx] = h
        return h

    h = lax.fori_loop(0, tt, step, h_scr[...], unroll=4)
    h_scr[...] = h
    hfin_ref[...] = h


def _lru_scan_in_specs(length, tt, tile, wrg_shape):
    return [
        pl.BlockSpec((tt, BATCH, LRU_WIDTH), lambda i: (tile(i), 0, 0)),
        pl.BlockSpec((1, BATCH, LRU_WIDTH), lambda i: (jnp.maximum(tile(i) * tt - 1, 0), 0, 0)),
        pl.BlockSpec((2, BATCH, LRU_WIDTH),
                     lambda i: (jnp.minimum((tile(i) + 1) * (tt // 2), length // 2 - 1), 0, 0)),
        _const_spec((BATCH, LRU_WIDTH)),
        _const_spec((CONV_W, LRU_WIDTH)), _const_spec((1, LRU_WIDTH)),
        _const_spec(wrg_shape),
        _const_spec((1, LRU_WIDTH)), _const_spec((1, LRU_WIDTH)), _const_spec((1, LRU_WIDTH)),
    ]


def _lru_scan(u_t, h0, conv_w, conv_b, wrg, ba, bx, lam, *, reverse, tt=32):
    length = u_t.shape[0]
    nsteps = length // tt
    tile = lambda i: (nsteps - 1 - i) if reverse else i
    blk = _nbytes((tt, BATCH, LRU_WIDTH), F32)
    return pl.pallas_call(
        functools.partial(_lru_scan_kernel, tt=tt, reverse=reverse),
        grid=(nsteps,),
        in_specs=_lru_scan_in_specs(length, tt, tile, wrg.shape),
        out_specs=[
            pl.BlockSpec((tt, BATCH, LRU_WIDTH), lambda i: (tile(i), 0, 0)),
            pl.BlockSpec((BATCH, LRU_WIDTH), lambda i: (0, 0)),
        ],
        out_shape=[
            jax.ShapeDtypeStruct((length, BATCH, LRU_WIDTH), F32),
            jax.ShapeDtypeStruct((BATCH, LRU_WIDTH), F32),
        ],
        scratch_shapes=[
            pltpu.VMEM((BATCH, LRU_WIDTH), F32),
            pltpu.VMEM((tt, BATCH, LRU_WIDTH), F32), pltpu.VMEM((tt, BATCH, LRU_WIDTH), F32),
        ],
        compiler_params=pltpu.CompilerParams(
            dimension_semantics=("arbitrary",),
            vmem_limit_bytes=_vmem_limit([blk, blk], [2 * blk, _nbytes(wrg.shape, BF16)], 6 * blk)),
        name="lru_scan_bwd" if reverse else "lru_scan_fwd",
    )(u_t, u_t, u_t, h0, conv_w, conv_b, wrg, ba, bx, lam)


def _lru_bwd_out_kernel(uc_ref, up_ref, un_ref, h0_ref, cw_ref, cb_ref, wrg_ref, ba_ref, bx_ref,
                        lam_ref, sf_ref, gate_ref, h_ref, g2_ref, w_ref, lng_ref, lnb_ref, o_ref,
                        h_scr, a_scr, b_scr, y_scr, *, tt):
    i = pl.program_id(0)
    nsteps = pl.num_programs(0)

    @pl.when(i == 0)
    def _():
        h_scr[...] = h0_ref[...]

    _lru_coeffs(uc_ref, up_ref, un_ref, cw_ref, cb_ref, wrg_ref, ba_ref, bx_ref, lam_ref,
                a_scr, b_scr, nsteps - 1 - i, nsteps, tt)

    def step(k, h):
        idx = tt - 1 - k
        h = a_scr[idx] * h + b_scr[idx]
        y_scr[idx] = h + sf_ref[idx]
        return h

    h_scr[...] = lax.fori_loop(0, tt, step, h_scr[...], unroll=4)
    z = (_gelu_tanh(gate_ref[...].astype(F32)) * y_scr[...]).astype(BF16)
    perm = _row_perm(to_time_major=False)
    parts = []
    for tau in range(tt // PERM_STEPS):
        sub = z[tau * PERM_STEPS:(tau + 1) * PERM_STEPS].reshape(PERM_STEPS * BATCH, LRU_WIDTH)
        zb = jnp.dot(perm, sub, preferred_element_type=F32).astype(BF16)
        parts.append(zb.reshape(BATCH, PERM_STEPS, LRU_WIDTH))
    zb = jnp.concatenate(parts, axis=1).reshape(BATCH * tt, LRU_WIDTH)
    out = jnp.dot(zb, w_ref[...], preferred_element_type=F32).reshape(BATCH, tt, D_MODEL)
    res = DEEPNORM_ALPHA * h_ref[...] + g2_ref[...] * out
    o_ref[...] = _layer_norm(res, lng_ref[...].reshape(1, 1, D_MODEL), lnb_ref[...].reshape(1, 1, D_MODEL))


def _lru_bwd_out(u_t, h0, conv_w, conv_b, wrg, ba, bx, lam, s_f, gate_t, h3, g2, w_out, ln_g, ln_b,
                 *, tt=32):
    nsteps = SEQ // tt
    tile = lambda i: nsteps - 1 - i
    tm_blk = lambda: pl.BlockSpec((tt, BATCH, LRU_WIDTH), lambda i: (tile(i), 0, 0))
    bm_blk = lambda: pl.BlockSpec((BATCH, tt, D_MODEL), lambda i: (0, tile(i), 0))
    blk = _nbytes((tt, BATCH, LRU_WIDTH), F32)
    return pl.pallas_call(
        functools.partial(_lru_bwd_out_kernel, tt=tt),
        grid=(nsteps,),
        in_specs=_lru_scan_in_specs(SEQ, tt, tile, wrg.shape) + [
            tm_blk(), tm_blk(), bm_blk(),
            _const_spec((BATCH, 1, D_MODEL)), _const_spec(w_out.shape),
            _const_spec((1, D_MODEL)), _const_spec((1, D_MODEL))],
        out_specs=bm_blk(),
        out_shape=jax.ShapeDtypeStruct((BATCH, SEQ, D_MODEL), F32),
        scratch_shapes=[pltpu.VMEM((BATCH, LRU_WIDTH), F32)] + [
            pltpu.VMEM((tt, BATCH, LRU_WIDTH), F32)] * 3,
        compiler_params=pltpu.CompilerParams(
            dimension_semantics=("arbitrary",),
            vmem_limit_bytes=_vmem_limit(
                [blk] * 5, [3 * blk, _nbytes(w_out.shape, BF16), _nbytes(wrg.shape, BF16)], 8 * blk)),
        name="lru_bwd_out",
    )(u_t, u_t, u_t, h0, conv_w, conv_b, wrg, ba, bx, lam, s_f, gate_t, h3, g2, w_out, ln_g, ln_b)


def _rope_tables():
    rows = SEQ // GRID_W
    row = jnp.repeat(jnp.arange(rows, dtype=F32), GRID_W)
    col = jnp.tile(jnp.arange(GRID_W, dtype=F32), rows)
    inv = ROPE_THETA ** (-jnp.arange(ROPE_FREQS, dtype=F32) / ROPE_FREQS)
    ang = jnp.concatenate([row[:, None] * inv, col[:, None] * inv], axis=-1)
    cos, sin = jnp.cos(ang), jnp.sin(ang)
    cos_t = jnp.tile(cos, (1, V7X_LANES // (HEAD_DIM // 2)))
    sin_t = jnp.tile(jnp.concatenate([-sin, sin], axis=-1), (1, V7X_LANES // HEAD_DIM))
    return cos_t, sin_t


def kernel(x, c, ctx, c_ctx, w_mod, b_mod, ln_g, ln_b, ffn_w_gate, ffn_w_up, ffn_w_down, mix_ab_w_in, attn_sink, pool_w, pool_scale, mix_ab_w_out, lru_w_in, lru_conv_w, lru_conv_b, lru_wa, lru_ba, lru_wx, lru_bx, lru_lambda, lru_w_out):
    assert x.shape == (BATCH, SEQ, D_MODEL) and ctx.shape == (BATCH, CTX_LEN, D_MODEL)
    c_all = jnp.concatenate(
        [c, c_ctx[None, :], jnp.zeros((MOD_ROWS - BATCH - 1, D_MODEL), F32)], axis=0)
    mod_all = _modulation(c_all, w_mod, b_mod)
    mod_lat = mod_all[:, :BATCH].reshape(DEPTH, BATCH, N_MOD, D_MODEL)
    mod_ctx = mod_all[:, BATCH:BATCH + 1].reshape(DEPTH, 1, N_MOD, D_MODEL)

    wg = ffn_w_gate.astype(BF16)
    wu = ffn_w_up.astype(BF16)
    wd = ffn_w_down.astype(BF16)
    row = lambda v: v.reshape(1, -1)
    n_ctx = BATCH * CTX_LEN

    h = x.reshape(BATCH * SEQ, D_MODEL)
    hc = ctx.reshape(n_ctx, D_MODEL)

    l = 0
    ml, mc = mod_lat[l], mod_ctx[l]
    ffn1 = ((l, 0), wg, wu, wd, row(ln_g[l, 0]), row(ln_b[l, 0]))
    ffn2 = ((l, 1), wg, wu, wd, row(ln_g[l, 2]), row(ln_b[l, 2]))
    h = _ffn(h, ml, 0, *ffn1, nb=BATCH, length=SEQ)
    hc = _ffn(hc, mc, 0, *ffn1, nb=1, length=n_ctx)
    w_in = mix_ab_w_in[0].astype(BF16)
    cos_t, sin_t = _rope_tables()
    q, k2, v2, u = _attn_inproj(h, ml, w_in, cos_t, sin_t, nb=BATCH, length=SEQ)
    q_c, k2_c, v2_c, u_c = _attn_inproj(hc, mc, w_in, None, None, nb=1, length=n_ctx)
    mix_args = (pool_w[0].astype(BF16), row(pool_scale[0]), mix_ab_w_out[0].astype(BF16),
                row(ln_g[l, 1]), row(ln_b[l, 1]))
    h = _attn_lat(attn_sink[0], q, k2, v2, k2_c, v2_c, u, h, ml, *mix_args)
    hc = _attn_ctx(attn_sink[0], q_c, k2_c, v2_c, u_c, hc, mc, *mix_args)
    h = _ffn(h, ml, 6, *ffn2, nb=BATCH, length=SEQ)
    hc = _ffn(hc, mc, 6, *ffn2, nb=1, length=n_ctx)

    l = 1
    ml, mc = mod_lat[l], mod_ctx[l]
    ffn1 = ((l, 0), wg, wu, wd, row(ln_g[l, 0]), row(ln_b[l, 0]))
    ffn2 = ((l, 1), wg, wu, wd, row(ln_g[l, 2]), row(ln_b[l, 2]))
    h = _ffn(h, ml, 0, *ffn1, nb=BATCH, length=SEQ)
    hc = _ffn(hc, mc, 0, *ffn1, nb=1, length=n_ctx)
    w_in = lru_w_in[0].astype(BF16)
    h3 = h.reshape(BATCH, SEQ, D_MODEL)
    gate_t, u_t = _lru_inproj(h3, ml[:, 4:5, :], ml[:, 3:4, :], w_in, with_gate=True)
    _, uc_t = _lru_inproj(hc.reshape(BATCH, CTX_LEN, D_MODEL), mc[:, 4:5, :], mc[:, 3:4, :],
                          w_in[:, LRU_WIDTH:], with_gate=False)
    wrg = jnp.concatenate([lru_wa[0], lru_wx[0]], axis=-1).astype(BF16)
    conv_w, conv_b = lru_conv_w[0], row(lru_conv_b[0])
    zeros = jnp.zeros((BATCH, LRU_WIDTH), F32)
    dir_args = [(conv_w, conv_b, wrg[d], row(lru_ba[0, d]), row(lru_bx[0, d]), row(lru_lambda[0, d]))
                for d in range(2)]
    _, hf_ctx = _lru_scan(uc_t, zeros, *dir_args[0], reverse=False)
    _, hb_ctx = _lru_scan(uc_t, zeros, *dir_args[1], reverse=True)
    s_f, _ = _lru_scan(u_t, hf_ctx, *dir_args[0], reverse=False)
    h3 = _lru_bwd_out(u_t, hb_ctx, *dir_args[1], s_f, gate_t, h3, ml[:, 5:6, :],
                      lru_w_out[0].astype(BF16), row(ln_g[l, 1]), row(ln_b[l, 1]))
    h = _ffn(h3.reshape(BATCH * SEQ, D_MODEL), ml, 6, *ffn2, nb=BATCH, length=SEQ)
    return h.reshape(BATCH, SEQ, D_MODEL)
```

```python
import functools

import jax
import jax.numpy as jnp
from jax import lax
from jax.experimental import pallas as pl
from jax.experimental.pallas import tpu as pltpu

D_MODEL = 1024
BATCH = 16
SEQ = 2048
DEPTH = 2
GRID_W = 64
CTX_LEN = 256
HEAD_DIM = 64
ATT_HEADS = 8
ATT_KV_HEADS = 2
ATT_GROUPS = ATT_HEADS // ATT_KV_HEADS
ATT_WIDTH = ATT_HEADS * HEAD_DIM
KV_WIDTH = ATT_KV_HEADS * HEAD_DIM
WINDOW = 128
BLOCK = 128
ATT_SCALE = HEAD_DIM ** -0.5
LOG2E = 1.4426950408889634
ROPE_THETA = 10000.0
ROPE_FREQS = HEAD_DIM // 4
POOL_WINDOWS = (2, 4, 8, 16)
POOL_WIDTH = D_MODEL // 2
POOL_GROUP_W = POOL_WIDTH // len(POOL_WINDOWS)
MIX_AB_IN = ATT_WIDTH + 2 * KV_WIDTH + POOL_WIDTH
LRU_WIDTH = D_MODEL
LRU_BLOCKS = 8
LRU_BLOCK_W = LRU_WIDTH // LRU_BLOCKS
LRU_C = 8.0
CONV_W = 4
CONV_LEFT = (CONV_W - 1) // 2
D_FF = 2816
N_MOD = 9
LN_EPS = 1e-5
NEG_INF = -1e30
DEEPNORM_ALPHA = (2 * DEPTH) ** 0.25

V7X_LANES = 128
V7X_SUBLANES = 8
V7X_VMEM_BYTES = 64 * 1024 * 1024
V7X_VMEM_USABLE_BYTES = 60000 * 1024

F32 = jnp.float32
BF16 = jnp.bfloat16

POOL_HALO = V7X_SUBLANES
PERM_STEPS = 16
KEY_SLAB_W = ATT_KV_HEADS * V7X_LANES
VAL_SLAB_W = 2 * ATT_KV_HEADS * V7X_LANES
SCORE_LOOKAHEAD = 2
EPILOGUE_BLOCKS = 2
MOD_ROWS = 24


def _nbytes(shape, dtype):
    n = 1
    for s in shape:
        n *= s
    return n * jnp.dtype(dtype).itemsize


def _vmem_limit(pipelined, resident, temporaries):
    est = 2 * sum(pipelined) + sum(resident) + temporaries
    return int(min(V7X_VMEM_USABLE_BYTES, max(est * 5 // 4, 16 * 1024 * 1024)))


def _const_spec(shape):
    nd = len(shape)
    return pl.BlockSpec(shape, lambda *_: (0,) * nd, pipeline_mode=pl.Buffered(1))


def _tok_spec(tm, width, nt):
    return pl.BlockSpec((tm, width), lambda b, t: (b * nt + t, 0))


def _mod_spec():
    return pl.BlockSpec((1, N_MOD, D_MODEL), lambda b, t: (b, 0, 0))


def _layer_norm(z, g, b):
    mu = jnp.mean(z, axis=-1, keepdims=True)
    zc = z - mu
    var = jnp.mean(zc * zc, axis=-1, keepdims=True)
    return zc * lax.rsqrt(var + LN_EPS) * g + b


def _gelu_tanh(x):
    return 0.5 * x * (1.0 + jnp.tanh(0.7978845608028654 * (x + 0.044715 * (x * x * x))))


def _mod_kernel(c_ref, w_ref, b_ref, o_ref):
    c = c_ref[...]
    a = c * jax.nn.sigmoid(c)
    o_ref[0] = jnp.dot(a, w_ref[0], preferred_element_type=F32,
                       precision=lax.Precision.HIGHEST) + b_ref[0]


def _modulation(c_all, w_mod, b_mod):
    tn = 1024
    n_out = N_MOD * D_MODEL
    blocks = [_nbytes((1, D_MODEL, tn), F32), _nbytes((1, MOD_ROWS, tn), F32)]
    return pl.pallas_call(
        _mod_kernel,
        grid=(DEPTH, n_out // tn),
        in_specs=[
            pl.BlockSpec((MOD_ROWS, D_MODEL), lambda l, j: (0, 0)),
            pl.BlockSpec((1, D_MODEL, tn), lambda l, j: (l, 0, j)),
            pl.BlockSpec((1, 1, tn), lambda l, j: (l, 0, j)),
        ],
        out_specs=pl.BlockSpec((1, MOD_ROWS, tn), lambda l, j: (l, 0, j)),
        out_shape=jax.ShapeDtypeStruct((DEPTH, MOD_ROWS, n_out), F32),
        compiler_params=pltpu.CompilerParams(
            dimension_semantics=("arbitrary", "arbitrary"),
            vmem_limit_bytes=_vmem_limit(blocks, [], 4 * blocks[0])),
        name="modulation",
    )(c_all, w_mod, b_mod.reshape(DEPTH, 1, n_out))


def _ffn_kernel(x_ref, mod_ref, wg_ref, wu_ref, wd_ref, lng_ref, lnb_ref, o_ref, *, j0, sub):
    shift = mod_ref[0, j0:j0 + 1, :]
    scale = mod_ref[0, j0 + 1:j0 + 2, :]
    gate = mod_ref[0, j0 + 2:j0 + 3, :]
    nsub = x_ref.shape[0] // sub

    def gate_up(s):
        x = x_ref[s * sub:(s + 1) * sub, :]
        xin = (x * (1.0 + scale) + shift).astype(BF16)
        return (jnp.dot(xin, wg_ref[...], preferred_element_type=F32),
                jnp.dot(xin, wu_ref[...], preferred_element_type=F32))

    pending = gate_up(0)
    for s in range(nsub):
        g, u = pending
        if s + 1 < nsub:
            pending = gate_up(s + 1)
        rows = slice(s * sub, (s + 1) * sub)
        a = (g * jax.nn.sigmoid(g) * u).astype(BF16)
        y = jnp.dot(a, wd_ref[...], preferred_element_type=F32)
        z = DEEPNORM_ALPHA * x_ref[rows, :] + (0.5 * gate) * y
        o_ref[rows, :] = _layer_norm(z, lng_ref[...], lnb_ref[...])


def _ffn(x, mod, j0, which, wg, wu, wd, ln_g, ln_b, *, nb, length, tm=1024, sub=256):
    nt = length // tm
    tok = _nbytes((tm, D_MODEL), F32)
    weights = [_nbytes(w.shape[2:], BF16) for w in (wg, wu, wd)]
    temps = 3 * _nbytes((tm, D_FF), F32) + 4 * tok
    wspec = lambda w: pl.BlockSpec((None, None) + w.shape[2:], lambda b, t: which + (0, 0),
                                   pipeline_mode=pl.Buffered(1))
    return pl.pallas_call(
        functools.partial(_ffn_kernel, j0=j0, sub=sub),
        grid=(nb, nt),
        in_specs=[
            _tok_spec(tm, D_MODEL, nt),
            _mod_spec(),
            wspec(wg), wspec(wu), wspec(wd),
            _const_spec((1, D_MODEL)), _const_spec((1, D_MODEL)),
        ],
        out_specs=_tok_spec(tm, D_MODEL, nt),
        out_shape=jax.ShapeDtypeStruct((nb * length, D_MODEL), F32),
        compiler_params=pltpu.CompilerParams(
            dimension_semantics=("arbitrary", "arbitrary"),
            vmem_limit_bytes=_vmem_limit([tok, tok], weights, temps)),
        name="ffn",
    )(x, mod, wg, wu, wd, ln_g, ln_b)


def _dup_halves(z, lane):
    zr = pltpu.roll(z, HEAD_DIM, 1)
    lo = lane < HEAD_DIM
    return jnp.where(lo, z, zr), jnp.where(lo, zr, z)


def _attn_inproj_kernel(*refs, rope):
    if rope:
        x_ref, mod_ref, w_ref, cos_ref, sin_ref, q_ref, k_ref, v_ref, u_ref = refs
    else:
        x_ref, mod_ref, w_ref, q_ref, k_ref, v_ref, u_ref = refs
    x = x_ref[...]
    xin = (x * (1.0 + mod_ref[0, 4:5, :]) + mod_ref[0, 3:4, :]).astype(BF16)
    p = jnp.dot(xin, w_ref[...], preferred_element_type=F32)
    lane = lax.broadcasted_iota(jnp.int32, (x.shape[0], V7X_LANES), 1)
    first_half = (lane & (HEAD_DIM - 1)) < HEAD_DIM // 2

    def rot(z):
        if not rope:
            return z
        zr = jnp.where(first_half, pltpu.roll(z, V7X_LANES - HEAD_DIM // 2, 1),
                       pltpu.roll(z, HEAD_DIM // 2, 1))
        return z * cos_ref[...] + zr * sin_ref[...]

    for c in range(ATT_WIDTH // V7X_LANES):
        sl = slice(c * V7X_LANES, (c + 1) * V7X_LANES)
        q_ref[:, sl] = (rot(p[:, sl]) * (ATT_SCALE * LOG2E)).astype(BF16)
    k0, k1 = _dup_halves(rot(p[:, ATT_WIDTH:ATT_WIDTH + KV_WIDTH]), lane)
    k_ref[...] = jnp.concatenate([k0, k1], axis=-1).T.astype(BF16)
    v = p[:, ATT_WIDTH + KV_WIDTH:ATT_WIDTH + 2 * KV_WIDTH]
    vr = pltpu.roll(v, HEAD_DIM, 1)
    lo = lane < HEAD_DIM
    slabs = (jnp.where(lo, v, 1.0), jnp.where(lo, 1.0, vr), jnp.where(lo, vr, 1.0), jnp.where(lo, 1.0, v))
    for i, slab in enumerate(slabs):
        v_ref[:, i * V7X_LANES:(i + 1) * V7X_LANES] = slab.astype(BF16)
    u_ref[...] = p[:, ATT_WIDTH + 2 * KV_WIDTH:]


def _attn_inproj(x, mod, w_in, cos, sin, *, nb, length, tm=512):
    nt = length // tm
    rope = cos is not None
    tok = _nbytes((tm, D_MODEL), F32)
    outs = [_nbytes((tm, ATT_WIDTH), BF16), _nbytes((tm, KEY_SLAB_W + VAL_SLAB_W), BF16),
            _nbytes((tm, POOL_WIDTH), F32)]
    in_specs = [_tok_spec(tm, D_MODEL, nt), _mod_spec(), _const_spec(w_in.shape)]
    args = [x, mod, w_in]
    if rope:
        in_specs += [pl.BlockSpec((tm, V7X_LANES), lambda b, t: (t, 0))] * 2
        args += [cos, sin]
    rows = nb * length
    return pl.pallas_call(
        functools.partial(_attn_inproj_kernel, rope=rope),
        grid=(nb, nt),
        in_specs=in_specs,
        out_specs=[
            _tok_spec(tm, ATT_WIDTH, nt),
            pl.BlockSpec((KEY_SLAB_W, tm), lambda b, t: (0, b * nt + t)),
            _tok_spec(tm, VAL_SLAB_W, nt),
            _tok_spec(tm, POOL_WIDTH, nt),
        ],
        out_shape=[
            jax.ShapeDtypeStruct((rows, ATT_WIDTH), BF16),
            jax.ShapeDtypeStruct((KEY_SLAB_W, rows), BF16),
            jax.ShapeDtypeStruct((rows, VAL_SLAB_W), BF16),
            jax.ShapeDtypeStruct((rows, POOL_WIDTH), F32),
        ],
        compiler_params=pltpu.CompilerParams(
            dimension_semantics=("arbitrary", "arbitrary"),
            vmem_limit_bytes=_vmem_limit([tok] + outs, [_nbytes(w_in.shape, BF16)],
                                         3 * _nbytes((tm, MIX_AB_IN), F32))),
        name="attn_inproj",
    )(*args)


def _attend_scores(q_blk, keys, kh):
    lane = lax.broadcasted_iota(jnp.int32, (BLOCK, V7X_LANES), 1)
    lo = lane < HEAD_DIM
    zero = jnp.zeros((BLOCK, V7X_LANES), BF16)
    parts = []
    for g in range(ATT_GROUPS):
        c = (kh * ATT_GROUPS + g) // 2
        qc = q_blk[:, c * V7X_LANES:(c + 1) * V7X_LANES]
        parts.append(jnp.where(lo if g % 2 == 0 else jnp.logical_not(lo), qc, zero))
    qs = jnp.concatenate(parts, axis=0)
    return [jnp.dot(qs, k, preferred_element_type=F32) for k in keys]


def _attend_finish(scores, values_even, values_odd, segments, sink_ref, kh):
    lane = lax.broadcasted_iota(jnp.int32, (BLOCK, V7X_LANES), 1)
    lo = lane < HEAD_DIM
    probs = ([[] for _ in scores], [[] for _ in scores])
    sink_terms = []
    for g in range(ATT_GROUPS):
        rows = slice(g * BLOCK, (g + 1) * BLOCK)
        segs = [[s[rows, a:b] if bias is None else s[rows, a:b] + bias for a, b, bias in seg]
                for s, seg in zip(scores, segments)]
        sink = sink_ref[kh * ATT_GROUPS + g] * LOG2E
        chunks = [x[:, c:c + V7X_LANES] for sl in segs for x in sl
                  for c in range(0, x.shape[1], V7X_LANES)]
        m = functools.reduce(jnp.maximum, chunks).max(axis=-1, keepdims=True)
        m = jnp.maximum(m, sink)
        sink_terms.append(jnp.exp2(sink - m))
        for i, sl in enumerate(segs):
            e = [jnp.exp2(x - m).astype(BF16) for x in sl]
            probs[g % 2][i].append(e[0] if len(e) == 1 else jnp.concatenate(e, axis=-1))

    def weighted_values(ps, vals):
        o = None
        for p, v in zip(ps, vals):
            t = jnp.dot(jnp.concatenate(p, axis=0), v, preferred_element_type=F32)
            o = t if o is None else o + t
        return o

    o_par = (weighted_values(probs[0], values_even), weighted_values(probs[1], values_odd))
    outs = []
    for g in range(ATT_GROUPS):
        o = o_par[g % 2][(g // 2) * BLOCK:(g // 2 + 1) * BLOCK]
        rinv = 1.0 / (o + sink_terms[g])
        outs.append(o * pltpu.roll(rinv, HEAD_DIM, 1))
    return jnp.where(lo, outs[0], outs[1]), jnp.where(lo, outs[2], outs[3])


def _pool_sums(ubuf_ref, nblk, band_ref):
    u_hi, u_lo = _split_bf16(ubuf_ref[...])
    sums = {}
    for j in range(nblk):
        win = slice(j * BLOCK, (j + 2) * BLOCK)
        for gi in range(len(POOL_WINDOWS)):
            cols = slice(gi * POOL_GROUP_W, (gi + 1) * POOL_GROUP_W)
            band = band_ref[gi]
            sums[j, gi] = (jnp.dot(band, u_hi[win, cols], preferred_element_type=F32)
                           + jnp.dot(band, u_lo[win, cols], preferred_element_type=F32))
    return sums


def _pool_finish(sums, ubuf_ref, nblk, pos0, length, wpool_ref, pscale_ref, mix_scr):
    for j in range(nblk):
        pos = pos0 + j * BLOCK + lax.broadcasted_iota(jnp.int32, (BLOCK, 1), 0)
        rows = slice(POOL_HALO + j * BLOCK, POOL_HALO + (j + 1) * BLOCK)
        for gi, w in enumerate(POOL_WINDOWS):
            r = w // 2
            cols = slice(gi * POOL_GROUP_W, (gi + 1) * POOL_GROUP_W)
            cnt = (jnp.minimum(pos + r, length - 1) - jnp.maximum(pos - r, 0) + 1).astype(F32)
            d = sums[j, gi] / cnt - ubuf_ref[rows, cols]
            y = jnp.dot(d.astype(BF16), wpool_ref[gi], preferred_element_type=F32)
            mix_scr[j * BLOCK:(j + 1) * BLOCK, ATT_WIDTH + gi * POOL_GROUP_W:
                    ATT_WIDTH + (gi + 1) * POOL_GROUP_W] = (y * pscale_ref[:, cols]).astype(BF16)


def _split_bf16(x):
    hi = x.astype(BF16)
    return hi, (x - hi.astype(F32)).astype(BF16)


def _pool_bands():
    i = jnp.arange(BLOCK)[:, None]
    c = jnp.arange(2 * BLOCK)[None, :]
    return jnp.stack([(jnp.abs(c - POOL_HALO - i) <= w // 2) for w in POOL_WINDOWS]).astype(BF16)


def _attn_out(mix_scr, rows, wout_ref, h_ref, gate, lng_ref, lnb_ref, o_ref):
    y = jnp.dot(mix_scr[rows, :], wout_ref[...], preferred_element_type=F32)
    z = DEEPNORM_ALPHA * h_ref[rows, :] + gate * y
    o_ref[rows, :] = _layer_norm(z, lng_ref[...], lnb_ref[...])


def _attn_lat_kernel(sink_ref, q_ref, kc_ref, kp_ref, kn_ref, vc_ref, vp_ref, vn_ref,
                     kctx_ref, vctx_ref, uc_ref, up_ref, un_ref, h_ref, mod_ref, bias_ref, band_ref,
                     wpool_ref, pscale_ref, wout_ref, lng_ref, lnb_ref, o_ref,
                     kbuf, vbuf, ubuf, mix_scr, *, tq, length):
    t = pl.program_id(1)
    last = pl.num_programs(1) - 1
    nblk = tq // BLOCK
    kbuf[:, 0:BLOCK] = kp_ref[...]
    kbuf[:, BLOCK:BLOCK + tq] = kc_ref[...]
    kbuf[:, BLOCK + tq:] = kn_ref[...]
    vbuf[0:BLOCK] = vp_ref[...]
    vbuf[BLOCK:BLOCK + tq] = vc_ref[...]
    vbuf[BLOCK + tq:] = vn_ref[...]
    ubuf[0:POOL_HALO] = jnp.where(t == 0, 0.0, up_ref[...])
    ubuf[POOL_HALO:POOL_HALO + tq] = uc_ref[...]
    ubuf[POOL_HALO + tq:2 * POOL_HALO + tq] = jnp.where(t == last, 0.0, un_ref[...])
    ubuf[2 * POOL_HALO + tq:] = jnp.zeros((2 * BLOCK - 2 * POOL_HALO, POOL_WIDTH), F32)
    lanes = lambda i: slice(i * V7X_LANES, (i + 1) * V7X_LANES)
    gate = mod_ref[0, 5:6, :]
    units = [(j, kh) for j in range(nblk) for kh in range(ATT_KV_HEADS)]

    def scores_of(j, kh):
        win = slice(j * BLOCK, j * BLOCK + 3 * BLOCK)
        return _attend_scores(q_ref[j * BLOCK:(j + 1) * BLOCK, :],
                              [kbuf[lanes(kh), win], kctx_ref[lanes(kh), :]], kh)

    sums = _pool_sums(ubuf, nblk, band_ref)
    pending = [scores_of(*u) for u in units[:SCORE_LOOKAHEAD]]
    _pool_finish(sums, ubuf, nblk, t * tq, length, wpool_ref, pscale_ref, mix_scr)
    for n, (j, kh) in enumerate(units):
        scores = pending.pop(0)
        if n + SCORE_LOOKAHEAD < len(units):
            pending.append(scores_of(*units[n + SCORE_LOOKAHEAD]))
        blk = t * nblk + j
        bidx = jnp.where(blk == 0, 0, jnp.where(blk == length // BLOCK - 1, 2, 1))
        bias = bias_ref[bidx]
        segments = [[(0, BLOCK, bias[:, :BLOCK]), (BLOCK, 2 * BLOCK, None),
                     (2 * BLOCK, 3 * BLOCK, bias[:, BLOCK:])], [(0, CTX_LEN, None)]]
        win = slice(j * BLOCK, j * BLOCK + 3 * BLOCK)
        rows = slice(j * BLOCK, (j + 1) * BLOCK)
        c0, c1 = _attend_finish(
            scores, [vbuf[win, lanes(2 * kh)], vctx_ref[:, lanes(2 * kh)]],
            [vbuf[win, lanes(2 * kh + 1)], vctx_ref[:, lanes(2 * kh + 1)]],
            segments, sink_ref, kh)
        mix_scr[rows, lanes(2 * kh)] = c0.astype(BF16)
        mix_scr[rows, lanes(2 * kh + 1)] = c1.astype(BF16)
        per_epilogue = EPILOGUE_BLOCKS * ATT_KV_HEADS
        if n > 0 and n % per_epilogue == 0:
            e = n // per_epilogue - 1
            done = slice(e * EPILOGUE_BLOCKS * BLOCK, (e + 1) * EPILOGUE_BLOCKS * BLOCK)
            _attn_out(mix_scr, done, wout_ref, h_ref, gate, lng_ref, lnb_ref, o_ref)
    _attn_out(mix_scr, slice(tq - EPILOGUE_BLOCKS * BLOCK, tq), wout_ref, h_ref, gate, lng_ref,
              lnb_ref, o_ref)


def _attn_ctx_kernel(sink_ref, q_ref, kctx_ref, vctx_ref, uc_ref, h_ref, mod_ref, band_ref,
                     wpool_ref, pscale_ref, wout_ref, lng_ref, lnb_ref, o_ref,
                     ubuf, mix_scr, *, tq):
    ubuf[0:POOL_HALO] = jnp.zeros((POOL_HALO, POOL_WIDTH), F32)
    ubuf[POOL_HALO:POOL_HALO + tq] = uc_ref[...]
    ubuf[POOL_HALO + tq:] = jnp.zeros((2 * BLOCK - POOL_HALO, POOL_WIDTH), F32)
    lanes = lambda i: slice(i * V7X_LANES, (i + 1) * V7X_LANES)
    nblk = tq // BLOCK
    units = [(j, kh) for j in range(nblk) for kh in range(ATT_KV_HEADS)]
    scores_of = lambda j, kh: _attend_scores(q_ref[j * BLOCK:(j + 1) * BLOCK, :],
                                             [kctx_ref[lanes(kh), :]], kh)
    sums = _pool_sums(ubuf, nblk, band_ref)
    pending = [scores_of(*u) for u in units[:SCORE_LOOKAHEAD]]
    _pool_finish(sums, ubuf, nblk, 0, tq, wpool_ref, pscale_ref, mix_scr)
    for n, (j, kh) in enumerate(units):
        scores = pending.pop(0)
        if n + SCORE_LOOKAHEAD < len(units):
            pending.append(scores_of(*units[n + SCORE_LOOKAHEAD]))
        rows = slice(j * BLOCK, (j + 1) * BLOCK)
        c0, c1 = _attend_finish(scores, [vctx_ref[:, lanes(2 * kh)]],
                                [vctx_ref[:, lanes(2 * kh + 1)]], [[(0, CTX_LEN, None)]],
                                sink_ref, kh)
        mix_scr[rows, lanes(2 * kh)] = c0.astype(BF16)
        mix_scr[rows, lanes(2 * kh + 1)] = c1.astype(BF16)
    _attn_out(mix_scr, slice(0, tq), wout_ref, h_ref, mod_ref[0, 5:6, :], lng_ref, lnb_ref, o_ref)


def _window_bias():
    qi = jnp.arange(BLOCK)[:, None]
    kc = jnp.arange(3 * BLOCK)[None, :]
    band = jnp.abs(qi + BLOCK - kc) <= WINDOW
    first = band & (kc >= BLOCK)
    final = band & (kc < 2 * BLOCK)
    full = jnp.where(jnp.stack([first, band, final]), 0.0, NEG_INF).astype(F32)
    return jnp.concatenate([full[:, :, :BLOCK], full[:, :, 2 * BLOCK:]], axis=-1)


def _smem_spec():
    return pl.BlockSpec(memory_space=pltpu.SMEM)


def _attn_lat(sink, q, k2, v2, kctx, vctx, u, h, mod, wpool, pscale, wout, ln_g, ln_b, *, tq=512):
    nt = SEQ // tq
    bpt = tq // BLOCK
    bps = SEQ // BLOCK
    hpt = tq // POOL_HALO
    hps = SEQ // POOL_HALO
    cur = lambda w: pl.BlockSpec((tq, w), lambda b, t: (b * nt + t, 0))
    prev_blk = lambda rows, w, per_tile, per_samp: pl.BlockSpec(
        (rows, w), lambda b, t: (b * per_samp + jnp.maximum(t * per_tile - 1, 0), 0))
    next_blk = lambda rows, w, per_tile, per_samp: pl.BlockSpec(
        (rows, w), lambda b, t: (b * per_samp + jnp.minimum((t + 1) * per_tile, per_samp - 1), 0))
    ctx_blk = lambda w: pl.BlockSpec((CTX_LEN, w), lambda b, t: (b, 0))
    kv_specs = lambda w: [cur(w), prev_blk(BLOCK, w, bpt, bps), next_blk(BLOCK, w, bpt, bps)]
    key_specs = [
        pl.BlockSpec((KEY_SLAB_W, tq), lambda b, t: (0, b * nt + t)),
        pl.BlockSpec((KEY_SLAB_W, BLOCK), lambda b, t: (0, b * bps + jnp.maximum(t * bpt - 1, 0))),
        pl.BlockSpec((KEY_SLAB_W, BLOCK),
                     lambda b, t: (0, b * bps + jnp.minimum((t + 1) * bpt, bps - 1))),
    ]
    bias = _window_bias()
    bands = _pool_bands()
    tok = _nbytes((tq, D_MODEL), F32)
    kv_rows = tq + 2 * BLOCK
    u_rows = tq + 2 * BLOCK
    pipelined = [_nbytes((tq, ATT_WIDTH), BF16), 2 * _nbytes((kv_rows, KEY_SLAB_W + VAL_SLAB_W), BF16),
                 _nbytes((tq + 2 * POOL_HALO, POOL_WIDTH), F32), 2 * tok]
    resident = [_nbytes(bias.shape, F32), _nbytes(bands.shape, BF16), _nbytes(wpool.shape, BF16),
                _nbytes(wout.shape, BF16)]
    scratch = [_nbytes((kv_rows, KEY_SLAB_W + VAL_SLAB_W), BF16),
               _nbytes((u_rows, POOL_WIDTH), F32), _nbytes((tq, D_MODEL), BF16)]
    return pl.pallas_call(
        functools.partial(_attn_lat_kernel, tq=tq, length=SEQ),
        grid=(BATCH, nt),
        in_specs=[_smem_spec(), cur(ATT_WIDTH)] + key_specs + kv_specs(VAL_SLAB_W) + [
            pl.BlockSpec((KEY_SLAB_W, CTX_LEN), lambda b, t: (0, b)), ctx_blk(VAL_SLAB_W),
            cur(POOL_WIDTH), prev_blk(POOL_HALO, POOL_WIDTH, hpt, hps),
            next_blk(POOL_HALO, POOL_WIDTH, hpt, hps),
            cur(D_MODEL), _mod_spec(),
            _const_spec(bias.shape), _const_spec(bands.shape),
            _const_spec(wpool.shape), _const_spec((1, POOL_WIDTH)),
            _const_spec(wout.shape), _const_spec((1, D_MODEL)), _const_spec((1, D_MODEL)),
        ],
        out_specs=cur(D_MODEL),
        out_shape=jax.ShapeDtypeStruct((BATCH * SEQ, D_MODEL), F32),
        scratch_shapes=[
            pltpu.VMEM((KEY_SLAB_W, kv_rows), BF16), pltpu.VMEM((kv_rows, VAL_SLAB_W), BF16),
            pltpu.VMEM((u_rows, POOL_WIDTH), F32), pltpu.VMEM((tq, D_MODEL), BF16),
        ],
        compiler_params=pltpu.CompilerParams(
            dimension_semantics=("arbitrary", "arbitrary"),
            vmem_limit_bytes=_vmem_limit(pipelined, resident + scratch, 8 * tok)),
        name="attn_latent",
    )(sink, q, k2, k2, k2, v2, v2, v2, kctx, vctx, u, u, u, h, mod, bias, bands, wpool, pscale, wout,
      ln_g, ln_b)


def _attn_ctx(sink, q, kctx, vctx, u, h, mod, wpool, pscale, wout, ln_g, ln_b):
    tq = CTX_LEN
    blk = lambda w: pl.BlockSpec((tq, w), lambda b, t: (b, 0))
    bands = _pool_bands()
    tok = _nbytes((tq, D_MODEL), F32)
    u_rows = tq + 2 * BLOCK
    pipelined = [_nbytes((tq, ATT_WIDTH), BF16), _nbytes((tq, KEY_SLAB_W + VAL_SLAB_W), BF16),
                 _nbytes((tq, POOL_WIDTH), F32), 2 * tok]
    resident = [_nbytes(bands.shape, BF16), _nbytes(wpool.shape, BF16), _nbytes(wout.shape, BF16)]
    scratch = [_nbytes((u_rows, POOL_WIDTH), F32), _nbytes((tq, D_MODEL), BF16)]
    return pl.pallas_call(
        functools.partial(_attn_ctx_kernel, tq=tq),
        grid=(BATCH, 1),
        in_specs=[
            _smem_spec(),
            blk(ATT_WIDTH), pl.BlockSpec((KEY_SLAB_W, tq), lambda b, t: (0, b)),
            blk(VAL_SLAB_W), blk(POOL_WIDTH), blk(D_MODEL),
            pl.BlockSpec((1, N_MOD, D_MODEL), lambda b, t: (0, 0, 0)),
            _const_spec(bands.shape), _const_spec(wpool.shape), _const_spec((1, POOL_WIDTH)),
            _const_spec(wout.shape), _const_spec((1, D_MODEL)), _const_spec((1, D_MODEL)),
        ],
        out_specs=blk(D_MODEL),
        out_shape=jax.ShapeDtypeStruct((BATCH * CTX_LEN, D_MODEL), F32),
        scratch_shapes=[
            pltpu.VMEM((u_rows, POOL_WIDTH), F32), pltpu.VMEM((tq, D_MODEL), BF16),
        ],
        compiler_params=pltpu.CompilerParams(
            dimension_semantics=("arbitrary", "arbitrary"),
            vmem_limit_bytes=_vmem_limit(pipelined, resident + scratch, 8 * tok)),
        name="attn_context",
    )(sink, q, kctx, vctx, u, h, mod, bands, wpool, pscale, wout, ln_g, ln_b)


def _row_perm(to_time_major):
    n = BATCH * PERM_STEPS
    r = lax.broadcasted_iota(jnp.int32, (n, n), 0)
    c = lax.broadcasted_iota(jnp.int32, (n, n), 1)
    if to_time_major:
        hit = ((r // BATCH) == (c % PERM_STEPS)) & ((r % BATCH) == (c // PERM_STEPS))
    else:
        hit = ((r // PERM_STEPS) == (c % BATCH)) & ((r % PERM_STEPS) == (c // BATCH))
    return jnp.where(hit, 1.0, 0.0).astype(BF16)


def _lru_inproj_kernel(*refs, tmt, with_gate):
    if with_gate:
        x_ref, scale_ref, shift_ref, w_ref, gate_ref, u_ref = refs
    else:
        x_ref, scale_ref, shift_ref, w_ref, u_ref = refs
    xin = (x_ref[...] * (1.0 + scale_ref[...]) + shift_ref[...]).astype(BF16)
    perm = _row_perm(to_time_major=True)
    parts = []
    for tau in range(tmt // PERM_STEPS):
        sub = xin[:, tau * PERM_STEPS:(tau + 1) * PERM_STEPS, :].reshape(BATCH * PERM_STEPS, D_MODEL)
        parts.append(jnp.dot(perm, sub, preferred_element_type=F32).astype(BF16))
    xt = jnp.concatenate(parts, axis=0)
    p = jnp.dot(xt, w_ref[...], preferred_element_type=F32)
    if with_gate:
        gate_ref[...] = p[:, :LRU_WIDTH].astype(BF16).reshape(tmt, BATCH, LRU_WIDTH)
        u_ref[...] = p[:, LRU_WIDTH:].reshape(tmt, BATCH, LRU_WIDTH)
    else:
        u_ref[...] = p.reshape(tmt, BATCH, LRU_WIDTH)


def _lru_inproj(x3, scale, shift, w, *, with_gate, tmt=32):
    length = x3.shape[1]
    blk = _nbytes((BATCH, tmt, D_MODEL), F32)
    out_spec = lambda: pl.BlockSpec((tmt, BATCH, LRU_WIDTH), lambda i: (i, 0, 0))
    out_specs = [out_spec()]
    out_shape = [jax.ShapeDtypeStruct((length, BATCH, LRU_WIDTH), F32)]
    outs = [blk]
    if with_gate:
        out_specs = [out_spec()] + out_specs
        out_shape = [jax.ShapeDtypeStruct((length, BATCH, LRU_WIDTH), BF16)] + out_shape
        outs.append(blk // 2)
    res = pl.pallas_call(
        functools.partial(_lru_inproj_kernel, tmt=tmt, with_gate=with_gate),
        grid=(length // tmt,),
        in_specs=[pl.BlockSpec((BATCH, tmt, D_MODEL), lambda i: (0, i, 0)),
                  _const_spec(scale.shape), _const_spec(shift.shape), _const_spec(w.shape)],
        out_specs=out_specs,
        out_shape=out_shape,
        compiler_params=pltpu.CompilerParams(
            dimension_semantics=("arbitrary",),
            vmem_limit_bytes=_vmem_limit([blk] + outs, [_nbytes(w.shape, BF16)], 5 * blk)),
        name="lru_inproj",
    )(x3, scale, shift, w)
    return res if with_gate else (None, res[0])


def _lru_coeffs(uc_ref, up_ref, un_ref, cw_ref, cb_ref, wrg_ref, ba_ref, bx_ref, lam_ref,
                a_scr, b_scr, ti, nsteps, tt):
    prev = jnp.where(ti == 0, 0.0, up_ref[...])
    nxt = jnp.where(ti == nsteps - 1, 0.0, un_ref[...])
    ext = jnp.concatenate([prev, uc_ref[...], nxt], axis=0)
    u = cb_ref[...].reshape(1, 1, LRU_WIDTH)
    for tap in range(CONV_W):
        u = u + ext[tap:tap + tt] * cw_ref[tap:tap + 1, :].reshape(1, 1, LRU_WIDTH)
    u2 = u.reshape(tt * BATCH, LRU_WIDTH)
    lam = lam_ref[...]
    decay2 = (-LRU_C * LOG2E) * (jnp.maximum(-lam, 0.0) + jnp.log1p(jnp.exp(-jnp.abs(lam))))
    for blk in range(LRU_BLOCKS):
        cols = slice(blk * LRU_BLOCK_W, (blk + 1) * LRU_BLOCK_W)
        ub = u2[:, cols]
        rg = jnp.dot(ub.astype(BF16), wrg_ref[blk], preferred_element_type=F32)
        r = jax.nn.sigmoid(rg[:, :LRU_BLOCK_W] + ba_ref[:, cols])
        gi = jax.nn.sigmoid(rg[:, LRU_BLOCK_W:] + bx_ref[:, cols])
        a = jnp.exp2(decay2[:, cols] * r)
        x = 1.0 - a * a
        b = jnp.where(x > 0.0, x * lax.rsqrt(x), 0.0) * (gi * ub)
        a_scr[:, :, cols] = a.reshape(tt, BATCH, LRU_BLOCK_W)
        b_scr[:, :, cols] = b.reshape(tt, BATCH, LRU_BLOCK_W)


def _lru_scan_kernel(uc_ref, up_ref, un_ref, h0_ref, cw_ref, cb_ref, wrg_ref, ba_ref, bx_ref,
                     lam_ref, s_ref, hfin_ref, h_scr, a_scr, b_scr, *, tt, reverse):
    i = pl.program_id(0)
    nsteps = pl.num_programs(0)
    ti = nsteps - 1 - i if reverse else i

    @pl.when(i == 0)
    def _():
        h_scr[...] = h0_ref[...]

    _lru_coeffs(uc_ref, up_ref, un_ref, cw_ref, cb_ref, wrg_ref, ba_ref, bx_ref, lam_ref,
                a_scr, b_scr, ti, nsteps, tt)

    def step(k, h):
        idx = tt - 1 - k if reverse else k
        h = a_scr[idx] * h + b_scr[idx]
        s_ref[idx] = h
        return h

    h = lax.fori_loop(0, tt, step, h_scr[...], unroll=4)
    h_scr[...] = h
    hfin_ref[...] = h


def _lru_scan_in_specs(length, tt, tile, wrg_shape):
    return [
        pl.BlockSpec((tt, BATCH, LRU_WIDTH), lambda i: (tile(i), 0, 0)),
        pl.BlockSpec((1, BATCH, LRU_WIDTH), lambda i: (jnp.maximum(tile(i) * tt - 1, 0), 0, 0)),
        pl.BlockSpec((2, BATCH, LRU_WIDTH),
                     lambda i: (jnp.minimum((tile(i) + 1) * (tt // 2), length // 2 - 1), 0, 0)),
        _const_spec((BATCH, LRU_WIDTH)),
        _const_spec((CONV_W, LRU_WIDTH)), _const_spec((1, LRU_WIDTH)),
        _const_spec(wrg_shape),
        _const_spec((1, LRU_WIDTH)), _const_spec((1, LRU_WIDTH)), _const_spec((1, LRU_WIDTH)),
    ]


def _lru_scan(u_t, h0, conv_w, conv_b, wrg, ba, bx, lam, *, reverse, tt=32):
    length = u_t.shape[0]
    nsteps = length // tt
    tile = lambda i: (nsteps - 1 - i) if reverse else i
    blk = _nbytes((tt, BATCH, LRU_WIDTH), F32)
    return pl.pallas_call(
        functools.partial(_lru_scan_kernel, tt=tt, reverse=reverse),
        grid=(nsteps,),
        in_specs=_lru_scan_in_specs(length, tt, tile, wrg.shape),
        out_specs=[
            pl.BlockSpec((tt, BATCH, LRU_WIDTH), lambda i: (tile(i), 0, 0)),
            pl.BlockSpec((BATCH, LRU_WIDTH), lambda i: (0, 0)),
        ],
        out_shape=[
            jax.ShapeDtypeStruct((length, BATCH, LRU_WIDTH), F32),
            jax.ShapeDtypeStruct((BATCH, LRU_WIDTH), F32),
        ],
        scratch_shapes=[
            pltpu.VMEM((BATCH, LRU_WIDTH), F32),
            pltpu.VMEM((tt, BATCH, LRU_WIDTH), F32), pltpu.VMEM((tt, BATCH, LRU_WIDTH), F32),
        ],
        compiler_params=pltpu.CompilerParams(
            dimension_semantics=("arbitrary",),
            vmem_limit_bytes=_vmem_limit([blk, blk], [2 * blk, _nbytes(wrg.shape, BF16)], 6 * blk)),
        name="lru_scan_bwd" if reverse else "lru_scan_fwd",
    )(u_t, u_t, u_t, h0, conv_w, conv_b, wrg, ba, bx, lam)


def _lru_bwd_out_kernel(uc_ref, up_ref, un_ref, h0_ref, cw_ref, cb_ref, wrg_ref, ba_ref, bx_ref,
                        lam_ref, sf_ref, gate_ref, h_ref, g2_ref, w_ref, lng_ref, lnb_ref, o_ref,
                        h_scr, a_scr, b_scr, y_scr, *, tt):
    i = pl.program_id(0)
    nsteps = pl.num_programs(0)

    @pl.when(i == 0)
    def _():
        h_scr[...] = h0_ref[...]

    _lru_coeffs(uc_ref, up_ref, un_ref, cw_ref, cb_ref, wrg_ref, ba_ref, bx_ref, lam_ref,
                a_scr, b_scr, nsteps - 1 - i, nsteps, tt)

    def step(k, h):
        idx = tt - 1 - k
        h = a_scr[idx] * h + b_scr[idx]
        y_scr[idx] = h + sf_ref[idx]
        return h

    h_scr[...] = lax.fori_loop(0, tt, step, h_scr[...], unroll=4)
    z = (_gelu_tanh(gate_ref[...].astype(F32)) * y_scr[...]).astype(BF16)
    perm = _row_perm(to_time_major=False)
    parts = []
    for tau in range(tt // PERM_STEPS):
        sub = z[tau * PERM_STEPS:(tau + 1) * PERM_STEPS].reshape(PERM_STEPS * BATCH, LRU_WIDTH)
        zb = jnp.dot(perm, sub, preferred_element_type=F32).astype(BF16)
        parts.append(zb.reshape(BATCH, PERM_STEPS, LRU_WIDTH))
    zb = jnp.concatenate(parts, axis=1).reshape(BATCH * tt, LRU_WIDTH)
    out = jnp.dot(zb, w_ref[...], preferred_element_type=F32).reshape(BATCH, tt, D_MODEL)
    res = DEEPNORM_ALPHA * h_ref[...] + g2_ref[...] * out
    o_ref[...] = _layer_norm(res, lng_ref[...].reshape(1, 1, D_MODEL), lnb_ref[...].reshape(1, 1, D_MODEL))


def _lru_bwd_out(u_t, h0, conv_w, conv_b, wrg, ba, bx, lam, s_f, gate_t, h3, g2, w_out, ln_g, ln_b,
                 *, tt=32):
    nsteps = SEQ // tt
    tile = lambda i: nsteps - 1 - i
    tm_blk = lambda: pl.BlockSpec((tt, BATCH, LRU_WIDTH), lambda i: (tile(i), 0, 0))
    bm_blk = lambda: pl.BlockSpec((BATCH, tt, D_MODEL), lambda i: (0, tile(i), 0))
    blk = _nbytes((tt, BATCH, LRU_WIDTH), F32)
    return pl.pallas_call(
        functools.partial(_lru_bwd_out_kernel, tt=tt),
        grid=(nsteps,),
        in_specs=_lru_scan_in_specs(SEQ, tt, tile, wrg.shape) + [
            tm_blk(), tm_blk(), bm_blk(),
            _const_spec((BATCH, 1, D_MODEL)), _const_spec(w_out.shape),
            _const_spec((1, D_MODEL)), _const_spec((1, D_MODEL))],
        out_specs=bm_blk(),
        out_shape=jax.ShapeDtypeStruct((BATCH, SEQ, D_MODEL), F32),
        scratch_shapes=[pltpu.VMEM((BATCH, LRU_WIDTH), F32)] + [
            pltpu.VMEM((tt, BATCH, LRU_WIDTH), F32)] * 3,
        compiler_params=pltpu.CompilerParams(
            dimension_semantics=("arbitrary",),
            vmem_limit_bytes=_vmem_limit(
                [blk] * 5, [3 * blk, _nbytes(w_out.shape, BF16), _nbytes(wrg.shape, BF16)], 8 * blk)),
        name="lru_bwd_out",
    )(u_t, u_t, u_t, h0, conv_w, conv_b, wrg, ba, bx, lam, s_f, gate_t, h3, g2, w_out, ln_g, ln_b)


def _rope_tables():
    rows = SEQ // GRID_W
    row = jnp.repeat(jnp.arange(rows, dtype=F32), GRID_W)
    col = jnp.tile(jnp.arange(GRID_W, dtype=F32), rows)
    inv = ROPE_THETA ** (-jnp.arange(ROPE_FREQS, dtype=F32) / ROPE_FREQS)
    ang = jnp.concatenate([row[:, None] * inv, col[:, None] * inv], axis=-1)
    cos, sin = jnp.cos(ang), jnp.sin(ang)
    cos_t = jnp.tile(cos, (1, V7X_LANES // (HEAD_DIM // 2)))
    sin_t = jnp.tile(jnp.concatenate([-sin, sin], axis=-1), (1, V7X_LANES // HEAD_DIM))
    return cos_t, sin_t


def kernel(x, c, ctx, c_ctx, w_mod, b_mod, ln_g, ln_b, ffn_w_gate, ffn_w_up, ffn_w_down, mix_ab_w_in, attn_sink, pool_w, pool_scale, mix_ab_w_out, lru_w_in, lru_conv_w, lru_conv_b, lru_wa, lru_ba, lru_wx, lru_bx, lru_lambda, lru_w_out):
    assert x.shape == (BATCH, SEQ, D_MODEL) and ctx.shape == (BATCH, CTX_LEN, D_MODEL)
    c_all = jnp.concatenate(
        [c, c_ctx[None, :], jnp.zeros((MOD_ROWS - BATCH - 1, D_MODEL), F32)], axis=0)
    mod_all = _modulation(c_all, w_mod, b_mod)
    mod_lat = mod_all[:, :BATCH].reshape(DEPTH, BATCH, N_MOD, D_MODEL)
    mod_ctx = mod_all[:, BATCH:BATCH + 1].reshape(DEPTH, 1, N_MOD, D_MODEL)

    wg = ffn_w_gate.astype(BF16)
    wu = ffn_w_up.astype(BF16)
    wd = ffn_w_down.astype(BF16)
    row = lambda v: v.reshape(1, -1)
    n_ctx = BATCH * CTX_LEN

    h = x.reshape(BATCH * SEQ, D_MODEL)
    hc = ctx.reshape(n_ctx, D_MODEL)

    l = 0
    ml, mc = mod_lat[l], mod_ctx[l]
    ffn1 = ((l, 0), wg, wu, wd, row(ln_g[l, 0]), row(ln_b[l, 0]))
    ffn2 = ((l, 1), wg, wu, wd, row(ln_g[l, 2]), row(ln_b[l, 2]))
    h = _ffn(h, ml, 0, *ffn1, nb=BATCH, length=SEQ)
    hc = _ffn(hc, mc, 0, *ffn1, nb=1, length=n_ctx)
    w_in = mix_ab_w_in[0].astype(BF16)
    cos_t, sin_t = _rope_tables()
    q, k2, v2, u = _attn_inproj(h, ml, w_in, cos_t, sin_t, nb=BATCH, length=SEQ)
    q_c, k2_c, v2_c, u_c = _attn_inproj(hc, mc, w_in, None, None, nb=1, length=n_ctx)
    mix_args = (pool_w[0].astype(BF16), row(pool_scale[0]), mix_ab_w_out[0].astype(BF16),
                row(ln_g[l, 1]), row(ln_b[l, 1]))
    h = _attn_lat(attn_sink[0], q, k2, v2, k2_c, v2_c, u, h, ml, *mix_args)
    hc = _attn_ctx(attn_sink[0], q_c, k2_c, v2_c, u_c, hc, mc, *mix_args)
    h = _ffn(h, ml, 6, *ffn2, nb=BATCH, length=SEQ)
    hc = _ffn(hc, mc, 6, *ffn2, nb=1, length=n_ctx)

    l = 1
    ml, mc = mod_lat[l], mod_ctx[l]
    ffn1 = ((l, 0), wg, wu, wd, row(ln_g[l, 0]), row(ln_b[l, 0]))
    ffn2 = ((l, 1), wg, wu, wd, row(ln_g[l, 2]), row(ln_b[l, 2]))
    h = _ffn(h, ml, 0, *ffn1, nb=BATCH, length=SEQ)
    hc = _ffn(hc, mc, 0, *ffn1, nb=1, length=n_ctx)
    w_in = lru_w_in[0].astype(BF16)
    h3 = h.reshape(BATCH, SEQ, D_MODEL)
    gate_t, u_t = _lru_inproj(h3, ml[:, 4:5, :], ml[:, 3:4, :], w_in, with_gate=True)
    _, uc_t = _lru_inproj(hc.reshape(BATCH, CTX_LEN, D_MODEL), mc[:, 4:5, :], mc[:, 3:4, :],
                          w_in[:, LRU_WIDTH:], with_gate=False)
    wrg = jnp.concatenate([lru_wa[0], lru_wx[0]], axis=-1).astype(BF16)
    conv_w, conv_b = lru_conv_w[0], row(lru_conv_b[0])
    zeros = jnp.zeros((BATCH, LRU_WIDTH), F32)
    dir_args = [(conv_w, conv_b, wrg[d], row(lru_ba[0, d]), row(lru_bx[0, d]), row(lru_lambda[0, d]))
                for d in range(2)]
    _, hf_ctx = _lru_scan(uc_t, zeros, *dir_args[0], reverse=False)
    _, hb_ctx = _lru_scan(uc_t, zeros, *dir_args[1], reverse=True)
    s_f, _ = _lru_scan(u_t, hf_ctx, *dir_args[0], reverse=False)
    h3 = _lru_bwd_out(u_t, hb_ctx, *dir_args[1], s_f, gate_t, h3, ml[:, 5:6, :],
                      lru_w_out[0].astype(BF16), row(ln_g[l, 1]), row(ln_b[l, 1]))
    h = _ffn(h3.reshape(BATCH * SEQ, D_MODEL), ml, 6, *ffn2, nb=BATCH, length=SEQ)
    return h.reshape(BATCH, SEQ, D_MODEL)
```

```python
import functools

import jax
import jax.numpy as jnp
from jax import lax
from jax.experimental import pallas as pl
from jax.experimental.pallas import tpu as pltpu

D_MODEL = 1024
BATCH = 16
SEQ = 2048
DEPTH = 2
GRID_W = 64
CTX_LEN = 256
HEAD_DIM = 64
ATT_HEADS = 8
ATT_KV_HEADS = 2
ATT_GROUPS = ATT_HEADS // ATT_KV_HEADS
ATT_WIDTH = ATT_HEADS * HEAD_DIM
KV_WIDTH = ATT_KV_HEADS * HEAD_DIM
WINDOW = 128
BLOCK = 128
ATT_SCALE = HEAD_DIM ** -0.5
LOG2E = 1.4426950408889634
ROPE_THETA = 10000.0
ROPE_FREQS = HEAD_DIM // 4
POOL_WINDOWS = (2, 4, 8, 16)
POOL_WIDTH = D_MODEL // 2
POOL_GROUP_W = POOL_WIDTH // len(POOL_WINDOWS)
MIX_AB_IN = ATT_WIDTH + 2 * KV_WIDTH + POOL_WIDTH
LRU_WIDTH = D_MODEL
LRU_BLOCKS = 8
LRU_BLOCK_W = LRU_WIDTH // LRU_BLOCKS
LRU_C = 8.0
CONV_W = 4
CONV_LEFT = (CONV_W - 1) // 2
D_FF = 2816
N_MOD = 9
LN_EPS = 1e-5
NEG_INF = -1e30
DEEPNORM_ALPHA = (2 * DEPTH) ** 0.25

V7X_LANES = 128
V7X_SUBLANES = 8
V7X_VMEM_BYTES = 64 * 1024 * 1024
V7X_VMEM_USABLE_BYTES = 60000 * 1024

F32 = jnp.float32
BF16 = jnp.bfloat16

POOL_HALO = V7X_SUBLANES
PERM_STEPS = 16
KEY_SLAB_W = ATT_KV_HEADS * V7X_LANES
VAL_SLAB_W = 2 * ATT_KV_HEADS * V7X_LANES
LRU_PROJ_CHUNK = 256
LRU_SCAN_LAG = 2
SCORE_LOOKAHEAD = 2
EPILOGUE_BLOCKS = 2
MOD_ROWS = 24


def _nbytes(shape, dtype):
    n = 1
    for s in shape:
        n *= s
    return n * jnp.dtype(dtype).itemsize


def _vmem_limit(pipelined, resident, temporaries):
    est = 2 * sum(pipelined) + sum(resident) + temporaries
    return int(min(V7X_VMEM_USABLE_BYTES, max(est * 5 // 4, 16 * 1024 * 1024)))


def _const_spec(shape):
    nd = len(shape)
    return pl.BlockSpec(shape, lambda *_: (0,) * nd, pipeline_mode=pl.Buffered(1))


def _tok_spec(tm, width, nt):
    return pl.BlockSpec((tm, width), lambda b, t: (b * nt + t, 0))


def _mod_spec():
    return pl.BlockSpec((1, N_MOD, D_MODEL), lambda b, t: (b, 0, 0))


def _layer_norm(z, g, b):
    mu = jnp.mean(z, axis=-1, keepdims=True)
    zc = z - mu
    var = jnp.mean(zc * zc, axis=-1, keepdims=True)
    return zc * lax.rsqrt(var + LN_EPS) * g + b


def _gelu_tanh(x):
    return 0.5 * x * (1.0 + jnp.tanh(0.7978845608028654 * (x + 0.044715 * (x * x * x))))


def _mod_kernel(c_ref, w_ref, b_ref, o_ref):
    c = c_ref[...]
    a = c * jax.nn.sigmoid(c)
    o_ref[0] = jnp.dot(a, w_ref[0], preferred_element_type=F32,
                       precision=lax.Precision.HIGHEST) + b_ref[0]


def _modulation(c_all, w_mod, b_mod):
    tn = 1024
    n_out = N_MOD * D_MODEL
    blocks = [_nbytes((1, D_MODEL, tn), F32), _nbytes((1, MOD_ROWS, tn), F32)]
    return pl.pallas_call(
        _mod_kernel,
        grid=(DEPTH, n_out // tn),
        in_specs=[
            pl.BlockSpec((MOD_ROWS, D_MODEL), lambda l, j: (0, 0)),
            pl.BlockSpec((1, D_MODEL, tn), lambda l, j: (l, 0, j)),
            pl.BlockSpec((1, 1, tn), lambda l, j: (l, 0, j)),
        ],
        out_specs=pl.BlockSpec((1, MOD_ROWS, tn), lambda l, j: (l, 0, j)),
        out_shape=jax.ShapeDtypeStruct((DEPTH, MOD_ROWS, n_out), F32),
        compiler_params=pltpu.CompilerParams(
            dimension_semantics=("arbitrary", "arbitrary"),
            vmem_limit_bytes=_vmem_limit(blocks, [], 4 * blocks[0])),
        name="modulation",
    )(c_all, w_mod, b_mod.reshape(DEPTH, 1, n_out))


def _ffn_kernel(x_ref, mod_ref, wg_ref, wu_ref, wd_ref, lng_ref, lnb_ref, o_ref, *, j0, sub):
    shift = mod_ref[0, j0:j0 + 1, :]
    scale = mod_ref[0, j0 + 1:j0 + 2, :]
    gate = mod_ref[0, j0 + 2:j0 + 3, :]
    nsub = x_ref.shape[0] // sub

    def gate_up(s):
        x = x_ref[s * sub:(s + 1) * sub, :]
        xin = (x * (1.0 + scale) + shift).astype(BF16)
        return (jnp.dot(xin, wg_ref[...], preferred_element_type=F32),
                jnp.dot(xin, wu_ref[...], preferred_element_type=F32))

    pending = gate_up(0)
    for s in range(nsub):
        g, u = pending
        if s + 1 < nsub:
            pending = gate_up(s + 1)
        rows = slice(s * sub, (s + 1) * sub)
        a = (g * jax.nn.sigmoid(g) * u).astype(BF16)
        y = jnp.dot(a, wd_ref[...], preferred_element_type=F32)
        z = DEEPNORM_ALPHA * x_ref[rows, :] + (0.5 * gate) * y
        o_ref[rows, :] = _layer_norm(z, lng_ref[...], lnb_ref[...])


def _ffn(x, mod, j0, which, wg, wu, wd, ln_g, ln_b, *, nb, length, tm=1024, sub=256):
    nt = length // tm
    tok = _nbytes((tm, D_MODEL), F32)
    weights = [_nbytes(w.shape[2:], BF16) for w in (wg, wu, wd)]
    temps = 3 * _nbytes((tm, D_FF), F32) + 4 * tok
    wspec = lambda w: pl.BlockSpec((None, None) + w.shape[2:], lambda b, t: which + (0, 0),
                                   pipeline_mode=pl.Buffered(1))
    return pl.pallas_call(
        functools.partial(_ffn_kernel, j0=j0, sub=sub),
        grid=(nb, nt),
        in_specs=[
            _tok_spec(tm, D_MODEL, nt),
            _mod_spec(),
            wspec(wg), wspec(wu), wspec(wd),
            _const_spec((1, D_MODEL)), _const_spec((1, D_MODEL)),
        ],
        out_specs=_tok_spec(tm, D_MODEL, nt),
        out_shape=jax.ShapeDtypeStruct((nb * length, D_MODEL), F32),
        compiler_params=pltpu.CompilerParams(
            dimension_semantics=("arbitrary", "arbitrary"),
            vmem_limit_bytes=_vmem_limit([tok, tok], weights, temps)),
        name="ffn",
    )(x, mod, wg, wu, wd, ln_g, ln_b)


def _dup_halves(z, lane):
    zr = pltpu.roll(z, HEAD_DIM, 1)
    lo = lane < HEAD_DIM
    return jnp.where(lo, z, zr), jnp.where(lo, zr, z)


def _attn_inproj_kernel(*refs, rope):
    if rope:
        x_ref, mod_ref, w_ref, cos_ref, sin_ref, q_ref, k_ref, v_ref, u_ref = refs
    else:
        x_ref, mod_ref, w_ref, q_ref, k_ref, v_ref, u_ref = refs
    x = x_ref[...]
    xin = (x * (1.0 + mod_ref[0, 4:5, :]) + mod_ref[0, 3:4, :]).astype(BF16)
    p = jnp.dot(xin, w_ref[...], preferred_element_type=F32)
    lane = lax.broadcasted_iota(jnp.int32, (x.shape[0], V7X_LANES), 1)
    first_half = (lane & (HEAD_DIM - 1)) < HEAD_DIM // 2

    def rot(z):
        if not rope:
            return z
        zr = jnp.where(first_half, pltpu.roll(z, V7X_LANES - HEAD_DIM // 2, 1),
                       pltpu.roll(z, HEAD_DIM // 2, 1))
        return z * cos_ref[...] + zr * sin_ref[...]

    for c in range(ATT_WIDTH // V7X_LANES):
        sl = slice(c * V7X_LANES, (c + 1) * V7X_LANES)
        q_ref[:, sl] = (rot(p[:, sl]) * (ATT_SCALE * LOG2E)).astype(BF16)
    k0, k1 = _dup_halves(rot(p[:, ATT_WIDTH:ATT_WIDTH + KV_WIDTH]), lane)
    k_ref[...] = jnp.concatenate([k0, k1], axis=-1).T.astype(BF16)
    v = p[:, ATT_WIDTH + KV_WIDTH:ATT_WIDTH + 2 * KV_WIDTH]
    vr = pltpu.roll(v, HEAD_DIM, 1)
    lo = lane < HEAD_DIM
    slabs = (jnp.where(lo, v, 1.0), jnp.where(lo, 1.0, vr), jnp.where(lo, vr, 1.0), jnp.where(lo, 1.0, v))
    for i, slab in enumerate(slabs):
        v_ref[:, i * V7X_LANES:(i + 1) * V7X_LANES] = slab.astype(BF16)
    u_ref[...] = p[:, ATT_WIDTH + 2 * KV_WIDTH:]


def _attn_inproj(x, mod, w_in, cos, sin, *, nb, length, tm=512):
    nt = length // tm
    rope = cos is not None
    tok = _nbytes((tm, D_MODEL), F32)
    outs = [_nbytes((tm, ATT_WIDTH), BF16), _nbytes((tm, KEY_SLAB_W + VAL_SLAB_W), BF16),
            _nbytes((tm, POOL_WIDTH), F32)]
    in_specs = [_tok_spec(tm, D_MODEL, nt), _mod_spec(), _const_spec(w_in.shape)]
    args = [x, mod, w_in]
    if rope:
        in_specs += [pl.BlockSpec((tm, V7X_LANES), lambda b, t: (t, 0))] * 2
        args += [cos, sin]
    rows = nb * length
    return pl.pallas_call(
        functools.partial(_attn_inproj_kernel, rope=rope),
        grid=(nb, nt),
        in_specs=in_specs,
        out_specs=[
            _tok_spec(tm, ATT_WIDTH, nt),
            pl.BlockSpec((KEY_SLAB_W, tm), lambda b, t: (0, b * nt + t)),
            _tok_spec(tm, VAL_SLAB_W, nt),
            _tok_spec(tm, POOL_WIDTH, nt),
        ],
        out_shape=[
            jax.ShapeDtypeStruct((rows, ATT_WIDTH), BF16),
            jax.ShapeDtypeStruct((KEY_SLAB_W, rows), BF16),
            jax.ShapeDtypeStruct((rows, VAL_SLAB_W), BF16),
            jax.ShapeDtypeStruct((rows, POOL_WIDTH), F32),
        ],
        compiler_params=pltpu.CompilerParams(
            dimension_semantics=("arbitrary", "arbitrary"),
            vmem_limit_bytes=_vmem_limit([tok] + outs, [_nbytes(w_in.shape, BF16)],
                                         3 * _nbytes((tm, MIX_AB_IN), F32))),
        name="attn_inproj",
    )(*args)


def _attend_scores(q_blk, keys, kh):
    lane = lax.broadcasted_iota(jnp.int32, (BLOCK, V7X_LANES), 1)
    lo = lane < HEAD_DIM
    zero = jnp.zeros((BLOCK, V7X_LANES), BF16)
    parts = []
    for g in range(ATT_GROUPS):
        c = (kh * ATT_GROUPS + g) // 2
        qc = q_blk[:, c * V7X_LANES:(c + 1) * V7X_LANES]
        parts.append(jnp.where(lo if g % 2 == 0 else jnp.logical_not(lo), qc, zero))
    qs = jnp.concatenate(parts, axis=0)
    return [jnp.dot(qs, k, preferred_element_type=F32) for k in keys]


def _attend_finish(scores, values_even, values_odd, segments, sink_ref, kh):
    lane = lax.broadcasted_iota(jnp.int32, (BLOCK, V7X_LANES), 1)
    lo = lane < HEAD_DIM
    probs = ([[] for _ in scores], [[] for _ in scores])
    sink_terms = []
    for g in range(ATT_GROUPS):
        rows = slice(g * BLOCK, (g + 1) * BLOCK)
        segs = [[s[rows, a:b] if bias is None else s[rows, a:b] + bias for a, b, bias in seg]
                for s, seg in zip(scores, segments)]
        sink = sink_ref[kh * ATT_GROUPS + g] * LOG2E
        chunks = [x[:, c:c + V7X_LANES] for sl in segs for x in sl
                  for c in range(0, x.shape[1], V7X_LANES)]
        m = functools.reduce(jnp.maximum, chunks).max(axis=-1, keepdims=True)
        m = jnp.maximum(m, sink)
        sink_terms.append(jnp.exp2(sink - m))
        for i, sl in enumerate(segs):
            e = [jnp.exp2(x - m).astype(BF16) for x in sl]
            probs[g % 2][i].append(e[0] if len(e) == 1 else jnp.concatenate(e, axis=-1))

    def weighted_values(ps, vals):
        o = None
        for p, v in zip(ps, vals):
            t = jnp.dot(jnp.concatenate(p, axis=0), v, preferred_element_type=F32)
            o = t if o is None else o + t
        return o

    o_par = (weighted_values(probs[0], values_even), weighted_values(probs[1], values_odd))
    outs = []
    for g in range(ATT_GROUPS):
        o = o_par[g % 2][(g // 2) * BLOCK:(g // 2 + 1) * BLOCK]
        rinv = 1.0 / (o + sink_terms[g])
        outs.append(o * pltpu.roll(rinv, HEAD_DIM, 1))
    return jnp.where(lo, outs[0], outs[1]), jnp.where(lo, outs[2], outs[3])


def _pool_sums(ubuf_ref, nblk, band_ref):
    u_hi, u_lo = _split_bf16(ubuf_ref[...])
    sums = {}
    for j in range(nblk):
        win = slice(j * BLOCK, (j + 2) * BLOCK)
        for gi in range(len(POOL_WINDOWS)):
            cols = slice(gi * POOL_GROUP_W, (gi + 1) * POOL_GROUP_W)
            band = band_ref[gi]
            sums[j, gi] = (jnp.dot(band, u_hi[win, cols], preferred_element_type=F32)
                           + jnp.dot(band, u_lo[win, cols], preferred_element_type=F32))
    return sums


def _pool_finish(sums, ubuf_ref, nblk, pos0, length, wpool_ref, pscale_ref, mix_scr):
    for j in range(nblk):
        pos = pos0 + j * BLOCK + lax.broadcasted_iota(jnp.int32, (BLOCK, 1), 0)
        rows = slice(POOL_HALO + j * BLOCK, POOL_HALO + (j + 1) * BLOCK)
        for gi, w in enumerate(POOL_WINDOWS):
            r = w // 2
            cols = slice(gi * POOL_GROUP_W, (gi + 1) * POOL_GROUP_W)
            cnt = (jnp.minimum(pos + r, length - 1) - jnp.maximum(pos - r, 0) + 1).astype(F32)
            d = sums[j, gi] / cnt - ubuf_ref[rows, cols]
            y = jnp.dot(d.astype(BF16), wpool_ref[gi], preferred_element_type=F32)
            mix_scr[j * BLOCK:(j + 1) * BLOCK, ATT_WIDTH + gi * POOL_GROUP_W:
                    ATT_WIDTH + (gi + 1) * POOL_GROUP_W] = (y * pscale_ref[:, cols]).astype(BF16)


def _split_bf16(x):
    hi = x.astype(BF16)
    return hi, (x - hi.astype(F32)).astype(BF16)


def _pool_bands():
    i = jnp.arange(BLOCK)[:, None]
    c = jnp.arange(2 * BLOCK)[None, :]
    return jnp.stack([(jnp.abs(c - POOL_HALO - i) <= w // 2) for w in POOL_WINDOWS]).astype(BF16)


def _attn_out(mix_scr, rows, wout_ref, h_ref, gate, lng_ref, lnb_ref, o_ref):
    y = jnp.dot(mix_scr[rows, :], wout_ref[...], preferred_element_type=F32)
    z = DEEPNORM_ALPHA * h_ref[rows, :] + gate * y
    o_ref[rows, :] = _layer_norm(z, lng_ref[...], lnb_ref[...])


def _attn_lat_kernel(sink_ref, q_ref, kc_ref, kp_ref, kn_ref, vc_ref, vp_ref, vn_ref,
                     kctx_ref, vctx_ref, uc_ref, up_ref, un_ref, h_ref, mod_ref, bias_ref, band_ref,
                     wpool_ref, pscale_ref, wout_ref, lng_ref, lnb_ref, o_ref,
                     kbuf, vbuf, ubuf, mix_scr, *, tq, length):
    t = pl.program_id(1)
    last = pl.num_programs(1) - 1
    nblk = tq // BLOCK
    kbuf[:, 0:BLOCK] = kp_ref[...]
    kbuf[:, BLOCK:BLOCK + tq] = kc_ref[...]
    kbuf[:, BLOCK + tq:] = kn_ref[...]
    vbuf[0:BLOCK] = vp_ref[...]
    vbuf[BLOCK:BLOCK + tq] = vc_ref[...]
    vbuf[BLOCK + tq:] = vn_ref[...]
    ubuf[0:POOL_HALO] = jnp.where(t == 0, 0.0, up_ref[...])
    ubuf[POOL_HALO:POOL_HALO + tq] = uc_ref[...]
    ubuf[POOL_HALO + tq:2 * POOL_HALO + tq] = jnp.where(t == last, 0.0, un_ref[...])
    ubuf[2 * POOL_HALO + tq:] = jnp.zeros((2 * BLOCK - 2 * POOL_HALO, POOL_WIDTH), F32)
    lanes = lambda i: slice(i * V7X_LANES, (i + 1) * V7X_LANES)
    gate = mod_ref[0, 5:6, :]
    units = [(j, kh) for j in range(nblk) for kh in range(ATT_KV_HEADS)]

    def scores_of(j, kh):
        win = slice(j * BLOCK, j * BLOCK + 3 * BLOCK)
        return _attend_scores(q_ref[j * BLOCK:(j + 1) * BLOCK, :],
                              [kbuf[lanes(kh), win], kctx_ref[lanes(kh), :]], kh)

    sums = _pool_sums(ubuf, nblk, band_ref)
    pending = [scores_of(*u) for u in units[:SCORE_LOOKAHEAD]]
    _pool_finish(sums, ubuf, nblk, t * tq, length, wpool_ref, pscale_ref, mix_scr)
    for n, (j, kh) in enumerate(units):
        scores = pending.pop(0)
        if n + SCORE_LOOKAHEAD < len(units):
            pending.append(scores_of(*units[n + SCORE_LOOKAHEAD]))
        blk = t * nblk + j
        bidx = jnp.where(blk == 0, 0, jnp.where(blk == length // BLOCK - 1, 2, 1))
        bias = bias_ref[bidx]
        segments = [[(0, BLOCK, bias[:, :BLOCK]), (BLOCK, 2 * BLOCK, None),
                     (2 * BLOCK, 3 * BLOCK, bias[:, BLOCK:])], [(0, CTX_LEN, None)]]
        win = slice(j * BLOCK, j * BLOCK + 3 * BLOCK)
        rows = slice(j * BLOCK, (j + 1) * BLOCK)
        c0, c1 = _attend_finish(
            scores, [vbuf[win, lanes(2 * kh)], vctx_ref[:, lanes(2 * kh)]],
            [vbuf[win, lanes(2 * kh + 1)], vctx_ref[:, lanes(2 * kh + 1)]],
            segments, sink_ref, kh)
        mix_scr[rows, lanes(2 * kh)] = c0.astype(BF16)
        mix_scr[rows, lanes(2 * kh + 1)] = c1.astype(BF16)
        per_epilogue = EPILOGUE_BLOCKS * ATT_KV_HEADS
        if n > 0 and n % per_epilogue == 0:
            e = n // per_epilogue - 1
            done = slice(e * EPILOGUE_BLOCKS * BLOCK, (e + 1) * EPILOGUE_BLOCKS * BLOCK)
            _attn_out(mix_scr, done, wout_ref, h_ref, gate, lng_ref, lnb_ref, o_ref)
    _attn_out(mix_scr, slice(tq - EPILOGUE_BLOCKS * BLOCK, tq), wout_ref, h_ref, gate, lng_ref,
              lnb_ref, o_ref)


def _attn_ctx_kernel(sink_ref, q_ref, kctx_ref, vctx_ref, uc_ref, h_ref, mod_ref, band_ref,
                     wpool_ref, pscale_ref, wout_ref, lng_ref, lnb_ref, o_ref,
                     ubuf, mix_scr, *, tq):
    ubuf[0:POOL_HALO] = jnp.zeros((POOL_HALO, POOL_WIDTH), F32)
    ubuf[POOL_HALO:POOL_HALO + tq] = uc_ref[...]
    ubuf[POOL_HALO + tq:] = jnp.zeros((2 * BLOCK - POOL_HALO, POOL_WIDTH), F32)
    lanes = lambda i: slice(i * V7X_LANES, (i + 1) * V7X_LANES)
    nblk = tq // BLOCK
    units = [(j, kh) for j in range(nblk) for kh in range(ATT_KV_HEADS)]
    scores_of = lambda j, kh: _attend_scores(q_ref[j * BLOCK:(j + 1) * BLOCK, :],
                                             [kctx_ref[lanes(kh), :]], kh)
    sums = _pool_sums(ubuf, nblk, band_ref)
    pending = [scores_of(*u) for u in units[:SCORE_LOOKAHEAD]]
    _pool_finish(sums, ubuf, nblk, 0, tq, wpool_ref, pscale_ref, mix_scr)
    for n, (j, kh) in enumerate(units):
        scores = pending.pop(0)
        if n + SCORE_LOOKAHEAD < len(units):
            pending.append(scores_of(*units[n + SCORE_LOOKAHEAD]))
        rows = slice(j * BLOCK, (j + 1) * BLOCK)
        c0, c1 = _attend_finish(scores, [vctx_ref[:, lanes(2 * kh)]],
                                [vctx_ref[:, lanes(2 * kh + 1)]], [[(0, CTX_LEN, None)]],
                                sink_ref, kh)
        mix_scr[rows, lanes(2 * kh)] = c0.astype(BF16)
        mix_scr[rows, lanes(2 * kh + 1)] = c1.astype(BF16)
    _attn_out(mix_scr, slice(0, tq), wout_ref, h_ref, mod_ref[0, 5:6, :], lng_ref, lnb_ref, o_ref)


def _window_bias():
    qi = jnp.arange(BLOCK)[:, None]
    kc = jnp.arange(3 * BLOCK)[None, :]
    band = jnp.abs(qi + BLOCK - kc) <= WINDOW
    first = band & (kc >= BLOCK)
    final = band & (kc < 2 * BLOCK)
    full = jnp.where(jnp.stack([first, band, final]), 0.0, NEG_INF).astype(F32)
    return jnp.concatenate([full[:, :, :BLOCK], full[:, :, 2 * BLOCK:]], axis=-1)


def _smem_spec():
    return pl.BlockSpec(memory_space=pltpu.SMEM)


def _attn_lat(sink, q, k2, v2, kctx, vctx, u, h, mod, wpool, pscale, wout, ln_g, ln_b, *, tq=512):
    nt = SEQ // tq
    bpt = tq // BLOCK
    bps = SEQ // BLOCK
    hpt = tq // POOL_HALO
    hps = SEQ // POOL_HALO
    cur = lambda w: pl.BlockSpec((tq, w), lambda b, t: (b * nt + t, 0))
    prev_blk = lambda rows, w, per_tile, per_samp: pl.BlockSpec(
        (rows, w), lambda b, t: (b * per_samp + jnp.maximum(t * per_tile - 1, 0), 0))
    next_blk = lambda rows, w, per_tile, per_samp: pl.BlockSpec(
        (rows, w), lambda b, t: (b * per_samp + jnp.minimum((t + 1) * per_tile, per_samp - 1), 0))
    ctx_blk = lambda w: pl.BlockSpec((CTX_LEN, w), lambda b, t: (b, 0))
    kv_specs = lambda w: [cur(w), prev_blk(BLOCK, w, bpt, bps), next_blk(BLOCK, w, bpt, bps)]
    key_specs = [
        pl.BlockSpec((KEY_SLAB_W, tq), lambda b, t: (0, b * nt + t)),
        pl.BlockSpec((KEY_SLAB_W, BLOCK), lambda b, t: (0, b * bps + jnp.maximum(t * bpt - 1, 0))),
        pl.BlockSpec((KEY_SLAB_W, BLOCK),
                     lambda b, t: (0, b * bps + jnp.minimum((t + 1) * bpt, bps - 1))),
    ]
    bias = _window_bias()
    bands = _pool_bands()
    tok = _nbytes((tq, D_MODEL), F32)
    kv_rows = tq + 2 * BLOCK
    u_rows = tq + 2 * BLOCK
    pipelined = [_nbytes((tq, ATT_WIDTH), BF16), 2 * _nbytes((kv_rows, KEY_SLAB_W + VAL_SLAB_W), BF16),
                 _nbytes((tq + 2 * POOL_HALO, POOL_WIDTH), F32), 2 * tok]
    resident = [_nbytes(bias.shape, F32), _nbytes(bands.shape, BF16), _nbytes(wpool.shape, BF16),
                _nbytes(wout.shape, BF16)]
    scratch = [_nbytes((kv_rows, KEY_SLAB_W + VAL_SLAB_W), BF16),
               _nbytes((u_rows, POOL_WIDTH), F32), _nbytes((tq, D_MODEL), BF16)]
    return pl.pallas_call(
        functools.partial(_attn_lat_kernel, tq=tq, length=SEQ),
        grid=(BATCH, nt),
        in_specs=[_smem_spec(), cur(ATT_WIDTH)] + key_specs + kv_specs(VAL_SLAB_W) + [
            pl.BlockSpec((KEY_SLAB_W, CTX_LEN), lambda b, t: (0, b)), ctx_blk(VAL_SLAB_W),
            cur(POOL_WIDTH), prev_blk(POOL_HALO, POOL_WIDTH, hpt, hps),
            next_blk(POOL_HALO, POOL_WIDTH, hpt, hps),
            cur(D_MODEL), _mod_spec(),
            _const_spec(bias.shape), _const_spec(bands.shape),
            _const_spec(wpool.shape), _const_spec((1, POOL_WIDTH)),
            _const_spec(wout.shape), _const_spec((1, D_MODEL)), _const_spec((1, D_MODEL)),
        ],
        out_specs=cur(D_MODEL),
        out_shape=jax.ShapeDtypeStruct((BATCH * SEQ, D_MODEL), F32),
        scratch_shapes=[
            pltpu.VMEM((KEY_SLAB_W, kv_rows), BF16), pltpu.VMEM((kv_rows, VAL_SLAB_W), BF16),
            pltpu.VMEM((u_rows, POOL_WIDTH), F32), pltpu.VMEM((tq, D_MODEL), BF16),
        ],
        compiler_params=pltpu.CompilerParams(
            dimension_semantics=("arbitrary", "arbitrary"),
            vmem_limit_bytes=_vmem_limit(pipelined, resident + scratch, 8 * tok)),
        name="attn_latent",
    )(sink, q, k2, k2, k2, v2, v2, v2, kctx, vctx, u, u, u, h, mod, bias, bands, wpool, pscale, wout,
      ln_g, ln_b)


def _attn_ctx(sink, q, kctx, vctx, u, h, mod, wpool, pscale, wout, ln_g, ln_b):
    tq = CTX_LEN
    blk = lambda w: pl.BlockSpec((tq, w), lambda b, t: (b, 0))
    bands = _pool_bands()
    tok = _nbytes((tq, D_MODEL), F32)
    u_rows = tq + 2 * BLOCK
    pipelined = [_nbytes((tq, ATT_WIDTH), BF16), _nbytes((tq, KEY_SLAB_W + VAL_SLAB_W), BF16),
                 _nbytes((tq, POOL_WIDTH), F32), 2 * tok]
    resident = [_nbytes(bands.shape, BF16), _nbytes(wpool.shape, BF16), _nbytes(wout.shape, BF16)]
    scratch = [_nbytes((u_rows, POOL_WIDTH), F32), _nbytes((tq, D_MODEL), BF16)]
    return pl.pallas_call(
        functools.partial(_attn_ctx_kernel, tq=tq),
        grid=(BATCH, 1),
        in_specs=[
            _smem_spec(),
            blk(ATT_WIDTH), pl.BlockSpec((KEY_SLAB_W, tq), lambda b, t: (0, b)),
            blk(VAL_SLAB_W), blk(POOL_WIDTH), blk(D_MODEL),
            pl.BlockSpec((1, N_MOD, D_MODEL), lambda b, t: (0, 0, 0)),
            _const_spec(bands.shape), _const_spec(wpool.shape), _const_spec((1, POOL_WIDTH)),
            _const_spec(wout.shape), _const_spec((1, D_MODEL)), _const_spec((1, D_MODEL)),
        ],
        out_specs=blk(D_MODEL),
        out_shape=jax.ShapeDtypeStruct((BATCH * CTX_LEN, D_MODEL), F32),
        scratch_shapes=[
            pltpu.VMEM((u_rows, POOL_WIDTH), F32), pltpu.VMEM((tq, D_MODEL), BF16),
        ],
        compiler_params=pltpu.CompilerParams(
            dimension_semantics=("arbitrary", "arbitrary"),
            vmem_limit_bytes=_vmem_limit(pipelined, resident + scratch, 8 * tok)),
        name="attn_context",
    )(sink, q, kctx, vctx, u, h, mod, bands, wpool, pscale, wout, ln_g, ln_b)


def _row_perm(to_time_major):
    n = BATCH * PERM_STEPS
    r = lax.broadcasted_iota(jnp.int32, (n, n), 0)
    c = lax.broadcasted_iota(jnp.int32, (n, n), 1)
    if to_time_major:
        hit = ((r // BATCH) == (c % PERM_STEPS)) & ((r % BATCH) == (c // PERM_STEPS))
    else:
        hit = ((r // PERM_STEPS) == (c % BATCH)) & ((r % PERM_STEPS) == (c // BATCH))
    return jnp.where(hit, 1.0, 0.0).astype(BF16)


def _lru_time_major_input(x_ref, scale_ref, shift_ref, tt):
    xin = (x_ref[...] * (1.0 + scale_ref[...]) + shift_ref[...]).astype(BF16)
    perm = _row_perm(to_time_major=True)
    parts = []
    for tau in range(tt // PERM_STEPS):
        sub = xin[:, tau * PERM_STEPS:(tau + 1) * PERM_STEPS, :].reshape(BATCH * PERM_STEPS, D_MODEL)
        parts.append(jnp.dot(perm, sub, preferred_element_type=F32).astype(BF16))
    return jnp.concatenate(parts, axis=0)


def _lru_in_fwd_kernel(*refs, tt, with_gate):
    (x_ref, scale_ref, shift_ref, w_ref, h0_ref, cw_ref, cb_ref, wrg_ref, ba_ref, bx_ref,
     lam_ref) = refs[:11]
    outs = refs[11:]
    if with_gate:
        gate_ref, u_ref, s_ref, hfin_ref, ring, ulast, h_scr, a_scr, b_scr = outs
    else:
        u_ref, s_ref, hfin_ref, ring, ulast, h_scr, a_scr, b_scr = outs
    i = pl.program_id(0)
    ntiles = pl.num_programs(0) - LRU_SCAN_LAG

    @pl.when(i == 0)
    def _():
        ring[...] = jnp.zeros(ring.shape, F32)
        ulast[...] = jnp.zeros(ulast.shape, F32)
        h_scr[...] = h0_ref[...]

    xt = _lru_time_major_input(x_ref, scale_ref, shift_ref, tt)
    slot_new = lax.rem(i, LRU_SCAN_LAG + 1)
    slot_cur = lax.rem(i + 1, LRU_SCAN_LAG + 1)
    slot_nxt = lax.rem(i + 2, LRU_SCAN_LAG + 1)

    def project_chunk(c):
        cols = slice(c * LRU_PROJ_CHUNK, (c + 1) * LRU_PROJ_CHUNK)
        val = jnp.dot(xt, w_ref[:, cols], preferred_element_type=F32)
        val = val.reshape(tt, BATCH, LRU_PROJ_CHUNK)
        if with_gate and c < LRU_WIDTH // LRU_PROJ_CHUNK:
            gate_ref[:, :, cols] = val.astype(BF16)
        else:
            ucols = slice(cols.start % LRU_WIDTH, cols.start % LRU_WIDTH + LRU_PROJ_CHUNK)
            u_ref[:, :, ucols] = val
            ring[slot_new, :, :, ucols] = val

    nchunks = w_ref.shape[1] // LRU_PROJ_CHUNK
    ahead = nchunks - (LRU_BLOCKS - 2)
    for c in range(ahead):
        project_chunk(c)

    cur = ring[slot_cur]
    nxt = jnp.where(i == ntiles + LRU_SCAN_LAG - 1, 0.0, ring[slot_nxt, 0:CONV_W - 1 - CONV_LEFT])
    _lru_coeffs(cur, ulast[...], nxt, cw_ref, cb_ref, wrg_ref, ba_ref, bx_ref, lam_ref,
                a_scr, b_scr, tt,
                interleave=[functools.partial(project_chunk, c) for c in range(max(ahead, 0), nchunks)])
    ulast[...] = cur[tt - CONV_LEFT:tt]

    def step(k, h):
        h = a_scr[k] * h + b_scr[k]
        s_ref[k] = h
        return h

    h_prev = h_scr[...]
    h = lax.fori_loop(0, tt, step, h_prev, unroll=4)
    h = jnp.where(i >= LRU_SCAN_LAG, h, h_prev)
    h_scr[...] = h
    hfin_ref[...] = h


def _lru_in_fwd(x3, scale, shift, w, h0, conv_w, conv_b, wrg, ba, bx, lam, *, with_gate, tt=32):
    length = x3.shape[1]
    ntiles = length // tt
    blk = _nbytes((tt, BATCH, LRU_WIDTH), F32)
    proj_tile = lambda i: jnp.minimum(i, ntiles - 1)
    scan_tile = lambda i: jnp.maximum(i - LRU_SCAN_LAG, 0)
    tm_spec = lambda tile: pl.BlockSpec((tt, BATCH, LRU_WIDTH), lambda i: (tile(i), 0, 0))
    out_specs = [tm_spec(proj_tile), tm_spec(scan_tile),
                 pl.BlockSpec((BATCH, LRU_WIDTH), lambda i: (0, 0))]
    out_shape = [jax.ShapeDtypeStruct((length, BATCH, LRU_WIDTH), F32)] * 2 + [
        jax.ShapeDtypeStruct((BATCH, LRU_WIDTH), F32)]
    outs = [blk, blk]
    if with_gate:
        out_specs = [tm_spec(proj_tile)] + out_specs
        out_shape = [jax.ShapeDtypeStruct((length, BATCH, LRU_WIDTH), BF16)] + out_shape
        outs.append(blk // 2)
    res = pl.pallas_call(
        functools.partial(_lru_in_fwd_kernel, tt=tt, with_gate=with_gate),
        grid=(ntiles + LRU_SCAN_LAG,),
        in_specs=[pl.BlockSpec((BATCH, tt, D_MODEL), lambda i: (0, proj_tile(i), 0)),
                  _const_spec(scale.shape), _const_spec(shift.shape), _const_spec(w.shape),
                  _const_spec((BATCH, LRU_WIDTH)),
                  _const_spec((CONV_W, LRU_WIDTH)), _const_spec((1, LRU_WIDTH)),
                  _const_spec(wrg.shape),
                  _const_spec((1, LRU_WIDTH)), _const_spec((1, LRU_WIDTH)), _const_spec((1, LRU_WIDTH))],
        out_specs=out_specs,
        out_shape=out_shape,
        scratch_shapes=[
            pltpu.VMEM((LRU_SCAN_LAG + 1, tt, BATCH, LRU_WIDTH), F32),
            pltpu.VMEM((CONV_LEFT, BATCH, LRU_WIDTH), F32),
            pltpu.VMEM((BATCH, LRU_WIDTH), F32),
            pltpu.VMEM((tt, BATCH, LRU_WIDTH), F32), pltpu.VMEM((tt, BATCH, LRU_WIDTH), F32),
        ],
        compiler_params=pltpu.CompilerParams(
            dimension_semantics=("arbitrary",),
            vmem_limit_bytes=_vmem_limit(
                [blk] + outs, [_nbytes(w.shape, BF16), _nbytes(wrg.shape, BF16),
                               (LRU_SCAN_LAG + 3) * blk], 8 * blk)),
        name="lru_in_fwd",
    )(x3, scale, shift, w, h0, conv_w, conv_b, wrg, ba, bx, lam)
    return res if with_gate else (None,) + tuple(res)


def _lru_coeffs(cur, prev, nxt, cw_ref, cb_ref, wrg_ref, ba_ref, bx_ref, lam_ref, a_scr, b_scr, tt,
                interleave=()):
    ext = jnp.concatenate([prev, cur, nxt], axis=0)
    u = cb_ref[...].reshape(1, 1, LRU_WIDTH)
    for tap in range(CONV_W):
        u = u + ext[tap:tap + tt] * cw_ref[tap:tap + 1, :].reshape(1, 1, LRU_WIDTH)
    u2 = u.reshape(tt * BATCH, LRU_WIDTH)
    ub16 = u2.astype(BF16)
    col = lambda blk: slice(blk * LRU_BLOCK_W, (blk + 1) * LRU_BLOCK_W)
    lam = lam_ref[...]
    half_decay2 = (-0.5 * LRU_C * LOG2E) * (jnp.maximum(-lam, 0.0) + jnp.log1p(jnp.exp(-jnp.abs(lam))))
    half_ba = 0.5 * ba_ref[...]
    half_bx = 0.5 * bx_ref[...]
    rgs = []
    for blk in range(LRU_BLOCKS + 1):
        if blk < LRU_BLOCKS:
            rgs.append(jnp.dot(ub16[:, col(blk)], wrg_ref[blk], preferred_element_type=F32))
            if blk < len(interleave):
                interleave[blk]()
        if blk == 0:
            continue
        blk -= 1
        rg = rgs[blk]
        cols = col(blk)
        tr = jnp.tanh(rg[:, :LRU_BLOCK_W] + half_ba[:, cols])
        tg = jnp.tanh(rg[:, LRU_BLOCK_W:] + half_bx[:, cols])
        hd = half_decay2[:, cols]
        a = jnp.exp2(hd + hd * tr)
        x = 1.0 - a * a
        hu = 0.5 * u2[:, cols]
        b = jnp.where(x > 0.0, x * lax.rsqrt(x), 0.0) * (hu + hu * tg)
        a_scr[:, :, cols] = a.reshape(tt, BATCH, LRU_BLOCK_W)
        b_scr[:, :, cols] = b.reshape(tt, BATCH, LRU_BLOCK_W)


def _lru_scan_kernel(uc_ref, up_ref, un_ref, h0_ref, cw_ref, cb_ref, wrg_ref, ba_ref, bx_ref,
                     lam_ref, s_ref, hfin_ref, h_scr, a_scr, b_scr, *, tt, reverse):
    i = pl.program_id(0)
    nsteps = pl.num_programs(0)
    ti = nsteps - 1 - i if reverse else i

    @pl.when(i == 0)
    def _():
        h_scr[...] = h0_ref[...]

    prev = jnp.where(ti == 0, 0.0, up_ref[...])
    nxt = jnp.where(ti == nsteps - 1, 0.0, un_ref[...])
    _lru_coeffs(uc_ref[...], prev, nxt, cw_ref, cb_ref, wrg_ref, ba_ref, bx_ref, lam_ref,
                a_scr, b_scr, tt)

    def step(k, h):
        idx = tt - 1 - k if reverse else k
        h = a_scr[idx] * h + b_scr[idx]
        s_ref[idx] = h
        return h

    h = lax.fori_loop(0, tt, step, h_scr[...], unroll=4)
    h_scr[...] = h
    hfin_ref[...] = h


def _lru_scan_in_specs(length, tt, tile, wrg_shape):
    return [
        pl.BlockSpec((tt, BATCH, LRU_WIDTH), lambda i: (tile(i), 0, 0)),
        pl.BlockSpec((1, BATCH, LRU_WIDTH), lambda i: (jnp.maximum(tile(i) * tt - 1, 0), 0, 0)),
        pl.BlockSpec((2, BATCH, LRU_WIDTH),
                     lambda i: (jnp.minimum((tile(i) + 1) * (tt // 2), length // 2 - 1), 0, 0)),
        _const_spec((BATCH, LRU_WIDTH)),
        _const_spec((CONV_W, LRU_WIDTH)), _const_spec((1, LRU_WIDTH)),
        _const_spec(wrg_shape),
        _const_spec((1, LRU_WIDTH)), _const_spec((1, LRU_WIDTH)), _const_spec((1, LRU_WIDTH)),
    ]


def _lru_scan(u_t, h0, conv_w, conv_b, wrg, ba, bx, lam, *, reverse, tt=32):
    length = u_t.shape[0]
    nsteps = length // tt
    tile = lambda i: (nsteps - 1 - i) if reverse else i
    blk = _nbytes((tt, BATCH, LRU_WIDTH), F32)
    return pl.pallas_call(
        functools.partial(_lru_scan_kernel, tt=tt, reverse=reverse),
        grid=(nsteps,),
        in_specs=_lru_scan_in_specs(length, tt, tile, wrg.shape),
        out_specs=[
            pl.BlockSpec((tt, BATCH, LRU_WIDTH), lambda i: (tile(i), 0, 0)),
            pl.BlockSpec((BATCH, LRU_WIDTH), lambda i: (0, 0)),
        ],
        out_shape=[
            jax.ShapeDtypeStruct((length, BATCH, LRU_WIDTH), F32),
            jax.ShapeDtypeStruct((BATCH, LRU_WIDTH), F32),
        ],
        scratch_shapes=[
            pltpu.VMEM((BATCH, LRU_WIDTH), F32),
            pltpu.VMEM((tt, BATCH, LRU_WIDTH), F32), pltpu.VMEM((tt, BATCH, LRU_WIDTH), F32),
        ],
        compiler_params=pltpu.CompilerParams(
            dimension_semantics=("arbitrary",),
            vmem_limit_bytes=_vmem_limit([blk, blk], [2 * blk, _nbytes(wrg.shape, BF16)], 6 * blk)),
        name="lru_scan_bwd" if reverse else "lru_scan_fwd",
    )(u_t, u_t, u_t, h0, conv_w, conv_b, wrg, ba, bx, lam)


def _lru_bwd_out_kernel(uc_ref, up_ref, un_ref, h0_ref, cw_ref, cb_ref, wrg_ref, ba_ref, bx_ref,
                        lam_ref, sf_ref, gate_ref, h_ref, g2_ref, w_ref, lng_ref, lnb_ref, o_ref,
                        h_scr, a_scr, b_scr, sb_ring, out_scr, *, tt):
    i = pl.program_id(0)
    ntiles = pl.num_programs(0) - 1

    @pl.when(i == 0)
    def _():
        h_scr[...] = h0_ref[...]
        sb_ring[...] = jnp.zeros(sb_ring.shape, F32)

    y = sb_ring[lax.rem(i + 1, 2)] + sf_ref[...]
    z = (_gelu_tanh(gate_ref[...].astype(F32)) * y).astype(BF16)
    perm = _row_perm(to_time_major=False)
    parts = []
    for tau in range(tt // PERM_STEPS):
        sub = z[tau * PERM_STEPS:(tau + 1) * PERM_STEPS].reshape(PERM_STEPS * BATCH, LRU_WIDTH)
        zb = jnp.dot(perm, sub, preferred_element_type=F32).astype(BF16)
        parts.append(zb.reshape(BATCH, PERM_STEPS, LRU_WIDTH))
    zb = jnp.concatenate(parts, axis=1).reshape(BATCH * tt, LRU_WIDTH)

    def project_chunk(c):
        cols = slice(c * LRU_PROJ_CHUNK, (c + 1) * LRU_PROJ_CHUNK)
        val = jnp.dot(zb, w_ref[:, cols], preferred_element_type=F32)
        out_scr[:, :, cols] = val.reshape(BATCH, tt, LRU_PROJ_CHUNK)

    prev = jnp.where(i >= ntiles - 1, 0.0, up_ref[...])
    nxt = jnp.where(i == 0, 0.0, un_ref[...])
    _lru_coeffs(uc_ref[...], prev, nxt, cw_ref, cb_ref, wrg_ref, ba_ref, bx_ref, lam_ref,
                a_scr, b_scr, tt,
                interleave=[functools.partial(project_chunk, c)
                            for c in range(D_MODEL // LRU_PROJ_CHUNK)])
    slot = lax.rem(i, 2)

    def step(k, h):
        idx = tt - 1 - k
        h = a_scr[idx] * h + b_scr[idx]
        sb_ring[slot, idx] = h
        return h

    h_scr[...] = lax.fori_loop(0, tt, step, h_scr[...], unroll=4)
    res = DEEPNORM_ALPHA * h_ref[...] + g2_ref[...] * out_scr[...]
    o_ref[...] = _layer_norm(res, lng_ref[...].reshape(1, 1, D_MODEL), lnb_ref[...].reshape(1, 1, D_MODEL))


def _lru_bwd_out(u_t, h0, conv_w, conv_b, wrg, ba, bx, lam, s_f, gate_t, h3, g2, w_out, ln_g, ln_b,
                 *, tt=32):
    ntiles = SEQ // tt
    scan_tile = lambda i: jnp.maximum(ntiles - 1 - i, 0)
    out_tile = lambda i: jnp.minimum(ntiles - i, ntiles - 1)
    tm_blk = lambda: pl.BlockSpec((tt, BATCH, LRU_WIDTH), lambda i: (out_tile(i), 0, 0))
    bm_blk = lambda: pl.BlockSpec((BATCH, tt, D_MODEL), lambda i: (0, out_tile(i), 0))
    blk = _nbytes((tt, BATCH, LRU_WIDTH), F32)
    return pl.pallas_call(
        functools.partial(_lru_bwd_out_kernel, tt=tt),
        grid=(ntiles + 1,),
        in_specs=_lru_scan_in_specs(SEQ, tt, scan_tile, wrg.shape) + [
            tm_blk(), tm_blk(), bm_blk(),
            _const_spec((BATCH, 1, D_MODEL)), _const_spec(w_out.shape),
            _const_spec((1, D_MODEL)), _const_spec((1, D_MODEL))],
        out_specs=bm_blk(),
        out_shape=jax.ShapeDtypeStruct((BATCH, SEQ, D_MODEL), F32),
        scratch_shapes=[
            pltpu.VMEM((BATCH, LRU_WIDTH), F32),
            pltpu.VMEM((tt, BATCH, LRU_WIDTH), F32), pltpu.VMEM((tt, BATCH, LRU_WIDTH), F32),
            pltpu.VMEM((2, tt, BATCH, LRU_WIDTH), F32), pltpu.VMEM((BATCH, tt, D_MODEL), F32),
        ],
        compiler_params=pltpu.CompilerParams(
            dimension_semantics=("arbitrary",),
            vmem_limit_bytes=_vmem_limit(
                [blk] * 5, [5 * blk, _nbytes(w_out.shape, BF16), _nbytes(wrg.shape, BF16)], 8 * blk)),
        name="lru_bwd_out",
    )(u_t, u_t, u_t, h0, conv_w, conv_b, wrg, ba, bx, lam, s_f, gate_t, h3, g2, w_out, ln_g, ln_b)


def _rope_tables():
    rows = SEQ // GRID_W
    row = jnp.repeat(jnp.arange(rows, dtype=F32), GRID_W)
    col = jnp.tile(jnp.arange(GRID_W, dtype=F32), rows)
    inv = ROPE_THETA ** (-jnp.arange(ROPE_FREQS, dtype=F32) / ROPE_FREQS)
    ang = jnp.concatenate([row[:, None] * inv, col[:, None] * inv], axis=-1)
    cos, sin = jnp.cos(ang), jnp.sin(ang)
    cos_t = jnp.tile(cos, (1, V7X_LANES // (HEAD_DIM // 2)))
    sin_t = jnp.tile(jnp.concatenate([-sin, sin], axis=-1), (1, V7X_LANES // HEAD_DIM))
    return cos_t, sin_t


def kernel(x, c, ctx, c_ctx, w_mod, b_mod, ln_g, ln_b, ffn_w_gate, ffn_w_up, ffn_w_down, mix_ab_w_in, attn_sink, pool_w, pool_scale, mix_ab_w_out, lru_w_in, lru_conv_w, lru_conv_b, lru_wa, lru_ba, lru_wx, lru_bx, lru_lambda, lru_w_out):
    assert x.shape == (BATCH, SEQ, D_MODEL) and ctx.shape == (BATCH, CTX_LEN, D_MODEL)
    c_all = jnp.concatenate(
        [c, c_ctx[None, :], jnp.zeros((MOD_ROWS - BATCH - 1, D_MODEL), F32)], axis=0)
    mod_all = _modulation(c_all, w_mod, b_mod)
    mod_lat = mod_all[:, :BATCH].reshape(DEPTH, BATCH, N_MOD, D_MODEL)
    mod_ctx = mod_all[:, BATCH:BATCH + 1].reshape(DEPTH, 1, N_MOD, D_MODEL)

    wg = ffn_w_gate.astype(BF16)
    wu = ffn_w_up.astype(BF16)
    wd = ffn_w_down.astype(BF16)
    row = lambda v: v.reshape(1, -1)
    n_ctx = BATCH * CTX_LEN

    h = x.reshape(BATCH * SEQ, D_MODEL)
    hc = ctx.reshape(n_ctx, D_MODEL)

    l = 0
    ml, mc = mod_lat[l], mod_ctx[l]
    ffn1 = ((l, 0), wg, wu, wd, row(ln_g[l, 0]), row(ln_b[l, 0]))
    ffn2 = ((l, 1), wg, wu, wd, row(ln_g[l, 2]), row(ln_b[l, 2]))
    h = _ffn(h, ml, 0, *ffn1, nb=BATCH, length=SEQ)
    hc = _ffn(hc, mc, 0, *ffn1, nb=1, length=n_ctx)
    w_in = mix_ab_w_in[0].astype(BF16)
    cos_t, sin_t = _rope_tables()
    q, k2, v2, u = _attn_inproj(h, ml, w_in, cos_t, sin_t, nb=BATCH, length=SEQ)
    q_c, k2_c, v2_c, u_c = _attn_inproj(hc, mc, w_in, None, None, nb=1, length=n_ctx)
    mix_args = (pool_w[0].astype(BF16), row(pool_scale[0]), mix_ab_w_out[0].astype(BF16),
                row(ln_g[l, 1]), row(ln_b[l, 1]))
    h = _attn_lat(attn_sink[0], q, k2, v2, k2_c, v2_c, u, h, ml, *mix_args)
    hc = _attn_ctx(attn_sink[0], q_c, k2_c, v2_c, u_c, hc, mc, *mix_args)
    h = _ffn(h, ml, 6, *ffn2, nb=BATCH, length=SEQ)
    hc = _ffn(hc, mc, 6, *ffn2, nb=1, length=n_ctx)

    l = 1
    ml, mc = mod_lat[l], mod_ctx[l]
    ffn1 = ((l, 0), wg, wu, wd, row(ln_g[l, 0]), row(ln_b[l, 0]))
    ffn2 = ((l, 1), wg, wu, wd, row(ln_g[l, 2]), row(ln_b[l, 2]))
    h = _ffn(h, ml, 0, *ffn1, nb=BATCH, length=SEQ)
    hc = _ffn(hc, mc, 0, *ffn1, nb=1, length=n_ctx)
    w_in = lru_w_in[0].astype(BF16)
    h3 = h.reshape(BATCH, SEQ, D_MODEL)
    wrg = (0.5 * jnp.concatenate([lru_wa[0], lru_wx[0]], axis=-1)).astype(BF16)
    conv_w, conv_b = lru_conv_w[0], row(lru_conv_b[0])
    zeros = jnp.zeros((BATCH, LRU_WIDTH), F32)
    dir_args = [(conv_w, conv_b, wrg[d], row(lru_ba[0, d]), row(lru_bx[0, d]), row(lru_lambda[0, d]))
                for d in range(2)]
    _, uc_t, _, hf_ctx = _lru_in_fwd(hc.reshape(BATCH, CTX_LEN, D_MODEL), mc[:, 4:5, :], mc[:, 3:4, :],
                                     w_in[:, LRU_WIDTH:], zeros, *dir_args[0], with_gate=False)
    gate_t, u_t, s_f, _ = _lru_in_fwd(h3, ml[:, 4:5, :], ml[:, 3:4, :], w_in, hf_ctx, *dir_args[0],
                                      with_gate=True)
    _, hb_ctx = _lru_scan(uc_t, zeros, *dir_args[1], reverse=True)
    h3 = _lru_bwd_out(u_t, hb_ctx, *dir_args[1], s_f, gate_t, h3, ml[:, 5:6, :],
                      lru_w_out[0].astype(BF16), row(ln_g[l, 1]), row(ln_b[l, 1]))
    h = _ffn(h3.reshape(BATCH * SEQ, D_MODEL), ml, 6, *ffn2, nb=BATCH, length=SEQ)
    return h.reshape(BATCH, SEQ, D_MODEL)
```

```python
import functools

import jax
import jax.numpy as jnp
from jax import lax
from jax.experimental import pallas as pl
from jax.experimental.pallas import tpu as pltpu

D_MODEL = 1024
BATCH = 16
SEQ = 2048
DEPTH = 2
GRID_W = 64
CTX_LEN = 256
HEAD_DIM = 64
ATT_HEADS = 8
ATT_KV_HEADS = 2
ATT_GROUPS = ATT_HEADS // ATT_KV_HEADS
ATT_WIDTH = ATT_HEADS * HEAD_DIM
KV_WIDTH = ATT_KV_HEADS * HEAD_DIM
WINDOW = 128
BLOCK = 128
ATT_SCALE = HEAD_DIM ** -0.5
LOG2E = 1.4426950408889634
ROPE_THETA = 10000.0
ROPE_FREQS = HEAD_DIM // 4
POOL_WINDOWS = (2, 4, 8, 16)
POOL_WIDTH = D_MODEL // 2
POOL_GROUP_W = POOL_WIDTH // len(POOL_WINDOWS)
MIX_AB_IN = ATT_WIDTH + 2 * KV_WIDTH + POOL_WIDTH
LRU_WIDTH = D_MODEL
LRU_BLOCKS = 8
LRU_BLOCK_W = LRU_WIDTH // LRU_BLOCKS
LRU_C = 8.0
CONV_W = 4
CONV_LEFT = (CONV_W - 1) // 2
D_FF = 2816
N_MOD = 9
LN_EPS = 1e-5
NEG_INF = -1e30
DEEPNORM_ALPHA = (2 * DEPTH) ** 0.25

V7X_LANES = 128
V7X_SUBLANES = 8
V7X_VMEM_BYTES = 64 * 1024 * 1024
V7X_VMEM_USABLE_BYTES = 60000 * 1024

F32 = jnp.float32
BF16 = jnp.bfloat16

POOL_HALO = V7X_SUBLANES
PERM_STEPS = 16
KEY_SLAB_W = ATT_KV_HEADS * V7X_LANES
VAL_SLAB_W = 2 * ATT_KV_HEADS * V7X_LANES
LRU_PROJ_CHUNK = 256
LRU_SCAN_LAG = 2
SCORE_LOOKAHEAD = 2
EPILOGUE_BLOCKS = 2
MOD_ROWS = 24


def _nbytes(shape, dtype):
    n = 1
    for s in shape:
        n *= s
    return n * jnp.dtype(dtype).itemsize


def _vmem_limit(pipelined, resident, temporaries):
    est = 2 * sum(pipelined) + sum(resident) + temporaries
    return int(min(V7X_VMEM_USABLE_BYTES, max(est * 5 // 4, 16 * 1024 * 1024)))


def _const_spec(shape):
    nd = len(shape)
    return pl.BlockSpec(shape, lambda *_: (0,) * nd, pipeline_mode=pl.Buffered(1))


def _tok_spec(tm, width, nt):
    return pl.BlockSpec((tm, width), lambda b, t: (b * nt + t, 0))


def _mod_spec():
    return pl.BlockSpec((1, N_MOD, D_MODEL), lambda b, t: (b, 0, 0))


def _layer_norm(z, g, b):
    mu = jnp.mean(z, axis=-1, keepdims=True)
    zc = z - mu
    var = jnp.mean(zc * zc, axis=-1, keepdims=True)
    return zc * lax.rsqrt(var + LN_EPS) * g + b


def _gelu_tanh(x):
    return 0.5 * x * (1.0 + jnp.tanh(0.7978845608028654 * (x + 0.044715 * (x * x * x))))


def _mod_kernel(c_ref, w_ref, b_ref, o_ref):
    c = c_ref[...]
    a_hi, a_lo = _split_bf16(c * jax.nn.sigmoid(c))
    w_hi, w_lo = _split_bf16(w_ref[0])
    acc = jnp.dot(a_hi, w_lo, preferred_element_type=F32) + jnp.dot(a_lo, w_hi, preferred_element_type=F32)
    o_ref[0] = acc + jnp.dot(a_hi, w_hi, preferred_element_type=F32) + b_ref[0]


def _modulation(c_all, w_mod, b_mod):
    tn = 1024
    n_out = N_MOD * D_MODEL
    blocks = [_nbytes((1, D_MODEL, tn), F32), _nbytes((1, MOD_ROWS, tn), F32)]
    return pl.pallas_call(
        _mod_kernel,
        grid=(DEPTH, n_out // tn),
        in_specs=[
            pl.BlockSpec((MOD_ROWS, D_MODEL), lambda l, j: (0, 0)),
            pl.BlockSpec((1, D_MODEL, tn), lambda l, j: (l, 0, j)),
            pl.BlockSpec((1, 1, tn), lambda l, j: (l, 0, j)),
        ],
        out_specs=pl.BlockSpec((1, MOD_ROWS, tn), lambda l, j: (l, 0, j)),
        out_shape=jax.ShapeDtypeStruct((DEPTH, MOD_ROWS, n_out), F32),
        compiler_params=pltpu.CompilerParams(
            dimension_semantics=("arbitrary", "arbitrary"),
            vmem_limit_bytes=_vmem_limit(blocks, [], 4 * blocks[0])),
        name="modulation",
    )(c_all, w_mod, b_mod.reshape(DEPTH, 1, n_out))


def _ffn_kernel(x_ref, mod_ref, wg_ref, wu_ref, wd_ref, lng_ref, lnb_ref, o_ref, *, j0, sub):
    shift = mod_ref[0, j0:j0 + 1, :]
    scale = mod_ref[0, j0 + 1:j0 + 2, :]
    gate = mod_ref[0, j0 + 2:j0 + 3, :]
    sizes = [sub] * (x_ref.shape[0] // sub)
    starts = [s * sub for s in range(len(sizes))]
    nsub = len(sizes)

    def gate_up(s):
        x = x_ref[starts[s]:starts[s] + sizes[s], :]
        xin = (x * (1.0 + scale) + shift).astype(BF16)
        return (jnp.dot(xin, wg_ref[...], preferred_element_type=F32),
                jnp.dot(xin, wu_ref[...], preferred_element_type=F32))

    pending = gate_up(0)
    for s in range(nsub):
        g, u = pending
        if s + 1 < nsub:
            pending = gate_up(s + 1)
        rows = slice(starts[s], starts[s] + sizes[s])
        a = (g * jax.nn.sigmoid(g) * u).astype(BF16)
        y = jnp.dot(a, wd_ref[...], preferred_element_type=F32)
        z = DEEPNORM_ALPHA * x_ref[rows, :] + (0.5 * gate) * y
        o_ref[rows, :] = _layer_norm(z, lng_ref[...], lnb_ref[...])


def _ffn(x, mod, j0, which, wg, wu, wd, ln_g, ln_b, *, nb, length, tm=1024, sub=256):
    nt = length // tm
    tok = _nbytes((tm, D_MODEL), F32)
    weights = [_nbytes(w.shape[2:], BF16) for w in (wg, wu, wd)]
    temps = 3 * _nbytes((tm, D_FF), F32) + 4 * tok
    wspec = lambda w: pl.BlockSpec((None, None) + w.shape[2:], lambda b, t: which + (0, 0),
                                   pipeline_mode=pl.Buffered(1))
    return pl.pallas_call(
        functools.partial(_ffn_kernel, j0=j0, sub=sub),
        grid=(nb, nt),
        in_specs=[
            _tok_spec(tm, D_MODEL, nt),
            _mod_spec(),
            wspec(wg), wspec(wu), wspec(wd),
            _const_spec((1, D_MODEL)), _const_spec((1, D_MODEL)),
        ],
        out_specs=_tok_spec(tm, D_MODEL, nt),
        out_shape=jax.ShapeDtypeStruct((nb * length, D_MODEL), F32),
        compiler_params=pltpu.CompilerParams(
            dimension_semantics=("arbitrary", "arbitrary"),
            vmem_limit_bytes=_vmem_limit([tok, tok], weights, temps)),
        name="ffn",
    )(x, mod, wg, wu, wd, ln_g, ln_b)


def _dup_halves(z, lane):
    zr = pltpu.roll(z, HEAD_DIM, 1)
    lo = lane < HEAD_DIM
    return jnp.where(lo, z, zr), jnp.where(lo, zr, z)


def _attn_inproj_kernel(*refs, rope):
    if rope:
        x_ref, mod_ref, w_ref, cos_ref, sin_ref, q_ref, k_ref, v_ref, u_ref = refs
    else:
        x_ref, mod_ref, w_ref, q_ref, k_ref, v_ref, u_ref = refs
    x = x_ref[...]
    xin = (x * (1.0 + mod_ref[0, 4:5, :]) + mod_ref[0, 3:4, :]).astype(BF16)
    p = jnp.dot(xin, w_ref[...], preferred_element_type=F32)
    lane = lax.broadcasted_iota(jnp.int32, (x.shape[0], V7X_LANES), 1)
    first_half = (lane & (HEAD_DIM - 1)) < HEAD_DIM // 2

    def rot(z):
        if not rope:
            return z
        zr = jnp.where(first_half, pltpu.roll(z, V7X_LANES - HEAD_DIM // 2, 1),
                       pltpu.roll(z, HEAD_DIM // 2, 1))
        return z * cos_ref[...] + zr * sin_ref[...]

    for c in range(ATT_WIDTH // V7X_LANES):
        sl = slice(c * V7X_LANES, (c + 1) * V7X_LANES)
        q_ref[:, sl] = (rot(p[:, sl]) * (ATT_SCALE * LOG2E)).astype(BF16)
    k0, k1 = _dup_halves(rot(p[:, ATT_WIDTH:ATT_WIDTH + KV_WIDTH]), lane)
    k_ref[...] = jnp.concatenate([k0, k1], axis=-1).T.astype(BF16)
    v = p[:, ATT_WIDTH + KV_WIDTH:ATT_WIDTH + 2 * KV_WIDTH]
    vr = pltpu.roll(v, HEAD_DIM, 1)
    lo = lane < HEAD_DIM
    slabs = (jnp.where(lo, v, 1.0), jnp.where(lo, 1.0, vr), jnp.where(lo, vr, 1.0), jnp.where(lo, 1.0, v))
    for i, slab in enumerate(slabs):
        v_ref[:, i * V7X_LANES:(i + 1) * V7X_LANES] = slab.astype(BF16)
    u_ref[...] = p[:, ATT_WIDTH + 2 * KV_WIDTH:]


def _attn_inproj(x, mod, w_in, cos, sin, *, nb, length, tm=512):
    nt = length // tm
    rope = cos is not None
    tok = _nbytes((tm, D_MODEL), F32)
    outs = [_nbytes((tm, ATT_WIDTH), BF16), _nbytes((tm, KEY_SLAB_W + VAL_SLAB_W), BF16),
            _nbytes((tm, POOL_WIDTH), F32)]
    in_specs = [_tok_spec(tm, D_MODEL, nt), _mod_spec(), _const_spec(w_in.shape)]
    args = [x, mod, w_in]
    if rope:
        in_specs += [pl.BlockSpec((tm, V7X_LANES), lambda b, t: (t, 0))] * 2
        args += [cos, sin]
    rows = nb * length
    return pl.pallas_call(
        functools.partial(_attn_inproj_kernel, rope=rope),
        grid=(nb, nt),
        in_specs=in_specs,
        out_specs=[
            _tok_spec(tm, ATT_WIDTH, nt),
            pl.BlockSpec((KEY_SLAB_W, tm), lambda b, t: (0, b * nt + t)),
            _tok_spec(tm, VAL_SLAB_W, nt),
            _tok_spec(tm, POOL_WIDTH, nt),
        ],
        out_shape=[
            jax.ShapeDtypeStruct((rows, ATT_WIDTH), BF16),
            jax.ShapeDtypeStruct((KEY_SLAB_W, rows), BF16),
            jax.ShapeDtypeStruct((rows, VAL_SLAB_W), BF16),
            jax.ShapeDtypeStruct((rows, POOL_WIDTH), F32),
        ],
        compiler_params=pltpu.CompilerParams(
            dimension_semantics=("arbitrary", "arbitrary"),
            vmem_limit_bytes=_vmem_limit([tok] + outs, [_nbytes(w_in.shape, BF16)],
                                         3 * _nbytes((tm, MIX_AB_IN), F32))),
        name="attn_inproj",
    )(*args)


def _attend_scores(q_blk, keys, kh):
    lane = lax.broadcasted_iota(jnp.int32, (BLOCK, V7X_LANES), 1)
    lo = lane < HEAD_DIM
    zero = jnp.zeros((BLOCK, V7X_LANES), BF16)
    parts = []
    for g in range(ATT_GROUPS):
        c = (kh * ATT_GROUPS + g) // 2
        qc = q_blk[:, c * V7X_LANES:(c + 1) * V7X_LANES]
        parts.append(jnp.where(lo if g % 2 == 0 else jnp.logical_not(lo), qc, zero))
    qs = jnp.concatenate(parts, axis=0)
    return [jnp.dot(qs, k, preferred_element_type=F32) for k in keys]


def _attend_finish(scores, values_even, values_odd, segments, sink_ref, kh):
    lane = lax.broadcasted_iota(jnp.int32, (BLOCK, V7X_LANES), 1)
    lo = lane < HEAD_DIM
    probs = ([[] for _ in scores], [[] for _ in scores])
    sink_terms = []
    for g in range(ATT_GROUPS):
        rows = slice(g * BLOCK, (g + 1) * BLOCK)
        segs = [[s[rows, a:b] if bias is None else s[rows, a:b] + bias for a, b, bias in seg]
                for s, seg in zip(scores, segments)]
        sink = sink_ref[kh * ATT_GROUPS + g] * LOG2E
        chunks = [x[:, c:c + V7X_LANES] for sl in segs for x in sl
                  for c in range(0, x.shape[1], V7X_LANES)]
        m = functools.reduce(jnp.maximum, chunks).max(axis=-1, keepdims=True)
        m = jnp.maximum(m, sink)
        sink_terms.append(jnp.exp2(sink - m))
        for i, sl in enumerate(segs):
            e = [jnp.exp2(x - m).astype(BF16) for x in sl]
            probs[g % 2][i].append(e[0] if len(e) == 1 else jnp.concatenate(e, axis=-1))

    def weighted_values(ps, vals):
        o = None
        for p, v in zip(ps, vals):
            t = jnp.dot(jnp.concatenate(p, axis=0), v, preferred_element_type=F32)
            o = t if o is None else o + t
        return o

    o_par = (weighted_values(probs[0], values_even), weighted_values(probs[1], values_odd))
    outs = []
    for g in range(ATT_GROUPS):
        o = o_par[g % 2][(g // 2) * BLOCK:(g // 2 + 1) * BLOCK]
        rinv = 1.0 / (o + sink_terms[g])
        outs.append(o * pltpu.roll(rinv, HEAD_DIM, 1))
    return jnp.where(lo, outs[0], outs[1]), jnp.where(lo, outs[2], outs[3])


def _pool_sums(ubuf_ref, nblk, band_ref):
    u_hi, u_lo = _split_bf16(ubuf_ref[...])
    sums = {}
    for j in range(nblk):
        win = slice(j * BLOCK, (j + 2) * BLOCK)
        for gi in range(len(POOL_WINDOWS)):
            cols = slice(gi * POOL_GROUP_W, (gi + 1) * POOL_GROUP_W)
            band = band_ref[gi]
            sums[j, gi] = (jnp.dot(band, u_hi[win, cols], preferred_element_type=F32)
                           + jnp.dot(band, u_lo[win, cols], preferred_element_type=F32))
    return sums


def _pool_finish(sums, ubuf_ref, nblk, pos0, length, wpool_ref, pscale_ref, mix_scr):
    for j in range(nblk):
        pos = pos0 + j * BLOCK + lax.broadcasted_iota(jnp.int32, (BLOCK, 1), 0)
        rows = slice(POOL_HALO + j * BLOCK, POOL_HALO + (j + 1) * BLOCK)
        for gi, w in enumerate(POOL_WINDOWS):
            r = w // 2
            cols = slice(gi * POOL_GROUP_W, (gi + 1) * POOL_GROUP_W)
            cnt = (jnp.minimum(pos + r, length - 1) - jnp.maximum(pos - r, 0) + 1).astype(F32)
            d = sums[j, gi] / cnt - ubuf_ref[rows, cols]
            y = jnp.dot(d.astype(BF16), wpool_ref[gi], preferred_element_type=F32)
            mix_scr[j * BLOCK:(j + 1) * BLOCK, ATT_WIDTH + gi * POOL_GROUP_W:
                    ATT_WIDTH + (gi + 1) * POOL_GROUP_W] = (y * pscale_ref[:, cols]).astype(BF16)


def _split_bf16(x):
    hi = x.astype(BF16)
    return hi, (x - hi.astype(F32)).astype(BF16)


def _pool_bands():
    i = jnp.arange(BLOCK)[:, None]
    c = jnp.arange(2 * BLOCK)[None, :]
    return jnp.stack([(jnp.abs(c - POOL_HALO - i) <= w // 2) for w in POOL_WINDOWS]).astype(BF16)


def _attn_out(mix_scr, rows, wout_ref, h_ref, gate, lng_ref, lnb_ref, o_ref):
    y = jnp.dot(mix_scr[rows, :], wout_ref[...], preferred_element_type=F32)
    z = DEEPNORM_ALPHA * h_ref[rows, :] + gate * y
    o_ref[rows, :] = _layer_norm(z, lng_ref[...], lnb_ref[...])


def _attn_lat_kernel(sink_ref, q_ref, kc_ref, kp_ref, kn_ref, vc_ref, vp_ref, vn_ref,
                     kctx_ref, vctx_ref, uc_ref, up_ref, un_ref, h_ref, mod_ref, bias_ref, band_ref,
                     wpool_ref, pscale_ref, wout_ref, lng_ref, lnb_ref, o_ref,
                     kbuf, vbuf, ubuf, mix_scr, *, tq, length):
    t = pl.program_id(1)
    last = pl.num_programs(1) - 1
    nblk = tq // BLOCK
    kbuf[:, 0:BLOCK] = kp_ref[...]
    kbuf[:, BLOCK:BLOCK + tq] = kc_ref[...]
    kbuf[:, BLOCK + tq:] = kn_ref[...]
    vbuf[0:BLOCK] = vp_ref[...]
    vbuf[BLOCK:BLOCK + tq] = vc_ref[...]
    vbuf[BLOCK + tq:] = vn_ref[...]
    ubuf[0:POOL_HALO] = jnp.where(t == 0, 0.0, up_ref[...])
    ubuf[POOL_HALO:POOL_HALO + tq] = uc_ref[...]
    ubuf[POOL_HALO + tq:2 * POOL_HALO + tq] = jnp.where(t == last, 0.0, un_ref[...])
    ubuf[2 * POOL_HALO + tq:] = jnp.zeros((2 * BLOCK - 2 * POOL_HALO, POOL_WIDTH), F32)
    lanes = lambda i: slice(i * V7X_LANES, (i + 1) * V7X_LANES)
    gate = mod_ref[0, 5:6, :]
    units = [(j, kh) for j in range(nblk) for kh in range(ATT_KV_HEADS)]

    def scores_of(j, kh):
        win = slice(j * BLOCK, j * BLOCK + 3 * BLOCK)
        return _attend_scores(q_ref[j * BLOCK:(j + 1) * BLOCK, :],
                              [kbuf[lanes(kh), win], kctx_ref[lanes(kh), :]], kh)

    sums = _pool_sums(ubuf, nblk, band_ref)
    pending = [scores_of(*u) for u in units[:SCORE_LOOKAHEAD]]
    _pool_finish(sums, ubuf, nblk, t * tq, length, wpool_ref, pscale_ref, mix_scr)
    for n, (j, kh) in enumerate(units):
        scores = pending.pop(0)
        if n + SCORE_LOOKAHEAD < len(units):
            pending.append(scores_of(*units[n + SCORE_LOOKAHEAD]))
        blk = t * nblk + j
        bidx = jnp.where(blk == 0, 0, jnp.where(blk == length // BLOCK - 1, 2, 1))
        bias = bias_ref[bidx]
        segments = [[(0, BLOCK, bias[:, :BLOCK]), (BLOCK, 2 * BLOCK, None),
                     (2 * BLOCK, 3 * BLOCK, bias[:, BLOCK:])], [(0, CTX_LEN, None)]]
        win = slice(j * BLOCK, j * BLOCK + 3 * BLOCK)
        rows = slice(j * BLOCK, (j + 1) * BLOCK)
        c0, c1 = _attend_finish(
            scores, [vbuf[win, lanes(2 * kh)], vctx_ref[:, lanes(2 * kh)]],
            [vbuf[win, lanes(2 * kh + 1)], vctx_ref[:, lanes(2 * kh + 1)]],
            segments, sink_ref, kh)
        mix_scr[rows, lanes(2 * kh)] = c0.astype(BF16)
        mix_scr[rows, lanes(2 * kh + 1)] = c1.astype(BF16)
        per_epilogue = EPILOGUE_BLOCKS * ATT_KV_HEADS
        if n > 0 and n % per_epilogue == 0:
            e = n // per_epilogue - 1
            done = slice(e * EPILOGUE_BLOCKS * BLOCK, (e + 1) * EPILOGUE_BLOCKS * BLOCK)
            _attn_out(mix_scr, done, wout_ref, h_ref, gate, lng_ref, lnb_ref, o_ref)
    _attn_out(mix_scr, slice(tq - EPILOGUE_BLOCKS * BLOCK, tq), wout_ref, h_ref, gate, lng_ref,
              lnb_ref, o_ref)


def _attn_ctx_kernel(sink_ref, q_ref, kctx_ref, vctx_ref, uc_ref, h_ref, mod_ref, band_ref,
                     wpool_ref, pscale_ref, wout_ref, lng_ref, lnb_ref, o_ref,
                     ubuf, mix_scr, *, tq):
    ubuf[0:POOL_HALO] = jnp.zeros((POOL_HALO, POOL_WIDTH), F32)
    ubuf[POOL_HALO:POOL_HALO + tq] = uc_ref[...]
    ubuf[POOL_HALO + tq:] = jnp.zeros((2 * BLOCK - POOL_HALO, POOL_WIDTH), F32)
    lanes = lambda i: slice(i * V7X_LANES, (i + 1) * V7X_LANES)
    nblk = tq // BLOCK
    units = [(j, kh) for j in range(nblk) for kh in range(ATT_KV_HEADS)]
    scores_of = lambda j, kh: _attend_scores(q_ref[j * BLOCK:(j + 1) * BLOCK, :],
                                             [kctx_ref[lanes(kh), :]], kh)
    sums = _pool_sums(ubuf, nblk, band_ref)
    pending = [scores_of(*u) for u in units[:SCORE_LOOKAHEAD]]
    _pool_finish(sums, ubuf, nblk, 0, tq, wpool_ref, pscale_ref, mix_scr)
    for n, (j, kh) in enumerate(units):
        scores = pending.pop(0)
        if n + SCORE_LOOKAHEAD < len(units):
            pending.append(scores_of(*units[n + SCORE_LOOKAHEAD]))
        rows = slice(j * BLOCK, (j + 1) * BLOCK)
        c0, c1 = _attend_finish(scores, [vctx_ref[:, lanes(2 * kh)]],
                                [vctx_ref[:, lanes(2 * kh + 1)]], [[(0, CTX_LEN, None)]],
                                sink_ref, kh)
        mix_scr[rows, lanes(2 * kh)] = c0.astype(BF16)
        mix_scr[rows, lanes(2 * kh + 1)] = c1.astype(BF16)
    _attn_out(mix_scr, slice(0, tq), wout_ref, h_ref, mod_ref[0, 5:6, :], lng_ref, lnb_ref, o_ref)


def _window_bias():
    qi = jnp.arange(BLOCK)[:, None]
    kc = jnp.arange(3 * BLOCK)[None, :]
    band = jnp.abs(qi + BLOCK - kc) <= WINDOW
    first = band & (kc >= BLOCK)
    final = band & (kc < 2 * BLOCK)
    full = jnp.where(jnp.stack([first, band, final]), 0.0, NEG_INF).astype(F32)
    return jnp.concatenate([full[:, :, :BLOCK], full[:, :, 2 * BLOCK:]], axis=-1)


def _smem_spec():
    return pl.BlockSpec(memory_space=pltpu.SMEM)


def _attn_lat(sink, q, k2, v2, kctx, vctx, u, h, mod, wpool, pscale, wout, ln_g, ln_b, *, tq=512):
    nt = SEQ // tq
    bpt = tq // BLOCK
    bps = SEQ // BLOCK
    hpt = tq // POOL_HALO
    hps = SEQ // POOL_HALO
    cur = lambda w: pl.BlockSpec((tq, w), lambda b, t: (b * nt + t, 0))
    prev_blk = lambda rows, w, per_tile, per_samp: pl.BlockSpec(
        (rows, w), lambda b, t: (b * per_samp + jnp.maximum(t * per_tile - 1, 0), 0))
    next_blk = lambda rows, w, per_tile, per_samp: pl.BlockSpec(
        (rows, w), lambda b, t: (b * per_samp + jnp.minimum((t + 1) * per_tile, per_samp - 1), 0))
    ctx_blk = lambda w: pl.BlockSpec((CTX_LEN, w), lambda b, t: (b, 0))
    kv_specs = lambda w: [cur(w), prev_blk(BLOCK, w, bpt, bps), next_blk(BLOCK, w, bpt, bps)]
    key_specs = [
        pl.BlockSpec((KEY_SLAB_W, tq), lambda b, t: (0, b * nt + t)),
        pl.BlockSpec((KEY_SLAB_W, BLOCK), lambda b, t: (0, b * bps + jnp.maximum(t * bpt - 1, 0))),
        pl.BlockSpec((KEY_SLAB_W, BLOCK),
                     lambda b, t: (0, b * bps + jnp.minimum((t + 1) * bpt, bps - 1))),
    ]
    bias = _window_bias()
    bands = _pool_bands()
    tok = _nbytes((tq, D_MODEL), F32)
    kv_rows = tq + 2 * BLOCK
    u_rows = tq + 2 * BLOCK
    pipelined = [_nbytes((tq, ATT_WIDTH), BF16), 2 * _nbytes((kv_rows, KEY_SLAB_W + VAL_SLAB_W), BF16),
                 _nbytes((tq + 2 * POOL_HALO, POOL_WIDTH), F32), 2 * tok]
    resident = [_nbytes(bias.shape, F32), _nbytes(bands.shape, BF16), _nbytes(wpool.shape, BF16),
                _nbytes(wout.shape, BF16)]
    scratch = [_nbytes((kv_rows, KEY_SLAB_W + VAL_SLAB_W), BF16),
               _nbytes((u_rows, POOL_WIDTH), F32), _nbytes((tq, D_MODEL), BF16)]
    return pl.pallas_call(
        functools.partial(_attn_lat_kernel, tq=tq, length=SEQ),
        grid=(BATCH, nt),
        in_specs=[_smem_spec(), cur(ATT_WIDTH)] + key_specs + kv_specs(VAL_SLAB_W) + [
            pl.BlockSpec((KEY_SLAB_W, CTX_LEN), lambda b, t: (0, b)), ctx_blk(VAL_SLAB_W),
            cur(POOL_WIDTH), prev_blk(POOL_HALO, POOL_WIDTH, hpt, hps),
            next_blk(POOL_HALO, POOL_WIDTH, hpt, hps),
            cur(D_MODEL), _mod_spec(),
            _const_spec(bias.shape), _const_spec(bands.shape),
            _const_spec(wpool.shape), _const_spec((1, POOL_WIDTH)),
            _const_spec(wout.shape), _const_spec((1, D_MODEL)), _const_spec((1, D_MODEL)),
        ],
        out_specs=cur(D_MODEL),
        out_shape=jax.ShapeDtypeStruct((BATCH * SEQ, D_MODEL), F32),
        scratch_shapes=[
            pltpu.VMEM((KEY_SLAB_W, kv_rows), BF16), pltpu.VMEM((kv_rows, VAL_SLAB_W), BF16),
            pltpu.VMEM((u_rows, POOL_WIDTH), F32), pltpu.VMEM((tq, D_MODEL), BF16),
        ],
        compiler_params=pltpu.CompilerParams(
            dimension_semantics=("arbitrary", "arbitrary"),
            vmem_limit_bytes=_vmem_limit(pipelined, resident + scratch, 8 * tok)),
        name="attn_latent",
    )(sink, q, k2, k2, k2, v2, v2, v2, kctx, vctx, u, u, u, h, mod, bias, bands, wpool, pscale, wout,
      ln_g, ln_b)


def _attn_ctx(sink, q, kctx, vctx, u, h, mod, wpool, pscale, wout, ln_g, ln_b):
    tq = CTX_LEN
    blk = lambda w: pl.BlockSpec((tq, w), lambda b, t: (b, 0))
    bands = _pool_bands()
    tok = _nbytes((tq, D_MODEL), F32)
    u_rows = tq + 2 * BLOCK
    pipelined = [_nbytes((tq, ATT_WIDTH), BF16), _nbytes((tq, KEY_SLAB_W + VAL_SLAB_W), BF16),
                 _nbytes((tq, POOL_WIDTH), F32), 2 * tok]
    resident = [_nbytes(bands.shape, BF16), _nbytes(wpool.shape, BF16), _nbytes(wout.shape, BF16)]
    scratch = [_nbytes((u_rows, POOL_WIDTH), F32), _nbytes((tq, D_MODEL), BF16)]
    return pl.pallas_call(
        functools.partial(_attn_ctx_kernel, tq=tq),
        grid=(BATCH, 1),
        in_specs=[
            _smem_spec(),
            blk(ATT_WIDTH), pl.BlockSpec((KEY_SLAB_W, tq), lambda b, t: (0, b)),
            blk(VAL_SLAB_W), blk(POOL_WIDTH), blk(D_MODEL),
            pl.BlockSpec((1, N_MOD, D_MODEL), lambda b, t: (0, 0, 0)),
            _const_spec(bands.shape), _const_spec(wpool.shape), _const_spec((1, POOL_WIDTH)),
            _const_spec(wout.shape), _const_spec((1, D_MODEL)), _const_spec((1, D_MODEL)),
        ],
        out_specs=blk(D_MODEL),
        out_shape=jax.ShapeDtypeStruct((BATCH * CTX_LEN, D_MODEL), F32),
        scratch_shapes=[
            pltpu.VMEM((u_rows, POOL_WIDTH), F32), pltpu.VMEM((tq, D_MODEL), BF16),
        ],
        compiler_params=pltpu.CompilerParams(
            dimension_semantics=("arbitrary", "arbitrary"),
            vmem_limit_bytes=_vmem_limit(pipelined, resident + scratch, 8 * tok)),
        name="attn_context",
    )(sink, q, kctx, vctx, u, h, mod, bands, wpool, pscale, wout, ln_g, ln_b)


def _row_perm(to_time_major):
    n = BATCH * PERM_STEPS
    r = lax.broadcasted_iota(jnp.int32, (n, n), 0)
    c = lax.broadcasted_iota(jnp.int32, (n, n), 1)
    if to_time_major:
        hit = ((r // BATCH) == (c % PERM_STEPS)) & ((r % BATCH) == (c // PERM_STEPS))
    else:
        hit = ((r // PERM_STEPS) == (c % BATCH)) & ((r % PERM_STEPS) == (c // BATCH))
    return jnp.where(hit, 1.0, 0.0).astype(BF16)


def _lru_time_major_input(x_ref, scale_ref, shift_ref, tt):
    xin = (x_ref[...] * (1.0 + scale_ref[...]) + shift_ref[...]).astype(BF16)
    perm = _row_perm(to_time_major=True)
    parts = []
    for tau in range(tt // PERM_STEPS):
        sub = xin[:, tau * PERM_STEPS:(tau + 1) * PERM_STEPS, :].reshape(BATCH * PERM_STEPS, D_MODEL)
        parts.append(jnp.dot(perm, sub, preferred_element_type=F32).astype(BF16))
    return jnp.concatenate(parts, axis=0)


def _lru_in_fwd_kernel(*refs, tt, with_gate):
    (x_ref, scale_ref, shift_ref, w_ref, h0_ref, cw_ref, cb_ref, wrg_ref, ba_ref, bx_ref,
     lam_ref) = refs[:11]
    outs = refs[11:]
    if with_gate:
        gate_ref, uconv_ref, s_ref, hfin_ref, ring, ulast, h_scr, a_scr, b_scr = outs
    else:
        uconv_ref, s_ref, hfin_ref, ring, ulast, h_scr, a_scr, b_scr = outs
    i = pl.program_id(0)
    ntiles = pl.num_programs(0) - LRU_SCAN_LAG

    @pl.when(i == 0)
    def _():
        ring[...] = jnp.zeros(ring.shape, F32)
        ulast[...] = jnp.zeros(ulast.shape, F32)
        h_scr[...] = h0_ref[...]

    xt = _lru_time_major_input(x_ref, scale_ref, shift_ref, tt)
    slot_new = lax.rem(i, LRU_SCAN_LAG + 1)
    slot_cur = lax.rem(i + 1, LRU_SCAN_LAG + 1)
    slot_nxt = lax.rem(i + 2, LRU_SCAN_LAG + 1)

    def project_chunk(c):
        cols = slice(c * LRU_PROJ_CHUNK, (c + 1) * LRU_PROJ_CHUNK)
        val = jnp.dot(xt, w_ref[:, cols], preferred_element_type=F32)
        val = val.reshape(tt, BATCH, LRU_PROJ_CHUNK)
        if with_gate and c < LRU_WIDTH // LRU_PROJ_CHUNK:
            gate_ref[:, :, cols] = val.astype(BF16)
        else:
            ucols = slice(cols.start % LRU_WIDTH, cols.start % LRU_WIDTH + LRU_PROJ_CHUNK)
            ring[slot_new, :, :, ucols] = val

    nchunks = w_ref.shape[1] // LRU_PROJ_CHUNK
    ahead = nchunks - (LRU_BLOCKS - 2)
    for c in range(ahead):
        project_chunk(c)

    cur = ring[slot_cur]
    nxt = jnp.where(i == ntiles + LRU_SCAN_LAG - 1, 0.0, ring[slot_nxt, 0:CONV_W - 1 - CONV_LEFT])
    u = _lru_conv(cur, ulast[...], nxt, cw_ref, cb_ref, tt)
    uconv_ref[...] = u
    _lru_coeffs(u, wrg_ref, ba_ref, bx_ref, lam_ref, a_scr, b_scr, tt,
                interleave=[functools.partial(project_chunk, c) for c in range(max(ahead, 0), nchunks)])
    ulast[...] = cur[tt - CONV_LEFT:tt]

    def step(k, h):
        h = a_scr[k] * h + b_scr[k]
        s_ref[k] = h
        return h

    h_prev = h_scr[...]
    h = lax.fori_loop(0, tt, step, h_prev, unroll=4)
    h = jnp.where(i >= LRU_SCAN_LAG, h, h_prev)
    h_scr[...] = h
    hfin_ref[...] = h


def _lru_in_fwd(x3, scale, shift, w, h0, conv_w, conv_b, wrg, ba, bx, lam, *, with_gate, tt=32):
    length = x3.shape[1]
    ntiles = length // tt
    blk = _nbytes((tt, BATCH, LRU_WIDTH), F32)
    proj_tile = lambda i: jnp.minimum(i, ntiles - 1)
    scan_tile = lambda i: jnp.maximum(i - LRU_SCAN_LAG, 0)
    tm_spec = lambda tile: pl.BlockSpec((tt, BATCH, LRU_WIDTH), lambda i: (tile(i), 0, 0))
    out_specs = [tm_spec(scan_tile), tm_spec(scan_tile),
                 pl.BlockSpec((BATCH, LRU_WIDTH), lambda i: (0, 0))]
    out_shape = [jax.ShapeDtypeStruct((length, BATCH, LRU_WIDTH), F32)] * 2 + [
        jax.ShapeDtypeStruct((BATCH, LRU_WIDTH), F32)]
    outs = [blk, blk]
    if with_gate:
        out_specs = [tm_spec(proj_tile)] + out_specs
        out_shape = [jax.ShapeDtypeStruct((length, BATCH, LRU_WIDTH), BF16)] + out_shape
        outs.append(blk // 2)
    res = pl.pallas_call(
        functools.partial(_lru_in_fwd_kernel, tt=tt, with_gate=with_gate),
        grid=(ntiles + LRU_SCAN_LAG,),
        in_specs=[pl.BlockSpec((BATCH, tt, D_MODEL), lambda i: (0, proj_tile(i), 0)),
                  _const_spec(scale.shape), _const_spec(shift.shape), _const_spec(w.shape),
                  _const_spec((BATCH, LRU_WIDTH)),
                  _const_spec((CONV_W, LRU_WIDTH)), _const_spec((1, LRU_WIDTH)),
                  _const_spec(wrg.shape),
                  _const_spec((1, LRU_WIDTH)), _const_spec((1, LRU_WIDTH)), _const_spec((1, LRU_WIDTH))],
        out_specs=out_specs,
        out_shape=out_shape,
        scratch_shapes=[
            pltpu.VMEM((LRU_SCAN_LAG + 1, tt, BATCH, LRU_WIDTH), F32),
            pltpu.VMEM((CONV_LEFT, BATCH, LRU_WIDTH), F32),
            pltpu.VMEM((BATCH, LRU_WIDTH), F32),
            pltpu.VMEM((tt, BATCH, LRU_WIDTH), F32), pltpu.VMEM((tt, BATCH, LRU_WIDTH), F32),
        ],
        compiler_params=pltpu.CompilerParams(
            dimension_semantics=("arbitrary",),
            vmem_limit_bytes=_vmem_limit(
                [blk] + outs, [_nbytes(w.shape, BF16), _nbytes(wrg.shape, BF16),
                               (LRU_SCAN_LAG + 3) * blk], 8 * blk)),
        name="lru_in_fwd",
    )(x3, scale, shift, w, h0, conv_w, conv_b, wrg, ba, bx, lam)
    return res if with_gate else (None,) + tuple(res)


def _lru_conv(cur, prev, nxt, cw_ref, cb_ref, tt):
    ext = jnp.concatenate([prev, cur, nxt], axis=0)
    u = cb_ref[...].reshape(1, 1, LRU_WIDTH)
    for tap in range(CONV_W):
        u = u + ext[tap:tap + tt] * cw_ref[tap:tap + 1, :].reshape(1, 1, LRU_WIDTH)
    return u


def _lru_coeffs(u, wrg_ref, ba_ref, bx_ref, lam_ref, a_scr, b_scr, tt, interleave=()):
    u2 = u.reshape(tt * BATCH, LRU_WIDTH)
    ub16 = u2.astype(BF16)
    col = lambda blk: slice(blk * LRU_BLOCK_W, (blk + 1) * LRU_BLOCK_W)
    lam = lam_ref[...]
    half_decay2 = (-0.5 * LRU_C * LOG2E) * (jnp.maximum(-lam, 0.0) + jnp.log1p(jnp.exp(-jnp.abs(lam))))
    half_ba = 0.5 * ba_ref[...]
    half_bx = 0.5 * bx_ref[...]
    rgs = []
    for blk in range(LRU_BLOCKS + 1):
        if blk < LRU_BLOCKS:
            rgs.append(jnp.dot(ub16[:, col(blk)], wrg_ref[blk], preferred_element_type=F32))
            if blk < len(interleave):
                interleave[blk]()
        if blk == 0:
            continue
        blk -= 1
        rg = rgs[blk]
        cols = col(blk)
        tr = jnp.tanh(rg[:, :LRU_BLOCK_W] + half_ba[:, cols])
        tg = jnp.tanh(rg[:, LRU_BLOCK_W:] + half_bx[:, cols])
        hd = half_decay2[:, cols]
        a = jnp.exp2(hd + hd * tr)
        x = 1.0 - a * a
        hu = 0.5 * u2[:, cols]
        b = jnp.where(x > 0.0, x * lax.rsqrt(x), 0.0) * (hu + hu * tg)
        a_scr[:, :, cols] = a.reshape(tt, BATCH, LRU_BLOCK_W)
        b_scr[:, :, cols] = b.reshape(tt, BATCH, LRU_BLOCK_W)


def _lru_bwd_state_kernel(u_ref, wrg_ref, ba_ref, bx_ref, lam_ref, hfin_ref, a_scr, b_scr, *, tt):
    @pl.when(pl.program_id(0) == 0)
    def _():
        hfin_ref[...] = jnp.zeros(hfin_ref.shape, F32)

    _lru_coeffs(u_ref[...], wrg_ref, ba_ref, bx_ref, lam_ref, a_scr, b_scr, tt)
    step = lambda k, h: a_scr[tt - 1 - k] * h + b_scr[tt - 1 - k]
    hfin_ref[...] = lax.fori_loop(0, tt, step, hfin_ref[...], unroll=4)


def _lru_bwd_state(u_conv, wrg, ba, bx, lam, *, tt=32):
    nsteps = u_conv.shape[0] // tt
    blk = _nbytes((tt, BATCH, LRU_WIDTH), F32)
    return pl.pallas_call(
        functools.partial(_lru_bwd_state_kernel, tt=tt),
        grid=(nsteps,),
        in_specs=[
            pl.BlockSpec((tt, BATCH, LRU_WIDTH), lambda i: (nsteps - 1 - i, 0, 0)),
            _const_spec(wrg.shape),
            _const_spec((1, LRU_WIDTH)), _const_spec((1, LRU_WIDTH)), _const_spec((1, LRU_WIDTH)),
        ],
        out_specs=pl.BlockSpec((BATCH, LRU_WIDTH), lambda i: (0, 0)),
        out_shape=jax.ShapeDtypeStruct((BATCH, LRU_WIDTH), F32),
        scratch_shapes=[pltpu.VMEM((tt, BATCH, LRU_WIDTH), F32)] * 2,
        compiler_params=pltpu.CompilerParams(
            dimension_semantics=("arbitrary",),
            vmem_limit_bytes=_vmem_limit([blk], [2 * blk, _nbytes(wrg.shape, BF16)], 6 * blk)),
        name="lru_bwd_state",
    )(u_conv, wrg, ba, bx, lam)


def _lru_bwd_out_kernel(u_ref, h0_ref, wrg_ref, ba_ref, bx_ref, lam_ref, sf_ref, gate_ref, h_ref,
                        g2_ref, w_ref, lng_ref, lnb_ref, o_ref,
                        h_scr, a_scr, b_scr, sb_ring, out_scr, *, tt):
    i = pl.program_id(0)
    ntiles = pl.num_programs(0) - 1

    @pl.when(i == 0)
    def _():
        h_scr[...] = h0_ref[...]
        sb_ring[...] = jnp.zeros(sb_ring.shape, F32)

    y = sb_ring[lax.rem(i + 1, 2)] + sf_ref[...]
    z = (_gelu_tanh(gate_ref[...].astype(F32)) * y).astype(BF16)
    perm = _row_perm(to_time_major=False)
    parts = []
    for tau in range(tt // PERM_STEPS):
        sub = z[tau * PERM_STEPS:(tau + 1) * PERM_STEPS].reshape(PERM_STEPS * BATCH, LRU_WIDTH)
        zb = jnp.dot(perm, sub, preferred_element_type=F32).astype(BF16)
        parts.append(zb.reshape(BATCH, PERM_STEPS, LRU_WIDTH))
    zb = jnp.concatenate(parts, axis=1).reshape(BATCH * tt, LRU_WIDTH)

    def project_chunk(c):
        cols = slice(c * LRU_PROJ_CHUNK, (c + 1) * LRU_PROJ_CHUNK)
        val = jnp.dot(zb, w_ref[:, cols], preferred_element_type=F32)
        out_scr[:, :, cols] = val.reshape(BATCH, tt, LRU_PROJ_CHUNK)

    _lru_coeffs(u_ref[...], wrg_ref, ba_ref, bx_ref, lam_ref, a_scr, b_scr, tt,
                interleave=[functools.partial(project_chunk, c)
                            for c in range(D_MODEL // LRU_PROJ_CHUNK)])
    slot = lax.rem(i, 2)

    def step(k, h):
        idx = tt - 1 - k
        h = a_scr[idx] * h + b_scr[idx]
        sb_ring[slot, idx] = h
        return h

    h_scr[...] = lax.fori_loop(0, tt, step, h_scr[...], unroll=4)
    res = DEEPNORM_ALPHA * h_ref[...] + g2_ref[...] * out_scr[...]
    o_ref[...] = _layer_norm(res, lng_ref[...].reshape(1, 1, D_MODEL), lnb_ref[...].reshape(1, 1, D_MODEL))


def _lru_bwd_out(u_conv, h0, wrg, ba, bx, lam, s_f, gate_t, h3, g2, w_out, ln_g, ln_b, *, tt=32):
    ntiles = SEQ // tt
    scan_tile = lambda i: jnp.maximum(ntiles - 1 - i, 0)
    out_tile = lambda i: jnp.minimum(ntiles - i, ntiles - 1)
    tm_blk = lambda: pl.BlockSpec((tt, BATCH, LRU_WIDTH), lambda i: (out_tile(i), 0, 0))
    bm_blk = lambda: pl.BlockSpec((BATCH, tt, D_MODEL), lambda i: (0, out_tile(i), 0))
    blk = _nbytes((tt, BATCH, LRU_WIDTH), F32)
    return pl.pallas_call(
        functools.partial(_lru_bwd_out_kernel, tt=tt),
        grid=(ntiles + 1,),
        in_specs=[
            pl.BlockSpec((tt, BATCH, LRU_WIDTH), lambda i: (scan_tile(i), 0, 0)),
            _const_spec((BATCH, LRU_WIDTH)), _const_spec(wrg.shape),
            _const_spec((1, LRU_WIDTH)), _const_spec((1, LRU_WIDTH)), _const_spec((1, LRU_WIDTH)),
            tm_blk(), tm_blk(), bm_blk(),
            _const_spec((BATCH, 1, D_MODEL)), _const_spec(w_out.shape),
            _const_spec((1, D_MODEL)), _const_spec((1, D_MODEL))],
        out_specs=bm_blk(),
        out_shape=jax.ShapeDtypeStruct((BATCH, SEQ, D_MODEL), F32),
        scratch_shapes=[
            pltpu.VMEM((BATCH, LRU_WIDTH), F32),
            pltpu.VMEM((tt, BATCH, LRU_WIDTH), F32), pltpu.VMEM((tt, BATCH, LRU_WIDTH), F32),
            pltpu.VMEM((2, tt, BATCH, LRU_WIDTH), F32), pltpu.VMEM((BATCH, tt, D_MODEL), F32),
        ],
        compiler_params=pltpu.CompilerParams(
            dimension_semantics=("arbitrary",),
            vmem_limit_bytes=_vmem_limit(
                [blk] * 5, [5 * blk, _nbytes(w_out.shape, BF16), _nbytes(wrg.shape, BF16)], 8 * blk)),
        name="lru_bwd_out",
    )(u_conv, h0, wrg, ba, bx, lam, s_f, gate_t, h3, g2, w_out, ln_g, ln_b)


def _rope_tables():
    rows = SEQ // GRID_W
    row = jnp.repeat(jnp.arange(rows, dtype=F32), GRID_W)
    col = jnp.tile(jnp.arange(GRID_W, dtype=F32), rows)
    inv = ROPE_THETA ** (-jnp.arange(ROPE_FREQS, dtype=F32) / ROPE_FREQS)
    ang = jnp.concatenate([row[:, None] * inv, col[:, None] * inv], axis=-1)
    cos, sin = jnp.cos(ang), jnp.sin(ang)
    cos_t = jnp.tile(cos, (1, V7X_LANES // (HEAD_DIM // 2)))
    sin_t = jnp.tile(jnp.concatenate([-sin, sin], axis=-1), (1, V7X_LANES // HEAD_DIM))
    return cos_t, sin_t


def kernel(x, c, ctx, c_ctx, w_mod, b_mod, ln_g, ln_b, ffn_w_gate, ffn_w_up, ffn_w_down, mix_ab_w_in, attn_sink, pool_w, pool_scale, mix_ab_w_out, lru_w_in, lru_conv_w, lru_conv_b, lru_wa, lru_ba, lru_wx, lru_bx, lru_lambda, lru_w_out):
    assert x.shape == (BATCH, SEQ, D_MODEL) and ctx.shape == (BATCH, CTX_LEN, D_MODEL)
    c_all = jnp.concatenate(
        [c, c_ctx[None, :], jnp.zeros((MOD_ROWS - BATCH - 1, D_MODEL), F32)], axis=0)
    mod_all = _modulation(c_all, w_mod, b_mod)
    mod_lat = mod_all[:, :BATCH].reshape(DEPTH, BATCH, N_MOD, D_MODEL)
    mod_ctx = mod_all[:, BATCH:BATCH + 1].reshape(DEPTH, 1, N_MOD, D_MODEL)

    wg = ffn_w_gate.astype(BF16)
    wu = ffn_w_up.astype(BF16)
    wd = ffn_w_down.astype(BF16)
    row = lambda v: v.reshape(1, -1)
    n_ctx = BATCH * CTX_LEN

    h = x.reshape(BATCH * SEQ, D_MODEL)
    hc = ctx.reshape(n_ctx, D_MODEL)

    l = 0
    ml, mc = mod_lat[l], mod_ctx[l]
    ffn1 = ((l, 0), wg, wu, wd, row(ln_g[l, 0]), row(ln_b[l, 0]))
    ffn2 = ((l, 1), wg, wu, wd, row(ln_g[l, 2]), row(ln_b[l, 2]))
    h = _ffn(h, ml, 0, *ffn1, nb=BATCH, length=SEQ)
    hc = _ffn(hc, mc, 0, *ffn1, nb=1, length=n_ctx)
    w_in = mix_ab_w_in[0].astype(BF16)
    cos_t, sin_t = _rope_tables()
    q, k2, v2, u = _attn_inproj(h, ml, w_in, cos_t, sin_t, nb=BATCH, length=SEQ)
    q_c, k2_c, v2_c, u_c = _attn_inproj(hc, mc, w_in, None, None, nb=1, length=n_ctx)
    mix_args = (pool_w[0].astype(BF16), row(pool_scale[0]), mix_ab_w_out[0].astype(BF16),
                row(ln_g[l, 1]), row(ln_b[l, 1]))
    h = _attn_lat(attn_sink[0], q, k2, v2, k2_c, v2_c, u, h, ml, *mix_args)
    hc = _attn_ctx(attn_sink[0], q_c, k2_c, v2_c, u_c, hc, mc, *mix_args)
    h = _ffn(h, ml, 6, *ffn2, nb=BATCH, length=SEQ)
    hc = _ffn(hc, mc, 6, *ffn2, nb=1, length=n_ctx)

    l = 1
    ml, mc = mod_lat[l], mod_ctx[l]
    ffn1 = ((l, 0), wg, wu, wd, row(ln_g[l, 0]), row(ln_b[l, 0]))
    ffn2 = ((l, 1), wg, wu, wd, row(ln_g[l, 2]), row(ln_b[l, 2]))
    h = _ffn(h, ml, 0, *ffn1, nb=BATCH, length=SEQ)
    hc = _ffn(hc, mc, 0, *ffn1, nb=1, length=n_ctx)
    w_in = lru_w_in[0].astype(BF16)
    h3 = h.reshape(BATCH, SEQ, D_MODEL)
    wrg = (0.5 * jnp.concatenate([lru_wa[0], lru_wx[0]], axis=-1)).astype(BF16)
    conv_w, conv_b = lru_conv_w[0], row(lru_conv_b[0])
    zeros = jnp.zeros((BATCH, LRU_WIDTH), F32)
    dir_args = [(wrg[d], row(lru_ba[0, d]), row(lru_bx[0, d]), row(lru_lambda[0, d]))
                for d in range(2)]
    _, uc_t, _, hf_ctx = _lru_in_fwd(hc.reshape(BATCH, CTX_LEN, D_MODEL), mc[:, 4:5, :], mc[:, 3:4, :],
                                     w_in[:, LRU_WIDTH:], zeros, conv_w, conv_b, *dir_args[0],
                                     with_gate=False)
    gate_t, u_t, s_f, _ = _lru_in_fwd(h3, ml[:, 4:5, :], ml[:, 3:4, :], w_in, hf_ctx, conv_w, conv_b,
                                      *dir_args[0], with_gate=True)
    hb_ctx = _lru_bwd_state(uc_t, *dir_args[1])
    h3 = _lru_bwd_out(u_t, hb_ctx, *dir_args[1], s_f, gate_t, h3, ml[:, 5:6, :],
                      lru_w_out[0].astype(BF16), row(ln_g[l, 1]), row(ln_b[l, 1]))
    h = _ffn(h3.reshape(BATCH * SEQ, D_MODEL), ml, 6, *ffn2, nb=BATCH, length=SEQ)
    return h.reshape(BATCH, SEQ, D_MODEL)
```

```python
import functools

import jax
import jax.numpy as jnp
from jax import lax
from jax.experimental import pallas as pl
from jax.experimental.pallas import tpu as pltpu

D_MODEL = 1024
BATCH = 16
SEQ = 2048
DEPTH = 2
GRID_W = 64
CTX_LEN = 256
HEAD_DIM = 64
ATT_HEADS = 8
ATT_KV_HEADS = 2
ATT_GROUPS = ATT_HEADS // ATT_KV_HEADS
ATT_WIDTH = ATT_HEADS * HEAD_DIM
KV_WIDTH = ATT_KV_HEADS * HEAD_DIM
WINDOW = 128
BLOCK = 128
ATT_SCALE = HEAD_DIM ** -0.5
LOG2E = 1.4426950408889634
ROPE_THETA = 10000.0
ROPE_FREQS = HEAD_DIM // 4
POOL_WINDOWS = (2, 4, 8, 16)
POOL_WIDTH = D_MODEL // 2
POOL_GROUP_W = POOL_WIDTH // len(POOL_WINDOWS)
MIX_AB_IN = ATT_WIDTH + 2 * KV_WIDTH + POOL_WIDTH
LRU_WIDTH = D_MODEL
LRU_BLOCKS = 8
LRU_BLOCK_W = LRU_WIDTH // LRU_BLOCKS
LRU_C = 8.0
CONV_W = 4
CONV_LEFT = (CONV_W - 1) // 2
D_FF = 2816
N_MOD = 9
LN_EPS = 1e-5
NEG_INF = -1e30
DEEPNORM_ALPHA = (2 * DEPTH) ** 0.25

V7X_LANES = 128
V7X_SUBLANES = 8
V7X_VMEM_BYTES = 64 * 1024 * 1024
V7X_VMEM_USABLE_BYTES = 60000 * 1024

F32 = jnp.float32
BF16 = jnp.bfloat16

POOL_HALO = V7X_SUBLANES
PERM_STEPS = 16
KEY_SLAB_W = ATT_KV_HEADS * V7X_LANES
VAL_SLAB_W = 2 * ATT_KV_HEADS * V7X_LANES
LRU_PROJ_CHUNK = 256
LRU_SCAN_LAG = 2
SCORE_LOOKAHEAD = 2
EPILOGUE_BLOCKS = 2
MOD_ROWS = 24


def _nbytes(shape, dtype):
    n = 1
    for s in shape:
        n *= s
    return n * jnp.dtype(dtype).itemsize


def _vmem_limit(pipelined, resident, temporaries):
    est = 2 * sum(pipelined) + sum(resident) + temporaries
    return int(min(V7X_VMEM_USABLE_BYTES, max(est * 5 // 4, 16 * 1024 * 1024)))


def _const_spec(shape):
    nd = len(shape)
    return pl.BlockSpec(shape, lambda *_: (0,) * nd, pipeline_mode=pl.Buffered(1))


def _tok_spec(tm, width, nt):
    return pl.BlockSpec((tm, width), lambda b, t: (b * nt + t, 0))


def _mod_spec():
    return pl.BlockSpec((1, N_MOD, D_MODEL), lambda b, t: (b, 0, 0))


def _layer_norm(z, g, b):
    mu = jnp.mean(z, axis=-1, keepdims=True)
    zc = z - mu
    var = jnp.mean(zc * zc, axis=-1, keepdims=True)
    return zc * lax.rsqrt(var + LN_EPS) * g + b


def _gelu_tanh(x):
    return 0.5 * x * (1.0 + jnp.tanh(0.7978845608028654 * (x + 0.044715 * (x * x * x))))


def _mod_kernel(c_ref, w_ref, b_ref, o_ref):
    c = c_ref[...]
    a_hi, a_lo = _split_bf16(c * jax.nn.sigmoid(c))
    w_hi, w_lo = _split_bf16(w_ref[0])
    acc = jnp.dot(a_hi, w_lo, preferred_element_type=F32) + jnp.dot(a_lo, w_hi, preferred_element_type=F32)
    o_ref[0] = acc + jnp.dot(a_hi, w_hi, preferred_element_type=F32) + b_ref[0]


def _modulation(c_all, w_mod, b_mod):
    tn = 2304
    n_out = N_MOD * D_MODEL
    blocks = [_nbytes((1, D_MODEL, tn), F32), _nbytes((1, MOD_ROWS, tn), F32)]
    return pl.pallas_call(
        _mod_kernel,
        grid=(DEPTH, n_out // tn),
        in_specs=[
            pl.BlockSpec((MOD_ROWS, D_MODEL), lambda l, j: (0, 0)),
            pl.BlockSpec((1, D_MODEL, tn), lambda l, j: (l, 0, j)),
            pl.BlockSpec((1, 1, tn), lambda l, j: (l, 0, j)),
        ],
        out_specs=pl.BlockSpec((1, MOD_ROWS, tn), lambda l, j: (l, 0, j)),
        out_shape=jax.ShapeDtypeStruct((DEPTH, MOD_ROWS, n_out), F32),
        compiler_params=pltpu.CompilerParams(
            dimension_semantics=("arbitrary", "arbitrary"),
            vmem_limit_bytes=_vmem_limit(blocks, [], 4 * blocks[0])),
        name="modulation",
    )(c_all, w_mod, b_mod.reshape(DEPTH, 1, n_out))


def _ffn_kernel(x_ref, mod_ref, wg_ref, wu_ref, wd_ref, lng_ref, lnb_ref, o_ref, *, j0, sub):
    shift = mod_ref[0, j0:j0 + 1, :]
    scale = mod_ref[0, j0 + 1:j0 + 2, :]
    gate = mod_ref[0, j0 + 2:j0 + 3, :]
    sizes = [sub] * (x_ref.shape[0] // sub)
    starts = [s * sub for s in range(len(sizes))]
    nsub = len(sizes)

    def gate_up(s):
        x = x_ref[starts[s]:starts[s] + sizes[s], :]
        xin = (x * (1.0 + scale) + shift).astype(BF16)
        return (jnp.dot(xin, wg_ref[...], preferred_element_type=F32),
                jnp.dot(xin, wu_ref[...], preferred_element_type=F32))

    pending = gate_up(0)
    for s in range(nsub):
        g, u = pending
        if s + 1 < nsub:
            pending = gate_up(s + 1)
        rows = slice(starts[s], starts[s] + sizes[s])
        a = (g * jax.nn.sigmoid(g) * u).astype(BF16)
        y = jnp.dot(a, wd_ref[...], preferred_element_type=F32)
        z = DEEPNORM_ALPHA * x_ref[rows, :] + (0.5 * gate) * y
        o_ref[rows, :] = _layer_norm(z, lng_ref[...], lnb_ref[...])


def _ffn(x, mod, j0, which, wg, wu, wd, ln_g, ln_b, *, nb, length, tm=1024, sub=256):
    nt = length // tm
    tok = _nbytes((tm, D_MODEL), F32)
    weights = [_nbytes(w.shape[2:], BF16) for w in (wg, wu, wd)]
    temps = 3 * _nbytes((tm, D_FF), F32) + 4 * tok
    wspec = lambda w: pl.BlockSpec((None, None) + w.shape[2:], lambda b, t: which + (0, 0),
                                   pipeline_mode=pl.Buffered(1))
    return pl.pallas_call(
        functools.partial(_ffn_kernel, j0=j0, sub=sub),
        grid=(nb, nt),
        in_specs=[
            _tok_spec(tm, D_MODEL, nt),
            _mod_spec(),
            wspec(wg), wspec(wu), wspec(wd),
            _const_spec((1, D_MODEL)), _const_spec((1, D_MODEL)),
        ],
        out_specs=_tok_spec(tm, D_MODEL, nt),
        out_shape=jax.ShapeDtypeStruct((nb * length, D_MODEL), F32),
        compiler_params=pltpu.CompilerParams(
            dimension_semantics=("arbitrary", "arbitrary"),
            vmem_limit_bytes=_vmem_limit([tok, tok], weights, temps)),
        name="ffn",
    )(x, mod, wg, wu, wd, ln_g, ln_b)


def _dup_halves(z, lane):
    zr = pltpu.roll(z, HEAD_DIM, 1)
    lo = lane < HEAD_DIM
    return jnp.where(lo, z, zr), jnp.where(lo, zr, z)


def _attn_inproj_kernel(*refs, rope):
    if rope:
        x_ref, mod_ref, w_ref, cos_ref, sin_ref, q_ref, k_ref, v_ref, u_ref = refs
    else:
        x_ref, mod_ref, w_ref, q_ref, k_ref, v_ref, u_ref = refs
    x = x_ref[...]
    xin = (x * (1.0 + mod_ref[0, 4:5, :]) + mod_ref[0, 3:4, :]).astype(BF16)
    p = jnp.dot(xin, w_ref[...], preferred_element_type=F32)
    lane = lax.broadcasted_iota(jnp.int32, (x.shape[0], V7X_LANES), 1)
    first_half = (lane & (HEAD_DIM - 1)) < HEAD_DIM // 2

    def rot(z):
        if not rope:
            return z
        zr = jnp.where(first_half, pltpu.roll(z, V7X_LANES - HEAD_DIM // 2, 1),
                       pltpu.roll(z, HEAD_DIM // 2, 1))
        return z * cos_ref[...] + zr * sin_ref[...]

    for c in range(ATT_WIDTH // V7X_LANES):
        sl = slice(c * V7X_LANES, (c + 1) * V7X_LANES)
        q_ref[:, sl] = (rot(p[:, sl]) * (ATT_SCALE * LOG2E)).astype(BF16)
    k0, k1 = _dup_halves(rot(p[:, ATT_WIDTH:ATT_WIDTH + KV_WIDTH]), lane)
    k_ref[...] = jnp.concatenate([k0, k1], axis=-1).T.astype(BF16)
    v = p[:, ATT_WIDTH + KV_WIDTH:ATT_WIDTH + 2 * KV_WIDTH]
    vr = pltpu.roll(v, HEAD_DIM, 1)
    lo = lane < HEAD_DIM
    slabs = (jnp.where(lo, v, 1.0), jnp.where(lo, 1.0, vr), jnp.where(lo, vr, 1.0), jnp.where(lo, 1.0, v))
    for i, slab in enumerate(slabs):
        v_ref[:, i * V7X_LANES:(i + 1) * V7X_LANES] = slab.astype(BF16)
    u_ref[...] = p[:, ATT_WIDTH + 2 * KV_WIDTH:]


def _attn_inproj(x, mod, w_in, cos, sin, *, nb, length, tm=1024):
    nt = length // tm
    rope = cos is not None
    tok = _nbytes((tm, D_MODEL), F32)
    outs = [_nbytes((tm, ATT_WIDTH), BF16), _nbytes((tm, KEY_SLAB_W + VAL_SLAB_W), BF16),
            _nbytes((tm, POOL_WIDTH), F32)]
    in_specs = [_tok_spec(tm, D_MODEL, nt), _mod_spec(), _const_spec(w_in.shape)]
    args = [x, mod, w_in]
    if rope:
        in_specs += [pl.BlockSpec((tm, V7X_LANES), lambda b, t: (t, 0))] * 2
        args += [cos, sin]
    rows = nb * length
    return pl.pallas_call(
        functools.partial(_attn_inproj_kernel, rope=rope),
        grid=(nb, nt),
        in_specs=in_specs,
        out_specs=[
            _tok_spec(tm, ATT_WIDTH, nt),
            pl.BlockSpec((KEY_SLAB_W, tm), lambda b, t: (0, b * nt + t)),
            _tok_spec(tm, VAL_SLAB_W, nt),
            _tok_spec(tm, POOL_WIDTH, nt),
        ],
        out_shape=[
            jax.ShapeDtypeStruct((rows, ATT_WIDTH), BF16),
            jax.ShapeDtypeStruct((KEY_SLAB_W, rows), BF16),
            jax.ShapeDtypeStruct((rows, VAL_SLAB_W), BF16),
            jax.ShapeDtypeStruct((rows, POOL_WIDTH), F32),
        ],
        compiler_params=pltpu.CompilerParams(
            dimension_semantics=("arbitrary", "arbitrary"),
            vmem_limit_bytes=_vmem_limit([tok] + outs, [_nbytes(w_in.shape, BF16)],
                                         3 * _nbytes((tm, MIX_AB_IN), F32))),
        name="attn_inproj",
    )(*args)


def _attend_scores(q_blk, keys, kh):
    lane = lax.broadcasted_iota(jnp.int32, (BLOCK, V7X_LANES), 1)
    lo = lane < HEAD_DIM
    zero = jnp.zeros((BLOCK, V7X_LANES), BF16)
    parts = []
    for g in range(ATT_GROUPS):
        c = (kh * ATT_GROUPS + g) // 2
        qc = q_blk[:, c * V7X_LANES:(c + 1) * V7X_LANES]
        parts.append(jnp.where(lo if g % 2 == 0 else jnp.logical_not(lo), qc, zero))
    qs = jnp.concatenate(parts, axis=0)
    return [jnp.dot(qs, k, preferred_element_type=F32) for k in keys]


def _attend_finish(scores, values_even, values_odd, segments, sink_ref, kh):
    lane = lax.broadcasted_iota(jnp.int32, (BLOCK, V7X_LANES), 1)
    lo = lane < HEAD_DIM
    probs = ([[] for _ in scores], [[] for _ in scores])
    sink_terms = []
    for g in range(ATT_GROUPS):
        rows = slice(g * BLOCK, (g + 1) * BLOCK)
        segs = [[s[rows, a:b] if bias is None else s[rows, a:b] + bias for a, b, bias in seg]
                for s, seg in zip(scores, segments)]
        sink = sink_ref[kh * ATT_GROUPS + g] * LOG2E
        chunks = [x[:, c:c + V7X_LANES] for sl in segs for x in sl
                  for c in range(0, x.shape[1], V7X_LANES)]
        m = functools.reduce(jnp.maximum, chunks).max(axis=-1, keepdims=True)
        m = jnp.maximum(m, sink)
        sink_terms.append(jnp.exp2(sink - m))
        for i, sl in enumerate(segs):
            e = [jnp.exp2(x - m).astype(BF16) for x in sl]
            probs[g % 2][i].append(e[0] if len(e) == 1 else jnp.concatenate(e, axis=-1))

    def weighted_values(ps, vals):
        o = None
        for p, v in zip(ps, vals):
            t = jnp.dot(jnp.concatenate(p, axis=0), v, preferred_element_type=F32)
            o = t if o is None else o + t
        return o

    o_par = (weighted_values(probs[0], values_even), weighted_values(probs[1], values_odd))
    outs = []
    for g in range(ATT_GROUPS):
        o = o_par[g % 2][(g // 2) * BLOCK:(g // 2 + 1) * BLOCK]
        rinv = 1.0 / (o + sink_terms[g])
        outs.append(o * pltpu.roll(rinv, HEAD_DIM, 1))
    return jnp.where(lo, outs[0], outs[1]), jnp.where(lo, outs[2], outs[3])


def _pool_sums(ubuf_ref, nblk, band_ref):
    u_hi, u_lo = _split_bf16(ubuf_ref[...])
    sums = {}
    for j in range(nblk):
        win = slice(j * BLOCK, (j + 2) * BLOCK)
        for gi in range(len(POOL_WINDOWS)):
            cols = slice(gi * POOL_GROUP_W, (gi + 1) * POOL_GROUP_W)
            band = band_ref[gi]
            sums[j, gi] = (jnp.dot(band, u_hi[win, cols], preferred_element_type=F32)
                           + jnp.dot(band, u_lo[win, cols], preferred_element_type=F32))
    return sums


def _pool_finish(sums, ubuf_ref, blocks, pos0, length, wpool_ref, pscale_ref, mix_scr):
    for j in blocks:
        pos = pos0 + j * BLOCK + lax.broadcasted_iota(jnp.int32, (BLOCK, 1), 0)
        rows = slice(POOL_HALO + j * BLOCK, POOL_HALO + (j + 1) * BLOCK)
        for gi, w in enumerate(POOL_WINDOWS):
            r = w // 2
            cols = slice(gi * POOL_GROUP_W, (gi + 1) * POOL_GROUP_W)
            cnt = (jnp.minimum(pos + r, length - 1) - jnp.maximum(pos - r, 0) + 1).astype(F32)
            d = sums[j, gi] / cnt - ubuf_ref[rows, cols]
            y = jnp.dot(d.astype(BF16), wpool_ref[gi], preferred_element_type=F32)
            mix_scr[j * BLOCK:(j + 1) * BLOCK, ATT_WIDTH + gi * POOL_GROUP_W:
                    ATT_WIDTH + (gi + 1) * POOL_GROUP_W] = (y * pscale_ref[:, cols]).astype(BF16)


def _split_bf16(x):
    hi = x.astype(BF16)
    return hi, (x - hi.astype(F32)).astype(BF16)


def _pool_bands():
    i = jnp.arange(BLOCK)[:, None]
    c = jnp.arange(2 * BLOCK)[None, :]
    return jnp.stack([(jnp.abs(c - POOL_HALO - i) <= w // 2) for w in POOL_WINDOWS]).astype(BF16)


def _attn_out(mix_scr, rows, wout_ref, h_ref, gate, lng_ref, lnb_ref, o_ref):
    y = jnp.dot(mix_scr[rows, :], wout_ref[...], preferred_element_type=F32)
    z = DEEPNORM_ALPHA * h_ref[rows, :] + gate * y
    o_ref[rows, :] = _layer_norm(z, lng_ref[...], lnb_ref[...])


def _attn_lat_kernel(sink_ref, q_ref, kc_ref, kp_ref, kn_ref, vc_ref, vp_ref, vn_ref,
                     kctx_ref, vctx_ref, uc_ref, up_ref, un_ref, h_ref, mod_ref, bias_ref, band_ref,
                     wpool_ref, pscale_ref, wout_ref, lng_ref, lnb_ref, o_ref,
                     kbuf, vbuf, ubuf, mix_scr, *, tq, length):
    t = pl.program_id(1)
    last = pl.num_programs(1) - 1
    nblk = tq // BLOCK
    kbuf[:, 0:BLOCK] = kp_ref[...]
    kbuf[:, BLOCK:BLOCK + tq] = kc_ref[...]
    kbuf[:, BLOCK + tq:] = kn_ref[...]
    vbuf[0:BLOCK] = vp_ref[...]
    vbuf[BLOCK:BLOCK + tq] = vc_ref[...]
    vbuf[BLOCK + tq:] = vn_ref[...]
    ubuf[0:POOL_HALO] = jnp.where(t == 0, 0.0, up_ref[...])
    ubuf[POOL_HALO:POOL_HALO + tq] = uc_ref[...]
    ubuf[POOL_HALO + tq:2 * POOL_HALO + tq] = jnp.where(t == last, 0.0, un_ref[...])
    ubuf[2 * POOL_HALO + tq:] = jnp.zeros((2 * BLOCK - 2 * POOL_HALO, POOL_WIDTH), F32)
    lanes = lambda i: slice(i * V7X_LANES, (i + 1) * V7X_LANES)
    gate = mod_ref[0, 5:6, :]
    units = [(j, kh) for j in range(nblk) for kh in range(ATT_KV_HEADS)]

    def scores_of(j, kh):
        win = slice(j * BLOCK, j * BLOCK + 3 * BLOCK)
        return _attend_scores(q_ref[j * BLOCK:(j + 1) * BLOCK, :],
                              [kbuf[lanes(kh), win], kctx_ref[lanes(kh), :]], kh)

    sums = _pool_sums(ubuf, nblk, band_ref)
    pending = [scores_of(*u) for u in units[:SCORE_LOOKAHEAD]]
    _pool_finish(sums, ubuf, range(nblk), t * tq, length, wpool_ref, pscale_ref, mix_scr)
    for n, (j, kh) in enumerate(units):
        scores = pending.pop(0)
        if n + SCORE_LOOKAHEAD < len(units):
            pending.append(scores_of(*units[n + SCORE_LOOKAHEAD]))
        blk = t * nblk + j
        bidx = jnp.where(blk == 0, 0, jnp.where(blk == length // BLOCK - 1, 2, 1))
        bias = bias_ref[bidx]
        segments = [[(0, BLOCK, bias[:, :BLOCK]), (BLOCK, 2 * BLOCK, None),
                     (2 * BLOCK, 3 * BLOCK, bias[:, BLOCK:])], [(0, CTX_LEN, None)]]
        win = slice(j * BLOCK, j * BLOCK + 3 * BLOCK)
        rows = slice(j * BLOCK, (j + 1) * BLOCK)
        c0, c1 = _attend_finish(
            scores, [vbuf[win, lanes(2 * kh)], vctx_ref[:, lanes(2 * kh)]],
            [vbuf[win, lanes(2 * kh + 1)], vctx_ref[:, lanes(2 * kh + 1)]],
            segments, sink_ref, kh)
        mix_scr[rows, lanes(2 * kh)] = c0.astype(BF16)
        mix_scr[rows, lanes(2 * kh + 1)] = c1.astype(BF16)
        per_epilogue = EPILOGUE_BLOCKS * ATT_KV_HEADS
        if n > 0 and n % per_epilogue == 0:
            e = n // per_epilogue - 1
            done = slice(e * EPILOGUE_BLOCKS * BLOCK, (e + 1) * EPILOGUE_BLOCKS * BLOCK)
            _attn_out(mix_scr, done, wout_ref, h_ref, gate, lng_ref, lnb_ref, o_ref)
    _attn_out(mix_scr, slice(tq - EPILOGUE_BLOCKS * BLOCK, tq), wout_ref, h_ref, gate, lng_ref,
              lnb_ref, o_ref)


def _attn_ctx_kernel(sink_ref, q_ref, kctx_ref, vctx_ref, uc_ref, h_ref, mod_ref, band_ref,
                     wpool_ref, pscale_ref, wout_ref, lng_ref, lnb_ref, o_ref,
                     ubuf, mix_scr, *, tq):
    ubuf[0:POOL_HALO] = jnp.zeros((POOL_HALO, POOL_WIDTH), F32)
    ubuf[POOL_HALO:POOL_HALO + tq] = uc_ref[...]
    ubuf[POOL_HALO + tq:] = jnp.zeros((2 * BLOCK - POOL_HALO, POOL_WIDTH), F32)
    lanes = lambda i: slice(i * V7X_LANES, (i + 1) * V7X_LANES)
    nblk = tq // BLOCK
    units = [(j, kh) for j in range(nblk) for kh in range(ATT_KV_HEADS)]
    scores_of = lambda j, kh: _attend_scores(q_ref[j * BLOCK:(j + 1) * BLOCK, :],
                                             [kctx_ref[lanes(kh), :]], kh)
    sums = _pool_sums(ubuf, nblk, band_ref)
    pending = [scores_of(*u) for u in units[:SCORE_LOOKAHEAD]]
    _pool_finish(sums, ubuf, range(nblk), 0, tq, wpool_ref, pscale_ref, mix_scr)
    for n, (j, kh) in enumerate(units):
        scores = pending.pop(0)
        if n + SCORE_LOOKAHEAD < len(units):
            pending.append(scores_of(*units[n + SCORE_LOOKAHEAD]))
        rows = slice(j * BLOCK, (j + 1) * BLOCK)
        c0, c1 = _attend_finish(scores, [vctx_ref[:, lanes(2 * kh)]],
                                [vctx_ref[:, lanes(2 * kh + 1)]], [[(0, CTX_LEN, None)]],
                                sink_ref, kh)
        mix_scr[rows, lanes(2 * kh)] = c0.astype(BF16)
        mix_scr[rows, lanes(2 * kh + 1)] = c1.astype(BF16)
    _attn_out(mix_scr, slice(0, tq), wout_ref, h_ref, mod_ref[0, 5:6, :], lng_ref, lnb_ref, o_ref)


def _window_bias():
    qi = jnp.arange(BLOCK)[:, None]
    kc = jnp.arange(3 * BLOCK)[None, :]
    band = jnp.abs(qi + BLOCK - kc) <= WINDOW
    first = band & (kc >= BLOCK)
    final = band & (kc < 2 * BLOCK)
    full = jnp.where(jnp.stack([first, band, final]), 0.0, NEG_INF).astype(F32)
    return jnp.concatenate([full[:, :, :BLOCK], full[:, :, 2 * BLOCK:]], axis=-1)


def _smem_spec():
    return pl.BlockSpec(memory_space=pltpu.SMEM)


def _attn_lat(sink, q, k2, v2, kctx, vctx, u, h, mod, wpool, pscale, wout, ln_g, ln_b, *, tq=512):
    nt = SEQ // tq
    bpt = tq // BLOCK
    bps = SEQ // BLOCK
    hpt = tq // POOL_HALO
    hps = SEQ // POOL_HALO
    cur = lambda w: pl.BlockSpec((tq, w), lambda b, t: (b * nt + t, 0))
    prev_blk = lambda rows, w, per_tile, per_samp: pl.BlockSpec(
        (rows, w), lambda b, t: (b * per_samp + jnp.maximum(t * per_tile - 1, 0), 0))
    next_blk = lambda rows, w, per_tile, per_samp: pl.BlockSpec(
        (rows, w), lambda b, t: (b * per_samp + jnp.minimum((t + 1) * per_tile, per_samp - 1), 0))
    ctx_blk = lambda w: pl.BlockSpec((CTX_LEN, w), lambda b, t: (b, 0))
    kv_specs = lambda w: [cur(w), prev_blk(BLOCK, w, bpt, bps), next_blk(BLOCK, w, bpt, bps)]
    key_specs = [
        pl.BlockSpec((KEY_SLAB_W, tq), lambda b, t: (0, b * nt + t)),
        pl.BlockSpec((KEY_SLAB_W, BLOCK), lambda b, t: (0, b * bps + jnp.maximum(t * bpt - 1, 0))),
        pl.BlockSpec((KEY_SLAB_W, BLOCK),
                     lambda b, t: (0, b * bps + jnp.minimum((t + 1) * bpt, bps - 1))),
    ]
    bias = _window_bias()
    bands = _pool_bands()
    tok = _nbytes((tq, D_MODEL), F32)
    kv_rows = tq + 2 * BLOCK
    u_rows = tq + 2 * BLOCK
    pipelined = [_nbytes((tq, ATT_WIDTH), BF16), 2 * _nbytes((kv_rows, KEY_SLAB_W + VAL_SLAB_W), BF16),
                 _nbytes((tq + 2 * POOL_HALO, POOL_WIDTH), F32), 2 * tok]
    resident = [_nbytes(bias.shape, F32), _nbytes(bands.shape, BF16), _nbytes(wpool.shape, BF16),
                _nbytes(wout.shape, BF16)]
    scratch = [_nbytes((kv_rows, KEY_SLAB_W + VAL_SLAB_W), BF16),
               _nbytes((u_rows, POOL_WIDTH), F32), _nbytes((tq, D_MODEL), BF16)]
    return pl.pallas_call(
        functools.partial(_attn_lat_kernel, tq=tq, length=SEQ),
        grid=(BATCH, nt),
        in_specs=[_smem_spec(), cur(ATT_WIDTH)] + key_specs + kv_specs(VAL_SLAB_W) + [
            pl.BlockSpec((KEY_SLAB_W, CTX_LEN), lambda b, t: (0, b)), ctx_blk(VAL_SLAB_W),
            cur(POOL_WIDTH), prev_blk(POOL_HALO, POOL_WIDTH, hpt, hps),
            next_blk(POOL_HALO, POOL_WIDTH, hpt, hps),
            cur(D_MODEL), _mod_spec(),
            _const_spec(bias.shape), _const_spec(bands.shape),
            _const_spec(wpool.shape), _const_spec((1, POOL_WIDTH)),
            _const_spec(wout.shape), _const_spec((1, D_MODEL)), _const_spec((1, D_MODEL)),
        ],
        out_specs=cur(D_MODEL),
        out_shape=jax.ShapeDtypeStruct((BATCH * SEQ, D_MODEL), F32),
        scratch_shapes=[
            pltpu.VMEM((KEY_SLAB_W, kv_rows), BF16), pltpu.VMEM((kv_rows, VAL_SLAB_W), BF16),
            pltpu.VMEM((u_rows, POOL_WIDTH), F32), pltpu.VMEM((tq, D_MODEL), BF16),
        ],
        compiler_params=pltpu.CompilerParams(
            dimension_semantics=("arbitrary", "arbitrary"),
            vmem_limit_bytes=_vmem_limit(pipelined, resident + scratch, 8 * tok)),
        name="attn_latent",
    )(sink, q, k2, k2, k2, v2, v2, v2, kctx, vctx, u, u, u, h, mod, bias, bands, wpool, pscale, wout,
      ln_g, ln_b)


def _attn_ctx(sink, q, kctx, vctx, u, h, mod, wpool, pscale, wout, ln_g, ln_b):
    tq = CTX_LEN
    blk = lambda w: pl.BlockSpec((tq, w), lambda b, t: (b, 0))
    bands = _pool_bands()
    tok = _nbytes((tq, D_MODEL), F32)
    u_rows = tq + 2 * BLOCK
    pipelined = [_nbytes((tq, ATT_WIDTH), BF16), _nbytes((tq, KEY_SLAB_W + VAL_SLAB_W), BF16),
                 _nbytes((tq, POOL_WIDTH), F32), 2 * tok]
    resident = [_nbytes(bands.shape, BF16), _nbytes(wpool.shape, BF16), _nbytes(wout.shape, BF16)]
    scratch = [_nbytes((u_rows, POOL_WIDTH), F32), _nbytes((tq, D_MODEL), BF16)]
    return pl.pallas_call(
        functools.partial(_attn_ctx_kernel, tq=tq),
        grid=(BATCH, 1),
        in_specs=[
            _smem_spec(),
            blk(ATT_WIDTH), pl.BlockSpec((KEY_SLAB_W, tq), lambda b, t: (0, b)),
            blk(VAL_SLAB_W), blk(POOL_WIDTH), blk(D_MODEL),
            pl.BlockSpec((1, N_MOD, D_MODEL), lambda b, t: (0, 0, 0)),
            _const_spec(bands.shape), _const_spec(wpool.shape), _const_spec((1, POOL_WIDTH)),
            _const_spec(wout.shape), _const_spec((1, D_MODEL)), _const_spec((1, D_MODEL)),
        ],
        out_specs=blk(D_MODEL),
        out_shape=jax.ShapeDtypeStruct((BATCH * CTX_LEN, D_MODEL), F32),
        scratch_shapes=[
            pltpu.VMEM((u_rows, POOL_WIDTH), F32), pltpu.VMEM((tq, D_MODEL), BF16),
        ],
        compiler_params=pltpu.CompilerParams(
            dimension_semantics=("arbitrary", "arbitrary"),
            vmem_limit_bytes=_vmem_limit(pipelined, resident + scratch, 8 * tok)),
        name="attn_context",
    )(sink, q, kctx, vctx, u, h, mod, bands, wpool, pscale, wout, ln_g, ln_b)


def _row_perm(to_time_major):
    n = BATCH * PERM_STEPS
    r = lax.broadcasted_iota(jnp.int32, (n, n), 0)
    c = lax.broadcasted_iota(jnp.int32, (n, n), 1)
    if to_time_major:
        hit = ((r // BATCH) == (c % PERM_STEPS)) & ((r % BATCH) == (c // PERM_STEPS))
    else:
        hit = ((r // PERM_STEPS) == (c % BATCH)) & ((r % PERM_STEPS) == (c // BATCH))
    return jnp.where(hit, 1.0, 0.0).astype(BF16)


def _lru_time_major_input(x_ref, scale_ref, shift_ref, tt):
    xin = (x_ref[...] * (1.0 + scale_ref[...]) + shift_ref[...]).astype(BF16)
    perm = _row_perm(to_time_major=True)
    parts = []
    for tau in range(tt // PERM_STEPS):
        sub = xin[:, tau * PERM_STEPS:(tau + 1) * PERM_STEPS, :].reshape(BATCH * PERM_STEPS, D_MODEL)
        parts.append(jnp.dot(perm, sub, preferred_element_type=F32).astype(BF16))
    return jnp.concatenate(parts, axis=0)


def _lru_in_fwd_kernel(*refs, tt, with_gate):
    (x_ref, scale_ref, shift_ref, w_ref, h0_ref, cw_ref, cb_ref, wrg_ref, ba_ref, bx_ref,
     lam_ref) = refs[:11]
    outs = refs[11:]
    if with_gate:
        gate_ref, uconv_ref, s_ref, hfin_ref, ring, ulast, h_scr, a_scr, b_scr = outs
    else:
        uconv_ref, s_ref, hfin_ref, ring, ulast, h_scr, a_scr, b_scr = outs
    i = pl.program_id(0)
    ntiles = pl.num_programs(0) - LRU_SCAN_LAG

    @pl.when(i == 0)
    def _():
        ring[...] = jnp.zeros(ring.shape, F32)
        ulast[...] = jnp.zeros(ulast.shape, F32)
        h_scr[...] = h0_ref[...]

    xt = _lru_time_major_input(x_ref, scale_ref, shift_ref, tt)
    slot_new = lax.rem(i, LRU_SCAN_LAG + 1)
    slot_cur = lax.rem(i + 1, LRU_SCAN_LAG + 1)
    slot_nxt = lax.rem(i + 2, LRU_SCAN_LAG + 1)

    def project_chunk(c):
        cols = slice(c * LRU_PROJ_CHUNK, (c + 1) * LRU_PROJ_CHUNK)
        val = jnp.dot(xt, w_ref[:, cols], preferred_element_type=F32)
        val = val.reshape(tt, BATCH, LRU_PROJ_CHUNK)
        if with_gate and c < LRU_WIDTH // LRU_PROJ_CHUNK:
            gate_ref[:, :, cols] = val.astype(BF16)
        else:
            ucols = slice(cols.start % LRU_WIDTH, cols.start % LRU_WIDTH + LRU_PROJ_CHUNK)
            ring[slot_new, :, :, ucols] = val

    nchunks = w_ref.shape[1] // LRU_PROJ_CHUNK
    ahead = nchunks - (LRU_BLOCKS - 2)
    for c in range(ahead):
        project_chunk(c)

    cur = ring[slot_cur]
    nxt = jnp.where(i == ntiles + LRU_SCAN_LAG - 1, 0.0, ring[slot_nxt, 0:CONV_W - 1 - CONV_LEFT])
    u = _lru_conv(cur, ulast[...], nxt, cw_ref, cb_ref, tt)
    uconv_ref[...] = u
    _lru_coeffs(u, wrg_ref, ba_ref, bx_ref, lam_ref, a_scr, b_scr, tt,
                interleave=[functools.partial(project_chunk, c) for c in range(max(ahead, 0), nchunks)])
    ulast[...] = cur[tt - CONV_LEFT:tt]

    h_prev = h_scr[...]
    h = h_prev
    for k in range(tt):
        h = a_scr[k] * h + b_scr[k]
        s_ref[k] = h
    h = jnp.where(i >= LRU_SCAN_LAG, h, h_prev)
    h_scr[...] = h
    hfin_ref[...] = h


def _lru_in_fwd(x3, scale, shift, w, h0, conv_w, conv_b, wrg, ba, bx, lam, *, with_gate, tt=32):
    length = x3.shape[1]
    ntiles = length // tt
    blk = _nbytes((tt, BATCH, LRU_WIDTH), F32)
    proj_tile = lambda i: jnp.minimum(i, ntiles - 1)
    scan_tile = lambda i: jnp.maximum(i - LRU_SCAN_LAG, 0)
    tm_spec = lambda tile: pl.BlockSpec((tt, BATCH, LRU_WIDTH), lambda i: (tile(i), 0, 0))
    out_specs = [tm_spec(scan_tile), tm_spec(scan_tile),
                 pl.BlockSpec((BATCH, LRU_WIDTH), lambda i: (0, 0))]
    out_shape = [jax.ShapeDtypeStruct((length, BATCH, LRU_WIDTH), F32)] * 2 + [
        jax.ShapeDtypeStruct((BATCH, LRU_WIDTH), F32)]
    outs = [blk, blk]
    if with_gate:
        out_specs = [tm_spec(proj_tile)] + out_specs
        out_shape = [jax.ShapeDtypeStruct((length, BATCH, LRU_WIDTH), BF16)] + out_shape
        outs.append(blk // 2)
    res = pl.pallas_call(
        functools.partial(_lru_in_fwd_kernel, tt=tt, with_gate=with_gate),
        grid=(ntiles + LRU_SCAN_LAG,),
        in_specs=[pl.BlockSpec((BATCH, tt, D_MODEL), lambda i: (0, proj_tile(i), 0)),
                  _const_spec(scale.shape), _const_spec(shift.shape), _const_spec(w.shape),
                  _const_spec((BATCH, LRU_WIDTH)),
                  _const_spec((CONV_W, LRU_WIDTH)), _const_spec((1, LRU_WIDTH)),
                  _const_spec(wrg.shape),
                  _const_spec((1, LRU_WIDTH)), _const_spec((1, LRU_WIDTH)), _const_spec((1, LRU_WIDTH))],
        out_specs=out_specs,
        out_shape=out_shape,
        scratch_shapes=[
            pltpu.VMEM((LRU_SCAN_LAG + 1, tt, BATCH, LRU_WIDTH), F32),
            pltpu.VMEM((CONV_LEFT, BATCH, LRU_WIDTH), F32),
            pltpu.VMEM((BATCH, LRU_WIDTH), F32),
            pltpu.VMEM((tt, BATCH, LRU_WIDTH), F32), pltpu.VMEM((tt, BATCH, LRU_WIDTH), F32),
        ],
        compiler_params=pltpu.CompilerParams(
            dimension_semantics=("arbitrary",),
            vmem_limit_bytes=_vmem_limit(
                [blk] + outs, [_nbytes(w.shape, BF16), _nbytes(wrg.shape, BF16),
                               (LRU_SCAN_LAG + 3) * blk], 8 * blk)),
        name="lru_in_fwd",
    )(x3, scale, shift, w, h0, conv_w, conv_b, wrg, ba, bx, lam)
    return res if with_gate else (None,) + tuple(res)


def _lru_conv(cur, prev, nxt, cw_ref, cb_ref, tt):
    ext = jnp.concatenate([prev, cur, nxt], axis=0)
    u = cb_ref[...].reshape(1, 1, LRU_WIDTH)
    for tap in range(CONV_W):
        u = u + ext[tap:tap + tt] * cw_ref[tap:tap + 1, :].reshape(1, 1, LRU_WIDTH)
    return u


def _lru_coeffs(u, wrg_ref, ba_ref, bx_ref, lam_ref, a_scr, b_scr, tt, interleave=()):
    u2 = u.reshape(tt * BATCH, LRU_WIDTH)
    ub16 = u2.astype(BF16)
    col = lambda blk: slice(blk * LRU_BLOCK_W, (blk + 1) * LRU_BLOCK_W)
    lam = lam_ref[...]
    half_decay2 = (-0.5 * LRU_C * LOG2E) * (jnp.maximum(-lam, 0.0) + jnp.log1p(jnp.exp(-jnp.abs(lam))))
    half_ba = 0.5 * ba_ref[...]
    half_bx = 0.5 * bx_ref[...]
    rgs = []
    for blk in range(LRU_BLOCKS + 1):
        if blk < LRU_BLOCKS:
            rgs.append(jnp.dot(ub16[:, col(blk)], wrg_ref[blk], preferred_element_type=F32))
            if blk < len(interleave):
                interleave[blk]()
        if blk == 0:
            continue
        blk -= 1
        rg = rgs[blk]
        cols = col(blk)
        tr = jnp.tanh(rg[:, :LRU_BLOCK_W] + half_ba[:, cols])
        tg = jnp.tanh(rg[:, LRU_BLOCK_W:] + half_bx[:, cols])
        hd = half_decay2[:, cols]
        a = jnp.exp2(hd + hd * tr)
        x = 1.0 - a * a
        hu = 0.5 * u2[:, cols]
        b = jnp.where(x > 0.0, x * lax.rsqrt(x), 0.0) * (hu + hu * tg)
        a_scr[:, :, cols] = a.reshape(tt, BATCH, LRU_BLOCK_W)
        b_scr[:, :, cols] = b.reshape(tt, BATCH, LRU_BLOCK_W)


def _lru_bwd_state_kernel(u_ref, wrg_ref, ba_ref, bx_ref, lam_ref, hfin_ref, a_scr, b_scr, *, tt):
    @pl.when(pl.program_id(0) == 0)
    def _():
        hfin_ref[...] = jnp.zeros(hfin_ref.shape, F32)

    _lru_coeffs(u_ref[...], wrg_ref, ba_ref, bx_ref, lam_ref, a_scr, b_scr, tt)
    step = lambda k, h: a_scr[tt - 1 - k] * h + b_scr[tt - 1 - k]
    hfin_ref[...] = lax.fori_loop(0, tt, step, hfin_ref[...], unroll=4)


def _lru_bwd_state(u_conv, wrg, ba, bx, lam, *, tt=32):
    nsteps = u_conv.shape[0] // tt
    blk = _nbytes((tt, BATCH, LRU_WIDTH), F32)
    return pl.pallas_call(
        functools.partial(_lru_bwd_state_kernel, tt=tt),
        grid=(nsteps,),
        in_specs=[
            pl.BlockSpec((tt, BATCH, LRU_WIDTH), lambda i: (nsteps - 1 - i, 0, 0)),
            _const_spec(wrg.shape),
            _const_spec((1, LRU_WIDTH)), _const_spec((1, LRU_WIDTH)), _const_spec((1, LRU_WIDTH)),
        ],
        out_specs=pl.BlockSpec((BATCH, LRU_WIDTH), lambda i: (0, 0)),
        out_shape=jax.ShapeDtypeStruct((BATCH, LRU_WIDTH), F32),
        scratch_shapes=[pltpu.VMEM((tt, BATCH, LRU_WIDTH), F32)] * 2,
        compiler_params=pltpu.CompilerParams(
            dimension_semantics=("arbitrary",),
            vmem_limit_bytes=_vmem_limit([blk], [2 * blk, _nbytes(wrg.shape, BF16)], 6 * blk)),
        name="lru_bwd_state",
    )(u_conv, wrg, ba, bx, lam)


def _lru_bwd_out_kernel(u_ref, h0_ref, wrg_ref, ba_ref, bx_ref, lam_ref, sf_ref, gate_ref, h_ref,
                        g2_ref, w_ref, lng_ref, lnb_ref, o_ref,
                        h_scr, a_scr, b_scr, sb_ring, out_scr, *, tt):
    i = pl.program_id(0)
    ntiles = pl.num_programs(0) - 1

    @pl.when(i == 0)
    def _():
        h_scr[...] = h0_ref[...]
        sb_ring[...] = jnp.zeros(sb_ring.shape, F32)

    y = sb_ring[lax.rem(i + 1, 2)] + sf_ref[...]
    z = (_gelu_tanh(gate_ref[...].astype(F32)) * y).astype(BF16)
    perm = _row_perm(to_time_major=False)
    parts = []
    for tau in range(tt // PERM_STEPS):
        sub = z[tau * PERM_STEPS:(tau + 1) * PERM_STEPS].reshape(PERM_STEPS * BATCH, LRU_WIDTH)
        zb = jnp.dot(perm, sub, preferred_element_type=F32).astype(BF16)
        parts.append(zb.reshape(BATCH, PERM_STEPS, LRU_WIDTH))
    zb = jnp.concatenate(parts, axis=1).reshape(BATCH * tt, LRU_WIDTH)

    def project_chunk(c):
        cols = slice(c * LRU_PROJ_CHUNK, (c + 1) * LRU_PROJ_CHUNK)
        val = jnp.dot(zb, w_ref[:, cols], preferred_element_type=F32)
        out_scr[:, :, cols] = val.reshape(BATCH, tt, LRU_PROJ_CHUNK)

    _lru_coeffs(u_ref[...], wrg_ref, ba_ref, bx_ref, lam_ref, a_scr, b_scr, tt,
                interleave=[functools.partial(project_chunk, c)
                            for c in range(D_MODEL // LRU_PROJ_CHUNK)])
    slot = lax.rem(i, 2)

    h = h_scr[...]
    for idx in reversed(range(tt)):
        h = a_scr[idx] * h + b_scr[idx]
        sb_ring[slot, idx] = h
    h_scr[...] = h
    res = DEEPNORM_ALPHA * h_ref[...] + g2_ref[...] * out_scr[...]
    o_ref[...] = _layer_norm(res, lng_ref[...].reshape(1, 1, D_MODEL), lnb_ref[...].reshape(1, 1, D_MODEL))


def _lru_bwd_out(u_conv, h0, wrg, ba, bx, lam, s_f, gate_t, h3, g2, w_out, ln_g, ln_b, *, tt=32):
    ntiles = SEQ // tt
    scan_tile = lambda i: jnp.maximum(ntiles - 1 - i, 0)
    out_tile = lambda i: jnp.minimum(ntiles - i, ntiles - 1)
    tm_blk = lambda: pl.BlockSpec((tt, BATCH, LRU_WIDTH), lambda i: (out_tile(i), 0, 0))
    bm_blk = lambda: pl.BlockSpec((BATCH, tt, D_MODEL), lambda i: (0, out_tile(i), 0))
    blk = _nbytes((tt, BATCH, LRU_WIDTH), F32)
    return pl.pallas_call(
        functools.partial(_lru_bwd_out_kernel, tt=tt),
        grid=(ntiles + 1,),
        in_specs=[
            pl.BlockSpec((tt, BATCH, LRU_WIDTH), lambda i: (scan_tile(i), 0, 0)),
            _const_spec((BATCH, LRU_WIDTH)), _const_spec(wrg.shape),
            _const_spec((1, LRU_WIDTH)), _const_spec((1, LRU_WIDTH)), _const_spec((1, LRU_WIDTH)),
            tm_blk(), tm_blk(), bm_blk(),
            _const_spec((BATCH, 1, D_MODEL)), _const_spec(w_out.shape),
            _const_spec((1, D_MODEL)), _const_spec((1, D_MODEL))],
        out_specs=bm_blk(),
        out_shape=jax.ShapeDtypeStruct((BATCH, SEQ, D_MODEL), F32),
        scratch_shapes=[
            pltpu.VMEM((BATCH, LRU_WIDTH), F32),
            pltpu.VMEM((tt, BATCH, LRU_WIDTH), F32), pltpu.VMEM((tt, BATCH, LRU_WIDTH), F32),
            pltpu.VMEM((2, tt, BATCH, LRU_WIDTH), F32), pltpu.VMEM((BATCH, tt, D_MODEL), F32),
        ],
        compiler_params=pltpu.CompilerParams(
            dimension_semantics=("arbitrary",),
            vmem_limit_bytes=_vmem_limit(
                [blk] * 5, [5 * blk, _nbytes(w_out.shape, BF16), _nbytes(wrg.shape, BF16)], 8 * blk)),
        name="lru_bwd_out",
    )(u_conv, h0, wrg, ba, bx, lam, s_f, gate_t, h3, g2, w_out, ln_g, ln_b)


def _rope_tables():
    rows = SEQ // GRID_W
    row = jnp.repeat(jnp.arange(rows, dtype=F32), GRID_W)
    col = jnp.tile(jnp.arange(GRID_W, dtype=F32), rows)
    inv = ROPE_THETA ** (-jnp.arange(ROPE_FREQS, dtype=F32) / ROPE_FREQS)
    ang = jnp.concatenate([row[:, None] * inv, col[:, None] * inv], axis=-1)
    cos, sin = jnp.cos(ang), jnp.sin(ang)
    cos_t = jnp.tile(cos, (1, V7X_LANES // (HEAD_DIM // 2)))
    sin_t = jnp.tile(jnp.concatenate([-sin, sin], axis=-1), (1, V7X_LANES // HEAD_DIM))
    return cos_t, sin_t


def kernel(x, c, ctx, c_ctx, w_mod, b_mod, ln_g, ln_b, ffn_w_gate, ffn_w_up, ffn_w_down, mix_ab_w_in, attn_sink, pool_w, pool_scale, mix_ab_w_out, lru_w_in, lru_conv_w, lru_conv_b, lru_wa, lru_ba, lru_wx, lru_bx, lru_lambda, lru_w_out):
    assert x.shape == (BATCH, SEQ, D_MODEL) and ctx.shape == (BATCH, CTX_LEN, D_MODEL)
    c_all = jnp.concatenate(
        [c, c_ctx[None, :], jnp.zeros((MOD_ROWS - BATCH - 1, D_MODEL), F32)], axis=0)
    mod_all = _modulation(c_all, w_mod, b_mod)
    mod_lat = mod_all[:, :BATCH].reshape(DEPTH, BATCH, N_MOD, D_MODEL)
    mod_ctx = mod_all[:, BATCH:BATCH + 1].reshape(DEPTH, 1, N_MOD, D_MODEL)

    wg = ffn_w_gate.astype(BF16)
    wu = ffn_w_up.astype(BF16)
    wd = ffn_w_down.astype(BF16)
    row = lambda v: v.reshape(1, -1)
    n_ctx = BATCH * CTX_LEN

    h = x.reshape(BATCH * SEQ, D_MODEL)
    hc = ctx.reshape(n_ctx, D_MODEL)

    l = 0
    ml, mc = mod_lat[l], mod_ctx[l]
    ffn1 = ((l, 0), wg, wu, wd, row(ln_g[l, 0]), row(ln_b[l, 0]))
    ffn2 = ((l, 1), wg, wu, wd, row(ln_g[l, 2]), row(ln_b[l, 2]))
    h = _ffn(h, ml, 0, *ffn1, nb=BATCH, length=SEQ)
    hc = _ffn(hc, mc, 0, *ffn1, nb=1, length=n_ctx)
    w_in = mix_ab_w_in[0].astype(BF16)
    cos_t, sin_t = _rope_tables()
    q, k2, v2, u = _attn_inproj(h, ml, w_in, cos_t, sin_t, nb=BATCH, length=SEQ)
    q_c, k2_c, v2_c, u_c = _attn_inproj(hc, mc, w_in, None, None, nb=1, length=n_ctx)
    mix_args = (pool_w[0].astype(BF16), row(pool_scale[0]), mix_ab_w_out[0].astype(BF16),
                row(ln_g[l, 1]), row(ln_b[l, 1]))
    h = _attn_lat(attn_sink[0], q, k2, v2, k2_c, v2_c, u, h, ml, *mix_args)
    hc = _attn_ctx(attn_sink[0], q_c, k2_c, v2_c, u_c, hc, mc, *mix_args)
    h = _ffn(h, ml, 6, *ffn2, nb=BATCH, length=SEQ)
    hc = _ffn(hc, mc, 6, *ffn2, nb=1, length=n_ctx)

    l = 1
    ml, mc = mod_lat[l], mod_ctx[l]
    ffn1 = ((l, 0), wg, wu, wd, row(ln_g[l, 0]), row(ln_b[l, 0]))
    ffn2 = ((l, 1), wg, wu, wd, row(ln_g[l, 2]), row(ln_b[l, 2]))
    h = _ffn(h, ml, 0, *ffn1, nb=BATCH, length=SEQ)
    hc = _ffn(hc, mc, 0, *ffn1, nb=1, length=n_ctx)
    w_in = lru_w_in[0].astype(BF16)
    h3 = h.reshape(BATCH, SEQ, D_MODEL)
    wrg = (0.5 * jnp.concatenate([lru_wa[0], lru_wx[0]], axis=-1)).astype(BF16)
    conv_w, conv_b = lru_conv_w[0], row(lru_conv_b[0])
    zeros = jnp.zeros((BATCH, LRU_WIDTH), F32)
    dir_args = [(wrg[d], row(lru_ba[0, d]), row(lru_bx[0, d]), row(lru_lambda[0, d]))
                for d in range(2)]
    _, uc_t, _, hf_ctx = _lru_in_fwd(hc.reshape(BATCH, CTX_LEN, D_MODEL), mc[:, 4:5, :], mc[:, 3:4, :],
                                     w_in[:, LRU_WIDTH:], zeros, conv_w, conv_b, *dir_args[0],
                                     with_gate=False)
    gate_t, u_t, s_f, _ = _lru_in_fwd(h3, ml[:, 4:5, :], ml[:, 3:4, :], w_in, hf_ctx, conv_w, conv_b,
                                      *dir_args[0], with_gate=True)
    hb_ctx = _lru_bwd_state(uc_t, *dir_args[1])
    h3 = _lru_bwd_out(u_t, hb_ctx, *dir_args[1], s_f, gate_t, h3, ml[:, 5:6, :],
                      lru_w_out[0].astype(BF16), row(ln_g[l, 1]), row(ln_b[l, 1]))
    h = _ffn(h3.reshape(BATCH * SEQ, D_MODEL), ml, 6, *ffn2, nb=BATCH, length=SEQ)
    return h.reshape(BATCH, SEQ, D_MODEL)
```

```python
import functools

import jax
import jax.numpy as jnp
from jax import lax
from jax.experimental import pallas as pl
from jax.experimental.pallas import tpu as pltpu

D_MODEL = 1024
BATCH = 16
SEQ = 2048
DEPTH = 2
GRID_W = 64
CTX_LEN = 256
HEAD_DIM = 64
ATT_HEADS = 8
ATT_KV_HEADS = 2
ATT_GROUPS = ATT_HEADS // ATT_KV_HEADS
ATT_WIDTH = ATT_HEADS * HEAD_DIM
KV_WIDTH = ATT_KV_HEADS * HEAD_DIM
WINDOW = 128
BLOCK = 128
ATT_SCALE = HEAD_DIM ** -0.5
LOG2E = 1.4426950408889634
ROPE_THETA = 10000.0
ROPE_FREQS = HEAD_DIM // 4
POOL_WINDOWS = (2, 4, 8, 16)
POOL_WIDTH = D_MODEL // 2
POOL_GROUP_W = POOL_WIDTH // len(POOL_WINDOWS)
MIX_AB_IN = ATT_WIDTH + 2 * KV_WIDTH + POOL_WIDTH
LRU_WIDTH = D_MODEL
LRU_BLOCKS = 8
LRU_BLOCK_W = LRU_WIDTH // LRU_BLOCKS
LRU_C = 8.0
CONV_W = 4
CONV_LEFT = (CONV_W - 1) // 2
D_FF = 2816
N_MOD = 9
LN_EPS = 1e-5
NEG_INF = -1e30
DEEPNORM_ALPHA = (2 * DEPTH) ** 0.25

V7X_LANES = 128
V7X_SUBLANES = 8
V7X_VMEM_BYTES = 64 * 1024 * 1024
V7X_VMEM_USABLE_BYTES = 60000 * 1024

F32 = jnp.float32
BF16 = jnp.bfloat16

POOL_HALO = V7X_SUBLANES
PERM_STEPS = 16
KEY_SLAB_W = ATT_KV_HEADS * V7X_LANES
VAL_SLAB_W = 2 * ATT_KV_HEADS * V7X_LANES
LRU_PROJ_CHUNK = 256
LRU_SCAN_LAG = 2
SCORE_LOOKAHEAD = 2
EPILOGUE_BLOCKS = 2
MOD_ROWS = 24


def _nbytes(shape, dtype):
    n = 1
    for s in shape:
        n *= s
    return n * jnp.dtype(dtype).itemsize


def _vmem_limit(pipelined, resident, temporaries):
    est = 2 * sum(pipelined) + sum(resident) + temporaries
    return int(min(V7X_VMEM_USABLE_BYTES, max(est * 5 // 4, 16 * 1024 * 1024)))


def _const_spec(shape):
    nd = len(shape)
    return pl.BlockSpec(shape, lambda *_: (0,) * nd, pipeline_mode=pl.Buffered(1))


def _tok_spec(tm, width, nt):
    return pl.BlockSpec((tm, width), lambda b, t: (b * nt + t, 0))


def _mod_spec():
    return pl.BlockSpec((1, N_MOD, D_MODEL), lambda b, t: (b, 0, 0))


def _layer_norm(z, g, b):
    mu = jnp.mean(z, axis=-1, keepdims=True)
    zc = z - mu
    var = jnp.mean(zc * zc, axis=-1, keepdims=True)
    return zc * lax.rsqrt(var + LN_EPS) * g + b


def _gelu_tanh(x):
    return 0.5 * x * (1.0 + jnp.tanh(0.7978845608028654 * (x + 0.044715 * (x * x * x))))


def _mod_kernel(c_ref, w_ref, b_ref, o_ref):
    c = c_ref[...]
    a_hi, a_lo = _split_bf16(c * jax.nn.sigmoid(c))
    w_hi, w_lo = _split_bf16(w_ref[0])
    acc = jnp.dot(a_hi, w_lo, preferred_element_type=F32) + jnp.dot(a_lo, w_hi, preferred_element_type=F32)
    o_ref[0] = acc + jnp.dot(a_hi, w_hi, preferred_element_type=F32) + b_ref[0]


def _modulation(c_all, w_mod, b_mod):
    tn = 2304
    n_out = N_MOD * D_MODEL
    blocks = [_nbytes((1, D_MODEL, tn), F32), _nbytes((1, MOD_ROWS, tn), F32)]
    return pl.pallas_call(
        _mod_kernel,
        grid=(DEPTH, n_out // tn),
        in_specs=[
            pl.BlockSpec((MOD_ROWS, D_MODEL), lambda l, j: (0, 0)),
            pl.BlockSpec((1, D_MODEL, tn), lambda l, j: (l, 0, j)),
            pl.BlockSpec((1, 1, tn), lambda l, j: (l, 0, j)),
        ],
        out_specs=pl.BlockSpec((1, MOD_ROWS, tn), lambda l, j: (l, 0, j)),
        out_shape=jax.ShapeDtypeStruct((DEPTH, MOD_ROWS, n_out), F32),
        compiler_params=pltpu.CompilerParams(
            dimension_semantics=("arbitrary", "arbitrary"),
            vmem_limit_bytes=_vmem_limit(blocks, [], 4 * blocks[0])),
        name="modulation",
    )(c_all, w_mod, b_mod.reshape(DEPTH, 1, n_out))


def _ffn_kernel(x_ref, mod_ref, wg_ref, wu_ref, wd_ref, lng_ref, lnb_ref, o_ref, *, j0, sub):
    shift = mod_ref[0, j0:j0 + 1, :]
    scale = mod_ref[0, j0 + 1:j0 + 2, :]
    gate = mod_ref[0, j0 + 2:j0 + 3, :]
    sizes = [sub] * (x_ref.shape[0] // sub)
    starts = [s * sub for s in range(len(sizes))]
    nsub = len(sizes)

    def gate_up(s):
        x = x_ref[starts[s]:starts[s] + sizes[s], :]
        xin = (x * (1.0 + scale) + shift).astype(BF16)
        return (jnp.dot(xin, wg_ref[...], preferred_element_type=F32),
                jnp.dot(xin, wu_ref[...], preferred_element_type=F32))

    pending = gate_up(0)
    for s in range(nsub):
        g, u = pending
        if s + 1 < nsub:
            pending = gate_up(s + 1)
        rows = slice(starts[s], starts[s] + sizes[s])
        a = (g * jax.nn.sigmoid(g) * u).astype(BF16)
        y = jnp.dot(a, wd_ref[...], preferred_element_type=F32)
        z = DEEPNORM_ALPHA * x_ref[rows, :] + (0.5 * gate) * y
        o_ref[rows, :] = _layer_norm(z, lng_ref[...], lnb_ref[...])


def _ffn(x, mod, j0, which, wg, wu, wd, ln_g, ln_b, *, nb, length, tm=1024, sub=256):
    nt = length // tm
    tok = _nbytes((tm, D_MODEL), F32)
    weights = [_nbytes(w.shape[2:], BF16) for w in (wg, wu, wd)]
    temps = 3 * _nbytes((tm, D_FF), F32) + 4 * tok
    wspec = lambda w: pl.BlockSpec((None, None) + w.shape[2:], lambda b, t: which + (0, 0),
                                   pipeline_mode=pl.Buffered(1))
    return pl.pallas_call(
        functools.partial(_ffn_kernel, j0=j0, sub=sub),
        grid=(nb, nt),
        in_specs=[
            _tok_spec(tm, D_MODEL, nt),
            _mod_spec(),
            wspec(wg), wspec(wu), wspec(wd),
            _const_spec((1, D_MODEL)), _const_spec((1, D_MODEL)),
        ],
        out_specs=_tok_spec(tm, D_MODEL, nt),
        out_shape=jax.ShapeDtypeStruct((nb * length, D_MODEL), F32),
        compiler_params=pltpu.CompilerParams(
            dimension_semantics=("arbitrary", "arbitrary"),
            vmem_limit_bytes=_vmem_limit([tok, tok], weights, temps)),
        name="ffn",
    )(x, mod, wg, wu, wd, ln_g, ln_b)


def _dup_halves(z, lane):
    zr = pltpu.roll(z, HEAD_DIM, 1)
    lo = lane < HEAD_DIM
    return jnp.where(lo, z, zr), jnp.where(lo, zr, z)


def _attn_inproj_kernel(*refs, rope):
    if rope:
        x_ref, mod_ref, w_ref, cos_ref, sin_ref, q_ref, k_ref, v_ref, u_ref = refs
    else:
        x_ref, mod_ref, w_ref, q_ref, k_ref, v_ref, u_ref = refs
    x = x_ref[...]
    xin = (x * (1.0 + mod_ref[0, 4:5, :]) + mod_ref[0, 3:4, :]).astype(BF16)
    p = jnp.dot(xin, w_ref[...], preferred_element_type=F32)
    lane = lax.broadcasted_iota(jnp.int32, (x.shape[0], V7X_LANES), 1)
    first_half = (lane & (HEAD_DIM - 1)) < HEAD_DIM // 2

    def rot(z):
        if not rope:
            return z
        zr = jnp.where(first_half, pltpu.roll(z, V7X_LANES - HEAD_DIM // 2, 1),
                       pltpu.roll(z, HEAD_DIM // 2, 1))
        return z * cos_ref[...] + zr * sin_ref[...]

    for c in range(ATT_WIDTH // V7X_LANES):
        sl = slice(c * V7X_LANES, (c + 1) * V7X_LANES)
        q_ref[:, sl] = (rot(p[:, sl]) * (ATT_SCALE * LOG2E)).astype(BF16)
    k0, k1 = _dup_halves(rot(p[:, ATT_WIDTH:ATT_WIDTH + KV_WIDTH]), lane)
    k_ref[...] = jnp.concatenate([k0, k1], axis=-1).T.astype(BF16)
    v = p[:, ATT_WIDTH + KV_WIDTH:ATT_WIDTH + 2 * KV_WIDTH]
    vr = pltpu.roll(v, HEAD_DIM, 1)
    lo = lane < HEAD_DIM
    slabs = (jnp.where(lo, v, 1.0), jnp.where(lo, 1.0, vr), jnp.where(lo, vr, 1.0), jnp.where(lo, 1.0, v))
    for i, slab in enumerate(slabs):
        v_ref[:, i * V7X_LANES:(i + 1) * V7X_LANES] = slab.astype(BF16)
    u_ref[...] = p[:, ATT_WIDTH + 2 * KV_WIDTH:]


def _attn_inproj(x, mod, w_in, cos, sin, *, nb, length, tm=1024):
    nt = length // tm
    rope = cos is not None
    tok = _nbytes((tm, D_MODEL), F32)
    outs = [_nbytes((tm, ATT_WIDTH), BF16), _nbytes((tm, KEY_SLAB_W + VAL_SLAB_W), BF16),
            _nbytes((tm, POOL_WIDTH), F32)]
    in_specs = [_tok_spec(tm, D_MODEL, nt), _mod_spec(), _const_spec(w_in.shape)]
    args = [x, mod, w_in]
    if rope:
        in_specs += [pl.BlockSpec((tm, V7X_LANES), lambda b, t: (t, 0))] * 2
        args += [cos, sin]
    rows = nb * length
    return pl.pallas_call(
        functools.partial(_attn_inproj_kernel, rope=rope),
        grid=(nb, nt),
        in_specs=in_specs,
        out_specs=[
            _tok_spec(tm, ATT_WIDTH, nt),
            pl.BlockSpec((KEY_SLAB_W, tm), lambda b, t: (0, b * nt + t)),
            _tok_spec(tm, VAL_SLAB_W, nt),
            _tok_spec(tm, POOL_WIDTH, nt),
        ],
        out_shape=[
            jax.ShapeDtypeStruct((rows, ATT_WIDTH), BF16),
            jax.ShapeDtypeStruct((KEY_SLAB_W, rows), BF16),
            jax.ShapeDtypeStruct((rows, VAL_SLAB_W), BF16),
            jax.ShapeDtypeStruct((rows, POOL_WIDTH), F32),
        ],
        compiler_params=pltpu.CompilerParams(
            dimension_semantics=("arbitrary", "arbitrary"),
            vmem_limit_bytes=_vmem_limit([tok] + outs, [_nbytes(w_in.shape, BF16)],
                                         3 * _nbytes((tm, MIX_AB_IN), F32))),
        name="attn_inproj",
    )(*args)


def _attend_scores(q_blk, keys, kh):
    lane = lax.broadcasted_iota(jnp.int32, (BLOCK, V7X_LANES), 1)
    lo = lane < HEAD_DIM
    zero = jnp.zeros((BLOCK, V7X_LANES), BF16)
    parts = []
    for g in range(ATT_GROUPS):
        c = (kh * ATT_GROUPS + g) // 2
        qc = q_blk[:, c * V7X_LANES:(c + 1) * V7X_LANES]
        parts.append(jnp.where(lo if g % 2 == 0 else jnp.logical_not(lo), qc, zero))
    qs = jnp.concatenate(parts, axis=0)
    return [jnp.dot(qs, k, preferred_element_type=F32) for k in keys]


def _attend_finish(scores, values_even, values_odd, segments, sink_ref, kh):
    lane = lax.broadcasted_iota(jnp.int32, (BLOCK, V7X_LANES), 1)
    lo = lane < HEAD_DIM
    probs = ([[] for _ in scores], [[] for _ in scores])
    sink_terms = []
    for g in range(ATT_GROUPS):
        rows = slice(g * BLOCK, (g + 1) * BLOCK)
        segs = [[s[rows, a:b] if bias is None else s[rows, a:b] + bias for a, b, bias in seg]
                for s, seg in zip(scores, segments)]
        sink = sink_ref[kh * ATT_GROUPS + g] * LOG2E
        chunks = [x[:, c:c + V7X_LANES] for sl in segs for x in sl
                  for c in range(0, x.shape[1], V7X_LANES)]
        m = functools.reduce(jnp.maximum, chunks).max(axis=-1, keepdims=True)
        m = jnp.maximum(m, sink)
        sink_terms.append(jnp.exp2(sink - m))
        for i, sl in enumerate(segs):
            e = [jnp.exp2(x - m).astype(BF16) for x in sl]
            probs[g % 2][i].append(e[0] if len(e) == 1 else jnp.concatenate(e, axis=-1))

    def weighted_values(ps, vals):
        o = None
        for p, v in zip(ps, vals):
            t = jnp.dot(jnp.concatenate(p, axis=0), v, preferred_element_type=F32)
            o = t if o is None else o + t
        return o

    o_par = (weighted_values(probs[0], values_even), weighted_values(probs[1], values_odd))
    outs = []
    for g in range(ATT_GROUPS):
        o = o_par[g % 2][(g // 2) * BLOCK:(g // 2 + 1) * BLOCK]
        rinv = 1.0 / (o + sink_terms[g])
        outs.append(o * pltpu.roll(rinv, HEAD_DIM, 1))
    return jnp.where(lo, outs[0], outs[1]), jnp.where(lo, outs[2], outs[3])


def _pool_sums(ubuf_ref, nblk, band_ref):
    u_hi, u_lo = _split_bf16(ubuf_ref[...])
    sums = {}
    for j in range(nblk):
        win = slice(j * BLOCK, (j + 2) * BLOCK)
        for gi in range(len(POOL_WINDOWS)):
            cols = slice(gi * POOL_GROUP_W, (gi + 1) * POOL_GROUP_W)
            band = band_ref[gi]
            sums[j, gi] = (jnp.dot(band, u_hi[win, cols], preferred_element_type=F32)
                           + jnp.dot(band, u_lo[win, cols], preferred_element_type=F32))
    return sums


def _pool_finish(sums, ubuf_ref, blocks, pos0, length, wpool_ref, pscale_ref, mix_scr):
    for j in blocks:
        pos = pos0 + j * BLOCK + lax.broadcasted_iota(jnp.int32, (BLOCK, 1), 0)
        rows = slice(POOL_HALO + j * BLOCK, POOL_HALO + (j + 1) * BLOCK)
        for gi, w in enumerate(POOL_WINDOWS):
            r = w // 2
            cols = slice(gi * POOL_GROUP_W, (gi + 1) * POOL_GROUP_W)
            cnt = (jnp.minimum(pos + r, length - 1) - jnp.maximum(pos - r, 0) + 1).astype(F32)
            d = sums[j, gi] / cnt - ubuf_ref[rows, cols]
            y = jnp.dot(d.astype(BF16), wpool_ref[gi], preferred_element_type=F32)
            mix_scr[j * BLOCK:(j + 1) * BLOCK, ATT_WIDTH + gi * POOL_GROUP_W:
                    ATT_WIDTH + (gi + 1) * POOL_GROUP_W] = (y * pscale_ref[:, cols]).astype(BF16)


def _split_bf16(x):
    hi = x.astype(BF16)
    return hi, (x - hi.astype(F32)).astype(BF16)


def _pool_bands():
    i = jnp.arange(BLOCK)[:, None]
    c = jnp.arange(2 * BLOCK)[None, :]
    return jnp.stack([(jnp.abs(c - POOL_HALO - i) <= w // 2) for w in POOL_WINDOWS]).astype(BF16)


def _attn_out(mix_scr, rows, wout_ref, h_ref, gate, lng_ref, lnb_ref, o_ref):
    y = jnp.dot(mix_scr[rows, :], wout_ref[...], preferred_element_type=F32)
    z = DEEPNORM_ALPHA * h_ref[rows, :] + gate * y
    o_ref[rows, :] = _layer_norm(z, lng_ref[...], lnb_ref[...])


def _attn_lat_kernel(sink_ref, q_ref, kc_ref, kp_ref, kn_ref, vc_ref, vp_ref, vn_ref,
                     kctx_ref, vctx_ref, uc_ref, up_ref, un_ref, h_ref, mod_ref, bias_ref, band_ref,
                     wpool_ref, pscale_ref, wout_ref, lng_ref, lnb_ref, o_ref,
                     kbuf, vbuf, ubuf, mix_scr, *, tq, length):
    t = pl.program_id(1)
    last = pl.num_programs(1) - 1
    nblk = tq // BLOCK
    kbuf[:, 0:BLOCK] = kp_ref[...]
    kbuf[:, BLOCK:BLOCK + tq] = kc_ref[...]
    kbuf[:, BLOCK + tq:] = kn_ref[...]
    vbuf[0:BLOCK] = vp_ref[...]
    vbuf[BLOCK:BLOCK + tq] = vc_ref[...]
    vbuf[BLOCK + tq:] = vn_ref[...]
    ubuf[0:POOL_HALO] = jnp.where(t == 0, 0.0, up_ref[...])
    ubuf[POOL_HALO:POOL_HALO + tq] = uc_ref[...]
    ubuf[POOL_HALO + tq:2 * POOL_HALO + tq] = jnp.where(t == last, 0.0, un_ref[...])
    ubuf[2 * POOL_HALO + tq:] = jnp.zeros((2 * BLOCK - 2 * POOL_HALO, POOL_WIDTH), F32)
    lanes = lambda i: slice(i * V7X_LANES, (i + 1) * V7X_LANES)
    gate = mod_ref[0, 5:6, :]
    units = [(j, kh) for j in range(nblk) for kh in range(ATT_KV_HEADS)]

    def scores_of(j, kh):
        win = slice(j * BLOCK, j * BLOCK + 3 * BLOCK)
        return _attend_scores(q_ref[j * BLOCK:(j + 1) * BLOCK, :],
                              [kbuf[lanes(kh), win], kctx_ref[lanes(kh), :]], kh)

    sums = _pool_sums(ubuf, nblk, band_ref)
    pending = [scores_of(*u) for u in units[:SCORE_LOOKAHEAD]]
    _pool_finish(sums, ubuf, range(nblk), t * tq, length, wpool_ref, pscale_ref, mix_scr)
    for n, (j, kh) in enumerate(units):
        scores = pending.pop(0)
        if n + SCORE_LOOKAHEAD < len(units):
            pending.append(scores_of(*units[n + SCORE_LOOKAHEAD]))
        blk = t * nblk + j
        bidx = jnp.where(blk == 0, 0, jnp.where(blk == length // BLOCK - 1, 2, 1))
        bias = bias_ref[bidx]
        segments = [[(0, BLOCK, bias[:, :BLOCK]), (BLOCK, 2 * BLOCK, None),
                     (2 * BLOCK, 3 * BLOCK, bias[:, BLOCK:])], [(0, CTX_LEN, None)]]
        win = slice(j * BLOCK, j * BLOCK + 3 * BLOCK)
        rows = slice(j * BLOCK, (j + 1) * BLOCK)
        c0, c1 = _attend_finish(
            scores, [vbuf[win, lanes(2 * kh)], vctx_ref[:, lanes(2 * kh)]],
            [vbuf[win, lanes(2 * kh + 1)], vctx_ref[:, lanes(2 * kh + 1)]],
            segments, sink_ref, kh)
        mix_scr[rows, lanes(2 * kh)] = c0.astype(BF16)
        mix_scr[rows, lanes(2 * kh + 1)] = c1.astype(BF16)
        per_epilogue = EPILOGUE_BLOCKS * ATT_KV_HEADS
        if n > 0 and n % per_epilogue == 0:
            e = n // per_epilogue - 1
            done = slice(e * EPILOGUE_BLOCKS * BLOCK, (e + 1) * EPILOGUE_BLOCKS * BLOCK)
            _attn_out(mix_scr, done, wout_ref, h_ref, gate, lng_ref, lnb_ref, o_ref)
    _attn_out(mix_scr, slice(tq - EPILOGUE_BLOCKS * BLOCK, tq), wout_ref, h_ref, gate, lng_ref,
              lnb_ref, o_ref)


def _attn_ctx_kernel(sink_ref, q_ref, kctx_ref, vctx_ref, uc_ref, h_ref, mod_ref, band_ref,
                     wpool_ref, pscale_ref, wout_ref, lng_ref, lnb_ref, o_ref,
                     ubuf, mix_scr, *, tq):
    ubuf[0:POOL_HALO] = jnp.zeros((POOL_HALO, POOL_WIDTH), F32)
    ubuf[POOL_HALO:POOL_HALO + tq] = uc_ref[...]
    ubuf[POOL_HALO + tq:] = jnp.zeros((2 * BLOCK - POOL_HALO, POOL_WIDTH), F32)
    lanes = lambda i: slice(i * V7X_LANES, (i + 1) * V7X_LANES)
    nblk = tq // BLOCK
    units = [(j, kh) for j in range(nblk) for kh in range(ATT_KV_HEADS)]
    scores_of = lambda j, kh: _attend_scores(q_ref[j * BLOCK:(j + 1) * BLOCK, :],
                                             [kctx_ref[lanes(kh), :]], kh)
    sums = _pool_sums(ubuf, nblk, band_ref)
    pending = [scores_of(*u) for u in units[:SCORE_LOOKAHEAD]]
    _pool_finish(sums, ubuf, range(nblk), 0, tq, wpool_ref, pscale_ref, mix_scr)
    for n, (j, kh) in enumerate(units):
        scores = pending.pop(0)
        if n + SCORE_LOOKAHEAD < len(units):
            pending.append(scores_of(*units[n + SCORE_LOOKAHEAD]))
        rows = slice(j * BLOCK, (j + 1) * BLOCK)
        c0, c1 = _attend_finish(scores, [vctx_ref[:, lanes(2 * kh)]],
                                [vctx_ref[:, lanes(2 * kh + 1)]], [[(0, CTX_LEN, None)]],
                                sink_ref, kh)
        mix_scr[rows, lanes(2 * kh)] = c0.astype(BF16)
        mix_scr[rows, lanes(2 * kh + 1)] = c1.astype(BF16)
    _attn_out(mix_scr, slice(0, tq), wout_ref, h_ref, mod_ref[0, 5:6, :], lng_ref, lnb_ref, o_ref)


def _window_bias():
    qi = jnp.arange(BLOCK)[:, None]
    kc = jnp.arange(3 * BLOCK)[None, :]
    band = jnp.abs(qi + BLOCK - kc) <= WINDOW
    first = band & (kc >= BLOCK)
    final = band & (kc < 2 * BLOCK)
    full = jnp.where(jnp.stack([first, band, final]), 0.0, NEG_INF).astype(F32)
    return jnp.concatenate([full[:, :, :BLOCK], full[:, :, 2 * BLOCK:]], axis=-1)


def _smem_spec():
    return pl.BlockSpec(memory_space=pltpu.SMEM)


def _attn_lat(sink, q, k2, v2, kctx, vctx, u, h, mod, wpool, pscale, wout, ln_g, ln_b, *, tq=1024):
    nt = SEQ // tq
    bpt = tq // BLOCK
    bps = SEQ // BLOCK
    hpt = tq // POOL_HALO
    hps = SEQ // POOL_HALO
    cur = lambda w: pl.BlockSpec((tq, w), lambda b, t: (b * nt + t, 0))
    prev_blk = lambda rows, w, per_tile, per_samp: pl.BlockSpec(
        (rows, w), lambda b, t: (b * per_samp + jnp.maximum(t * per_tile - 1, 0), 0))
    next_blk = lambda rows, w, per_tile, per_samp: pl.BlockSpec(
        (rows, w), lambda b, t: (b * per_samp + jnp.minimum((t + 1) * per_tile, per_samp - 1), 0))
    ctx_blk = lambda w: pl.BlockSpec((CTX_LEN, w), lambda b, t: (b, 0))
    kv_specs = lambda w: [cur(w), prev_blk(BLOCK, w, bpt, bps), next_blk(BLOCK, w, bpt, bps)]
    key_specs = [
        pl.BlockSpec((KEY_SLAB_W, tq), lambda b, t: (0, b * nt + t)),
        pl.BlockSpec((KEY_SLAB_W, BLOCK), lambda b, t: (0, b * bps + jnp.maximum(t * bpt - 1, 0))),
        pl.BlockSpec((KEY_SLAB_W, BLOCK),
                     lambda b, t: (0, b * bps + jnp.minimum((t + 1) * bpt, bps - 1))),
    ]
    bias = _window_bias()
    bands = _pool_bands()
    tok = _nbytes((tq, D_MODEL), F32)
    kv_rows = tq + 2 * BLOCK
    u_rows = tq + 2 * BLOCK
    pipelined = [_nbytes((tq, ATT_WIDTH), BF16), 2 * _nbytes((kv_rows, KEY_SLAB_W + VAL_SLAB_W), BF16),
                 _nbytes((tq + 2 * POOL_HALO, POOL_WIDTH), F32), 2 * tok]
    resident = [_nbytes(bias.shape, F32), _nbytes(bands.shape, BF16), _nbytes(wpool.shape, BF16),
                _nbytes(wout.shape, BF16)]
    scratch = [_nbytes((kv_rows, KEY_SLAB_W + VAL_SLAB_W), BF16),
               _nbytes((u_rows, POOL_WIDTH), F32), _nbytes((tq, D_MODEL), BF16)]
    return pl.pallas_call(
        functools.partial(_attn_lat_kernel, tq=tq, length=SEQ),
        grid=(BATCH, nt),
        in_specs=[_smem_spec(), cur(ATT_WIDTH)] + key_specs + kv_specs(VAL_SLAB_W) + [
            pl.BlockSpec((KEY_SLAB_W, CTX_LEN), lambda b, t: (0, b)), ctx_blk(VAL_SLAB_W),
            cur(POOL_WIDTH), prev_blk(POOL_HALO, POOL_WIDTH, hpt, hps),
            next_blk(POOL_HALO, POOL_WIDTH, hpt, hps),
            cur(D_MODEL), _mod_spec(),
            _const_spec(bias.shape), _const_spec(bands.shape),
            _const_spec(wpool.shape), _const_spec((1, POOL_WIDTH)),
            _const_spec(wout.shape), _const_spec((1, D_MODEL)), _const_spec((1, D_MODEL)),
        ],
        out_specs=cur(D_MODEL),
        out_shape=jax.ShapeDtypeStruct((BATCH * SEQ, D_MODEL), F32),
        scratch_shapes=[
            pltpu.VMEM((KEY_SLAB_W, kv_rows), BF16), pltpu.VMEM((kv_rows, VAL_SLAB_W), BF16),
            pltpu.VMEM((u_rows, POOL_WIDTH), F32), pltpu.VMEM((tq, D_MODEL), BF16),
        ],
        compiler_params=pltpu.CompilerParams(
            dimension_semantics=("arbitrary", "arbitrary"),
            vmem_limit_bytes=_vmem_limit(pipelined, resident + scratch, 8 * tok)),
        name="attn_latent",
    )(sink, q, k2, k2, k2, v2, v2, v2, kctx, vctx, u, u, u, h, mod, bias, bands, wpool, pscale, wout,
      ln_g, ln_b)


def _attn_ctx(sink, q, kctx, vctx, u, h, mod, wpool, pscale, wout, ln_g, ln_b):
    tq = CTX_LEN
    blk = lambda w: pl.BlockSpec((tq, w), lambda b, t: (b, 0))
    bands = _pool_bands()
    tok = _nbytes((tq, D_MODEL), F32)
    u_rows = tq + 2 * BLOCK
    pipelined = [_nbytes((tq, ATT_WIDTH), BF16), _nbytes((tq, KEY_SLAB_W + VAL_SLAB_W), BF16),
                 _nbytes((tq, POOL_WIDTH), F32), 2 * tok]
    resident = [_nbytes(bands.shape, BF16), _nbytes(wpool.shape, BF16), _nbytes(wout.shape, BF16)]
    scratch = [_nbytes((u_rows, POOL_WIDTH), F32), _nbytes((tq, D_MODEL), BF16)]
    return pl.pallas_call(
        functools.partial(_attn_ctx_kernel, tq=tq),
        grid=(BATCH, 1),
        in_specs=[
            _smem_spec(),
            blk(ATT_WIDTH), pl.BlockSpec((KEY_SLAB_W, tq), lambda b, t: (0, b)),
            blk(VAL_SLAB_W), blk(POOL_WIDTH), blk(D_MODEL),
            pl.BlockSpec((1, N_MOD, D_MODEL), lambda b, t: (0, 0, 0)),
            _const_spec(bands.shape), _const_spec(wpool.shape), _const_spec((1, POOL_WIDTH)),
            _const_spec(wout.shape), _const_spec((1, D_MODEL)), _const_spec((1, D_MODEL)),
        ],
        out_specs=blk(D_MODEL),
        out_shape=jax.ShapeDtypeStruct((BATCH * CTX_LEN, D_MODEL), F32),
        scratch_shapes=[
            pltpu.VMEM((u_rows, POOL_WIDTH), F32), pltpu.VMEM((tq, D_MODEL), BF16),
        ],
        compiler_params=pltpu.CompilerParams(
            dimension_semantics=("arbitrary", "arbitrary"),
            vmem_limit_bytes=_vmem_limit(pipelined, resident + scratch, 8 * tok)),
        name="attn_context",
    )(sink, q, kctx, vctx, u, h, mod, bands, wpool, pscale, wout, ln_g, ln_b)


def _row_perm(to_time_major):
    n = BATCH * PERM_STEPS
    r = lax.broadcasted_iota(jnp.int32, (n, n), 0)
    c = lax.broadcasted_iota(jnp.int32, (n, n), 1)
    if to_time_major:
        hit = ((r // BATCH) == (c % PERM_STEPS)) & ((r % BATCH) == (c // PERM_STEPS))
    else:
        hit = ((r // PERM_STEPS) == (c % BATCH)) & ((r % PERM_STEPS) == (c // BATCH))
    return jnp.where(hit, 1.0, 0.0).astype(BF16)


def _lru_time_major_input(x_ref, scale_ref, shift_ref, tt):
    xin = (x_ref[...] * (1.0 + scale_ref[...]) + shift_ref[...]).astype(BF16)
    perm = _row_perm(to_time_major=True)
    parts = []
    for tau in range(tt // PERM_STEPS):
        sub = xin[:, tau * PERM_STEPS:(tau + 1) * PERM_STEPS, :].reshape(BATCH * PERM_STEPS, D_MODEL)
        parts.append(jnp.dot(perm, sub, preferred_element_type=F32).astype(BF16))
    return jnp.concatenate(parts, axis=0)


def _lru_in_fwd_kernel(*refs, tt, with_gate):
    (x_ref, scale_ref, shift_ref, w_ref, h0_ref, cw_ref, cb_ref, wrg_ref, ba_ref, bx_ref,
     lam_ref) = refs[:11]
    outs = refs[11:]
    if with_gate:
        gate_ref, uconv_ref, s_ref, hfin_ref, ring, ulast, h_scr, a_scr, b_scr = outs
    else:
        uconv_ref, s_ref, hfin_ref, ring, ulast, h_scr, a_scr, b_scr = outs
    i = pl.program_id(0)
    ntiles = pl.num_programs(0) - LRU_SCAN_LAG

    @pl.when(i == 0)
    def _():
        ring[...] = jnp.zeros(ring.shape, F32)
        ulast[...] = jnp.zeros(ulast.shape, F32)
        h_scr[...] = h0_ref[...]

    xt = _lru_time_major_input(x_ref, scale_ref, shift_ref, tt)
    slot_new = lax.rem(i, LRU_SCAN_LAG + 1)
    slot_cur = lax.rem(i + 1, LRU_SCAN_LAG + 1)
    slot_nxt = lax.rem(i + 2, LRU_SCAN_LAG + 1)

    def project_chunk(c):
        cols = slice(c * LRU_PROJ_CHUNK, (c + 1) * LRU_PROJ_CHUNK)
        val = jnp.dot(xt, w_ref[:, cols], preferred_element_type=F32)
        val = val.reshape(tt, BATCH, LRU_PROJ_CHUNK)
        if with_gate and c < LRU_WIDTH // LRU_PROJ_CHUNK:
            gate_ref[:, :, cols] = val.astype(BF16)
        else:
            ucols = slice(cols.start % LRU_WIDTH, cols.start % LRU_WIDTH + LRU_PROJ_CHUNK)
            ring[slot_new, :, :, ucols] = val

    nchunks = w_ref.shape[1] // LRU_PROJ_CHUNK
    ahead = nchunks - (LRU_BLOCKS - 2)
    for c in range(ahead):
        project_chunk(c)

    cur = ring[slot_cur]
    nxt = jnp.where(i == ntiles + LRU_SCAN_LAG - 1, 0.0, ring[slot_nxt, 0:CONV_W - 1 - CONV_LEFT])
    u = _lru_conv(cur, ulast[...], nxt, cw_ref, cb_ref, tt)
    uconv_ref[...] = u
    _lru_coeffs(u, wrg_ref, ba_ref, bx_ref, lam_ref, a_scr, b_scr, tt,
                interleave=[functools.partial(project_chunk, c) for c in range(max(ahead, 0), nchunks)])
    ulast[...] = cur[tt - CONV_LEFT:tt]

    h_prev = h_scr[...]
    h = h_prev
    for k in range(tt):
        h = a_scr[k] * h + b_scr[k]
        s_ref[k] = h
    h = jnp.where(i >= LRU_SCAN_LAG, h, h_prev)
    h_scr[...] = h
    hfin_ref[...] = h


def _lru_in_fwd(x3, scale, shift, w, h0, conv_w, conv_b, wrg, ba, bx, lam, *, with_gate, tt=32):
    length = x3.shape[1]
    ntiles = length // tt
    blk = _nbytes((tt, BATCH, LRU_WIDTH), F32)
    proj_tile = lambda i: jnp.minimum(i, ntiles - 1)
    scan_tile = lambda i: jnp.maximum(i - LRU_SCAN_LAG, 0)
    tm_spec = lambda tile: pl.BlockSpec((tt, BATCH, LRU_WIDTH), lambda i: (tile(i), 0, 0))
    out_specs = [tm_spec(scan_tile), tm_spec(scan_tile),
                 pl.BlockSpec((BATCH, LRU_WIDTH), lambda i: (0, 0))]
    out_shape = [jax.ShapeDtypeStruct((length, BATCH, LRU_WIDTH), F32)] * 2 + [
        jax.ShapeDtypeStruct((BATCH, LRU_WIDTH), F32)]
    outs = [blk, blk]
    if with_gate:
        out_specs = [tm_spec(proj_tile)] + out_specs
        out_shape = [jax.ShapeDtypeStruct((length, BATCH, LRU_WIDTH), BF16)] + out_shape
        outs.append(blk // 2)
    res = pl.pallas_call(
        functools.partial(_lru_in_fwd_kernel, tt=tt, with_gate=with_gate),
        grid=(ntiles + LRU_SCAN_LAG,),
        in_specs=[pl.BlockSpec((BATCH, tt, D_MODEL), lambda i: (0, proj_tile(i), 0)),
                  _const_spec(scale.shape), _const_spec(shift.shape), _const_spec(w.shape),
                  _const_spec((BATCH, LRU_WIDTH)),
                  _const_spec((CONV_W, LRU_WIDTH)), _const_spec((1, LRU_WIDTH)),
                  _const_spec(wrg.shape),
                  _const_spec((1, LRU_WIDTH)), _const_spec((1, LRU_WIDTH)), _const_spec((1, LRU_WIDTH))],
        out_specs=out_specs,
        out_shape=out_shape,
        scratch_shapes=[
            pltpu.VMEM((LRU_SCAN_LAG + 1, tt, BATCH, LRU_WIDTH), F32),
            pltpu.VMEM((CONV_LEFT, BATCH, LRU_WIDTH), F32),
            pltpu.VMEM((BATCH, LRU_WIDTH), F32),
            pltpu.VMEM((tt, BATCH, LRU_WIDTH), F32), pltpu.VMEM((tt, BATCH, LRU_WIDTH), F32),
        ],
        compiler_params=pltpu.CompilerParams(
            dimension_semantics=("arbitrary",),
            vmem_limit_bytes=_vmem_limit(
                [blk] + outs, [_nbytes(w.shape, BF16), _nbytes(wrg.shape, BF16),
                               (LRU_SCAN_LAG + 3) * blk], 8 * blk)),
        name="lru_in_fwd",
    )(x3, scale, shift, w, h0, conv_w, conv_b, wrg, ba, bx, lam)
    return res if with_gate else (None,) + tuple(res)


def _lru_conv(cur, prev, nxt, cw_ref, cb_ref, tt):
    ext = jnp.concatenate([prev, cur, nxt], axis=0)
    u = cb_ref[...].reshape(1, 1, LRU_WIDTH)
    for tap in range(CONV_W):
        u = u + ext[tap:tap + tt] * cw_ref[tap:tap + 1, :].reshape(1, 1, LRU_WIDTH)
    return u


def _lru_coeffs(u, wrg_ref, ba_ref, bx_ref, lam_ref, a_scr, b_scr, tt, interleave=()):
    u2 = u.reshape(tt * BATCH, LRU_WIDTH)
    ub16 = u2.astype(BF16)
    col = lambda blk: slice(blk * LRU_BLOCK_W, (blk + 1) * LRU_BLOCK_W)
    lam = lam_ref[...]
    half_decay2 = (-0.5 * LRU_C * LOG2E) * (jnp.maximum(-lam, 0.0) + jnp.log1p(jnp.exp(-jnp.abs(lam))))
    half_ba = 0.5 * ba_ref[...]
    half_bx = 0.5 * bx_ref[...]
    rgs = []
    for blk in range(LRU_BLOCKS + 1):
        if blk < LRU_BLOCKS:
            rgs.append(jnp.dot(ub16[:, col(blk)], wrg_ref[blk], preferred_element_type=F32))
            if blk < len(interleave):
                interleave[blk]()
        if blk == 0:
            continue
        blk -= 1
        rg = rgs[blk]
        cols = col(blk)
        tr = jnp.tanh(rg[:, :LRU_BLOCK_W] + half_ba[:, cols])
        tg = jnp.tanh(rg[:, LRU_BLOCK_W:] + half_bx[:, cols])
        hd = half_decay2[:, cols]
        a = jnp.exp2(hd + hd * tr)
        x = 1.0 - a * a
        hu = 0.5 * u2[:, cols]
        b = jnp.where(x > 0.0, x * lax.rsqrt(x), 0.0) * (hu + hu * tg)
        a_scr[:, :, cols] = a.reshape(tt, BATCH, LRU_BLOCK_W)
        b_scr[:, :, cols] = b.reshape(tt, BATCH, LRU_BLOCK_W)


def _lru_bwd_state_kernel(u_ref, wrg_ref, ba_ref, bx_ref, lam_ref, hfin_ref, a_scr, b_scr, *, tt):
    @pl.when(pl.program_id(0) == 0)
    def _():
        hfin_ref[...] = jnp.zeros(hfin_ref.shape, F32)

    _lru_coeffs(u_ref[...], wrg_ref, ba_ref, bx_ref, lam_ref, a_scr, b_scr, tt)
    step = lambda k, h: a_scr[tt - 1 - k] * h + b_scr[tt - 1 - k]
    hfin_ref[...] = lax.fori_loop(0, tt, step, hfin_ref[...], unroll=4)


def _lru_bwd_state(u_conv, wrg, ba, bx, lam, *, tt=32):
    nsteps = u_conv.shape[0] // tt
    blk = _nbytes((tt, BATCH, LRU_WIDTH), F32)
    return pl.pallas_call(
        functools.partial(_lru_bwd_state_kernel, tt=tt),
        grid=(nsteps,),
        in_specs=[
            pl.BlockSpec((tt, BATCH, LRU_WIDTH), lambda i: (nsteps - 1 - i, 0, 0)),
            _const_spec(wrg.shape),
            _const_spec((1, LRU_WIDTH)), _const_spec((1, LRU_WIDTH)), _const_spec((1, LRU_WIDTH)),
        ],
        out_specs=pl.BlockSpec((BATCH, LRU_WIDTH), lambda i: (0, 0)),
        out_shape=jax.ShapeDtypeStruct((BATCH, LRU_WIDTH), F32),
        scratch_shapes=[pltpu.VMEM((tt, BATCH, LRU_WIDTH), F32)] * 2,
        compiler_params=pltpu.CompilerParams(
            dimension_semantics=("arbitrary",),
            vmem_limit_bytes=_vmem_limit([blk], [2 * blk, _nbytes(wrg.shape, BF16)], 6 * blk)),
        name="lru_bwd_state",
    )(u_conv, wrg, ba, bx, lam)


def _lru_bwd_out_kernel(u_ref, h0_ref, wrg_ref, ba_ref, bx_ref, lam_ref, sf_ref, gate_ref, h_ref,
                        g2_ref, w_ref, lng_ref, lnb_ref, o_ref,
                        h_scr, a_scr, b_scr, sb_ring, out_scr, *, tt):
    i = pl.program_id(0)
    ntiles = pl.num_programs(0) - 1

    @pl.when(i == 0)
    def _():
        h_scr[...] = h0_ref[...]
        sb_ring[...] = jnp.zeros(sb_ring.shape, F32)

    y = sb_ring[lax.rem(i + 1, 2)] + sf_ref[...]
    z = (_gelu_tanh(gate_ref[...].astype(F32)) * y).astype(BF16)
    perm = _row_perm(to_time_major=False)
    parts = []
    for tau in range(tt // PERM_STEPS):
        sub = z[tau * PERM_STEPS:(tau + 1) * PERM_STEPS].reshape(PERM_STEPS * BATCH, LRU_WIDTH)
        zb = jnp.dot(perm, sub, preferred_element_type=F32).astype(BF16)
        parts.append(zb.reshape(BATCH, PERM_STEPS, LRU_WIDTH))
    zb = jnp.concatenate(parts, axis=1).reshape(BATCH * tt, LRU_WIDTH)

    def project_chunk(c):
        cols = slice(c * LRU_PROJ_CHUNK, (c + 1) * LRU_PROJ_CHUNK)
        val = jnp.dot(zb, w_ref[:, cols], preferred_element_type=F32)
        out_scr[:, :, cols] = val.reshape(BATCH, tt, LRU_PROJ_CHUNK)

    _lru_coeffs(u_ref[...], wrg_ref, ba_ref, bx_ref, lam_ref, a_scr, b_scr, tt,
                interleave=[functools.partial(project_chunk, c)
                            for c in range(D_MODEL // LRU_PROJ_CHUNK)])
    slot = lax.rem(i, 2)

    h = h_scr[...]
    for idx in reversed(range(tt)):
        h = a_scr[idx] * h + b_scr[idx]
        sb_ring[slot, idx] = h
    h_scr[...] = h
    res = DEEPNORM_ALPHA * h_ref[...] + g2_ref[...] * out_scr[...]
    o_ref[...] = _layer_norm(res, lng_ref[...].reshape(1, 1, D_MODEL), lnb_ref[...].reshape(1, 1, D_MODEL))


def _lru_bwd_out(u_conv, h0, wrg, ba, bx, lam, s_f, gate_t, h3, g2, w_out, ln_g, ln_b, *, tt=32):
    ntiles = SEQ // tt
    scan_tile = lambda i: jnp.maximum(ntiles - 1 - i, 0)
    out_tile = lambda i: jnp.minimum(ntiles - i, ntiles - 1)
    tm_blk = lambda: pl.BlockSpec((tt, BATCH, LRU_WIDTH), lambda i: (out_tile(i), 0, 0))
    bm_blk = lambda: pl.BlockSpec((BATCH, tt, D_MODEL), lambda i: (0, out_tile(i), 0))
    blk = _nbytes((tt, BATCH, LRU_WIDTH), F32)
    return pl.pallas_call(
        functools.partial(_lru_bwd_out_kernel, tt=tt),
        grid=(ntiles + 1,),
        in_specs=[
            pl.BlockSpec((tt, BATCH, LRU_WIDTH), lambda i: (scan_tile(i), 0, 0)),
            _const_spec((BATCH, LRU_WIDTH)), _const_spec(wrg.shape),
            _const_spec((1, LRU_WIDTH)), _const_spec((1, LRU_WIDTH)), _const_spec((1, LRU_WIDTH)),
            tm_blk(), tm_blk(), bm_blk(),
            _const_spec((BATCH, 1, D_MODEL)), _const_spec(w_out.shape),
            _const_spec((1, D_MODEL)), _const_spec((1, D_MODEL))],
        out_specs=bm_blk(),
        out_shape=jax.ShapeDtypeStruct((BATCH, SEQ, D_MODEL), F32),
        scratch_shapes=[
            pltpu.VMEM((BATCH, LRU_WIDTH), F32),
            pltpu.VMEM((tt, BATCH, LRU_WIDTH), F32), pltpu.VMEM((tt, BATCH, LRU_WIDTH), F32),
            pltpu.VMEM((2, tt, BATCH, LRU_WIDTH), F32), pltpu.VMEM((BATCH, tt, D_MODEL), F32),
        ],
        compiler_params=pltpu.CompilerParams(
            dimension_semantics=("arbitrary",),
            vmem_limit_bytes=_vmem_limit(
                [blk] * 5, [5 * blk, _nbytes(w_out.shape, BF16), _nbytes(wrg.shape, BF16)], 8 * blk)),
        name="lru_bwd_out",
    )(u_conv, h0, wrg, ba, bx, lam, s_f, gate_t, h3, g2, w_out, ln_g, ln_b)


def _rope_tables():
    rows = SEQ // GRID_W
    row = jnp.repeat(jnp.arange(rows, dtype=F32), GRID_W)
    col = jnp.tile(jnp.arange(GRID_W, dtype=F32), rows)
    inv = ROPE_THETA ** (-jnp.arange(ROPE_FREQS, dtype=F32) / ROPE_FREQS)
    ang = jnp.concatenate([row[:, None] * inv, col[:, None] * inv], axis=-1)
    cos, sin = jnp.cos(ang), jnp.sin(ang)
    cos_t = jnp.tile(cos, (1, V7X_LANES // (HEAD_DIM // 2)))
    sin_t = jnp.tile(jnp.concatenate([-sin, sin], axis=-1), (1, V7X_LANES // HEAD_DIM))
    return cos_t, sin_t


def kernel(x, c, ctx, c_ctx, w_mod, b_mod, ln_g, ln_b, ffn_w_gate, ffn_w_up, ffn_w_down, mix_ab_w_in, attn_sink, pool_w, pool_scale, mix_ab_w_out, lru_w_in, lru_conv_w, lru_conv_b, lru_wa, lru_ba, lru_wx, lru_bx, lru_lambda, lru_w_out):
    assert x.shape == (BATCH, SEQ, D_MODEL) and ctx.shape == (BATCH, CTX_LEN, D_MODEL)
    c_all = jnp.concatenate(
        [c, c_ctx[None, :], jnp.zeros((MOD_ROWS - BATCH - 1, D_MODEL), F32)], axis=0)
    mod_all = _modulation(c_all, w_mod, b_mod)
    mod_lat = mod_all[:, :BATCH].reshape(DEPTH, BATCH, N_MOD, D_MODEL)
    mod_ctx = mod_all[:, BATCH:BATCH + 1].reshape(DEPTH, 1, N_MOD, D_MODEL)

    wg = ffn_w_gate.astype(BF16)
    wu = ffn_w_up.astype(BF16)
    wd = ffn_w_down.astype(BF16)
    row = lambda v: v.reshape(1, -1)
    n_ctx = BATCH * CTX_LEN

    h = x.reshape(BATCH * SEQ, D_MODEL)
    hc = ctx.reshape(n_ctx, D_MODEL)

    l = 0
    ml, mc = mod_lat[l], mod_ctx[l]
    ffn1 = ((l, 0), wg, wu, wd, row(ln_g[l, 0]), row(ln_b[l, 0]))
    ffn2 = ((l, 1), wg, wu, wd, row(ln_g[l, 2]), row(ln_b[l, 2]))
    h = _ffn(h, ml, 0, *ffn1, nb=BATCH, length=SEQ)
    hc = _ffn(hc, mc, 0, *ffn1, nb=1, length=n_ctx)
    w_in = mix_ab_w_in[0].astype(BF16)
    cos_t, sin_t = _rope_tables()
    q, k2, v2, u = _attn_inproj(h, ml, w_in, cos_t, sin_t, nb=BATCH, length=SEQ)
    q_c, k2_c, v2_c, u_c = _attn_inproj(hc, mc, w_in, None, None, nb=1, length=n_ctx)
    mix_args = (pool_w[0].astype(BF16), row(pool_scale[0]), mix_ab_w_out[0].astype(BF16),
                row(ln_g[l, 1]), row(ln_b[l, 1]))
    h = _attn_lat(attn_sink[0], q, k2, v2, k2_c, v2_c, u, h, ml, *mix_args)
    hc = _attn_ctx(attn_sink[0], q_c, k2_c, v2_c, u_c, hc, mc, *mix_args)
    h = _ffn(h, ml, 6, *ffn2, nb=BATCH, length=SEQ)
    hc = _ffn(hc, mc, 6, *ffn2, nb=1, length=n_ctx)

    l = 1
    ml, mc = mod_lat[l], mod_ctx[l]
    ffn1 = ((l, 0), wg, wu, wd, row(ln_g[l, 0]), row(ln_b[l, 0]))
    ffn2 = ((l, 1), wg, wu, wd, row(ln_g[l, 2]), row(ln_b[l, 2]))
    h = _ffn(h, ml, 0, *ffn1, nb=BATCH, length=SEQ)
    hc = _ffn(hc, mc, 0, *ffn1, nb=1, length=n_ctx)
    w_in = lru_w_in[0].astype(BF16)
    h3 = h.reshape(BATCH, SEQ, D_MODEL)
    wrg = (0.5 * jnp.concatenate([lru_wa[0], lru_wx[0]], axis=-1)).astype(BF16)
    conv_w, conv_b = lru_conv_w[0], row(lru_conv_b[0])
    zeros = jnp.zeros((BATCH, LRU_WIDTH), F32)
    dir_args = [(wrg[d], row(lru_ba[0, d]), row(lru_bx[0, d]), row(lru_lambda[0, d]))
                for d in range(2)]
    _, uc_t, _, hf_ctx = _lru_in_fwd(hc.reshape(BATCH, CTX_LEN, D_MODEL), mc[:, 4:5, :], mc[:, 3:4, :],
                                     w_in[:, LRU_WIDTH:], zeros, conv_w, conv_b, *dir_args[0],
                                     with_gate=False)
    gate_t, u_t, s_f, _ = _lru_in_fwd(h3, ml[:, 4:5, :], ml[:, 3:4, :], w_in, hf_ctx, conv_w, conv_b,
                                      *dir_args[0], with_gate=True)
    hb_ctx = _lru_bwd_state(uc_t, *dir_args[1])
    h3 = _lru_bwd_out(u_t, hb_ctx, *dir_args[1], s_f, gate_t, h3, ml[:, 5:6, :],
                      lru_w_out[0].astype(BF16), row(ln_g[l, 1]), row(ln_b[l, 1]))
    h = _ffn(h3.reshape(BATCH * SEQ, D_MODEL), ml, 6, *ffn2, nb=BATCH, length=SEQ)
    return h.reshape(BATCH, SEQ, D_MODEL)
```

```python
import functools

import jax
import jax.numpy as jnp
from jax import lax
from jax.experimental import pallas as pl
from jax.experimental.pallas import tpu as pltpu

D_MODEL = 1024
BATCH = 16
SEQ = 2048
DEPTH = 2
GRID_W = 64
CTX_LEN = 256
HEAD_DIM = 64
ATT_HEADS = 8
ATT_KV_HEADS = 2
ATT_GROUPS = ATT_HEADS // ATT_KV_HEADS
ATT_WIDTH = ATT_HEADS * HEAD_DIM
KV_WIDTH = ATT_KV_HEADS * HEAD_DIM
WINDOW = 128
BLOCK = 128
ATT_SCALE = HEAD_DIM ** -0.5
LOG2E = 1.4426950408889634
ROPE_THETA = 10000.0
ROPE_FREQS = HEAD_DIM // 4
POOL_WINDOWS = (2, 4, 8, 16)
POOL_WIDTH = D_MODEL // 2
POOL_GROUP_W = POOL_WIDTH // len(POOL_WINDOWS)
MIX_AB_IN = ATT_WIDTH + 2 * KV_WIDTH + POOL_WIDTH
LRU_WIDTH = D_MODEL
LRU_BLOCKS = 8
LRU_BLOCK_W = LRU_WIDTH // LRU_BLOCKS
LRU_C = 8.0
CONV_W = 4
CONV_LEFT = (CONV_W - 1) // 2
D_FF = 2816
N_MOD = 9
LN_EPS = 1e-5
NEG_INF = -1e30
DEEPNORM_ALPHA = (2 * DEPTH) ** 0.25

V7X_LANES = 128
V7X_SUBLANES = 8
V7X_VMEM_BYTES = 64 * 1024 * 1024
V7X_VMEM_USABLE_BYTES = 60000 * 1024

F32 = jnp.float32
BF16 = jnp.bfloat16

POOL_HALO = V7X_SUBLANES
PERM_STEPS = 16
KEY_SLAB_W = ATT_KV_HEADS * V7X_LANES
VAL_SLAB_W = 2 * ATT_KV_HEADS * V7X_LANES
LRU_PROJ_CHUNK = 256
LRU_SCAN_LAG = 2
SCORE_LOOKAHEAD = 2
EPILOGUE_BLOCKS = 2
MOD_ROWS = 24


def _nbytes(shape, dtype):
    n = 1
    for s in shape:
        n *= s
    return n * jnp.dtype(dtype).itemsize


def _vmem_limit(pipelined, resident, temporaries):
    est = 2 * sum(pipelined) + sum(resident) + temporaries
    return int(min(V7X_VMEM_USABLE_BYTES, max(est * 5 // 4, 16 * 1024 * 1024)))


def _const_spec(shape):
    nd = len(shape)
    return pl.BlockSpec(shape, lambda *_: (0,) * nd, pipeline_mode=pl.Buffered(1))


def _tok_spec(tm, width, nt):
    return pl.BlockSpec((tm, width), lambda b, t: (b * nt + t, 0))


def _mod_spec():
    return pl.BlockSpec((1, N_MOD, D_MODEL), lambda b, t: (b, 0, 0))


def _layer_norm(z, g, b):
    mu = jnp.mean(z, axis=-1, keepdims=True)
    zc = z - mu
    var = jnp.mean(zc * zc, axis=-1, keepdims=True)
    return zc * lax.rsqrt(var + LN_EPS) * g + b


def _gelu_tanh(x):
    c = 0.7978845608028654
    half = 0.5 * x
    return half + half * jnp.tanh(x * (c + (c * 0.044715) * (x * x)))


def _mod_kernel(c_ref, w_ref, b_ref, o_ref):
    c = c_ref[...]
    a_hi, a_lo = _split_bf16(c * jax.nn.sigmoid(c))
    w_hi, w_lo = _split_bf16(w_ref[0])
    acc = jnp.dot(a_hi, w_lo, preferred_element_type=F32) + jnp.dot(a_lo, w_hi, preferred_element_type=F32)
    o_ref[0] = acc + jnp.dot(a_hi, w_hi, preferred_element_type=F32) + b_ref[0]


def _modulation(c_all, w_mod, b_mod):
    tn = 2304
    n_out = N_MOD * D_MODEL
    blocks = [_nbytes((1, D_MODEL, tn), F32), _nbytes((1, MOD_ROWS, tn), F32)]
    return pl.pallas_call(
        _mod_kernel,
        grid=(DEPTH, n_out // tn),
        in_specs=[
            pl.BlockSpec((MOD_ROWS, D_MODEL), lambda l, j: (0, 0)),
            pl.BlockSpec((1, D_MODEL, tn), lambda l, j: (l, 0, j)),
            pl.BlockSpec((1, 1, tn), lambda l, j: (l, 0, j)),
        ],
        out_specs=pl.BlockSpec((1, MOD_ROWS, tn), lambda l, j: (l, 0, j)),
        out_shape=jax.ShapeDtypeStruct((DEPTH, MOD_ROWS, n_out), F32),
        compiler_params=pltpu.CompilerParams(
            dimension_semantics=("arbitrary", "arbitrary"),
            vmem_limit_bytes=_vmem_limit(blocks, [], 4 * blocks[0])),
        name="modulation",
    )(c_all, w_mod, b_mod.reshape(DEPTH, 1, n_out))


def _ffn_kernel(x_ref, mod_ref, wg_ref, wu_ref, wd_ref, lng_ref, lnb_ref, o_ref, *, j0, sub):
    shift = mod_ref[0, j0:j0 + 1, :]
    scale = mod_ref[0, j0 + 1:j0 + 2, :]
    gate = mod_ref[0, j0 + 2:j0 + 3, :]
    sizes = [sub] * (x_ref.shape[0] // sub)
    starts = [s * sub for s in range(len(sizes))]
    nsub = len(sizes)

    def gate_up(s):
        x = x_ref[starts[s]:starts[s] + sizes[s], :]
        xin = (x * (1.0 + scale) + shift).astype(BF16)
        return (jnp.dot(xin, wg_ref[...], preferred_element_type=F32),
                jnp.dot(xin, wu_ref[...], preferred_element_type=F32))

    pending = gate_up(0)
    for s in range(nsub):
        g, u = pending
        if s + 1 < nsub:
            pending = gate_up(s + 1)
        rows = slice(starts[s], starts[s] + sizes[s])
        a = (g * jax.nn.sigmoid(g) * u).astype(BF16)
        y = jnp.dot(a, wd_ref[...], preferred_element_type=F32)
        z = DEEPNORM_ALPHA * x_ref[rows, :] + (0.5 * gate) * y
        o_ref[rows, :] = _layer_norm(z, lng_ref[...], lnb_ref[...])


def _ffn(x, mod, j0, which, wg, wu, wd, ln_g, ln_b, *, nb, length, tm=1024, sub=256):
    nt = length // tm
    tok = _nbytes((tm, D_MODEL), F32)
    weights = [_nbytes(w.shape[2:], BF16) for w in (wg, wu, wd)]
    temps = 3 * _nbytes((tm, D_FF), F32) + 4 * tok
    wspec = lambda w: pl.BlockSpec((None, None) + w.shape[2:], lambda b, t: which + (0, 0),
                                   pipeline_mode=pl.Buffered(1))
    return pl.pallas_call(
        functools.partial(_ffn_kernel, j0=j0, sub=sub),
        grid=(nb, nt),
        in_specs=[
            _tok_spec(tm, D_MODEL, nt),
            _mod_spec(),
            wspec(wg), wspec(wu), wspec(wd),
            _const_spec((1, D_MODEL)), _const_spec((1, D_MODEL)),
        ],
        out_specs=_tok_spec(tm, D_MODEL, nt),
        out_shape=jax.ShapeDtypeStruct((nb * length, D_MODEL), F32),
        compiler_params=pltpu.CompilerParams(
            dimension_semantics=("arbitrary", "arbitrary"),
            vmem_limit_bytes=_vmem_limit([tok, tok], weights, temps)),
        name="ffn",
    )(x, mod, wg, wu, wd, ln_g, ln_b)


def _dup_halves(z, lane):
    zr = pltpu.roll(z, HEAD_DIM, 1)
    lo = lane < HEAD_DIM
    return jnp.where(lo, z, zr), jnp.where(lo, zr, z)


def _attn_inproj_kernel(*refs, rope):
    if rope:
        x_ref, mod_ref, w_ref, cos_ref, sin_ref, q_ref, k_ref, v_ref, u_ref = refs
    else:
        x_ref, mod_ref, w_ref, q_ref, k_ref, v_ref, u_ref = refs
    x = x_ref[...]
    xin = (x * (1.0 + mod_ref[0, 4:5, :]) + mod_ref[0, 3:4, :]).astype(BF16)
    p = jnp.dot(xin, w_ref[...], preferred_element_type=F32)
    lane = lax.broadcasted_iota(jnp.int32, (x.shape[0], V7X_LANES), 1)
    first_half = (lane & (HEAD_DIM - 1)) < HEAD_DIM // 2

    def rot(z):
        if not rope:
            return z
        zr = jnp.where(first_half, pltpu.roll(z, V7X_LANES - HEAD_DIM // 2, 1),
                       pltpu.roll(z, HEAD_DIM // 2, 1))
        return z * cos_ref[...] + zr * sin_ref[...]

    for c in range(ATT_WIDTH // V7X_LANES):
        sl = slice(c * V7X_LANES, (c + 1) * V7X_LANES)
        q_ref[:, sl] = (rot(p[:, sl]) * (ATT_SCALE * LOG2E)).astype(BF16)
    k0, k1 = _dup_halves(rot(p[:, ATT_WIDTH:ATT_WIDTH + KV_WIDTH]), lane)
    k_ref[...] = jnp.concatenate([k0, k1], axis=-1).T.astype(BF16)
    v = p[:, ATT_WIDTH + KV_WIDTH:ATT_WIDTH + 2 * KV_WIDTH]
    vr = pltpu.roll(v, HEAD_DIM, 1)
    lo = lane < HEAD_DIM
    slabs = (jnp.where(lo, v, 1.0), jnp.where(lo, 1.0, vr), jnp.where(lo, vr, 1.0), jnp.where(lo, 1.0, v))
    for i, slab in enumerate(slabs):
        v_ref[:, i * V7X_LANES:(i + 1) * V7X_LANES] = slab.astype(BF16)
    u_ref[...] = p[:, ATT_WIDTH + 2 * KV_WIDTH:]


def _attn_inproj(x, mod, w_in, cos, sin, *, nb, length, tm=1024):
    nt = length // tm
    rope = cos is not None
    tok = _nbytes((tm, D_MODEL), F32)
    outs = [_nbytes((tm, ATT_WIDTH), BF16), _nbytes((tm, KEY_SLAB_W + VAL_SLAB_W), BF16),
            _nbytes((tm, POOL_WIDTH), F32)]
    in_specs = [_tok_spec(tm, D_MODEL, nt), _mod_spec(), _const_spec(w_in.shape)]
    args = [x, mod, w_in]
    if rope:
        in_specs += [pl.BlockSpec((tm, V7X_LANES), lambda b, t: (t, 0))] * 2
        args += [cos, sin]
    rows = nb * length
    return pl.pallas_call(
        functools.partial(_attn_inproj_kernel, rope=rope),
        grid=(nb, nt),
        in_specs=in_specs,
        out_specs=[
            _tok_spec(tm, ATT_WIDTH, nt),
            pl.BlockSpec((KEY_SLAB_W, tm), lambda b, t: (0, b * nt + t)),
            _tok_spec(tm, VAL_SLAB_W, nt),
            _tok_spec(tm, POOL_WIDTH, nt),
        ],
        out_shape=[
            jax.ShapeDtypeStruct((rows, ATT_WIDTH), BF16),
            jax.ShapeDtypeStruct((KEY_SLAB_W, rows), BF16),
            jax.ShapeDtypeStruct((rows, VAL_SLAB_W), BF16),
            jax.ShapeDtypeStruct((rows, POOL_WIDTH), F32),
        ],
        compiler_params=pltpu.CompilerParams(
            dimension_semantics=("arbitrary", "arbitrary"),
            vmem_limit_bytes=_vmem_limit([tok] + outs, [_nbytes(w_in.shape, BF16)],
                                         3 * _nbytes((tm, MIX_AB_IN), F32))),
        name="attn_inproj",
    )(*args)


def _attend_scores(q_blk, keys, kh):
    lane = lax.broadcasted_iota(jnp.int32, (BLOCK, V7X_LANES), 1)
    lo = lane < HEAD_DIM
    zero = jnp.zeros((BLOCK, V7X_LANES), BF16)
    parts = []
    for g in range(ATT_GROUPS):
        c = (kh * ATT_GROUPS + g) // 2
        qc = q_blk[:, c * V7X_LANES:(c + 1) * V7X_LANES]
        parts.append(jnp.where(lo if g % 2 == 0 else jnp.logical_not(lo), qc, zero))
    qs = jnp.concatenate(parts, axis=0)
    return [jnp.dot(qs, k, preferred_element_type=F32) for k in keys]


def _attend_finish(scores, values_even, values_odd, segments, sink_ref, kh):
    lane = lax.broadcasted_iota(jnp.int32, (BLOCK, V7X_LANES), 1)
    lo = lane < HEAD_DIM
    probs = ([[] for _ in scores], [[] for _ in scores])
    sink_terms = []
    for g in range(ATT_GROUPS):
        rows = slice(g * BLOCK, (g + 1) * BLOCK)
        segs = [[s[rows, a:b] if bias is None else s[rows, a:b] + bias for a, b, bias in seg]
                for s, seg in zip(scores, segments)]
        sink = sink_ref[kh * ATT_GROUPS + g] * LOG2E
        chunks = [x[:, c:c + V7X_LANES] for sl in segs for x in sl
                  for c in range(0, x.shape[1], V7X_LANES)]
        m = functools.reduce(jnp.maximum, chunks).max(axis=-1, keepdims=True)
        m = jnp.maximum(m, sink)
        sink_terms.append(jnp.exp2(sink - m))
        for i, sl in enumerate(segs):
            e = [jnp.exp2(x - m).astype(BF16) for x in sl]
            probs[g % 2][i].append(e[0] if len(e) == 1 else jnp.concatenate(e, axis=-1))

    def weighted_values(ps, vals):
        o = None
        for p, v in zip(ps, vals):
            t = jnp.dot(jnp.concatenate(p, axis=0), v, preferred_element_type=F32)
            o = t if o is None else o + t
        return o

    o_par = (weighted_values(probs[0], values_even), weighted_values(probs[1], values_odd))
    outs = []
    for g in range(ATT_GROUPS):
        o = o_par[g % 2][(g // 2) * BLOCK:(g // 2 + 1) * BLOCK]
        rinv = 1.0 / (o + sink_terms[g])
        outs.append(o * pltpu.roll(rinv, HEAD_DIM, 1))
    return jnp.where(lo, outs[0], outs[1]), jnp.where(lo, outs[2], outs[3])


def _pool_sums(ubuf_ref, nblk, band_ref):
    u_hi, u_lo = _split_bf16(ubuf_ref[...])
    sums = {}
    for j in range(nblk):
        win = slice(j * BLOCK, (j + 2) * BLOCK)
        for gi in range(len(POOL_WINDOWS)):
            cols = slice(gi * POOL_GROUP_W, (gi + 1) * POOL_GROUP_W)
            band = band_ref[gi]
            sums[j, gi] = (jnp.dot(band, u_hi[win, cols], preferred_element_type=F32)
                           + jnp.dot(band, u_lo[win, cols], preferred_element_type=F32))
    return sums


def _pool_finish(sums, ubuf_ref, blocks, pos0, length, wpool_ref, pscale_ref, mix_scr):
    for j in blocks:
        pos = pos0 + j * BLOCK + lax.broadcasted_iota(jnp.int32, (BLOCK, 1), 0)
        rows = slice(POOL_HALO + j * BLOCK, POOL_HALO + (j + 1) * BLOCK)
        for gi, w in enumerate(POOL_WINDOWS):
            r = w // 2
            cols = slice(gi * POOL_GROUP_W, (gi + 1) * POOL_GROUP_W)
            cnt = (jnp.minimum(pos + r, length - 1) - jnp.maximum(pos - r, 0) + 1).astype(F32)
            d = sums[j, gi] / cnt - ubuf_ref[rows, cols]
            y = jnp.dot(d.astype(BF16), wpool_ref[gi], preferred_element_type=F32)
            mix_scr[j * BLOCK:(j + 1) * BLOCK, ATT_WIDTH + gi * POOL_GROUP_W:
                    ATT_WIDTH + (gi + 1) * POOL_GROUP_W] = (y * pscale_ref[:, cols]).astype(BF16)


def _split_bf16(x):
    hi = x.astype(BF16)
    return hi, (x - hi.astype(F32)).astype(BF16)


def _pool_bands():
    i = jnp.arange(BLOCK)[:, None]
    c = jnp.arange(2 * BLOCK)[None, :]
    return jnp.stack([(jnp.abs(c - POOL_HALO - i) <= w // 2) for w in POOL_WINDOWS]).astype(BF16)


def _attn_out(mix_scr, rows, wout_ref, h_ref, gate, lng_ref, lnb_ref, o_ref):
    y = jnp.dot(mix_scr[rows, :], wout_ref[...], preferred_element_type=F32)
    z = DEEPNORM_ALPHA * h_ref[rows, :] + gate * y
    o_ref[rows, :] = _layer_norm(z, lng_ref[...], lnb_ref[...])


def _attn_lat_kernel(sink_ref, q_ref, kc_ref, kp_ref, kn_ref, vc_ref, vp_ref, vn_ref,
                     kctx_ref, vctx_ref, uc_ref, up_ref, un_ref, h_ref, mod_ref, bias_ref, band_ref,
                     wpool_ref, pscale_ref, wout_ref, lng_ref, lnb_ref, o_ref,
                     kbuf, vbuf, ubuf, mix_scr, *, tq, length):
    t = pl.program_id(1)
    last = pl.num_programs(1) - 1
    nblk = tq // BLOCK
    kbuf[:, 0:BLOCK] = kp_ref[...]
    kbuf[:, BLOCK:BLOCK + tq] = kc_ref[...]
    kbuf[:, BLOCK + tq:] = kn_ref[...]
    vbuf[0:BLOCK] = vp_ref[...]
    vbuf[BLOCK:BLOCK + tq] = vc_ref[...]
    vbuf[BLOCK + tq:] = vn_ref[...]
    ubuf[0:POOL_HALO] = jnp.where(t == 0, 0.0, up_ref[...])
    ubuf[POOL_HALO:POOL_HALO + tq] = uc_ref[...]
    ubuf[POOL_HALO + tq:2 * POOL_HALO + tq] = jnp.where(t == last, 0.0, un_ref[...])
    ubuf[2 * POOL_HALO + tq:] = jnp.zeros((2 * BLOCK - 2 * POOL_HALO, POOL_WIDTH), F32)
    lanes = lambda i: slice(i * V7X_LANES, (i + 1) * V7X_LANES)
    gate = mod_ref[0, 5:6, :]
    units = [(j, kh) for j in range(nblk) for kh in range(ATT_KV_HEADS)]

    def scores_of(j, kh):
        win = slice(j * BLOCK, j * BLOCK + 3 * BLOCK)
        return _attend_scores(q_ref[j * BLOCK:(j + 1) * BLOCK, :],
                              [kbuf[lanes(kh), win], kctx_ref[lanes(kh), :]], kh)

    sums = _pool_sums(ubuf, nblk, band_ref)
    pending = [scores_of(*u) for u in units[:SCORE_LOOKAHEAD]]
    _pool_finish(sums, ubuf, range(nblk), t * tq, length, wpool_ref, pscale_ref, mix_scr)
    for n, (j, kh) in enumerate(units):
        scores = pending.pop(0)
        if n + SCORE_LOOKAHEAD < len(units):
            pending.append(scores_of(*units[n + SCORE_LOOKAHEAD]))
        blk = t * nblk + j
        bidx = jnp.where(blk == 0, 0, jnp.where(blk == length // BLOCK - 1, 2, 1))
        bias = bias_ref[bidx]
        segments = [[(0, BLOCK, bias[:, :BLOCK]), (BLOCK, 2 * BLOCK, None),
                     (2 * BLOCK, 3 * BLOCK, bias[:, BLOCK:])], [(0, CTX_LEN, None)]]
        win = slice(j * BLOCK, j * BLOCK + 3 * BLOCK)
        rows = slice(j * BLOCK, (j + 1) * BLOCK)
        c0, c1 = _attend_finish(
            scores, [vbuf[win, lanes(2 * kh)], vctx_ref[:, lanes(2 * kh)]],
            [vbuf[win, lanes(2 * kh + 1)], vctx_ref[:, lanes(2 * kh + 1)]],
            segments, sink_ref, kh)
        mix_scr[rows, lanes(2 * kh)] = c0.astype(BF16)
        mix_scr[rows, lanes(2 * kh + 1)] = c1.astype(BF16)
        per_epilogue = EPILOGUE_BLOCKS * ATT_KV_HEADS
        if n > 0 and n % per_epilogue == 0:
            e = n // per_epilogue - 1
            done = slice(e * EPILOGUE_BLOCKS * BLOCK, (e + 1) * EPILOGUE_BLOCKS * BLOCK)
            _attn_out(mix_scr, done, wout_ref, h_ref, gate, lng_ref, lnb_ref, o_ref)
    _attn_out(mix_scr, slice(tq - EPILOGUE_BLOCKS * BLOCK, tq), wout_ref, h_ref, gate, lng_ref,
              lnb_ref, o_ref)


def _attn_ctx_kernel(sink_ref, q_ref, kctx_ref, vctx_ref, uc_ref, h_ref, mod_ref, band_ref,
                     wpool_ref, pscale_ref, wout_ref, lng_ref, lnb_ref, o_ref,
                     ubuf, mix_scr, *, tq):
    ubuf[0:POOL_HALO] = jnp.zeros((POOL_HALO, POOL_WIDTH), F32)
    ubuf[POOL_HALO:POOL_HALO + tq] = uc_ref[...]
    ubuf[POOL_HALO + tq:] = jnp.zeros((2 * BLOCK - POOL_HALO, POOL_WIDTH), F32)
    lanes = lambda i: slice(i * V7X_LANES, (i + 1) * V7X_LANES)
    nblk = tq // BLOCK
    units = [(j, kh) for j in range(nblk) for kh in range(ATT_KV_HEADS)]
    scores_of = lambda j, kh: _attend_scores(q_ref[j * BLOCK:(j + 1) * BLOCK, :],
                                             [kctx_ref[lanes(kh), :]], kh)
    sums = _pool_sums(ubuf, nblk, band_ref)
    pending = [scores_of(*u) for u in units[:SCORE_LOOKAHEAD]]
    _pool_finish(sums, ubuf, range(nblk), 0, tq, wpool_ref, pscale_ref, mix_scr)
    for n, (j, kh) in enumerate(units):
        scores = pending.pop(0)
        if n + SCORE_LOOKAHEAD < len(units):
            pending.append(scores_of(*units[n + SCORE_LOOKAHEAD]))
        rows = slice(j * BLOCK, (j + 1) * BLOCK)
        c0, c1 = _attend_finish(scores, [vctx_ref[:, lanes(2 * kh)]],
                                [vctx_ref[:, lanes(2 * kh + 1)]], [[(0, CTX_LEN, None)]],
                                sink_ref, kh)
        mix_scr[rows, lanes(2 * kh)] = c0.astype(BF16)
        mix_scr[rows, lanes(2 * kh + 1)] = c1.astype(BF16)
    _attn_out(mix_scr, slice(0, tq), wout_ref, h_ref, mod_ref[0, 5:6, :], lng_ref, lnb_ref, o_ref)


def _window_bias():
    qi = jnp.arange(BLOCK)[:, None]
    kc = jnp.arange(3 * BLOCK)[None, :]
    band = jnp.abs(qi + BLOCK - kc) <= WINDOW
    first = band & (kc >= BLOCK)
    final = band & (kc < 2 * BLOCK)
    full = jnp.where(jnp.stack([first, band, final]), 0.0, NEG_INF).astype(F32)
    return jnp.concatenate([full[:, :, :BLOCK], full[:, :, 2 * BLOCK:]], axis=-1)


def _smem_spec():
    return pl.BlockSpec(memory_space=pltpu.SMEM)


def _attn_lat(sink, q, k2, v2, kctx, vctx, u, h, mod, wpool, pscale, wout, ln_g, ln_b, *, tq=512):
    nt = SEQ // tq
    bpt = tq // BLOCK
    bps = SEQ // BLOCK
    hpt = tq // POOL_HALO
    hps = SEQ // POOL_HALO
    cur = lambda w: pl.BlockSpec((tq, w), lambda b, t: (b * nt + t, 0))
    prev_blk = lambda rows, w, per_tile, per_samp: pl.BlockSpec(
        (rows, w), lambda b, t: (b * per_samp + jnp.maximum(t * per_tile - 1, 0), 0))
    next_blk = lambda rows, w, per_tile, per_samp: pl.BlockSpec(
        (rows, w), lambda b, t: (b * per_samp + jnp.minimum((t + 1) * per_tile, per_samp - 1), 0))
    ctx_blk = lambda w: pl.BlockSpec((CTX_LEN, w), lambda b, t: (b, 0))
    kv_specs = lambda w: [cur(w), prev_blk(BLOCK, w, bpt, bps), next_blk(BLOCK, w, bpt, bps)]
    key_specs = [
        pl.BlockSpec((KEY_SLAB_W, tq), lambda b, t: (0, b * nt + t)),
        pl.BlockSpec((KEY_SLAB_W, BLOCK), lambda b, t: (0, b * bps + jnp.maximum(t * bpt - 1, 0))),
        pl.BlockSpec((KEY_SLAB_W, BLOCK),
                     lambda b, t: (0, b * bps + jnp.minimum((t + 1) * bpt, bps - 1))),
    ]
    bias = _window_bias()
    bands = _pool_bands()
    tok = _nbytes((tq, D_MODEL), F32)
    kv_rows = tq + 2 * BLOCK
    u_rows = tq + 2 * BLOCK
    pipelined = [_nbytes((tq, ATT_WIDTH), BF16), 2 * _nbytes((kv_rows, KEY_SLAB_W + VAL_SLAB_W), BF16),
                 _nbytes((tq + 2 * POOL_HALO, POOL_WIDTH), F32), 2 * tok]
    resident = [_nbytes(bias.shape, F32), _nbytes(bands.shape, BF16), _nbytes(wpool.shape, BF16),
                _nbytes(wout.shape, BF16)]
    scratch = [_nbytes((kv_rows, KEY_SLAB_W + VAL_SLAB_W), BF16),
               _nbytes((u_rows, POOL_WIDTH), F32), _nbytes((tq, D_MODEL), BF16)]
    return pl.pallas_call(
        functools.partial(_attn_lat_kernel, tq=tq, length=SEQ),
        grid=(BATCH, nt),
        in_specs=[_smem_spec(), cur(ATT_WIDTH)] + key_specs + kv_specs(VAL_SLAB_W) + [
            pl.BlockSpec((KEY_SLAB_W, CTX_LEN), lambda b, t: (0, b)), ctx_blk(VAL_SLAB_W),
            cur(POOL_WIDTH), prev_blk(POOL_HALO, POOL_WIDTH, hpt, hps),
            next_blk(POOL_HALO, POOL_WIDTH, hpt, hps),
            cur(D_MODEL), _mod_spec(),
            _const_spec(bias.shape), _const_spec(bands.shape),
            _const_spec(wpool.shape), _const_spec((1, POOL_WIDTH)),
            _const_spec(wout.shape), _const_spec((1, D_MODEL)), _const_spec((1, D_MODEL)),
        ],
        out_specs=cur(D_MODEL),
        out_shape=jax.ShapeDtypeStruct((BATCH * SEQ, D_MODEL), F32),
        scratch_shapes=[
            pltpu.VMEM((KEY_SLAB_W, kv_rows), BF16), pltpu.VMEM((kv_rows, VAL_SLAB_W), BF16),
            pltpu.VMEM((u_rows, POOL_WIDTH), F32), pltpu.VMEM((tq, D_MODEL), BF16),
        ],
        compiler_params=pltpu.CompilerParams(
            dimension_semantics=("arbitrary", "arbitrary"),
            vmem_limit_bytes=_vmem_limit(pipelined, resident + scratch, 8 * tok)),
        name="attn_latent",
    )(sink, q, k2, k2, k2, v2, v2, v2, kctx, vctx, u, u, u, h, mod, bias, bands, wpool, pscale, wout,
      ln_g, ln_b)


def _attn_ctx(sink, q, kctx, vctx, u, h, mod, wpool, pscale, wout, ln_g, ln_b):
    tq = CTX_LEN
    blk = lambda w: pl.BlockSpec((tq, w), lambda b, t: (b, 0))
    bands = _pool_bands()
    tok = _nbytes((tq, D_MODEL), F32)
    u_rows = tq + 2 * BLOCK
    pipelined = [_nbytes((tq, ATT_WIDTH), BF16), _nbytes((tq, KEY_SLAB_W + VAL_SLAB_W), BF16),
                 _nbytes((tq, POOL_WIDTH), F32), 2 * tok]
    resident = [_nbytes(bands.shape, BF16), _nbytes(wpool.shape, BF16), _nbytes(wout.shape, BF16)]
    scratch = [_nbytes((u_rows, POOL_WIDTH), F32), _nbytes((tq, D_MODEL), BF16)]
    return pl.pallas_call(
        functools.partial(_attn_ctx_kernel, tq=tq),
        grid=(BATCH, 1),
        in_specs=[
            _smem_spec(),
            blk(ATT_WIDTH), pl.BlockSpec((KEY_SLAB_W, tq), lambda b, t: (0, b)),
            blk(VAL_SLAB_W), blk(POOL_WIDTH), blk(D_MODEL),
            pl.BlockSpec((1, N_MOD, D_MODEL), lambda b, t: (0, 0, 0)),
            _const_spec(bands.shape), _const_spec(wpool.shape), _const_spec((1, POOL_WIDTH)),
            _const_spec(wout.shape), _const_spec((1, D_MODEL)), _const_spec((1, D_MODEL)),
        ],
        out_specs=blk(D_MODEL),
        out_shape=jax.ShapeDtypeStruct((BATCH * CTX_LEN, D_MODEL), F32),
        scratch_shapes=[
            pltpu.VMEM((u_rows, POOL_WIDTH), F32), pltpu.VMEM((tq, D_MODEL), BF16),
        ],
        compiler_params=pltpu.CompilerParams(
            dimension_semantics=("arbitrary", "arbitrary"),
            vmem_limit_bytes=_vmem_limit(pipelined, resident + scratch, 8 * tok)),
        name="attn_context",
    )(sink, q, kctx, vctx, u, h, mod, bands, wpool, pscale, wout, ln_g, ln_b)


def _row_perm(to_time_major):
    n = BATCH * PERM_STEPS
    r = lax.broadcasted_iota(jnp.int32, (n, n), 0)
    c = lax.broadcasted_iota(jnp.int32, (n, n), 1)
    if to_time_major:
        hit = ((r // BATCH) == (c % PERM_STEPS)) & ((r % BATCH) == (c // PERM_STEPS))
    else:
        hit = ((r // PERM_STEPS) == (c % BATCH)) & ((r % PERM_STEPS) == (c // BATCH))
    return jnp.where(hit, 1.0, 0.0).astype(BF16)


def _lru_time_major_input(x_ref, scale_ref, shift_ref, tt):
    xin = (x_ref[...] * (1.0 + scale_ref[...]) + shift_ref[...]).astype(BF16)
    perm = _row_perm(to_time_major=True)
    parts = []
    for tau in range(tt // PERM_STEPS):
        sub = xin[:, tau * PERM_STEPS:(tau + 1) * PERM_STEPS, :].reshape(BATCH * PERM_STEPS, D_MODEL)
        parts.append(jnp.dot(perm, sub, preferred_element_type=F32).astype(BF16))
    return jnp.concatenate(parts, axis=0)


def _lru_in_fwd_kernel(*refs, tt, with_gate):
    (x_ref, scale_ref, shift_ref, w_ref, h0_ref, cw_ref, cb_ref, wrg_ref, ba_ref, bx_ref,
     lam_ref) = refs[:11]
    outs = refs[11:]
    if with_gate:
        gate_ref, uconv_ref, s_ref, hfin_ref, ring, ulast, h_scr, a_scr, b_scr = outs
    else:
        uconv_ref, s_ref, hfin_ref, ring, ulast, h_scr, a_scr, b_scr = outs
    i = pl.program_id(0)
    ntiles = pl.num_programs(0) - LRU_SCAN_LAG

    @pl.when(i == 0)
    def _():
        ring[...] = jnp.zeros(ring.shape, F32)
        ulast[...] = jnp.zeros(ulast.shape, F32)
        h_scr[...] = h0_ref[...]

    xt = _lru_time_major_input(x_ref, scale_ref, shift_ref, tt)
    slot_new = lax.rem(i, LRU_SCAN_LAG + 1)
    slot_cur = lax.rem(i + 1, LRU_SCAN_LAG + 1)
    slot_nxt = lax.rem(i + 2, LRU_SCAN_LAG + 1)

    def project_chunk(c):
        cols = slice(c * LRU_PROJ_CHUNK, (c + 1) * LRU_PROJ_CHUNK)
        val = jnp.dot(xt, w_ref[:, cols], preferred_element_type=F32)
        val = val.reshape(tt, BATCH, LRU_PROJ_CHUNK)
        if with_gate and c < LRU_WIDTH // LRU_PROJ_CHUNK:
            gate_ref[:, :, cols] = val.astype(BF16)
        else:
            ucols = slice(cols.start % LRU_WIDTH, cols.start % LRU_WIDTH + LRU_PROJ_CHUNK)
            ring[slot_new, :, :, ucols] = val

    nchunks = w_ref.shape[1] // LRU_PROJ_CHUNK
    ahead = nchunks - (LRU_BLOCKS - 2)
    for c in range(ahead):
        project_chunk(c)

    cur = ring[slot_cur]
    nxt = jnp.where(i == ntiles + LRU_SCAN_LAG - 1, 0.0, ring[slot_nxt, 0:CONV_W - 1 - CONV_LEFT])
    u = _lru_conv(cur, ulast[...], nxt, cw_ref, cb_ref, tt)
    uconv_ref[...] = u
    _lru_coeffs(u, wrg_ref, ba_ref, bx_ref, lam_ref, a_scr, b_scr, tt,
                interleave=[functools.partial(project_chunk, c) for c in range(max(ahead, 0), nchunks)])
    ulast[...] = cur[tt - CONV_LEFT:tt]

    h_prev = h_scr[...]
    h = h_prev
    for k in range(tt):
        h = a_scr[k] * h + b_scr[k]
        s_ref[k] = h
    h = jnp.where(i >= LRU_SCAN_LAG, h, h_prev)
    h_scr[...] = h
    hfin_ref[...] = h


def _lru_in_fwd(x3, scale, shift, w, h0, conv_w, conv_b, wrg, ba, bx, lam, *, with_gate, tt=32):
    length = x3.shape[1]
    ntiles = length // tt
    blk = _nbytes((tt, BATCH, LRU_WIDTH), F32)
    proj_tile = lambda i: jnp.minimum(i, ntiles - 1)
    scan_tile = lambda i: jnp.maximum(i - LRU_SCAN_LAG, 0)
    tm_spec = lambda tile: pl.BlockSpec((tt, BATCH, LRU_WIDTH), lambda i: (tile(i), 0, 0))
    out_specs = [tm_spec(scan_tile), tm_spec(scan_tile),
                 pl.BlockSpec((BATCH, LRU_WIDTH), lambda i: (0, 0))]
    out_shape = [jax.ShapeDtypeStruct((length, BATCH, LRU_WIDTH), F32)] * 2 + [
        jax.ShapeDtypeStruct((BATCH, LRU_WIDTH), F32)]
    outs = [blk, blk]
    if with_gate:
        out_specs = [tm_spec(proj_tile)] + out_specs
        out_shape = [jax.ShapeDtypeStruct((length, BATCH, LRU_WIDTH), BF16)] + out_shape
        outs.append(blk // 2)
    res = pl.pallas_call(
        functools.partial(_lru_in_fwd_kernel, tt=tt, with_gate=with_gate),
        grid=(ntiles + LRU_SCAN_LAG,),
        in_specs=[pl.BlockSpec((BATCH, tt, D_MODEL), lambda i: (0, proj_tile(i), 0)),
                  _const_spec(scale.shape), _const_spec(shift.shape), _const_spec(w.shape),
                  _const_spec((BATCH, LRU_WIDTH)),
                  _const_spec((CONV_W, LRU_WIDTH)), _const_spec((1, LRU_WIDTH)),
                  _const_spec(wrg.shape),
                  _const_spec((1, LRU_WIDTH)), _const_spec((1, LRU_WIDTH)), _const_spec((1, LRU_WIDTH))],
        out_specs=out_specs,
        out_shape=out_shape,
        scratch_shapes=[
            pltpu.VMEM((LRU_SCAN_LAG + 1, tt, BATCH, LRU_WIDTH), F32),
            pltpu.VMEM((CONV_LEFT, BATCH, LRU_WIDTH), F32),
            pltpu.VMEM((BATCH, LRU_WIDTH), F32),
            pltpu.VMEM((tt, BATCH, LRU_WIDTH), F32), pltpu.VMEM((tt, BATCH, LRU_WIDTH), F32),
        ],
        compiler_params=pltpu.CompilerParams(
            dimension_semantics=("arbitrary",),
            vmem_limit_bytes=_vmem_limit(
                [blk] + outs, [_nbytes(w.shape, BF16), _nbytes(wrg.shape, BF16),
                               (LRU_SCAN_LAG + 3) * blk], 8 * blk)),
        name="lru_in_fwd",
    )(x3, scale, shift, w, h0, conv_w, conv_b, wrg, ba, bx, lam)
    return res if with_gate else (None,) + tuple(res)


def _lru_conv(cur, prev, nxt, cw_ref, cb_ref, tt):
    ext = jnp.concatenate([prev, cur, nxt], axis=0)
    u = cb_ref[...].reshape(1, 1, LRU_WIDTH)
    for tap in range(CONV_W):
        u = u + ext[tap:tap + tt] * cw_ref[tap:tap + 1, :].reshape(1, 1, LRU_WIDTH)
    return u


def _lru_coeffs(u, wrg_ref, ba_ref, bx_ref, lam_ref, a_scr, b_scr, tt, interleave=()):
    u2 = u.reshape(tt * BATCH, LRU_WIDTH)
    ub16 = u2.astype(BF16)
    col = lambda blk: slice(blk * LRU_BLOCK_W, (blk + 1) * LRU_BLOCK_W)
    lam = lam_ref[...]
    half_decay2 = (-0.5 * LRU_C * LOG2E) * (jnp.maximum(-lam, 0.0) + jnp.log1p(jnp.exp(-jnp.abs(lam))))
    half_ba = 0.5 * ba_ref[...]
    half_bx = 0.5 * bx_ref[...]
    rgs = []
    for blk in range(LRU_BLOCKS + 1):
        if blk < LRU_BLOCKS:
            rgs.append(jnp.dot(ub16[:, col(blk)], wrg_ref[blk], preferred_element_type=F32))
            if blk < len(interleave):
                interleave[blk]()
        if blk == 0:
            continue
        blk -= 1
        rg = rgs[blk]
        cols = col(blk)
        tr = jnp.tanh(rg[:, :LRU_BLOCK_W] + half_ba[:, cols])
        tg = jnp.tanh(rg[:, LRU_BLOCK_W:] + half_bx[:, cols])
        hd = half_decay2[:, cols]
        a = jnp.exp2(hd + hd * tr)
        x = 1.0 - a * a
        hu = 0.5 * u2[:, cols]
        b = jnp.where(x > 0.0, x * lax.rsqrt(x), 0.0) * (hu + hu * tg)
        a_scr[:, :, cols] = a.reshape(tt, BATCH, LRU_BLOCK_W)
        b_scr[:, :, cols] = b.reshape(tt, BATCH, LRU_BLOCK_W)


def _lru_bwd_state_kernel(u_ref, wrg_ref, ba_ref, bx_ref, lam_ref, hfin_ref, a_scr, b_scr, *, tt):
    @pl.when(pl.program_id(0) == 0)
    def _():
        hfin_ref[...] = jnp.zeros(hfin_ref.shape, F32)

    _lru_coeffs(u_ref[...], wrg_ref, ba_ref, bx_ref, lam_ref, a_scr, b_scr, tt)
    step = lambda k, h: a_scr[tt - 1 - k] * h + b_scr[tt - 1 - k]
    hfin_ref[...] = lax.fori_loop(0, tt, step, hfin_ref[...], unroll=4)


def _lru_bwd_state(u_conv, wrg, ba, bx, lam, *, tt=32):
    nsteps = u_conv.shape[0] // tt
    blk = _nbytes((tt, BATCH, LRU_WIDTH), F32)
    return pl.pallas_call(
        functools.partial(_lru_bwd_state_kernel, tt=tt),
        grid=(nsteps,),
        in_specs=[
            pl.BlockSpec((tt, BATCH, LRU_WIDTH), lambda i: (nsteps - 1 - i, 0, 0)),
            _const_spec(wrg.shape),
            _const_spec((1, LRU_WIDTH)), _const_spec((1, LRU_WIDTH)), _const_spec((1, LRU_WIDTH)),
        ],
        out_specs=pl.BlockSpec((BATCH, LRU_WIDTH), lambda i: (0, 0)),
        out_shape=jax.ShapeDtypeStruct((BATCH, LRU_WIDTH), F32),
        scratch_shapes=[pltpu.VMEM((tt, BATCH, LRU_WIDTH), F32)] * 2,
        compiler_params=pltpu.CompilerParams(
            dimension_semantics=("arbitrary",),
            vmem_limit_bytes=_vmem_limit([blk], [2 * blk, _nbytes(wrg.shape, BF16)], 6 * blk)),
        name="lru_bwd_state",
    )(u_conv, wrg, ba, bx, lam)


def _lru_bwd_out_kernel(u_ref, h0_ref, wrg_ref, ba_ref, bx_ref, lam_ref, sf_ref, gate_ref, h_ref,
                        g2_ref, w_ref, lng_ref, lnb_ref, o_ref,
                        h_scr, a_scr, b_scr, sb_ring, out_scr, *, tt):
    i = pl.program_id(0)
    ntiles = pl.num_programs(0) - 1

    @pl.when(i == 0)
    def _():
        h_scr[...] = h0_ref[...]
        sb_ring[...] = jnp.zeros(sb_ring.shape, F32)

    y = sb_ring[lax.rem(i + 1, 2)] + sf_ref[...]
    z = (_gelu_tanh(gate_ref[...].astype(F32)) * y).astype(BF16)
    perm = _row_perm(to_time_major=False)
    parts = []
    for tau in range(tt // PERM_STEPS):
        sub = z[tau * PERM_STEPS:(tau + 1) * PERM_STEPS].reshape(PERM_STEPS * BATCH, LRU_WIDTH)
        zb = jnp.dot(perm, sub, preferred_element_type=F32).astype(BF16)
        parts.append(zb.reshape(BATCH, PERM_STEPS, LRU_WIDTH))
    zb = jnp.concatenate(parts, axis=1).reshape(BATCH * tt, LRU_WIDTH)

    def project_chunk(c):
        cols = slice(c * LRU_PROJ_CHUNK, (c + 1) * LRU_PROJ_CHUNK)
        val = jnp.dot(zb, w_ref[:, cols], preferred_element_type=F32)
        out_scr[:, :, cols] = val.reshape(BATCH, tt, LRU_PROJ_CHUNK)

    _lru_coeffs(u_ref[...], wrg_ref, ba_ref, bx_ref, lam_ref, a_scr, b_scr, tt,
                interleave=[functools.partial(project_chunk, c)
                            for c in range(D_MODEL // LRU_PROJ_CHUNK)])
    slot = lax.rem(i, 2)

    h = h_scr[...]
    for idx in reversed(range(tt)):
        h = a_scr[idx] * h + b_scr[idx]
        sb_ring[slot, idx] = h
    h_scr[...] = h
    res = DEEPNORM_ALPHA * h_ref[...] + g2_ref[...] * out_scr[...]
    o_ref[...] = _layer_norm(res, lng_ref[...].reshape(1, 1, D_MODEL), lnb_ref[...].reshape(1, 1, D_MODEL))


def _lru_bwd_out(u_conv, h0, wrg, ba, bx, lam, s_f, gate_t, h3, g2, w_out, ln_g, ln_b, *, tt=32):
    ntiles = SEQ // tt
    scan_tile = lambda i: jnp.maximum(ntiles - 1 - i, 0)
    out_tile = lambda i: jnp.minimum(ntiles - i, ntiles - 1)
    tm_blk = lambda: pl.BlockSpec((tt, BATCH, LRU_WIDTH), lambda i: (out_tile(i), 0, 0))
    bm_blk = lambda: pl.BlockSpec((BATCH, tt, D_MODEL), lambda i: (0, out_tile(i), 0))
    blk = _nbytes((tt, BATCH, LRU_WIDTH), F32)
    return pl.pallas_call(
        functools.partial(_lru_bwd_out_kernel, tt=tt),
        grid=(ntiles + 1,),
        in_specs=[
            pl.BlockSpec((tt, BATCH, LRU_WIDTH), lambda i: (scan_tile(i), 0, 0)),
            _const_spec((BATCH, LRU_WIDTH)), _const_spec(wrg.shape),
            _const_spec((1, LRU_WIDTH)), _const_spec((1, LRU_WIDTH)), _const_spec((1, LRU_WIDTH)),
            tm_blk(), tm_blk(), bm_blk(),
            _const_spec((BATCH, 1, D_MODEL)), _const_spec(w_out.shape),
            _const_spec((1, D_MODEL)), _const_spec((1, D_MODEL))],
        out_specs=bm_blk(),
        out_shape=jax.ShapeDtypeStruct((BATCH, SEQ, D_MODEL), F32),
        scratch_shapes=[
            pltpu.VMEM((BATCH, LRU_WIDTH), F32),
            pltpu.VMEM((tt, BATCH, LRU_WIDTH), F32), pltpu.VMEM((tt, BATCH, LRU_WIDTH), F32),
            pltpu.VMEM((2, tt, BATCH, LRU_WIDTH), F32), pltpu.VMEM((BATCH, tt, D_MODEL), F32),
        ],
        compiler_params=pltpu.CompilerParams(
            dimension_semantics=("arbitrary",),
            vmem_limit_bytes=_vmem_limit(
                [blk] * 5, [5 * blk, _nbytes(w_out.shape, BF16), _nbytes(wrg.shape, BF16)], 8 * blk)),
        name="lru_bwd_out",
    )(u_conv, h0, wrg, ba, bx, lam, s_f, gate_t, h3, g2, w_out, ln_g, ln_b)


def _rope_tables():
    rows = SEQ // GRID_W
    row = jnp.repeat(jnp.arange(rows, dtype=F32), GRID_W)
    col = jnp.tile(jnp.arange(GRID_W, dtype=F32), rows)
    inv = ROPE_THETA ** (-jnp.arange(ROPE_FREQS, dtype=F32) / ROPE_FREQS)
    ang = jnp.concatenate([row[:, None] * inv, col[:, None] * inv], axis=-1)
    cos, sin = jnp.cos(ang), jnp.sin(ang)
    cos_t = jnp.tile(cos, (1, V7X_LANES // (HEAD_DIM // 2)))
    sin_t = jnp.tile(jnp.concatenate([-sin, sin], axis=-1), (1, V7X_LANES // HEAD_DIM))
    return cos_t, sin_t


def kernel(x, c, ctx, c_ctx, w_mod, b_mod, ln_g, ln_b, ffn_w_gate, ffn_w_up, ffn_w_down, mix_ab_w_in, attn_sink, pool_w, pool_scale, mix_ab_w_out, lru_w_in, lru_conv_w, lru_conv_b, lru_wa, lru_ba, lru_wx, lru_bx, lru_lambda, lru_w_out):
    assert x.shape == (BATCH, SEQ, D_MODEL) and ctx.shape == (BATCH, CTX_LEN, D_MODEL)
    c_all = jnp.concatenate(
        [c, c_ctx[None, :], jnp.zeros((MOD_ROWS - BATCH - 1, D_MODEL), F32)], axis=0)
    mod_all = _modulation(c_all, w_mod, b_mod)
    mod_lat = mod_all[:, :BATCH].reshape(DEPTH, BATCH, N_MOD, D_MODEL)
    mod_ctx = mod_all[:, BATCH:BATCH + 1].reshape(DEPTH, 1, N_MOD, D_MODEL)

    wg = ffn_w_gate.astype(BF16)
    wu = ffn_w_up.astype(BF16)
    wd = ffn_w_down.astype(BF16)
    row = lambda v: v.reshape(1, -1)
    n_ctx = BATCH * CTX_LEN

    h = x.reshape(BATCH * SEQ, D_MODEL)
    hc = ctx.reshape(n_ctx, D_MODEL)

    l = 0
    ml, mc = mod_lat[l], mod_ctx[l]
    ffn1 = ((l, 0), wg, wu, wd, row(ln_g[l, 0]), row(ln_b[l, 0]))
    ffn2 = ((l, 1), wg, wu, wd, row(ln_g[l, 2]), row(ln_b[l, 2]))
    h = _ffn(h, ml, 0, *ffn1, nb=BATCH, length=SEQ)
    hc = _ffn(hc, mc, 0, *ffn1, nb=1, length=n_ctx)
    w_in = mix_ab_w_in[0].astype(BF16)
    cos_t, sin_t = _rope_tables()
    q, k2, v2, u = _attn_inproj(h, ml, w_in, cos_t, sin_t, nb=BATCH, length=SEQ)
    q_c, k2_c, v2_c, u_c = _attn_inproj(hc, mc, w_in, None, None, nb=1, length=n_ctx)
    mix_args = (pool_w[0].astype(BF16), row(pool_scale[0]), mix_ab_w_out[0].astype(BF16),
                row(ln_g[l, 1]), row(ln_b[l, 1]))
    h = _attn_lat(attn_sink[0], q, k2, v2, k2_c, v2_c, u, h, ml, *mix_args)
    hc = _attn_ctx(attn_sink[0], q_c, k2_c, v2_c, u_c, hc, mc, *mix_args)
    h = _ffn(h, ml, 6, *ffn2, nb=BATCH, length=SEQ)
    hc = _ffn(hc, mc, 6, *ffn2, nb=1, length=n_ctx)

    l = 1
    ml, mc = mod_lat[l], mod_ctx[l]
    ffn1 = ((l, 0), wg, wu, wd, row(ln_g[l, 0]), row(ln_b[l, 0]))
    ffn2 = ((l, 1), wg, wu, wd, row(ln_g[l, 2]), row(ln_b[l, 2]))
    h = _ffn(h, ml, 0, *ffn1, nb=BATCH, length=SEQ)
    hc = _ffn(hc, mc, 0, *ffn1, nb=1, length=n_ctx)
    w_in = lru_w_in[0].astype(BF16)
    h3 = h.reshape(BATCH, SEQ, D_MODEL)
    wrg = (0.5 * jnp.concatenate([lru_wa[0], lru_wx[0]], axis=-1)).astype(BF16)
    conv_w, conv_b = lru_conv_w[0], row(lru_conv_b[0])
    zeros = jnp.zeros((BATCH, LRU_WIDTH), F32)
    dir_args = [(wrg[d], row(lru_ba[0, d]), row(lru_bx[0, d]), row(lru_lambda[0, d]))
                for d in range(2)]
    _, uc_t, _, hf_ctx = _lru_in_fwd(hc.reshape(BATCH, CTX_LEN, D_MODEL), mc[:, 4:5, :], mc[:, 3:4, :],
                                     w_in[:, LRU_WIDTH:], zeros, conv_w, conv_b, *dir_args[0],
                                     with_gate=False)
    gate_t, u_t, s_f, _ = _lru_in_fwd(h3, ml[:, 4:5, :], ml[:, 3:4, :], w_in, hf_ctx, conv_w, conv_b,
                                      *dir_args[0], with_gate=True)
    hb_ctx = _lru_bwd_state(uc_t, *dir_args[1])
    h3 = _lru_bwd_out(u_t, hb_ctx, *dir_args[1], s_f, gate_t, h3, ml[:, 5:6, :],
                      lru_w_out[0].astype(BF16), row(ln_g[l, 1]), row(ln_b[l, 1]))
    h = _ffn(h3.reshape(BATCH * SEQ, D_MODEL), ml, 6, *ffn2, nb=BATCH, length=SEQ)
    return h.reshape(BATCH, SEQ, D_MODEL)
```

```python
import functools

import jax
import jax.numpy as jnp
from jax import lax
from jax.experimental import pallas as pl
from jax.experimental.pallas import tpu as pltpu

D_MODEL = 1024
BATCH = 16
SEQ = 2048
DEPTH = 2
GRID_W = 64
CTX_LEN = 256
HEAD_DIM = 64
ATT_HEADS = 8
ATT_KV_HEADS = 2
ATT_GROUPS = ATT_HEADS // ATT_KV_HEADS
ATT_WIDTH = ATT_HEADS * HEAD_DIM
KV_WIDTH = ATT_KV_HEADS * HEAD_DIM
WINDOW = 128
BLOCK = 128
ATT_SCALE = HEAD_DIM ** -0.5
LOG2E = 1.4426950408889634
ROPE_THETA = 10000.0
ROPE_FREQS = HEAD_DIM // 4
POOL_WINDOWS = (2, 4, 8, 16)
POOL_WIDTH = D_MODEL // 2
POOL_GROUP_W = POOL_WIDTH // len(POOL_WINDOWS)
MIX_AB_IN = ATT_WIDTH + 2 * KV_WIDTH + POOL_WIDTH
LRU_WIDTH = D_MODEL
LRU_BLOCKS = 8
LRU_BLOCK_W = LRU_WIDTH // LRU_BLOCKS
LRU_C = 8.0
CONV_W = 4
CONV_LEFT = (CONV_W - 1) // 2
D_FF = 2816
N_MOD = 9
LN_EPS = 1e-5
NEG_INF = -1e30
DEEPNORM_ALPHA = (2 * DEPTH) ** 0.25

V7X_LANES = 128
V7X_SUBLANES = 8
V7X_VMEM_BYTES = 64 * 1024 * 1024
V7X_VMEM_USABLE_BYTES = 60000 * 1024

F32 = jnp.float32
BF16 = jnp.bfloat16

POOL_HALO = V7X_SUBLANES
PERM_STEPS = 16
KEY_SLAB_W = ATT_KV_HEADS * V7X_LANES
VAL_SLAB_W = 2 * ATT_KV_HEADS * V7X_LANES
FFN_LOOKAHEAD = 1
LRU_PROJ_CHUNK = 256
LRU_SCAN_LAG = 2
SCORE_LOOKAHEAD = 2
EPILOGUE_BLOCKS = 2
MOD_ROWS = 24


def _nbytes(shape, dtype):
    n = 1
    for s in shape:
        n *= s
    return n * jnp.dtype(dtype).itemsize


def _vmem_limit(pipelined, resident, temporaries):
    est = 2 * sum(pipelined) + sum(resident) + temporaries
    return int(min(V7X_VMEM_USABLE_BYTES, max(est * 5 // 4, 16 * 1024 * 1024)))


def _const_spec(shape):
    nd = len(shape)
    return pl.BlockSpec(shape, lambda *_: (0,) * nd, pipeline_mode=pl.Buffered(1))


def _tok_spec(tm, width, nt):
    return pl.BlockSpec((tm, width), lambda b, t: (b * nt + t, 0))


def _mod_spec():
    return pl.BlockSpec((1, N_MOD, D_MODEL), lambda b, t: (b, 0, 0))


def _layer_norm(z, g, b):
    mu = jnp.mean(z, axis=-1, keepdims=True)
    zc = z - mu
    var = jnp.mean(zc * zc, axis=-1, keepdims=True)
    return zc * lax.rsqrt(var + LN_EPS) * g + b


def _gelu_tanh(x):
    c = 0.7978845608028654
    half = 0.5 * x
    return half + half * jnp.tanh(x * (c + (c * 0.044715) * (x * x)))


def _mod_kernel(c_ref, w_ref, b_ref, o_ref):
    c = c_ref[...]
    a_hi, a_lo = _split_bf16(c * jax.nn.sigmoid(c))
    w_hi, w_lo = _split_bf16(w_ref[0])
    acc = jnp.dot(a_hi, w_lo, preferred_element_type=F32) + jnp.dot(a_lo, w_hi, preferred_element_type=F32)
    o_ref[0] = acc + jnp.dot(a_hi, w_hi, preferred_element_type=F32) + b_ref[0]


def _modulation(c_all, w_mod, b_mod):
    tn = 2304
    n_out = N_MOD * D_MODEL
    blocks = [_nbytes((1, D_MODEL, tn), F32), _nbytes((1, MOD_ROWS, tn), F32)]
    return pl.pallas_call(
        _mod_kernel,
        grid=(DEPTH, n_out // tn),
        in_specs=[
            pl.BlockSpec((MOD_ROWS, D_MODEL), lambda l, j: (0, 0)),
            pl.BlockSpec((1, D_MODEL, tn), lambda l, j: (l, 0, j)),
            pl.BlockSpec((1, 1, tn), lambda l, j: (l, 0, j)),
        ],
        out_specs=pl.BlockSpec((1, MOD_ROWS, tn), lambda l, j: (l, 0, j)),
        out_shape=jax.ShapeDtypeStruct((DEPTH, MOD_ROWS, n_out), F32),
        compiler_params=pltpu.CompilerParams(
            dimension_semantics=("arbitrary", "arbitrary"),
            vmem_limit_bytes=_vmem_limit(blocks, [], 4 * blocks[0])),
        name="modulation",
    )(c_all, w_mod, b_mod.reshape(DEPTH, 1, n_out))


def _ffn_kernel(x_ref, mod_ref, wg_ref, wu_ref, wd_ref, lng_ref, lnb_ref, o_ref, *, j0, sub):
    shift = mod_ref[0, j0:j0 + 1, :]
    scale = mod_ref[0, j0 + 1:j0 + 2, :]
    gate = mod_ref[0, j0 + 2:j0 + 3, :]
    sizes = [sub] * (x_ref.shape[0] // sub)
    starts = [s * sub for s in range(len(sizes))]
    nsub = len(sizes)

    def gate_up(s):
        x = x_ref[starts[s]:starts[s] + sizes[s], :]
        xin = (x * (1.0 + scale) + shift).astype(BF16)
        return (jnp.dot(xin, wg_ref[...], preferred_element_type=F32),
                jnp.dot(xin, wu_ref[...], preferred_element_type=F32))

    pending = [gate_up(s) for s in range(min(FFN_LOOKAHEAD, nsub))]
    for s in range(nsub):
        g, u = pending.pop(0)
        if s + FFN_LOOKAHEAD < nsub:
            pending.append(gate_up(s + FFN_LOOKAHEAD))
        rows = slice(starts[s], starts[s] + sizes[s])
        a = (g * jax.nn.sigmoid(g) * u).astype(BF16)
        y = jnp.dot(a, wd_ref[...], preferred_element_type=F32)
        z = DEEPNORM_ALPHA * x_ref[rows, :] + (0.5 * gate) * y
        o_ref[rows, :] = _layer_norm(z, lng_ref[...], lnb_ref[...])


def _ffn(x, mod, j0, which, wg, wu, wd, ln_g, ln_b, *, nb, length, tm=1024, sub=256):
    nt = length // tm
    tok = _nbytes((tm, D_MODEL), F32)
    weights = [_nbytes(w.shape[2:], BF16) for w in (wg, wu, wd)]
    temps = 3 * _nbytes((tm, D_FF), F32) + 4 * tok
    wspec = lambda w: pl.BlockSpec((None, None) + w.shape[2:], lambda b, t: which + (0, 0),
                                   pipeline_mode=pl.Buffered(1))
    return pl.pallas_call(
        functools.partial(_ffn_kernel, j0=j0, sub=sub),
        grid=(nb, nt),
        in_specs=[
            _tok_spec(tm, D_MODEL, nt),
            _mod_spec(),
            wspec(wg), wspec(wu), wspec(wd),
            _const_spec((1, D_MODEL)), _const_spec((1, D_MODEL)),
        ],
        out_specs=_tok_spec(tm, D_MODEL, nt),
        out_shape=jax.ShapeDtypeStruct((nb * length, D_MODEL), F32),
        compiler_params=pltpu.CompilerParams(
            dimension_semantics=("arbitrary", "arbitrary"),
            vmem_limit_bytes=_vmem_limit([tok, tok], weights, temps)),
        name="ffn",
    )(x, mod, wg, wu, wd, ln_g, ln_b)


def _dup_halves(z, lane):
    zr = pltpu.roll(z, HEAD_DIM, 1)
    lo = lane < HEAD_DIM
    return jnp.where(lo, z, zr), jnp.where(lo, zr, z)


def _attn_inproj_kernel(*refs, rope):
    if rope:
        x_ref, mod_ref, w_ref, cos_ref, sin_ref, q_ref, k_ref, v_ref, u_ref = refs
    else:
        x_ref, mod_ref, w_ref, q_ref, k_ref, v_ref, u_ref = refs
    x = x_ref[...]
    xin = (x * (1.0 + mod_ref[0, 4:5, :]) + mod_ref[0, 3:4, :]).astype(BF16)
    p = jnp.dot(xin, w_ref[...], preferred_element_type=F32)
    lane = lax.broadcasted_iota(jnp.int32, (x.shape[0], V7X_LANES), 1)
    first_half = (lane & (HEAD_DIM - 1)) < HEAD_DIM // 2

    def rot(z):
        if not rope:
            return z
        zr = jnp.where(first_half, pltpu.roll(z, V7X_LANES - HEAD_DIM // 2, 1),
                       pltpu.roll(z, HEAD_DIM // 2, 1))
        return z * cos_ref[...] + zr * sin_ref[...]

    for c in range(ATT_WIDTH // V7X_LANES):
        sl = slice(c * V7X_LANES, (c + 1) * V7X_LANES)
        q_ref[:, sl] = (rot(p[:, sl]) * (ATT_SCALE * LOG2E)).astype(BF16)
    k0, k1 = _dup_halves(rot(p[:, ATT_WIDTH:ATT_WIDTH + KV_WIDTH]), lane)
    k_ref[...] = jnp.concatenate([k0, k1], axis=-1).T.astype(BF16)
    v = p[:, ATT_WIDTH + KV_WIDTH:ATT_WIDTH + 2 * KV_WIDTH]
    vr = pltpu.roll(v, HEAD_DIM, 1)
    lo = lane < HEAD_DIM
    slabs = (jnp.where(lo, v, 1.0), jnp.where(lo, 1.0, vr), jnp.where(lo, vr, 1.0), jnp.where(lo, 1.0, v))
    for i, slab in enumerate(slabs):
        v_ref[:, i * V7X_LANES:(i + 1) * V7X_LANES] = slab.astype(BF16)
    u_ref[...] = p[:, ATT_WIDTH + 2 * KV_WIDTH:]


def _attn_inproj(x, mod, w_in, cos, sin, *, nb, length, tm=1024):
    nt = length // tm
    rope = cos is not None
    tok = _nbytes((tm, D_MODEL), F32)
    outs = [_nbytes((tm, ATT_WIDTH), BF16), _nbytes((tm, KEY_SLAB_W + VAL_SLAB_W), BF16),
            _nbytes((tm, POOL_WIDTH), F32)]
    in_specs = [_tok_spec(tm, D_MODEL, nt), _mod_spec(), _const_spec(w_in.shape)]
    args = [x, mod, w_in]
    if rope:
        in_specs += [pl.BlockSpec((tm, V7X_LANES), lambda b, t: (t, 0))] * 2
        args += [cos, sin]
    rows = nb * length
    return pl.pallas_call(
        functools.partial(_attn_inproj_kernel, rope=rope),
        grid=(nb, nt),
        in_specs=in_specs,
        out_specs=[
            _tok_spec(tm, ATT_WIDTH, nt),
            pl.BlockSpec((KEY_SLAB_W, tm), lambda b, t: (0, b * nt + t)),
            _tok_spec(tm, VAL_SLAB_W, nt),
            _tok_spec(tm, POOL_WIDTH, nt),
        ],
        out_shape=[
            jax.ShapeDtypeStruct((rows, ATT_WIDTH), BF16),
            jax.ShapeDtypeStruct((KEY_SLAB_W, rows), BF16),
            jax.ShapeDtypeStruct((rows, VAL_SLAB_W), BF16),
            jax.ShapeDtypeStruct((rows, POOL_WIDTH), F32),
        ],
        compiler_params=pltpu.CompilerParams(
            dimension_semantics=("arbitrary", "arbitrary"),
            vmem_limit_bytes=_vmem_limit([tok] + outs, [_nbytes(w_in.shape, BF16)],
                                         3 * _nbytes((tm, MIX_AB_IN), F32))),
        name="attn_inproj",
    )(*args)


def _attend_scores(q_blk, keys, kh):
    lane = lax.broadcasted_iota(jnp.int32, (BLOCK, V7X_LANES), 1)
    lo = lane < HEAD_DIM
    zero = jnp.zeros((BLOCK, V7X_LANES), BF16)
    parts = []
    for g in range(ATT_GROUPS):
        c = (kh * ATT_GROUPS + g) // 2
        qc = q_blk[:, c * V7X_LANES:(c + 1) * V7X_LANES]
        parts.append(jnp.where(lo if g % 2 == 0 else jnp.logical_not(lo), qc, zero))
    qs = jnp.concatenate(parts, axis=0)
    return [jnp.dot(qs, k, preferred_element_type=F32) for k in keys]


def _attend_finish(scores, values_even, values_odd, segments, sink_ref, kh):
    lane = lax.broadcasted_iota(jnp.int32, (BLOCK, V7X_LANES), 1)
    lo = lane < HEAD_DIM
    probs = ([[] for _ in scores], [[] for _ in scores])
    sink_terms = []
    for g in range(ATT_GROUPS):
        rows = slice(g * BLOCK, (g + 1) * BLOCK)
        segs = [[s[rows, a:b] if bias is None else s[rows, a:b] + bias for a, b, bias in seg]
                for s, seg in zip(scores, segments)]
        sink = sink_ref[kh * ATT_GROUPS + g] * LOG2E
        chunks = [x[:, c:c + V7X_LANES] for sl in segs for x in sl
                  for c in range(0, x.shape[1], V7X_LANES)]
        m = functools.reduce(jnp.maximum, chunks).max(axis=-1, keepdims=True)
        m = jnp.maximum(m, sink)
        sink_terms.append(jnp.exp2(sink - m))
        for i, sl in enumerate(segs):
            e = [jnp.exp2(x - m).astype(BF16) for x in sl]
            probs[g % 2][i].append(e[0] if len(e) == 1 else jnp.concatenate(e, axis=-1))

    def weighted_values(ps, vals):
        o = None
        for p, v in zip(ps, vals):
            t = jnp.dot(jnp.concatenate(p, axis=0), v, preferred_element_type=F32)
            o = t if o is None else o + t
        return o

    o_par = (weighted_values(probs[0], values_even), weighted_values(probs[1], values_odd))
    outs = []
    for g in range(ATT_GROUPS):
        o = o_par[g % 2][(g // 2) * BLOCK:(g // 2 + 1) * BLOCK]
        rinv = 1.0 / (o + sink_terms[g])
        outs.append(o * pltpu.roll(rinv, HEAD_DIM, 1))
    return jnp.where(lo, outs[0], outs[1]), jnp.where(lo, outs[2], outs[3])


def _pool_sums(ubuf_ref, nblk, band_ref):
    u_hi, u_lo = _split_bf16(ubuf_ref[...])
    sums = {}
    for j in range(nblk):
        win = slice(j * BLOCK, (j + 2) * BLOCK)
        for gi in range(len(POOL_WINDOWS)):
            cols = slice(gi * POOL_GROUP_W, (gi + 1) * POOL_GROUP_W)
            band = band_ref[gi]
            sums[j, gi] = (jnp.dot(band, u_hi[win, cols], preferred_element_type=F32)
                           + jnp.dot(band, u_lo[win, cols], preferred_element_type=F32))
    return sums


def _pool_finish(sums, ubuf_ref, blocks, pos0, length, wpool_ref, pscale_ref, mix_scr):
    for j in blocks:
        pos = pos0 + j * BLOCK + lax.broadcasted_iota(jnp.int32, (BLOCK, 1), 0)
        rows = slice(POOL_HALO + j * BLOCK, POOL_HALO + (j + 1) * BLOCK)
        for gi, w in enumerate(POOL_WINDOWS):
            r = w // 2
            cols = slice(gi * POOL_GROUP_W, (gi + 1) * POOL_GROUP_W)
            cnt = (jnp.minimum(pos + r, length - 1) - jnp.maximum(pos - r, 0) + 1).astype(F32)
            d = sums[j, gi] / cnt - ubuf_ref[rows, cols]
            y = jnp.dot(d.astype(BF16), wpool_ref[gi], preferred_element_type=F32)
            mix_scr[j * BLOCK:(j + 1) * BLOCK, ATT_WIDTH + gi * POOL_GROUP_W:
                    ATT_WIDTH + (gi + 1) * POOL_GROUP_W] = (y * pscale_ref[:, cols]).astype(BF16)


def _split_bf16(x):
    hi = x.astype(BF16)
    return hi, (x - hi.astype(F32)).astype(BF16)


def _pool_bands():
    i = jnp.arange(BLOCK)[:, None]
    c = jnp.arange(2 * BLOCK)[None, :]
    return jnp.stack([(jnp.abs(c - POOL_HALO - i) <= w // 2) for w in POOL_WINDOWS]).astype(BF16)


def _attn_out(mix_scr, rows, wout_ref, h_ref, gate, lng_ref, lnb_ref, o_ref):
    y = jnp.dot(mix_scr[rows, :], wout_ref[...], preferred_element_type=F32)
    z = DEEPNORM_ALPHA * h_ref[rows, :] + gate * y
    o_ref[rows, :] = _layer_norm(z, lng_ref[...], lnb_ref[...])


def _attn_lat_kernel(sink_ref, q_ref, kc_ref, kp_ref, kn_ref, vc_ref, vp_ref, vn_ref,
                     kctx_ref, vctx_ref, uc_ref, up_ref, un_ref, h_ref, mod_ref, bias_ref, band_ref,
                     wpool_ref, pscale_ref, wout_ref, lng_ref, lnb_ref, o_ref,
                     kbuf, vbuf, ubuf, mix_scr, *, tq, length):
    t = pl.program_id(1)
    last = pl.num_programs(1) - 1
    nblk = tq // BLOCK
    kbuf[:, 0:BLOCK] = kp_ref[...]
    kbuf[:, BLOCK:BLOCK + tq] = kc_ref[...]
    kbuf[:, BLOCK + tq:] = kn_ref[...]
    vbuf[0:BLOCK] = vp_ref[...]
    vbuf[BLOCK:BLOCK + tq] = vc_ref[...]
    vbuf[BLOCK + tq:] = vn_ref[...]
    ubuf[0:POOL_HALO] = jnp.where(t == 0, 0.0, up_ref[...])
    ubuf[POOL_HALO:POOL_HALO + tq] = uc_ref[...]
    ubuf[POOL_HALO + tq:2 * POOL_HALO + tq] = jnp.where(t == last, 0.0, un_ref[...])
    ubuf[2 * POOL_HALO + tq:] = jnp.zeros((2 * BLOCK - 2 * POOL_HALO, POOL_WIDTH), F32)
    lanes = lambda i: slice(i * V7X_LANES, (i + 1) * V7X_LANES)
    gate = mod_ref[0, 5:6, :]
    units = [(j, kh) for j in range(nblk) for kh in range(ATT_KV_HEADS)]

    def scores_of(j, kh):
        win = slice(j * BLOCK, j * BLOCK + 3 * BLOCK)
        return _attend_scores(q_ref[j * BLOCK:(j + 1) * BLOCK, :],
                              [kbuf[lanes(kh), win], kctx_ref[lanes(kh), :]], kh)

    sums = _pool_sums(ubuf, nblk, band_ref)
    pending = [scores_of(*u) for u in units[:SCORE_LOOKAHEAD]]
    _pool_finish(sums, ubuf, range(nblk), t * tq, length, wpool_ref, pscale_ref, mix_scr)
    for n, (j, kh) in enumerate(units):
        scores = pending.pop(0)
        if n + SCORE_LOOKAHEAD < len(units):
            pending.append(scores_of(*units[n + SCORE_LOOKAHEAD]))
        blk = t * nblk + j
        bidx = jnp.where(blk == 0, 0, jnp.where(blk == length // BLOCK - 1, 2, 1))
        bias = bias_ref[bidx]
        segments = [[(0, BLOCK, bias[:, :BLOCK]), (BLOCK, 2 * BLOCK, None),
                     (2 * BLOCK, 3 * BLOCK, bias[:, BLOCK:])], [(0, CTX_LEN, None)]]
        win = slice(j * BLOCK, j * BLOCK + 3 * BLOCK)
        rows = slice(j * BLOCK, (j + 1) * BLOCK)
        c0, c1 = _attend_finish(
            scores, [vbuf[win, lanes(2 * kh)], vctx_ref[:, lanes(2 * kh)]],
            [vbuf[win, lanes(2 * kh + 1)], vctx_ref[:, lanes(2 * kh + 1)]],
            segments, sink_ref, kh)
        mix_scr[rows, lanes(2 * kh)] = c0.astype(BF16)
        mix_scr[rows, lanes(2 * kh + 1)] = c1.astype(BF16)
        per_epilogue = EPILOGUE_BLOCKS * ATT_KV_HEADS
        if n > 0 and n % per_epilogue == 0:
            e = n // per_epilogue - 1
            done = slice(e * EPILOGUE_BLOCKS * BLOCK, (e + 1) * EPILOGUE_BLOCKS * BLOCK)
            _attn_out(mix_scr, done, wout_ref, h_ref, gate, lng_ref, lnb_ref, o_ref)
    _attn_out(mix_scr, slice(tq - EPILOGUE_BLOCKS * BLOCK, tq), wout_ref, h_ref, gate, lng_ref,
              lnb_ref, o_ref)


def _attn_ctx_kernel(sink_ref, q_ref, kctx_ref, vctx_ref, uc_ref, h_ref, mod_ref, band_ref,
                     wpool_ref, pscale_ref, wout_ref, lng_ref, lnb_ref, o_ref,
                     ubuf, mix_scr, *, tq):
    ubuf[0:POOL_HALO] = jnp.zeros((POOL_HALO, POOL_WIDTH), F32)
    ubuf[POOL_HALO:POOL_HALO + tq] = uc_ref[...]
    ubuf[POOL_HALO + tq:] = jnp.zeros((2 * BLOCK - POOL_HALO, POOL_WIDTH), F32)
    lanes = lambda i: slice(i * V7X_LANES, (i + 1) * V7X_LANES)
    nblk = tq // BLOCK
    units = [(j, kh) for j in range(nblk) for kh in range(ATT_KV_HEADS)]
    scores_of = lambda j, kh: _attend_scores(q_ref[j * BLOCK:(j + 1) * BLOCK, :],
                                             [kctx_ref[lanes(kh), :]], kh)
    sums = _pool_sums(ubuf, nblk, band_ref)
    pending = [scores_of(*u) for u in units[:SCORE_LOOKAHEAD]]
    _pool_finish(sums, ubuf, range(nblk), 0, tq, wpool_ref, pscale_ref, mix_scr)
    for n, (j, kh) in enumerate(units):
        scores = pending.pop(0)
        if n + SCORE_LOOKAHEAD < len(units):
            pending.append(scores_of(*units[n + SCORE_LOOKAHEAD]))
        rows = slice(j * BLOCK, (j + 1) * BLOCK)
        c0, c1 = _attend_finish(scores, [vctx_ref[:, lanes(2 * kh)]],
                                [vctx_ref[:, lanes(2 * kh + 1)]], [[(0, CTX_LEN, None)]],
                                sink_ref, kh)
        mix_scr[rows, lanes(2 * kh)] = c0.astype(BF16)
        mix_scr[rows, lanes(2 * kh + 1)] = c1.astype(BF16)
    _attn_out(mix_scr, slice(0, tq), wout_ref, h_ref, mod_ref[0, 5:6, :], lng_ref, lnb_ref, o_ref)


def _window_bias():
    qi = jnp.arange(BLOCK)[:, None]
    kc = jnp.arange(3 * BLOCK)[None, :]
    band = jnp.abs(qi + BLOCK - kc) <= WINDOW
    first = band & (kc >= BLOCK)
    final = band & (kc < 2 * BLOCK)
    full = jnp.where(jnp.stack([first, band, final]), 0.0, NEG_INF).astype(F32)
    return jnp.concatenate([full[:, :, :BLOCK], full[:, :, 2 * BLOCK:]], axis=-1)


def _smem_spec():
    return pl.BlockSpec(memory_space=pltpu.SMEM)


def _attn_lat(sink, q, k2, v2, kctx, vctx, u, h, mod, wpool, pscale, wout, ln_g, ln_b, *, tq=512):
    nt = SEQ // tq
    bpt = tq // BLOCK
    bps = SEQ // BLOCK
    hpt = tq // POOL_HALO
    hps = SEQ // POOL_HALO
    cur = lambda w: pl.BlockSpec((tq, w), lambda b, t: (b * nt + t, 0))
    prev_blk = lambda rows, w, per_tile, per_samp: pl.BlockSpec(
        (rows, w), lambda b, t: (b * per_samp + jnp.maximum(t * per_tile - 1, 0), 0))
    next_blk = lambda rows, w, per_tile, per_samp: pl.BlockSpec(
        (rows, w), lambda b, t: (b * per_samp + jnp.minimum((t + 1) * per_tile, per_samp - 1), 0))
    ctx_blk = lambda w: pl.BlockSpec((CTX_LEN, w), lambda b, t: (b, 0))
    kv_specs = lambda w: [cur(w), prev_blk(BLOCK, w, bpt, bps), next_blk(BLOCK, w, bpt, bps)]
    key_specs = [
        pl.BlockSpec((KEY_SLAB_W, tq), lambda b, t: (0, b * nt + t)),
        pl.BlockSpec((KEY_SLAB_W, BLOCK), lambda b, t: (0, b * bps + jnp.maximum(t * bpt - 1, 0))),
        pl.BlockSpec((KEY_SLAB_W, BLOCK),
                     lambda b, t: (0, b * bps + jnp.minimum((t + 1) * bpt, bps - 1))),
    ]
    bias = _window_bias()
    bands = _pool_bands()
    tok = _nbytes((tq, D_MODEL), F32)
    kv_rows = tq + 2 * BLOCK
    u_rows = tq + 2 * BLOCK
    pipelined = [_nbytes((tq, ATT_WIDTH), BF16), 2 * _nbytes((kv_rows, KEY_SLAB_W + VAL_SLAB_W), BF16),
                 _nbytes((tq + 2 * POOL_HALO, POOL_WIDTH), F32), 2 * tok]
    resident = [_nbytes(bias.shape, F32), _nbytes(bands.shape, BF16), _nbytes(wpool.shape, BF16),
                _nbytes(wout.shape, BF16)]
    scratch = [_nbytes((kv_rows, KEY_SLAB_W + VAL_SLAB_W), BF16),
               _nbytes((u_rows, POOL_WIDTH), F32), _nbytes((tq, D_MODEL), BF16)]
    return pl.pallas_call(
        functools.partial(_attn_lat_kernel, tq=tq, length=SEQ),
        grid=(BATCH, nt),
        in_specs=[_smem_spec(), cur(ATT_WIDTH)] + key_specs + kv_specs(VAL_SLAB_W) + [
            pl.BlockSpec((KEY_SLAB_W, CTX_LEN), lambda b, t: (0, b)), ctx_blk(VAL_SLAB_W),
            cur(POOL_WIDTH), prev_blk(POOL_HALO, POOL_WIDTH, hpt, hps),
            next_blk(POOL_HALO, POOL_WIDTH, hpt, hps),
            cur(D_MODEL), _mod_spec(),
            _const_spec(bias.shape), _const_spec(bands.shape),
            _const_spec(wpool.shape), _const_spec((1, POOL_WIDTH)),
            _const_spec(wout.shape), _const_spec((1, D_MODEL)), _const_spec((1, D_MODEL)),
        ],
        out_specs=cur(D_MODEL),
        out_shape=jax.ShapeDtypeStruct((BATCH * SEQ, D_MODEL), F32),
        scratch_shapes=[
            pltpu.VMEM((KEY_SLAB_W, kv_rows), BF16), pltpu.VMEM((kv_rows, VAL_SLAB_W), BF16),
            pltpu.VMEM((u_rows, POOL_WIDTH), F32), pltpu.VMEM((tq, D_MODEL), BF16),
        ],
        compiler_params=pltpu.CompilerParams(
            dimension_semantics=("arbitrary", "arbitrary"),
            vmem_limit_bytes=_vmem_limit(pipelined, resident + scratch, 8 * tok)),
        name="attn_latent",
    )(sink, q, k2, k2, k2, v2, v2, v2, kctx, vctx, u, u, u, h, mod, bias, bands, wpool, pscale, wout,
      ln_g, ln_b)


def _attn_ctx(sink, q, kctx, vctx, u, h, mod, wpool, pscale, wout, ln_g, ln_b):
    tq = CTX_LEN
    blk = lambda w: pl.BlockSpec((tq, w), lambda b, t: (b, 0))
    bands = _pool_bands()
    tok = _nbytes((tq, D_MODEL), F32)
    u_rows = tq + 2 * BLOCK
    pipelined = [_nbytes((tq, ATT_WIDTH), BF16), _nbytes((tq, KEY_SLAB_W + VAL_SLAB_W), BF16),
                 _nbytes((tq, POOL_WIDTH), F32), 2 * tok]
    resident = [_nbytes(bands.shape, BF16), _nbytes(wpool.shape, BF16), _nbytes(wout.shape, BF16)]
    scratch = [_nbytes((u_rows, POOL_WIDTH), F32), _nbytes((tq, D_MODEL), BF16)]
    return pl.pallas_call(
        functools.partial(_attn_ctx_kernel, tq=tq),
        grid=(BATCH, 1),
        in_specs=[
            _smem_spec(),
            blk(ATT_WIDTH), pl.BlockSpec((KEY_SLAB_W, tq), lambda b, t: (0, b)),
            blk(VAL_SLAB_W), blk(POOL_WIDTH), blk(D_MODEL),
            pl.BlockSpec((1, N_MOD, D_MODEL), lambda b, t: (0, 0, 0)),
            _const_spec(bands.shape), _const_spec(wpool.shape), _const_spec((1, POOL_WIDTH)),
            _const_spec(wout.shape), _const_spec((1, D_MODEL)), _const_spec((1, D_MODEL)),
        ],
        out_specs=blk(D_MODEL),
        out_shape=jax.ShapeDtypeStruct((BATCH * CTX_LEN, D_MODEL), F32),
        scratch_shapes=[
            pltpu.VMEM((u_rows, POOL_WIDTH), F32), pltpu.VMEM((tq, D_MODEL), BF16),
        ],
        compiler_params=pltpu.CompilerParams(
            dimension_semantics=("arbitrary", "arbitrary"),
            vmem_limit_bytes=_vmem_limit(pipelined, resident + scratch, 8 * tok)),
        name="attn_context",
    )(sink, q, kctx, vctx, u, h, mod, bands, wpool, pscale, wout, ln_g, ln_b)


def _row_perm(to_time_major):
    n = BATCH * PERM_STEPS
    r = lax.broadcasted_iota(jnp.int32, (n, n), 0)
    c = lax.broadcasted_iota(jnp.int32, (n, n), 1)
    if to_time_major:
        hit = ((r // BATCH) == (c % PERM_STEPS)) & ((r % BATCH) == (c // PERM_STEPS))
    else:
        hit = ((r // PERM_STEPS) == (c % BATCH)) & ((r % PERM_STEPS) == (c // BATCH))
    return jnp.where(hit, 1.0, 0.0).astype(BF16)


def _lru_time_major_input(x_ref, scale_ref, shift_ref, tt):
    xin = (x_ref[...] * (1.0 + scale_ref[...]) + shift_ref[...]).astype(BF16)
    perm = _row_perm(to_time_major=True)
    parts = []
    for tau in range(tt // PERM_STEPS):
        sub = xin[:, tau * PERM_STEPS:(tau + 1) * PERM_STEPS, :].reshape(BATCH * PERM_STEPS, D_MODEL)
        parts.append(jnp.dot(perm, sub, preferred_element_type=F32).astype(BF16))
    return jnp.concatenate(parts, axis=0)


def _lru_in_fwd_kernel(*refs, tt, with_gate):
    (x0_ref, xn_ref, scale_ref, shift_ref, w_ref, h0_ref, cw_ref, cb_ref, wrg_ref, ba_ref, bx_ref,
     lam_ref) = refs[:12]
    outs = refs[12:]
    if with_gate:
        gate_ref, uconv_ref, s_ref, hfin_ref, ring, ulast, h_scr, a_scr, b_scr, xt_scr = outs
    else:
        uconv_ref, s_ref, hfin_ref, ring, ulast, h_scr, a_scr, b_scr, xt_scr = outs
    i = pl.program_id(0)
    ntiles = pl.num_programs(0) - LRU_SCAN_LAG

    @pl.when(i == 0)
    def _():
        ring[...] = jnp.zeros(ring.shape, F32)
        ulast[...] = jnp.zeros(ulast.shape, F32)
        h_scr[...] = h0_ref[...]
        xt_scr[0] = _lru_time_major_input(x0_ref, scale_ref, shift_ref, tt)

    xt = xt_scr[lax.rem(i, 2)]
    slot_new = lax.rem(i, LRU_SCAN_LAG + 1)
    slot_cur = lax.rem(i + 1, LRU_SCAN_LAG + 1)
    slot_nxt = lax.rem(i + 2, LRU_SCAN_LAG + 1)

    def project_chunk(c):
        cols = slice(c * LRU_PROJ_CHUNK, (c + 1) * LRU_PROJ_CHUNK)
        val = jnp.dot(xt, w_ref[:, cols], preferred_element_type=F32)
        val = val.reshape(tt, BATCH, LRU_PROJ_CHUNK)
        if with_gate and c < LRU_WIDTH // LRU_PROJ_CHUNK:
            gate_ref[:, :, cols] = val.astype(BF16)
        else:
            ucols = slice(cols.start % LRU_WIDTH, cols.start % LRU_WIDTH + LRU_PROJ_CHUNK)
            ring[slot_new, :, :, ucols] = val

    nchunks = w_ref.shape[1] // LRU_PROJ_CHUNK
    ahead = nchunks - (LRU_BLOCKS - 2)
    for c in range(ahead):
        project_chunk(c)

    cur = ring[slot_cur]
    nxt = jnp.where(i == ntiles + LRU_SCAN_LAG - 1, 0.0, ring[slot_nxt, 0:CONV_W - 1 - CONV_LEFT])
    u = _lru_conv(cur, ulast[...], nxt, cw_ref, cb_ref, tt)
    uconv_ref[...] = u
    _lru_coeffs(u, wrg_ref, ba_ref, bx_ref, lam_ref, a_scr, b_scr, tt,
                interleave=[functools.partial(project_chunk, c) for c in range(max(ahead, 0), nchunks)])
    ulast[...] = cur[tt - CONV_LEFT:tt]
    xt_scr[lax.rem(i + 1, 2)] = _lru_time_major_input(xn_ref, scale_ref, shift_ref, tt)

    h_prev = h_scr[...]
    h = h_prev
    for k in range(tt):
        h = a_scr[k] * h + b_scr[k]
        s_ref[k] = h
    h = jnp.where(i >= LRU_SCAN_LAG, h, h_prev)
    h_scr[...] = h
    hfin_ref[...] = h


def _lru_in_fwd(x3, scale, shift, w, h0, conv_w, conv_b, wrg, ba, bx, lam, *, with_gate, tt=32):
    length = x3.shape[1]
    ntiles = length // tt
    blk = _nbytes((tt, BATCH, LRU_WIDTH), F32)
    proj_tile = lambda i: jnp.minimum(i, ntiles - 1)
    scan_tile = lambda i: jnp.maximum(i - LRU_SCAN_LAG, 0)
    tm_spec = lambda tile: pl.BlockSpec((tt, BATCH, LRU_WIDTH), lambda i: (tile(i), 0, 0))
    out_specs = [tm_spec(scan_tile), tm_spec(scan_tile),
                 pl.BlockSpec((BATCH, LRU_WIDTH), lambda i: (0, 0))]
    out_shape = [jax.ShapeDtypeStruct((length, BATCH, LRU_WIDTH), F32)] * 2 + [
        jax.ShapeDtypeStruct((BATCH, LRU_WIDTH), F32)]
    outs = [blk, blk]
    if with_gate:
        out_specs = [tm_spec(proj_tile)] + out_specs
        out_shape = [jax.ShapeDtypeStruct((length, BATCH, LRU_WIDTH), BF16)] + out_shape
        outs.append(blk // 2)
    res = pl.pallas_call(
        functools.partial(_lru_in_fwd_kernel, tt=tt, with_gate=with_gate),
        grid=(ntiles + LRU_SCAN_LAG,),
        in_specs=[_const_spec((BATCH, tt, D_MODEL)),
                  pl.BlockSpec((BATCH, tt, D_MODEL), lambda i: (0, proj_tile(i + 1), 0)),
                  _const_spec(scale.shape), _const_spec(shift.shape), _const_spec(w.shape),
                  _const_spec((BATCH, LRU_WIDTH)),
                  _const_spec((CONV_W, LRU_WIDTH)), _const_spec((1, LRU_WIDTH)),
                  _const_spec(wrg.shape),
                  _const_spec((1, LRU_WIDTH)), _const_spec((1, LRU_WIDTH)), _const_spec((1, LRU_WIDTH))],
        out_specs=out_specs,
        out_shape=out_shape,
        scratch_shapes=[
            pltpu.VMEM((LRU_SCAN_LAG + 1, tt, BATCH, LRU_WIDTH), F32),
            pltpu.VMEM((CONV_LEFT, BATCH, LRU_WIDTH), F32),
            pltpu.VMEM((BATCH, LRU_WIDTH), F32),
            pltpu.VMEM((tt, BATCH, LRU_WIDTH), F32), pltpu.VMEM((tt, BATCH, LRU_WIDTH), F32),
            pltpu.VMEM((2, tt * BATCH, D_MODEL), BF16),
        ],
        compiler_params=pltpu.CompilerParams(
            dimension_semantics=("arbitrary",),
            vmem_limit_bytes=_vmem_limit(
                [blk] + outs, [_nbytes(w.shape, BF16), _nbytes(wrg.shape, BF16),
                               (LRU_SCAN_LAG + 5) * blk], 8 * blk)),
        name="lru_in_fwd",
    )(x3, x3, scale, shift, w, h0, conv_w, conv_b, wrg, ba, bx, lam)
    return res if with_gate else (None,) + tuple(res)


def _lru_conv(cur, prev, nxt, cw_ref, cb_ref, tt):
    ext = jnp.concatenate([prev, cur, nxt], axis=0)
    u = cb_ref[...].reshape(1, 1, LRU_WIDTH)
    for tap in range(CONV_W):
        u = u + ext[tap:tap + tt] * cw_ref[tap:tap + 1, :].reshape(1, 1, LRU_WIDTH)
    return u


def _lru_coeffs(u, wrg_ref, ba_ref, bx_ref, lam_ref, a_scr, b_scr, tt, interleave=()):
    u2 = u.reshape(tt * BATCH, LRU_WIDTH)
    ub16 = u2.astype(BF16)
    col = lambda blk: slice(blk * LRU_BLOCK_W, (blk + 1) * LRU_BLOCK_W)
    lam = lam_ref[...]
    half_decay2 = (-0.5 * LRU_C * LOG2E) * (jnp.maximum(-lam, 0.0) + jnp.log1p(jnp.exp(-jnp.abs(lam))))
    half_ba = 0.5 * ba_ref[...]
    half_bx = 0.5 * bx_ref[...]
    rgs = []
    for blk in range(LRU_BLOCKS + 1):
        if blk < LRU_BLOCKS:
            rgs.append(jnp.dot(ub16[:, col(blk)], wrg_ref[blk], preferred_element_type=F32))
            if blk < len(interleave):
                interleave[blk]()
        if blk == 0:
            continue
        blk -= 1
        rg = rgs[blk]
        cols = col(blk)
        tr = jnp.tanh(rg[:, :LRU_BLOCK_W] + half_ba[:, cols])
        tg = jnp.tanh(rg[:, LRU_BLOCK_W:] + half_bx[:, cols])
        hd = half_decay2[:, cols]
        a = jnp.exp2(hd + hd * tr)
        x = 1.0 - a * a
        hu = 0.5 * u2[:, cols]
        b = jnp.where(x > 0.0, x * lax.rsqrt(x), 0.0) * (hu + hu * tg)
        a_scr[:, :, cols] = a.reshape(tt, BATCH, LRU_BLOCK_W)
        b_scr[:, :, cols] = b.reshape(tt, BATCH, LRU_BLOCK_W)


def _lru_bwd_state_kernel(u_ref, wrg_ref, ba_ref, bx_ref, lam_ref, hfin_ref, a_scr, b_scr, *, tt):
    @pl.when(pl.program_id(0) == 0)
    def _():
        hfin_ref[...] = jnp.zeros(hfin_ref.shape, F32)

    _lru_coeffs(u_ref[...], wrg_ref, ba_ref, bx_ref, lam_ref, a_scr, b_scr, tt)
    step = lambda k, h: a_scr[tt - 1 - k] * h + b_scr[tt - 1 - k]
    hfin_ref[...] = lax.fori_loop(0, tt, step, hfin_ref[...], unroll=4)


def _lru_bwd_state(u_conv, wrg, ba, bx, lam, *, tt=32):
    nsteps = u_conv.shape[0] // tt
    blk = _nbytes((tt, BATCH, LRU_WIDTH), F32)
    return pl.pallas_call(
        functools.partial(_lru_bwd_state_kernel, tt=tt),
        grid=(nsteps,),
        in_specs=[
            pl.BlockSpec((tt, BATCH, LRU_WIDTH), lambda i: (nsteps - 1 - i, 0, 0)),
            _const_spec(wrg.shape),
            _const_spec((1, LRU_WIDTH)), _const_spec((1, LRU_WIDTH)), _const_spec((1, LRU_WIDTH)),
        ],
        out_specs=pl.BlockSpec((BATCH, LRU_WIDTH), lambda i: (0, 0)),
        out_shape=jax.ShapeDtypeStruct((BATCH, LRU_WIDTH), F32),
        scratch_shapes=[pltpu.VMEM((tt, BATCH, LRU_WIDTH), F32)] * 2,
        compiler_params=pltpu.CompilerParams(
            dimension_semantics=("arbitrary",),
            vmem_limit_bytes=_vmem_limit([blk], [2 * blk, _nbytes(wrg.shape, BF16)], 6 * blk)),
        name="lru_bwd_state",
    )(u_conv, wrg, ba, bx, lam)


def _lru_bwd_out_kernel(u_ref, h0_ref, wrg_ref, ba_ref, bx_ref, lam_ref, sf_ref, gate_ref, h_ref,
                        g2_ref, w_ref, lng_ref, lnb_ref, o_ref,
                        h_scr, a_scr, b_scr, sb_ring, out_scr, *, tt):
    i = pl.program_id(0)
    ntiles = pl.num_programs(0) - 1

    @pl.when(i == 0)
    def _():
        h_scr[...] = h0_ref[...]
        sb_ring[...] = jnp.zeros(sb_ring.shape, F32)

    y = sb_ring[lax.rem(i + 1, 2)] + sf_ref[...]
    z = (_gelu_tanh(gate_ref[...].astype(F32)) * y).astype(BF16)
    perm = _row_perm(to_time_major=False)
    parts = []
    for tau in range(tt // PERM_STEPS):
        sub = z[tau * PERM_STEPS:(tau + 1) * PERM_STEPS].reshape(PERM_STEPS * BATCH, LRU_WIDTH)
        zb = jnp.dot(perm, sub, preferred_element_type=F32).astype(BF16)
        parts.append(zb.reshape(BATCH, PERM_STEPS, LRU_WIDTH))
    zb = jnp.concatenate(parts, axis=1).reshape(BATCH * tt, LRU_WIDTH)

    def project_chunk(c):
        cols = slice(c * LRU_PROJ_CHUNK, (c + 1) * LRU_PROJ_CHUNK)
        val = jnp.dot(zb, w_ref[:, cols], preferred_element_type=F32)
        out_scr[:, :, cols] = val.reshape(BATCH, tt, LRU_PROJ_CHUNK)

    _lru_coeffs(u_ref[...], wrg_ref, ba_ref, bx_ref, lam_ref, a_scr, b_scr, tt,
                interleave=[functools.partial(project_chunk, c)
                            for c in range(D_MODEL // LRU_PROJ_CHUNK)])
    slot = lax.rem(i, 2)

    h = h_scr[...]
    for idx in reversed(range(tt)):
        h = a_scr[idx] * h + b_scr[idx]
        sb_ring[slot, idx] = h
    h_scr[...] = h
    res = DEEPNORM_ALPHA * h_ref[...] + g2_ref[...] * out_scr[...]
    o_ref[...] = _layer_norm(res, lng_ref[...].reshape(1, 1, D_MODEL), lnb_ref[...].reshape(1, 1, D_MODEL))


def _lru_bwd_out(u_conv, h0, wrg, ba, bx, lam, s_f, gate_t, h3, g2, w_out, ln_g, ln_b, *, tt=32):
    ntiles = SEQ // tt
    scan_tile = lambda i: jnp.maximum(ntiles - 1 - i, 0)
    out_tile = lambda i: jnp.minimum(ntiles - i, ntiles - 1)
    tm_blk = lambda: pl.BlockSpec((tt, BATCH, LRU_WIDTH), lambda i: (out_tile(i), 0, 0))
    bm_blk = lambda: pl.BlockSpec((BATCH, tt, D_MODEL), lambda i: (0, out_tile(i), 0))
    blk = _nbytes((tt, BATCH, LRU_WIDTH), F32)
    return pl.pallas_call(
        functools.partial(_lru_bwd_out_kernel, tt=tt),
        grid=(ntiles + 1,),
        in_specs=[
            pl.BlockSpec((tt, BATCH, LRU_WIDTH), lambda i: (scan_tile(i), 0, 0)),
            _const_spec((BATCH, LRU_WIDTH)), _const_spec(wrg.shape),
            _const_spec((1, LRU_WIDTH)), _const_spec((1, LRU_WIDTH)), _const_spec((1, LRU_WIDTH)),
            tm_blk(), tm_blk(), bm_blk(),
            _const_spec((BATCH, 1, D_MODEL)), _const_spec(w_out.shape),
            _const_spec((1, D_MODEL)), _const_spec((1, D_MODEL))],
        out_specs=bm_blk(),
        out_shape=jax.ShapeDtypeStruct((BATCH, SEQ, D_MODEL), F32),
        scratch_shapes=[
            pltpu.VMEM((BATCH, LRU_WIDTH), F32),
            pltpu.VMEM((tt, BATCH, LRU_WIDTH), F32), pltpu.VMEM((tt, BATCH, LRU_WIDTH), F32),
            pltpu.VMEM((2, tt, BATCH, LRU_WIDTH), F32), pltpu.VMEM((BATCH, tt, D_MODEL), F32),
        ],
        compiler_params=pltpu.CompilerParams(
            dimension_semantics=("arbitrary",),
            vmem_limit_bytes=_vmem_limit(
                [blk] * 5, [5 * blk, _nbytes(w_out.shape, BF16), _nbytes(wrg.shape, BF16)], 8 * blk)),
        name="lru_bwd_out",
    )(u_conv, h0, wrg, ba, bx, lam, s_f, gate_t, h3, g2, w_out, ln_g, ln_b)


def _rope_tables():
    rows = SEQ // GRID_W
    row = jnp.repeat(jnp.arange(rows, dtype=F32), GRID_W)
    col = jnp.tile(jnp.arange(GRID_W, dtype=F32), rows)
    inv = ROPE_THETA ** (-jnp.arange(ROPE_FREQS, dtype=F32) / ROPE_FREQS)
    ang = jnp.concatenate([row[:, None] * inv, col[:, None] * inv], axis=-1)
    cos, sin = jnp.cos(ang), jnp.sin(ang)
    cos_t = jnp.tile(cos, (1, V7X_LANES // (HEAD_DIM // 2)))
    sin_t = jnp.tile(jnp.concatenate([-sin, sin], axis=-1), (1, V7X_LANES // HEAD_DIM))
    return cos_t, sin_t


def kernel(x, c, ctx, c_ctx, w_mod, b_mod, ln_g, ln_b, ffn_w_gate, ffn_w_up, ffn_w_down, mix_ab_w_in, attn_sink, pool_w, pool_scale, mix_ab_w_out, lru_w_in, lru_conv_w, lru_conv_b, lru_wa, lru_ba, lru_wx, lru_bx, lru_lambda, lru_w_out):
    assert x.shape == (BATCH, SEQ, D_MODEL) and ctx.shape == (BATCH, CTX_LEN, D_MODEL)
    c_all = jnp.concatenate(
        [c, c_ctx[None, :], jnp.zeros((MOD_ROWS - BATCH - 1, D_MODEL), F32)], axis=0)
    mod_all = _modulation(c_all, w_mod, b_mod)
    mod_lat = mod_all[:, :BATCH].reshape(DEPTH, BATCH, N_MOD, D_MODEL)
    mod_ctx = mod_all[:, BATCH:BATCH + 1].reshape(DEPTH, 1, N_MOD, D_MODEL)

    wg = ffn_w_gate.astype(BF16)
    wu = ffn_w_up.astype(BF16)
    wd = ffn_w_down.astype(BF16)
    row = lambda v: v.reshape(1, -1)
    n_ctx = BATCH * CTX_LEN

    h = x.reshape(BATCH * SEQ, D_MODEL)
    hc = ctx.reshape(n_ctx, D_MODEL)

    l = 0
    ml, mc = mod_lat[l], mod_ctx[l]
    ffn1 = ((l, 0), wg, wu, wd, row(ln_g[l, 0]), row(ln_b[l, 0]))
    ffn2 = ((l, 1), wg, wu, wd, row(ln_g[l, 2]), row(ln_b[l, 2]))
    h = _ffn(h, ml, 0, *ffn1, nb=BATCH, length=SEQ)
    hc = _ffn(hc, mc, 0, *ffn1, nb=1, length=n_ctx)
    w_in = mix_ab_w_in[0].astype(BF16)
    cos_t, sin_t = _rope_tables()
    q, k2, v2, u = _attn_inproj(h, ml, w_in, cos_t, sin_t, nb=BATCH, length=SEQ)
    q_c, k2_c, v2_c, u_c = _attn_inproj(hc, mc, w_in, None, None, nb=1, length=n_ctx)
    mix_args = (pool_w[0].astype(BF16), row(pool_scale[0]), mix_ab_w_out[0].astype(BF16),
                row(ln_g[l, 1]), row(ln_b[l, 1]))
    h = _attn_lat(attn_sink[0], q, k2, v2, k2_c, v2_c, u, h, ml, *mix_args)
    hc = _attn_ctx(attn_sink[0], q_c, k2_c, v2_c, u_c, hc, mc, *mix_args)
    h = _ffn(h, ml, 6, *ffn2, nb=BATCH, length=SEQ)
    hc = _ffn(hc, mc, 6, *ffn2, nb=1, length=n_ctx)

    l = 1
    ml, mc = mod_lat[l], mod_ctx[l]
    ffn1 = ((l, 0), wg, wu, wd, row(ln_g[l, 0]), row(ln_b[l, 0]))
    ffn2 = ((l, 1), wg, wu, wd, row(ln_g[l, 2]), row(ln_b[l, 2]))
    h = _ffn(h, ml, 0, *ffn1, nb=BATCH, length=SEQ)
    hc = _ffn(hc, mc, 0, *ffn1, nb=1, length=n_ctx)
    w_in = lru_w_in[0].astype(BF16)
    h3 = h.reshape(BATCH, SEQ, D_MODEL)
    wrg = (0.5 * jnp.concatenate([lru_wa[0], lru_wx[0]], axis=-1)).astype(BF16)
    conv_w, conv_b = lru_conv_w[0], row(lru_conv_b[0])
    zeros = jnp.zeros((BATCH, LRU_WIDTH), F32)
    dir_args = [(wrg[d], row(lru_ba[0, d]), row(lru_bx[0, d]), row(lru_lambda[0, d]))
                for d in range(2)]
    _, uc_t, _, hf_ctx = _lru_in_fwd(hc.reshape(BATCH, CTX_LEN, D_MODEL), mc[:, 4:5, :], mc[:, 3:4, :],
                                     w_in[:, LRU_WIDTH:], zeros, conv_w, conv_b, *dir_args[0],
                                     with_gate=False)
    gate_t, u_t, s_f, _ = _lru_in_fwd(h3, ml[:, 4:5, :], ml[:, 3:4, :], w_in, hf_ctx, conv_w, conv_b,
                                      *dir_args[0], with_gate=True)
    hb_ctx = _lru_bwd_state(uc_t, *dir_args[1])
    h3 = _lru_bwd_out(u_t, hb_ctx, *dir_args[1], s_f, gate_t, h3, ml[:, 5:6, :],
                      lru_w_out[0].astype(BF16), row(ln_g[l, 1]), row(ln_b[l, 1]))
    h = _ffn(h3.reshape(BATCH * SEQ, D_MODEL), ml, 6, *ffn2, nb=BATCH, length=SEQ)
    return h.reshape(BATCH, SEQ, D_MODEL)
```

```python
import functools

import jax
import jax.numpy as jnp
from jax import lax
from jax.experimental import pallas as pl
from jax.experimental.pallas import tpu as pltpu

D_MODEL = 1024
BATCH = 16
SEQ = 2048
DEPTH = 2
GRID_W = 64
CTX_LEN = 256
HEAD_DIM = 64
ATT_HEADS = 8
ATT_KV_HEADS = 2
ATT_GROUPS = ATT_HEADS // ATT_KV_HEADS
ATT_WIDTH = ATT_HEADS * HEAD_DIM
KV_WIDTH = ATT_KV_HEADS * HEAD_DIM
WINDOW = 128
BLOCK = 128
ATT_SCALE = HEAD_DIM ** -0.5
LOG2E = 1.4426950408889634
ROPE_THETA = 10000.0
ROPE_FREQS = HEAD_DIM // 4
POOL_WINDOWS = (2, 4, 8, 16)
POOL_WIDTH = D_MODEL // 2
POOL_GROUP_W = POOL_WIDTH // len(POOL_WINDOWS)
MIX_AB_IN = ATT_WIDTH + 2 * KV_WIDTH + POOL_WIDTH
LRU_WIDTH = D_MODEL
LRU_BLOCKS = 8
LRU_BLOCK_W = LRU_WIDTH // LRU_BLOCKS
LRU_C = 8.0
CONV_W = 4
CONV_LEFT = (CONV_W - 1) // 2
D_FF = 2816
N_MOD = 9
LN_EPS = 1e-5
NEG_INF = -1e30
DEEPNORM_ALPHA = (2 * DEPTH) ** 0.25

V7X_LANES = 128
V7X_SUBLANES = 8
V7X_VMEM_BYTES = 64 * 1024 * 1024
V7X_VMEM_USABLE_BYTES = 60000 * 1024

F32 = jnp.float32
BF16 = jnp.bfloat16

POOL_HALO = V7X_SUBLANES
PERM_STEPS = 16
KEY_SLAB_W = ATT_KV_HEADS * V7X_LANES
VAL_SLAB_W = 2 * ATT_KV_HEADS * V7X_LANES
LRU_PROJ_CHUNK = 256
LRU_SCAN_LAG = 2
SCORE_LOOKAHEAD = 2
EPILOGUE_BLOCKS = 2
MOD_ROWS = 24


def _nbytes(shape, dtype):
    n = 1
    for s in shape:
        n *= s
    return n * jnp.dtype(dtype).itemsize


def _vmem_limit(pipelined, resident, temporaries):
    est = 2 * sum(pipelined) + sum(resident) + temporaries
    return int(min(V7X_VMEM_USABLE_BYTES, max(est * 5 // 4, 16 * 1024 * 1024)))


def _const_spec(shape):
    nd = len(shape)
    return pl.BlockSpec(shape, lambda *_: (0,) * nd, pipeline_mode=pl.Buffered(1))


def _tok_spec(tm, width, nt):
    return pl.BlockSpec((tm, width), lambda b, t: (b * nt + t, 0))


def _mod_spec():
    return pl.BlockSpec((1, N_MOD, D_MODEL), lambda b, t: (b, 0, 0))


def _layer_norm(z, g, b):
    mu = jnp.mean(z, axis=-1, keepdims=True)
    zc = z - mu
    var = jnp.mean(zc * zc, axis=-1, keepdims=True)
    return zc * lax.rsqrt(var + LN_EPS) * g + b


def _gelu_tanh(x):
    c = 0.7978845608028654
    half = 0.5 * x
    return half + half * jnp.tanh(x * (c + (c * 0.044715) * (x * x)))


def _mod_kernel(c_ref, w_ref, b_ref, o_ref):
    c = c_ref[...]
    a_hi, a_lo = _split_bf16(c * jax.nn.sigmoid(c))
    w_hi, w_lo = _split_bf16(w_ref[0])
    acc = jnp.dot(a_hi, w_lo, preferred_element_type=F32) + jnp.dot(a_lo, w_hi, preferred_element_type=F32)
    o_ref[0] = acc + jnp.dot(a_hi, w_hi, preferred_element_type=F32) + b_ref[0]


def _modulation(c_all, w_mod, b_mod):
    tn = 2304
    n_out = N_MOD * D_MODEL
    blocks = [_nbytes((1, D_MODEL, tn), F32), _nbytes((1, MOD_ROWS, tn), F32)]
    return pl.pallas_call(
        _mod_kernel,
        grid=(DEPTH, n_out // tn),
        in_specs=[
            pl.BlockSpec((MOD_ROWS, D_MODEL), lambda l, j: (0, 0)),
            pl.BlockSpec((1, D_MODEL, tn), lambda l, j: (l, 0, j)),
            pl.BlockSpec((1, 1, tn), lambda l, j: (l, 0, j)),
        ],
        out_specs=pl.BlockSpec((1, MOD_ROWS, tn), lambda l, j: (l, 0, j)),
        out_shape=jax.ShapeDtypeStruct((DEPTH, MOD_ROWS, n_out), F32),
        compiler_params=pltpu.CompilerParams(
            dimension_semantics=("arbitrary", "arbitrary"),
            vmem_limit_bytes=_vmem_limit(blocks, [], 4 * blocks[0])),
        name="modulation",
    )(c_all, w_mod, b_mod.reshape(DEPTH, 1, n_out))


def _ffn_kernel(x_ref, mod_ref, wg_ref, wu_ref, wd_ref, lng_ref, lnb_ref, o_ref, *, j0, sub):
    shift = mod_ref[0, j0:j0 + 1, :]
    scale = mod_ref[0, j0 + 1:j0 + 2, :]
    gate = mod_ref[0, j0 + 2:j0 + 3, :]
    sizes = [sub] * (x_ref.shape[0] // sub)
    starts = [s * sub for s in range(len(sizes))]
    nsub = len(sizes)

    def gate_up(s):
        x = x_ref[starts[s]:starts[s] + sizes[s], :]
        xin = (x * (1.0 + scale) + shift).astype(BF16)
        return (jnp.dot(xin, wg_ref[...], preferred_element_type=F32),
                jnp.dot(xin, wu_ref[...], preferred_element_type=F32))

    pending = gate_up(0)
    for s in range(nsub):
        g, u = pending
        if s + 1 < nsub:
            pending = gate_up(s + 1)
        rows = slice(starts[s], starts[s] + sizes[s])
        a = (g * jax.nn.sigmoid(g) * u).astype(BF16)
        y = jnp.dot(a, wd_ref[...], preferred_element_type=F32)
        z = DEEPNORM_ALPHA * x_ref[rows, :] + (0.5 * gate) * y
        o_ref[rows, :] = _layer_norm(z, lng_ref[...], lnb_ref[...])


def _ffn(x, mod, j0, which, wg, wu, wd, ln_g, ln_b, *, nb, length, tm=1024, sub=256):
    nt = length // tm
    tok = _nbytes((tm, D_MODEL), F32)
    weights = [_nbytes(w.shape[2:], BF16) for w in (wg, wu, wd)]
    temps = 3 * _nbytes((tm, D_FF), F32) + 4 * tok
    wspec = lambda w: pl.BlockSpec((None, None) + w.shape[2:], lambda b, t: which + (0, 0),
                                   pipeline_mode=pl.Buffered(1))
    return pl.pallas_call(
        functools.partial(_ffn_kernel, j0=j0, sub=sub),
        grid=(nb, nt),
        in_specs=[
            _tok_spec(tm, D_MODEL, nt),
            _mod_spec(),
            wspec(wg), wspec(wu), wspec(wd),
            _const_spec((1, D_MODEL)), _const_spec((1, D_MODEL)),
        ],
        out_specs=_tok_spec(tm, D_MODEL, nt),
        out_shape=jax.ShapeDtypeStruct((nb * length, D_MODEL), F32),
        compiler_params=pltpu.CompilerParams(
            dimension_semantics=("arbitrary", "arbitrary"),
            vmem_limit_bytes=_vmem_limit([tok, tok], weights, temps)),
        name="ffn",
    )(x, mod, wg, wu, wd, ln_g, ln_b)


def _ffn_dual_kernel(xl_ref, xc_ref, ml_ref, mc_ref, wg_ref, wu_ref, wd_ref, lng_ref, lnb_ref,
                     ol_ref, oc_ref, *, j0, sub, n_lat):
    i = pl.program_id(0)

    @pl.when(i < n_lat)
    def _():
        _ffn_kernel(xl_ref, ml_ref, wg_ref, wu_ref, wd_ref, lng_ref, lnb_ref, ol_ref, j0=j0, sub=sub)

    @pl.when(i >= n_lat)
    def _():
        _ffn_kernel(xc_ref, mc_ref, wg_ref, wu_ref, wd_ref, lng_ref, lnb_ref, oc_ref, j0=j0, sub=sub)


def _ffn_dual(x_lat, x_ctx, mod_lat, mod_ctx, j0, which, wg, wu, wd, ln_g, ln_b, *,
              tm=1024, tm_ctx=512, sub=256):
    n_lat = x_lat.shape[0] // tm
    n_ctx = x_ctx.shape[0] // tm_ctx
    per_sample = SEQ // tm
    lat_tile = lambda i: jnp.minimum(i, n_lat - 1)
    ctx_tile = lambda i: jnp.maximum(i - n_lat, 0)
    lat_spec = lambda: pl.BlockSpec((tm, D_MODEL), lambda i: (lat_tile(i), 0))
    ctx_spec = lambda: pl.BlockSpec((tm_ctx, D_MODEL), lambda i: (ctx_tile(i), 0))
    wspec = lambda w: pl.BlockSpec((None, None) + w.shape[2:], lambda i: which + (0, 0),
                                   pipeline_mode=pl.Buffered(1))
    tok = _nbytes((tm, D_MODEL), F32)
    tok_c = _nbytes((tm_ctx, D_MODEL), F32)
    weights = [_nbytes(w.shape[2:], BF16) for w in (wg, wu, wd)]
    temps = 3 * _nbytes((tm, D_FF), F32) + 4 * tok
    return pl.pallas_call(
        functools.partial(_ffn_dual_kernel, j0=j0, sub=sub, n_lat=n_lat),
        grid=(n_lat + n_ctx,),
        in_specs=[
            lat_spec(), ctx_spec(),
            pl.BlockSpec((1, N_MOD, D_MODEL), lambda i: (lat_tile(i) // per_sample, 0, 0)),
            pl.BlockSpec((1, N_MOD, D_MODEL), lambda i: (0, 0, 0)),
            wspec(wg), wspec(wu), wspec(wd),
            _const_spec((1, D_MODEL)), _const_spec((1, D_MODEL)),
        ],
        out_specs=[lat_spec(), ctx_spec()],
        out_shape=[jax.ShapeDtypeStruct(x_lat.shape, F32), jax.ShapeDtypeStruct(x_ctx.shape, F32)],
        compiler_params=pltpu.CompilerParams(
            dimension_semantics=("arbitrary",),
            vmem_limit_bytes=_vmem_limit([tok, tok, tok_c, tok_c], weights, temps)),
        name="ffn_dual",
    )(x_lat, x_ctx, mod_lat, mod_ctx, wg, wu, wd, ln_g, ln_b)


def _dup_halves(z, lane):
    zr = pltpu.roll(z, HEAD_DIM, 1)
    lo = lane < HEAD_DIM
    return jnp.where(lo, z, zr), jnp.where(lo, zr, z)


def _attn_inproj_kernel(*refs, rope):
    if rope:
        x_ref, mod_ref, w_ref, cos_ref, sin_ref, q_ref, k_ref, v_ref, u_ref = refs
    else:
        x_ref, mod_ref, w_ref, q_ref, k_ref, v_ref, u_ref = refs
    x = x_ref[...]
    xin = (x * (1.0 + mod_ref[0, 4:5, :]) + mod_ref[0, 3:4, :]).astype(BF16)
    p = jnp.dot(xin, w_ref[...], preferred_element_type=F32)
    lane = lax.broadcasted_iota(jnp.int32, (x.shape[0], V7X_LANES), 1)
    first_half = (lane & (HEAD_DIM - 1)) < HEAD_DIM // 2

    def rot(z):
        if not rope:
            return z
        zr = jnp.where(first_half, pltpu.roll(z, V7X_LANES - HEAD_DIM // 2, 1),
                       pltpu.roll(z, HEAD_DIM // 2, 1))
        return z * cos_ref[...] + zr * sin_ref[...]

    for c in range(ATT_WIDTH // V7X_LANES):
        sl = slice(c * V7X_LANES, (c + 1) * V7X_LANES)
        q_ref[:, sl] = (rot(p[:, sl]) * (ATT_SCALE * LOG2E)).astype(BF16)
    k0, k1 = _dup_halves(rot(p[:, ATT_WIDTH:ATT_WIDTH + KV_WIDTH]), lane)
    k_ref[...] = jnp.concatenate([k0, k1], axis=-1).T.astype(BF16)
    v = p[:, ATT_WIDTH + KV_WIDTH:ATT_WIDTH + 2 * KV_WIDTH]
    vr = pltpu.roll(v, HEAD_DIM, 1)
    lo = lane < HEAD_DIM
    slabs = (jnp.where(lo, v, 1.0), jnp.where(lo, 1.0, vr), jnp.where(lo, vr, 1.0), jnp.where(lo, 1.0, v))
    for i, slab in enumerate(slabs):
        v_ref[:, i * V7X_LANES:(i + 1) * V7X_LANES] = slab.astype(BF16)
    u_ref[...] = p[:, ATT_WIDTH + 2 * KV_WIDTH:]


def _attn_inproj(x, mod, w_in, cos, sin, *, nb, length, tm=1024):
    nt = length // tm
    rope = cos is not None
    tok = _nbytes((tm, D_MODEL), F32)
    outs = [_nbytes((tm, ATT_WIDTH), BF16), _nbytes((tm, KEY_SLAB_W + VAL_SLAB_W), BF16),
            _nbytes((tm, POOL_WIDTH), F32)]
    in_specs = [_tok_spec(tm, D_MODEL, nt), _mod_spec(), _const_spec(w_in.shape)]
    args = [x, mod, w_in]
    if rope:
        in_specs += [pl.BlockSpec((tm, V7X_LANES), lambda b, t: (t, 0))] * 2
        args += [cos, sin]
    rows = nb * length
    return pl.pallas_call(
        functools.partial(_attn_inproj_kernel, rope=rope),
        grid=(nb, nt),
        in_specs=in_specs,
        out_specs=[
            _tok_spec(tm, ATT_WIDTH, nt),
            pl.BlockSpec((KEY_SLAB_W, tm), lambda b, t: (0, b * nt + t)),
            _tok_spec(tm, VAL_SLAB_W, nt),
            _tok_spec(tm, POOL_WIDTH, nt),
        ],
        out_shape=[
            jax.ShapeDtypeStruct((rows, ATT_WIDTH), BF16),
            jax.ShapeDtypeStruct((KEY_SLAB_W, rows), BF16),
            jax.ShapeDtypeStruct((rows, VAL_SLAB_W), BF16),
            jax.ShapeDtypeStruct((rows, POOL_WIDTH), F32),
        ],
        compiler_params=pltpu.CompilerParams(
            dimension_semantics=("arbitrary", "arbitrary"),
            vmem_limit_bytes=_vmem_limit([tok] + outs, [_nbytes(w_in.shape, BF16)],
                                         3 * _nbytes((tm, MIX_AB_IN), F32))),
        name="attn_inproj",
    )(*args)


def _attend_scores(q_blk, keys, kh):
    lane = lax.broadcasted_iota(jnp.int32, (BLOCK, V7X_LANES), 1)
    lo = lane < HEAD_DIM
    zero = jnp.zeros((BLOCK, V7X_LANES), BF16)
    parts = []
    for g in range(ATT_GROUPS):
        c = (kh * ATT_GROUPS + g) // 2
        qc = q_blk[:, c * V7X_LANES:(c + 1) * V7X_LANES]
        parts.append(jnp.where(lo if g % 2 == 0 else jnp.logical_not(lo), qc, zero))
    qs = jnp.concatenate(parts, axis=0)
    return [jnp.dot(qs, k, preferred_element_type=F32) for k in keys]


def _attend_finish(scores, values_even, values_odd, segments, sink_ref, kh):
    lane = lax.broadcasted_iota(jnp.int32, (BLOCK, V7X_LANES), 1)
    lo = lane < HEAD_DIM
    probs = ([[] for _ in scores], [[] for _ in scores])
    sink_terms = []
    for g in range(ATT_GROUPS):
        rows = slice(g * BLOCK, (g + 1) * BLOCK)
        segs = [[s[rows, a:b] if bias is None else s[rows, a:b] + bias for a, b, bias in seg]
                for s, seg in zip(scores, segments)]
        sink = sink_ref[kh * ATT_GROUPS + g] * LOG2E
        chunks = [x[:, c:c + V7X_LANES] for sl in segs for x in sl
                  for c in range(0, x.shape[1], V7X_LANES)]
        m = functools.reduce(jnp.maximum, chunks).max(axis=-1, keepdims=True)
        m = jnp.maximum(m, sink)
        sink_terms.append(jnp.exp2(sink - m))
        for i, sl in enumerate(segs):
            e = [jnp.exp2(x - m).astype(BF16) for x in sl]
            probs[g % 2][i].append(e[0] if len(e) == 1 else jnp.concatenate(e, axis=-1))

    def weighted_values(ps, vals):
        o = None
        for p, v in zip(ps, vals):
            t = jnp.dot(jnp.concatenate(p, axis=0), v, preferred_element_type=F32)
            o = t if o is None else o + t
        return o

    o_par = (weighted_values(probs[0], values_even), weighted_values(probs[1], values_odd))
    outs = []
    for g in range(ATT_GROUPS):
        o = o_par[g % 2][(g // 2) * BLOCK:(g // 2 + 1) * BLOCK]
        rinv = 1.0 / (o + sink_terms[g])
        outs.append(o * pltpu.roll(rinv, HEAD_DIM, 1))
    return jnp.where(lo, outs[0], outs[1]), jnp.where(lo, outs[2], outs[3])


def _pool_sums(ubuf_ref, nblk, band_ref):
    u_hi, u_lo = _split_bf16(ubuf_ref[...])
    sums = {}
    for j in range(nblk):
        win = slice(j * BLOCK, (j + 2) * BLOCK)
        for gi in range(len(POOL_WINDOWS)):
            cols = slice(gi * POOL_GROUP_W, (gi + 1) * POOL_GROUP_W)
            band = band_ref[gi]
            sums[j, gi] = (jnp.dot(band, u_hi[win, cols], preferred_element_type=F32)
                           + jnp.dot(band, u_lo[win, cols], preferred_element_type=F32))
    return sums


def _pool_finish(sums, ubuf_ref, blocks, pos0, length, wpool_ref, pscale_ref, mix_scr):
    for j in blocks:
        pos = pos0 + j * BLOCK + lax.broadcasted_iota(jnp.int32, (BLOCK, 1), 0)
        rows = slice(POOL_HALO + j * BLOCK, POOL_HALO + (j + 1) * BLOCK)
        for gi, w in enumerate(POOL_WINDOWS):
            r = w // 2
            cols = slice(gi * POOL_GROUP_W, (gi + 1) * POOL_GROUP_W)
            cnt = (jnp.minimum(pos + r, length - 1) - jnp.maximum(pos - r, 0) + 1).astype(F32)
            d = sums[j, gi] / cnt - ubuf_ref[rows, cols]
            y = jnp.dot(d.astype(BF16), wpool_ref[gi], preferred_element_type=F32)
            mix_scr[j * BLOCK:(j + 1) * BLOCK, ATT_WIDTH + gi * POOL_GROUP_W:
                    ATT_WIDTH + (gi + 1) * POOL_GROUP_W] = (y * pscale_ref[:, cols]).astype(BF16)


def _split_bf16(x):
    hi = x.astype(BF16)
    return hi, (x - hi.astype(F32)).astype(BF16)


def _pool_bands():
    i = jnp.arange(BLOCK)[:, None]
    c = jnp.arange(2 * BLOCK)[None, :]
    return jnp.stack([(jnp.abs(c - POOL_HALO - i) <= w // 2) for w in POOL_WINDOWS]).astype(BF16)


def _attn_out(mix_scr, rows, wout_ref, h_ref, gate, lng_ref, lnb_ref, o_ref):
    y = jnp.dot(mix_scr[rows, :], wout_ref[...], preferred_element_type=F32)
    z = DEEPNORM_ALPHA * h_ref[rows, :] + gate * y
    o_ref[rows, :] = _layer_norm(z, lng_ref[...], lnb_ref[...])


def _attn_lat_kernel(sink_ref, q_ref, kc_ref, kp_ref, kn_ref, vc_ref, vp_ref, vn_ref,
                     kctx_ref, vctx_ref, uc_ref, up_ref, un_ref, h_ref, mod_ref, bias_ref, band_ref,
                     wpool_ref, pscale_ref, wout_ref, lng_ref, lnb_ref, o_ref,
                     kbuf, vbuf, ubuf, mix_scr, *, tq, length):
    t = pl.program_id(1)
    last = pl.num_programs(1) - 1
    nblk = tq // BLOCK
    kbuf[:, 0:BLOCK] = kp_ref[...]
    kbuf[:, BLOCK:BLOCK + tq] = kc_ref[...]
    kbuf[:, BLOCK + tq:] = kn_ref[...]
    vbuf[0:BLOCK] = vp_ref[...]
    vbuf[BLOCK:BLOCK + tq] = vc_ref[...]
    vbuf[BLOCK + tq:] = vn_ref[...]
    ubuf[0:POOL_HALO] = jnp.where(t == 0, 0.0, up_ref[...])
    ubuf[POOL_HALO:POOL_HALO + tq] = uc_ref[...]
    ubuf[POOL_HALO + tq:2 * POOL_HALO + tq] = jnp.where(t == last, 0.0, un_ref[...])
    ubuf[2 * POOL_HALO + tq:] = jnp.zeros((2 * BLOCK - 2 * POOL_HALO, POOL_WIDTH), F32)
    lanes = lambda i: slice(i * V7X_LANES, (i + 1) * V7X_LANES)
    gate = mod_ref[0, 5:6, :]
    units = [(j, kh) for j in range(nblk) for kh in range(ATT_KV_HEADS)]

    def scores_of(j, kh):
        win = slice(j * BLOCK, j * BLOCK + 3 * BLOCK)
        return _attend_scores(q_ref[j * BLOCK:(j + 1) * BLOCK, :],
                              [kbuf[lanes(kh), win], kctx_ref[lanes(kh), :]], kh)

    sums = _pool_sums(ubuf, nblk, band_ref)
    pending = [scores_of(*u) for u in units[:SCORE_LOOKAHEAD]]
    _pool_finish(sums, ubuf, range(nblk), t * tq, length, wpool_ref, pscale_ref, mix_scr)
    for n, (j, kh) in enumerate(units):
        scores = pending.pop(0)
        if n + SCORE_LOOKAHEAD < len(units):
            pending.append(scores_of(*units[n + SCORE_LOOKAHEAD]))
        blk = t * nblk + j
        bidx = jnp.where(blk == 0, 0, jnp.where(blk == length // BLOCK - 1, 2, 1))
        bias = bias_ref[bidx]
        segments = [[(0, BLOCK, bias[:, :BLOCK]), (BLOCK, 2 * BLOCK, None),
                     (2 * BLOCK, 3 * BLOCK, bias[:, BLOCK:])], [(0, CTX_LEN, None)]]
        win = slice(j * BLOCK, j * BLOCK + 3 * BLOCK)
        rows = slice(j * BLOCK, (j + 1) * BLOCK)
        c0, c1 = _attend_finish(
            scores, [vbuf[win, lanes(2 * kh)], vctx_ref[:, lanes(2 * kh)]],
            [vbuf[win, lanes(2 * kh + 1)], vctx_ref[:, lanes(2 * kh + 1)]],
            segments, sink_ref, kh)
        mix_scr[rows, lanes(2 * kh)] = c0.astype(BF16)
        mix_scr[rows, lanes(2 * kh + 1)] = c1.astype(BF16)
        per_epilogue = EPILOGUE_BLOCKS * ATT_KV_HEADS
        if n > 0 and n % per_epilogue == 0:
            e = n // per_epilogue - 1
            done = slice(e * EPILOGUE_BLOCKS * BLOCK, (e + 1) * EPILOGUE_BLOCKS * BLOCK)
            _attn_out(mix_scr, done, wout_ref, h_ref, gate, lng_ref, lnb_ref, o_ref)
    _attn_out(mix_scr, slice(tq - EPILOGUE_BLOCKS * BLOCK, tq), wout_ref, h_ref, gate, lng_ref,
              lnb_ref, o_ref)


def _attn_ctx_kernel(sink_ref, q_ref, kctx_ref, vctx_ref, uc_ref, h_ref, mod_ref, band_ref,
                     wpool_ref, pscale_ref, wout_ref, lng_ref, lnb_ref, o_ref,
                     ubuf, mix_scr, *, tq):
    ubuf[0:POOL_HALO] = jnp.zeros((POOL_HALO, POOL_WIDTH), F32)
    ubuf[POOL_HALO:POOL_HALO + tq] = uc_ref[...]
    ubuf[POOL_HALO + tq:] = jnp.zeros((2 * BLOCK - POOL_HALO, POOL_WIDTH), F32)
    lanes = lambda i: slice(i * V7X_LANES, (i + 1) * V7X_LANES)
    nblk = tq // BLOCK
    units = [(j, kh) for j in range(nblk) for kh in range(ATT_KV_HEADS)]
    scores_of = lambda j, kh: _attend_scores(q_ref[j * BLOCK:(j + 1) * BLOCK, :],
                                             [kctx_ref[lanes(kh), :]], kh)
    sums = _pool_sums(ubuf, nblk, band_ref)
    pending = [scores_of(*u) for u in units[:SCORE_LOOKAHEAD]]
    _pool_finish(sums, ubuf, range(nblk), 0, tq, wpool_ref, pscale_ref, mix_scr)
    for n, (j, kh) in enumerate(units):
        scores = pending.pop(0)
        if n + SCORE_LOOKAHEAD < len(units):
            pending.append(scores_of(*units[n + SCORE_LOOKAHEAD]))
        rows = slice(j * BLOCK, (j + 1) * BLOCK)
        c0, c1 = _attend_finish(scores, [vctx_ref[:, lanes(2 * kh)]],
                                [vctx_ref[:, lanes(2 * kh + 1)]], [[(0, CTX_LEN, None)]],
                                sink_ref, kh)
        mix_scr[rows, lanes(2 * kh)] = c0.astype(BF16)
        mix_scr[rows, lanes(2 * kh + 1)] = c1.astype(BF16)
    _attn_out(mix_scr, slice(0, tq), wout_ref, h_ref, mod_ref[0, 5:6, :], lng_ref, lnb_ref, o_ref)


def _window_bias():
    qi = jnp.arange(BLOCK)[:, None]
    kc = jnp.arange(3 * BLOCK)[None, :]
    band = jnp.abs(qi + BLOCK - kc) <= WINDOW
    first = band & (kc >= BLOCK)
    final = band & (kc < 2 * BLOCK)
    full = jnp.where(jnp.stack([first, band, final]), 0.0, NEG_INF).astype(F32)
    return jnp.concatenate([full[:, :, :BLOCK], full[:, :, 2 * BLOCK:]], axis=-1)


def _smem_spec():
    return pl.BlockSpec(memory_space=pltpu.SMEM)


def _attn_lat(sink, q, k2, v2, kctx, vctx, u, h, mod, wpool, pscale, wout, ln_g, ln_b, *, tq=512):
    nt = SEQ // tq
    bpt = tq // BLOCK
    bps = SEQ // BLOCK
    hpt = tq // POOL_HALO
    hps = SEQ // POOL_HALO
    cur = lambda w: pl.BlockSpec((tq, w), lambda b, t: (b * nt + t, 0))
    prev_blk = lambda rows, w, per_tile, per_samp: pl.BlockSpec(
        (rows, w), lambda b, t: (b * per_samp + jnp.maximum(t * per_tile - 1, 0), 0))
    next_blk = lambda rows, w, per_tile, per_samp: pl.BlockSpec(
        (rows, w), lambda b, t: (b * per_samp + jnp.minimum((t + 1) * per_tile, per_samp - 1), 0))
    ctx_blk = lambda w: pl.BlockSpec((CTX_LEN, w), lambda b, t: (b, 0))
    kv_specs = lambda w: [cur(w), prev_blk(BLOCK, w, bpt, bps), next_blk(BLOCK, w, bpt, bps)]
    key_specs = [
        pl.BlockSpec((KEY_SLAB_W, tq), lambda b, t: (0, b * nt + t)),
        pl.BlockSpec((KEY_SLAB_W, BLOCK), lambda b, t: (0, b * bps + jnp.maximum(t * bpt - 1, 0))),
        pl.BlockSpec((KEY_SLAB_W, BLOCK),
                     lambda b, t: (0, b * bps + jnp.minimum((t + 1) * bpt, bps - 1))),
    ]
    bias = _window_bias()
    bands = _pool_bands()
    tok = _nbytes((tq, D_MODEL), F32)
    kv_rows = tq + 2 * BLOCK
    u_rows = tq + 2 * BLOCK
    pipelined = [_nbytes((tq, ATT_WIDTH), BF16), 2 * _nbytes((kv_rows, KEY_SLAB_W + VAL_SLAB_W), BF16),
                 _nbytes((tq + 2 * POOL_HALO, POOL_WIDTH), F32), 2 * tok]
    resident = [_nbytes(bias.shape, F32), _nbytes(bands.shape, BF16), _nbytes(wpool.shape, BF16),
                _nbytes(wout.shape, BF16)]
    scratch = [_nbytes((kv_rows, KEY_SLAB_W + VAL_SLAB_W), BF16),
               _nbytes((u_rows, POOL_WIDTH), F32), _nbytes((tq, D_MODEL), BF16)]
    return pl.pallas_call(
        functools.partial(_attn_lat_kernel, tq=tq, length=SEQ),
        grid=(BATCH, nt),
        in_specs=[_smem_spec(), cur(ATT_WIDTH)] + key_specs + kv_specs(VAL_SLAB_W) + [
            pl.BlockSpec((KEY_SLAB_W, CTX_LEN), lambda b, t: (0, b)), ctx_blk(VAL_SLAB_W),
            cur(POOL_WIDTH), prev_blk(POOL_HALO, POOL_WIDTH, hpt, hps),
            next_blk(POOL_HALO, POOL_WIDTH, hpt, hps),
            cur(D_MODEL), _mod_spec(),
            _const_spec(bias.shape), _const_spec(bands.shape),
            _const_spec(wpool.shape), _const_spec((1, POOL_WIDTH)),
            _const_spec(wout.shape), _const_spec((1, D_MODEL)), _const_spec((1, D_MODEL)),
        ],
        out_specs=cur(D_MODEL),
        out_shape=jax.ShapeDtypeStruct((BATCH * SEQ, D_MODEL), F32),
        scratch_shapes=[
            pltpu.VMEM((KEY_SLAB_W, kv_rows), BF16), pltpu.VMEM((kv_rows, VAL_SLAB_W), BF16),
            pltpu.VMEM((u_rows, POOL_WIDTH), F32), pltpu.VMEM((tq, D_MODEL), BF16),
        ],
        compiler_params=pltpu.CompilerParams(
            dimension_semantics=("arbitrary", "arbitrary"),
            vmem_limit_bytes=_vmem_limit(pipelined, resident + scratch, 8 * tok)),
        name="attn_latent",
    )(sink, q, k2, k2, k2, v2, v2, v2, kctx, vctx, u, u, u, h, mod, bias, bands, wpool, pscale, wout,
      ln_g, ln_b)


def _attn_ctx(sink, q, kctx, vctx, u, h, mod, wpool, pscale, wout, ln_g, ln_b):
    tq = CTX_LEN
    blk = lambda w: pl.BlockSpec((tq, w), lambda b, t: (b, 0))
    bands = _pool_bands()
    tok = _nbytes((tq, D_MODEL), F32)
    u_rows = tq + 2 * BLOCK
    pipelined = [_nbytes((tq, ATT_WIDTH), BF16), _nbytes((tq, KEY_SLAB_W + VAL_SLAB_W), BF16),
                 _nbytes((tq, POOL_WIDTH), F32), 2 * tok]
    resident = [_nbytes(bands.shape, BF16), _nbytes(wpool.shape, BF16), _nbytes(wout.shape, BF16)]
    scratch = [_nbytes((u_rows, POOL_WIDTH), F32), _nbytes((tq, D_MODEL), BF16)]
    return pl.pallas_call(
        functools.partial(_attn_ctx_kernel, tq=tq),
        grid=(BATCH, 1),
        in_specs=[
            _smem_spec(),
            blk(ATT_WIDTH), pl.BlockSpec((KEY_SLAB_W, tq), lambda b, t: (0, b)),
            blk(VAL_SLAB_W), blk(POOL_WIDTH), blk(D_MODEL),
            pl.BlockSpec((1, N_MOD, D_MODEL), lambda b, t: (0, 0, 0)),
            _const_spec(bands.shape), _const_spec(wpool.shape), _const_spec((1, POOL_WIDTH)),
            _const_spec(wout.shape), _const_spec((1, D_MODEL)), _const_spec((1, D_MODEL)),
        ],
        out_specs=blk(D_MODEL),
        out_shape=jax.ShapeDtypeStruct((BATCH * CTX_LEN, D_MODEL), F32),
        scratch_shapes=[
            pltpu.VMEM((u_rows, POOL_WIDTH), F32), pltpu.VMEM((tq, D_MODEL), BF16),
        ],
        compiler_params=pltpu.CompilerParams(
            dimension_semantics=("arbitrary", "arbitrary"),
            vmem_limit_bytes=_vmem_limit(pipelined, resident + scratch, 8 * tok)),
        name="attn_context",
    )(sink, q, kctx, vctx, u, h, mod, bands, wpool, pscale, wout, ln_g, ln_b)


def _row_perm(to_time_major):
    n = BATCH * PERM_STEPS
    r = lax.broadcasted_iota(jnp.int32, (n, n), 0)
    c = lax.broadcasted_iota(jnp.int32, (n, n), 1)
    if to_time_major:
        hit = ((r // BATCH) == (c % PERM_STEPS)) & ((r % BATCH) == (c // PERM_STEPS))
    else:
        hit = ((r // PERM_STEPS) == (c % BATCH)) & ((r % PERM_STEPS) == (c // BATCH))
    return jnp.where(hit, 1.0, 0.0).astype(BF16)


def _lru_time_major_input(x_ref, scale_ref, shift_ref, tt):
    xin = (x_ref[...] * (1.0 + scale_ref[...]) + shift_ref[...]).astype(BF16)
    perm = _row_perm(to_time_major=True)
    parts = []
    for tau in range(tt // PERM_STEPS):
        sub = xin[:, tau * PERM_STEPS:(tau + 1) * PERM_STEPS, :].reshape(BATCH * PERM_STEPS, D_MODEL)
        parts.append(jnp.dot(perm, sub, preferred_element_type=F32).astype(BF16))
    return jnp.concatenate(parts, axis=0)


def _lru_in_fwd_kernel(*refs, tt, with_gate):
    (x_ref, scale_ref, shift_ref, w_ref, h0_ref, cw_ref, cb_ref, wrg_ref, ba_ref, bx_ref,
     lam_ref) = refs[:11]
    outs = refs[11:]
    if with_gate:
        gate_ref, uconv_ref, s_ref, hfin_ref, ring, ulast, h_scr, a_scr, b_scr = outs
    else:
        uconv_ref, s_ref, hfin_ref, ring, ulast, h_scr, a_scr, b_scr = outs
    i = pl.program_id(0)
    ntiles = pl.num_programs(0) - LRU_SCAN_LAG

    @pl.when(i == 0)
    def _():
        ring[...] = jnp.zeros(ring.shape, F32)
        ulast[...] = jnp.zeros(ulast.shape, F32)
        h_scr[...] = h0_ref[...]

    xt = _lru_time_major_input(x_ref, scale_ref, shift_ref, tt)
    slot_new = lax.rem(i, LRU_SCAN_LAG + 1)
    slot_cur = lax.rem(i + 1, LRU_SCAN_LAG + 1)
    slot_nxt = lax.rem(i + 2, LRU_SCAN_LAG + 1)

    def project_chunk(c):
        cols = slice(c * LRU_PROJ_CHUNK, (c + 1) * LRU_PROJ_CHUNK)
        val = jnp.dot(xt, w_ref[:, cols], preferred_element_type=F32)
        val = val.reshape(tt, BATCH, LRU_PROJ_CHUNK)
        if with_gate and c < LRU_WIDTH // LRU_PROJ_CHUNK:
            gate_ref[:, :, cols] = val.astype(BF16)
        else:
            ucols = slice(cols.start % LRU_WIDTH, cols.start % LRU_WIDTH + LRU_PROJ_CHUNK)
            ring[slot_new, :, :, ucols] = val

    nchunks = w_ref.shape[1] // LRU_PROJ_CHUNK
    ahead = nchunks - (LRU_BLOCKS - 2)
    for c in range(ahead):
        project_chunk(c)

    cur = ring[slot_cur]
    nxt = jnp.where(i == ntiles + LRU_SCAN_LAG - 1, 0.0, ring[slot_nxt, 0:CONV_W - 1 - CONV_LEFT])
    u = _lru_conv(cur, ulast[...], nxt, cw_ref, cb_ref, tt)
    uconv_ref[...] = u
    _lru_coeffs(u, wrg_ref, ba_ref, bx_ref, lam_ref, a_scr, b_scr, tt,
                interleave=[functools.partial(project_chunk, c) for c in range(max(ahead, 0), nchunks)])
    ulast[...] = cur[tt - CONV_LEFT:tt]

    h_prev = h_scr[...]
    h = h_prev
    for k in range(tt):
        h = a_scr[k] * h + b_scr[k]
        s_ref[k] = h
    h = jnp.where(i >= LRU_SCAN_LAG, h, h_prev)
    h_scr[...] = h
    hfin_ref[...] = h


def _lru_in_fwd(x3, scale, shift, w, h0, conv_w, conv_b, wrg, ba, bx, lam, *, with_gate, tt=32):
    length = x3.shape[1]
    ntiles = length // tt
    blk = _nbytes((tt, BATCH, LRU_WIDTH), F32)
    proj_tile = lambda i: jnp.minimum(i, ntiles - 1)
    scan_tile = lambda i: jnp.maximum(i - LRU_SCAN_LAG, 0)
    tm_spec = lambda tile: pl.BlockSpec((tt, BATCH, LRU_WIDTH), lambda i: (tile(i), 0, 0))
    out_specs = [tm_spec(scan_tile), tm_spec(scan_tile),
                 pl.BlockSpec((BATCH, LRU_WIDTH), lambda i: (0, 0))]
    out_shape = [jax.ShapeDtypeStruct((length, BATCH, LRU_WIDTH), F32)] * 2 + [
        jax.ShapeDtypeStruct((BATCH, LRU_WIDTH), F32)]
    outs = [blk, blk]
    if with_gate:
        out_specs = [tm_spec(proj_tile)] + out_specs
        out_shape = [jax.ShapeDtypeStruct((length, BATCH, LRU_WIDTH), BF16)] + out_shape
        outs.append(blk // 2)
    res = pl.pallas_call(
        functools.partial(_lru_in_fwd_kernel, tt=tt, with_gate=with_gate),
        grid=(ntiles + LRU_SCAN_LAG,),
        in_specs=[pl.BlockSpec((BATCH, tt, D_MODEL), lambda i: (0, proj_tile(i), 0)),
                  _const_spec(scale.shape), _const_spec(shift.shape), _const_spec(w.shape),
                  _const_spec((BATCH, LRU_WIDTH)),
                  _const_spec((CONV_W, LRU_WIDTH)), _const_spec((1, LRU_WIDTH)),
                  _const_spec(wrg.shape),
                  _const_spec((1, LRU_WIDTH)), _const_spec((1, LRU_WIDTH)), _const_spec((1, LRU_WIDTH))],
        out_specs=out_specs,
        out_shape=out_shape,
        scratch_shapes=[
            pltpu.VMEM((LRU_SCAN_LAG + 1, tt, BATCH, LRU_WIDTH), F32),
            pltpu.VMEM((CONV_LEFT, BATCH, LRU_WIDTH), F32),
            pltpu.VMEM((BATCH, LRU_WIDTH), F32),
            pltpu.VMEM((tt, BATCH, LRU_WIDTH), F32), pltpu.VMEM((tt, BATCH, LRU_WIDTH), F32),
        ],
        compiler_params=pltpu.CompilerParams(
            dimension_semantics=("arbitrary",),
            vmem_limit_bytes=_vmem_limit(
                [blk] + outs, [_nbytes(w.shape, BF16), _nbytes(wrg.shape, BF16),
                               (LRU_SCAN_LAG + 3) * blk], 8 * blk)),
        name="lru_in_fwd",
    )(x3, scale, shift, w, h0, conv_w, conv_b, wrg, ba, bx, lam)
    return res if with_gate else (None,) + tuple(res)


def _lru_conv(cur, prev, nxt, cw_ref, cb_ref, tt):
    ext = jnp.concatenate([prev, cur, nxt], axis=0)
    u = cb_ref[...].reshape(1, 1, LRU_WIDTH)
    for tap in range(CONV_W):
        u = u + ext[tap:tap + tt] * cw_ref[tap:tap + 1, :].reshape(1, 1, LRU_WIDTH)
    return u


def _lru_coeffs(u, wrg_ref, ba_ref, bx_ref, lam_ref, a_scr, b_scr, tt, interleave=()):
    u2 = u.reshape(tt * BATCH, LRU_WIDTH)
    ub16 = u2.astype(BF16)
    col = lambda blk: slice(blk * LRU_BLOCK_W, (blk + 1) * LRU_BLOCK_W)
    lam = lam_ref[...]
    half_decay2 = (-0.5 * LRU_C * LOG2E) * (jnp.maximum(-lam, 0.0) + jnp.log1p(jnp.exp(-jnp.abs(lam))))
    half_ba = 0.5 * ba_ref[...]
    half_bx = 0.5 * bx_ref[...]
    rgs = []
    for blk in range(LRU_BLOCKS + 1):
        if blk < LRU_BLOCKS:
            rgs.append(jnp.dot(ub16[:, col(blk)], wrg_ref[blk], preferred_element_type=F32))
            if blk < len(interleave):
                interleave[blk]()
        if blk == 0:
            continue
        blk -= 1
        rg = rgs[blk]
        cols = col(blk)
        tr = jnp.tanh(rg[:, :LRU_BLOCK_W] + half_ba[:, cols])
        tg = jnp.tanh(rg[:, LRU_BLOCK_W:] + half_bx[:, cols])
        hd = half_decay2[:, cols]
        a = jnp.exp2(hd + hd * tr)
        x = 1.0 - a * a
        hu = 0.5 * u2[:, cols]
        b = jnp.where(x > 0.0, x * lax.rsqrt(x), 0.0) * (hu + hu * tg)
        a_scr[:, :, cols] = a.reshape(tt, BATCH, LRU_BLOCK_W)
        b_scr[:, :, cols] = b.reshape(tt, BATCH, LRU_BLOCK_W)


def _lru_bwd_state_kernel(u_ref, wrg_ref, ba_ref, bx_ref, lam_ref, hfin_ref, a_scr, b_scr, *, tt):
    @pl.when(pl.program_id(0) == 0)
    def _():
        hfin_ref[...] = jnp.zeros(hfin_ref.shape, F32)

    _lru_coeffs(u_ref[...], wrg_ref, ba_ref, bx_ref, lam_ref, a_scr, b_scr, tt)
    step = lambda k, h: a_scr[tt - 1 - k] * h + b_scr[tt - 1 - k]
    hfin_ref[...] = lax.fori_loop(0, tt, step, hfin_ref[...], unroll=4)


def _lru_bwd_state(u_conv, wrg, ba, bx, lam, *, tt=32):
    nsteps = u_conv.shape[0] // tt
    blk = _nbytes((tt, BATCH, LRU_WIDTH), F32)
    return pl.pallas_call(
        functools.partial(_lru_bwd_state_kernel, tt=tt),
        grid=(nsteps,),
        in_specs=[
            pl.BlockSpec((tt, BATCH, LRU_WIDTH), lambda i: (nsteps - 1 - i, 0, 0)),
            _const_spec(wrg.shape),
            _const_spec((1, LRU_WIDTH)), _const_spec((1, LRU_WIDTH)), _const_spec((1, LRU_WIDTH)),
        ],
        out_specs=pl.BlockSpec((BATCH, LRU_WIDTH), lambda i: (0, 0)),
        out_shape=jax.ShapeDtypeStruct((BATCH, LRU_WIDTH), F32),
        scratch_shapes=[pltpu.VMEM((tt, BATCH, LRU_WIDTH), F32)] * 2,
        compiler_params=pltpu.CompilerParams(
            dimension_semantics=("arbitrary",),
            vmem_limit_bytes=_vmem_limit([blk], [2 * blk, _nbytes(wrg.shape, BF16)], 6 * blk)),
        name="lru_bwd_state",
    )(u_conv, wrg, ba, bx, lam)


def _lru_bwd_out_kernel(u_ref, h0_ref, wrg_ref, ba_ref, bx_ref, lam_ref, sf_ref, gate_ref, h_ref,
                        g2_ref, w_ref, lng_ref, lnb_ref, o_ref,
                        h_scr, a_scr, b_scr, sb_ring, out_scr, *, tt):
    i = pl.program_id(0)
    ntiles = pl.num_programs(0) - 1

    @pl.when(i == 0)
    def _():
        h_scr[...] = h0_ref[...]
        sb_ring[...] = jnp.zeros(sb_ring.shape, F32)

    y = sb_ring[lax.rem(i + 1, 2)] + sf_ref[...]
    z = (_gelu_tanh(gate_ref[...].astype(F32)) * y).astype(BF16)
    perm = _row_perm(to_time_major=False)
    parts = []
    for tau in range(tt // PERM_STEPS):
        sub = z[tau * PERM_STEPS:(tau + 1) * PERM_STEPS].reshape(PERM_STEPS * BATCH, LRU_WIDTH)
        zb = jnp.dot(perm, sub, preferred_element_type=F32).astype(BF16)
        parts.append(zb.reshape(BATCH, PERM_STEPS, LRU_WIDTH))
    zb = jnp.concatenate(parts, axis=1).reshape(BATCH * tt, LRU_WIDTH)

    def project_chunk(c):
        cols = slice(c * LRU_PROJ_CHUNK, (c + 1) * LRU_PROJ_CHUNK)
        val = jnp.dot(zb, w_ref[:, cols], preferred_element_type=F32)
        out_scr[:, :, cols] = val.reshape(BATCH, tt, LRU_PROJ_CHUNK)

    _lru_coeffs(u_ref[...], wrg_ref, ba_ref, bx_ref, lam_ref, a_scr, b_scr, tt,
                interleave=[functools.partial(project_chunk, c)
                            for c in range(D_MODEL // LRU_PROJ_CHUNK)])
    slot = lax.rem(i, 2)

    h = h_scr[...]
    for idx in reversed(range(tt)):
        h = a_scr[idx] * h + b_scr[idx]
        sb_ring[slot, idx] = h
    h_scr[...] = h
    res = DEEPNORM_ALPHA * h_ref[...] + g2_ref[...] * out_scr[...]
    o_ref[...] = _layer_norm(res, lng_ref[...].reshape(1, 1, D_MODEL), lnb_ref[...].reshape(1, 1, D_MODEL))


def _lru_bwd_out(u_conv, h0, wrg, ba, bx, lam, s_f, gate_t, h3, g2, w_out, ln_g, ln_b, *, tt=32):
    ntiles = SEQ // tt
    scan_tile = lambda i: jnp.maximum(ntiles - 1 - i, 0)
    out_tile = lambda i: jnp.minimum(ntiles - i, ntiles - 1)
    tm_blk = lambda: pl.BlockSpec((tt, BATCH, LRU_WIDTH), lambda i: (out_tile(i), 0, 0))
    bm_blk = lambda: pl.BlockSpec((BATCH, tt, D_MODEL), lambda i: (0, out_tile(i), 0))
    blk = _nbytes((tt, BATCH, LRU_WIDTH), F32)
    return pl.pallas_call(
        functools.partial(_lru_bwd_out_kernel, tt=tt),
        grid=(ntiles + 1,),
        in_specs=[
            pl.BlockSpec((tt, BATCH, LRU_WIDTH), lambda i: (scan_tile(i), 0, 0)),
            _const_spec((BATCH, LRU_WIDTH)), _const_spec(wrg.shape),
            _const_spec((1, LRU_WIDTH)), _const_spec((1, LRU_WIDTH)), _const_spec((1, LRU_WIDTH)),
            tm_blk(), tm_blk(), bm_blk(),
            _const_spec((BATCH, 1, D_MODEL)), _const_spec(w_out.shape),
            _const_spec((1, D_MODEL)), _const_spec((1, D_MODEL))],
        out_specs=bm_blk(),
        out_shape=jax.ShapeDtypeStruct((BATCH, SEQ, D_MODEL), F32),
        scratch_shapes=[
            pltpu.VMEM((BATCH, LRU_WIDTH), F32),
            pltpu.VMEM((tt, BATCH, LRU_WIDTH), F32), pltpu.VMEM((tt, BATCH, LRU_WIDTH), F32),
            pltpu.VMEM((2, tt, BATCH, LRU_WIDTH), F32), pltpu.VMEM((BATCH, tt, D_MODEL), F32),
        ],
        compiler_params=pltpu.CompilerParams(
            dimension_semantics=("arbitrary",),
            vmem_limit_bytes=_vmem_limit(
                [blk] * 5, [5 * blk, _nbytes(w_out.shape, BF16), _nbytes(wrg.shape, BF16)], 8 * blk)),
        name="lru_bwd_out",
    )(u_conv, h0, wrg, ba, bx, lam, s_f, gate_t, h3, g2, w_out, ln_g, ln_b)


def _rope_tables():
    rows = SEQ // GRID_W
    row = jnp.repeat(jnp.arange(rows, dtype=F32), GRID_W)
    col = jnp.tile(jnp.arange(GRID_W, dtype=F32), rows)
    inv = ROPE_THETA ** (-jnp.arange(ROPE_FREQS, dtype=F32) / ROPE_FREQS)
    ang = jnp.concatenate([row[:, None] * inv, col[:, None] * inv], axis=-1)
    cos, sin = jnp.cos(ang), jnp.sin(ang)
    cos_t = jnp.tile(cos, (1, V7X_LANES // (HEAD_DIM // 2)))
    sin_t = jnp.tile(jnp.concatenate([-sin, sin], axis=-1), (1, V7X_LANES // HEAD_DIM))
    return cos_t, sin_t


def kernel(x, c, ctx, c_ctx, w_mod, b_mod, ln_g, ln_b, ffn_w_gate, ffn_w_up, ffn_w_down, mix_ab_w_in, attn_sink, pool_w, pool_scale, mix_ab_w_out, lru_w_in, lru_conv_w, lru_conv_b, lru_wa, lru_ba, lru_wx, lru_bx, lru_lambda, lru_w_out):
    assert x.shape == (BATCH, SEQ, D_MODEL) and ctx.shape == (BATCH, CTX_LEN, D_MODEL)
    c_all = jnp.concatenate(
        [c, c_ctx[None, :], jnp.zeros((MOD_ROWS - BATCH - 1, D_MODEL), F32)], axis=0)
    mod_all = _modulation(c_all, w_mod, b_mod)
    mod_lat = mod_all[:, :BATCH].reshape(DEPTH, BATCH, N_MOD, D_MODEL)
    mod_ctx = mod_all[:, BATCH:BATCH + 1].reshape(DEPTH, 1, N_MOD, D_MODEL)

    wg = ffn_w_gate.astype(BF16)
    wu = ffn_w_up.astype(BF16)
    wd = ffn_w_down.astype(BF16)
    row = lambda v: v.reshape(1, -1)
    n_ctx = BATCH * CTX_LEN

    h = x.reshape(BATCH * SEQ, D_MODEL)
    hc = ctx.reshape(n_ctx, D_MODEL)

    l = 0
    ml, mc = mod_lat[l], mod_ctx[l]
    ffn1 = ((l, 0), wg, wu, wd, row(ln_g[l, 0]), row(ln_b[l, 0]))
    ffn2 = ((l, 1), wg, wu, wd, row(ln_g[l, 2]), row(ln_b[l, 2]))
    h, hc = _ffn_dual(h, hc, ml, mc, 0, *ffn1)
    w_in = mix_ab_w_in[0].astype(BF16)
    cos_t, sin_t = _rope_tables()
    q, k2, v2, u = _attn_inproj(h, ml, w_in, cos_t, sin_t, nb=BATCH, length=SEQ)
    q_c, k2_c, v2_c, u_c = _attn_inproj(hc, mc, w_in, None, None, nb=1, length=n_ctx)
    mix_args = (pool_w[0].astype(BF16), row(pool_scale[0]), mix_ab_w_out[0].astype(BF16),
                row(ln_g[l, 1]), row(ln_b[l, 1]))
    h = _attn_lat(attn_sink[0], q, k2, v2, k2_c, v2_c, u, h, ml, *mix_args)
    hc = _attn_ctx(attn_sink[0], q_c, k2_c, v2_c, u_c, hc, mc, *mix_args)
    h, hc = _ffn_dual(h, hc, ml, mc, 6, *ffn2)

    l = 1
    ml, mc = mod_lat[l], mod_ctx[l]
    ffn1 = ((l, 0), wg, wu, wd, row(ln_g[l, 0]), row(ln_b[l, 0]))
    ffn2 = ((l, 1), wg, wu, wd, row(ln_g[l, 2]), row(ln_b[l, 2]))
    h, hc = _ffn_dual(h, hc, ml, mc, 0, *ffn1)
    w_in = lru_w_in[0].astype(BF16)
    h3 = h.reshape(BATCH, SEQ, D_MODEL)
    wrg = (0.5 * jnp.concatenate([lru_wa[0], lru_wx[0]], axis=-1)).astype(BF16)
    conv_w, conv_b = lru_conv_w[0], row(lru_conv_b[0])
    zeros = jnp.zeros((BATCH, LRU_WIDTH), F32)
    dir_args = [(wrg[d], row(lru_ba[0, d]), row(lru_bx[0, d]), row(lru_lambda[0, d]))
                for d in range(2)]
    _, uc_t, _, hf_ctx = _lru_in_fwd(hc.reshape(BATCH, CTX_LEN, D_MODEL), mc[:, 4:5, :], mc[:, 3:4, :],
                                     w_in[:, LRU_WIDTH:], zeros, conv_w, conv_b, *dir_args[0],
                                     with_gate=False)
    gate_t, u_t, s_f, _ = _lru_in_fwd(h3, ml[:, 4:5, :], ml[:, 3:4, :], w_in, hf_ctx, conv_w, conv_b,
                                      *dir_args[0], with_gate=True)
    hb_ctx = _lru_bwd_state(uc_t, *dir_args[1])
    h3 = _lru_bwd_out(u_t, hb_ctx, *dir_args[1], s_f, gate_t, h3, ml[:, 5:6, :],
                      lru_w_out[0].astype(BF16), row(ln_g[l, 1]), row(ln_b[l, 1]))
    h = _ffn(h3.reshape(BATCH * SEQ, D_MODEL), ml, 6, *ffn2, nb=BATCH, length=SEQ)
    return h.reshape(BATCH, SEQ, D_MODEL)
```

```python
import functools

import jax
import jax.numpy as jnp
from jax import lax
from jax.experimental import pallas as pl
from jax.experimental.pallas import tpu as pltpu

D_MODEL = 1024
BATCH = 16
SEQ = 2048
DEPTH = 2
GRID_W = 64
CTX_LEN = 256
HEAD_DIM = 64
ATT_HEADS = 8
ATT_KV_HEADS = 2
ATT_GROUPS = ATT_HEADS // ATT_KV_HEADS
ATT_WIDTH = ATT_HEADS * HEAD_DIM
KV_WIDTH = ATT_KV_HEADS * HEAD_DIM
WINDOW = 128
BLOCK = 128
ATT_SCALE = HEAD_DIM ** -0.5
LOG2E = 1.4426950408889634
ROPE_THETA = 10000.0
ROPE_FREQS = HEAD_DIM // 4
POOL_WINDOWS = (2, 4, 8, 16)
POOL_WIDTH = D_MODEL // 2
POOL_GROUP_W = POOL_WIDTH // len(POOL_WINDOWS)
MIX_AB_IN = ATT_WIDTH + 2 * KV_WIDTH + POOL_WIDTH
LRU_WIDTH = D_MODEL
LRU_BLOCKS = 8
LRU_BLOCK_W = LRU_WIDTH // LRU_BLOCKS
LRU_C = 8.0
CONV_W = 4
CONV_LEFT = (CONV_W - 1) // 2
D_FF = 2816
N_MOD = 9
LN_EPS = 1e-5
NEG_INF = -1e30
DEEPNORM_ALPHA = (2 * DEPTH) ** 0.25

V7X_LANES = 128
V7X_SUBLANES = 8
V7X_VMEM_BYTES = 64 * 1024 * 1024
V7X_VMEM_USABLE_BYTES = 60000 * 1024

F32 = jnp.float32
BF16 = jnp.bfloat16

POOL_HALO = V7X_SUBLANES
PERM_STEPS = 16
KEY_SLAB_W = ATT_KV_HEADS * V7X_LANES
VAL_SLAB_W = 2 * ATT_KV_HEADS * V7X_LANES
LRU_PROJ_CHUNK = 256
LRU_SCAN_LAG = 2
SCORE_LOOKAHEAD = 2
EPILOGUE_BLOCKS = 2
MOD_ROWS = 24


def _nbytes(shape, dtype):
    n = 1
    for s in shape:
        n *= s
    return n * jnp.dtype(dtype).itemsize


def _vmem_limit(pipelined, resident, temporaries):
    est = 2 * sum(pipelined) + sum(resident) + temporaries
    return int(min(V7X_VMEM_USABLE_BYTES, max(est * 5 // 4, 16 * 1024 * 1024)))


def _const_spec(shape):
    nd = len(shape)
    return pl.BlockSpec(shape, lambda *_: (0,) * nd, pipeline_mode=pl.Buffered(1))


def _tok_spec(tm, width, nt):
    return pl.BlockSpec((tm, width), lambda b, t: (b * nt + t, 0))


def _mod_spec():
    return pl.BlockSpec((1, N_MOD, D_MODEL), lambda b, t: (b, 0, 0))


def _layer_norm(z, g, b):
    mu = jnp.mean(z, axis=-1, keepdims=True)
    zc = z - mu
    var = jnp.mean(zc * zc, axis=-1, keepdims=True)
    return zc * lax.rsqrt(var + LN_EPS) * g + b


def _gelu_tanh(x):
    c = 0.7978845608028654
    half = 0.5 * x
    return half + half * jnp.tanh(x * (c + (c * 0.044715) * (x * x)))


def _mod_kernel(c_ref, w_ref, b_ref, o_ref):
    c = c_ref[...]
    a_hi, a_lo = _split_bf16(c * jax.nn.sigmoid(c))
    w_hi, w_lo = _split_bf16(w_ref[0])
    acc = jnp.dot(a_hi, w_lo, preferred_element_type=F32) + jnp.dot(a_lo, w_hi, preferred_element_type=F32)
    o_ref[0] = acc + jnp.dot(a_hi, w_hi, preferred_element_type=F32) + b_ref[0]


def _modulation(c_all, w_mod, b_mod):
    tn = 2304
    n_out = N_MOD * D_MODEL
    blocks = [_nbytes((1, D_MODEL, tn), F32), _nbytes((1, MOD_ROWS, tn), F32)]
    return pl.pallas_call(
        _mod_kernel,
        grid=(DEPTH, n_out // tn),
        in_specs=[
            pl.BlockSpec((MOD_ROWS, D_MODEL), lambda l, j: (0, 0)),
            pl.BlockSpec((1, D_MODEL, tn), lambda l, j: (l, 0, j)),
            pl.BlockSpec((1, 1, tn), lambda l, j: (l, 0, j)),
        ],
        out_specs=pl.BlockSpec((1, MOD_ROWS, tn), lambda l, j: (l, 0, j)),
        out_shape=jax.ShapeDtypeStruct((DEPTH, MOD_ROWS, n_out), F32),
        compiler_params=pltpu.CompilerParams(
            dimension_semantics=("arbitrary", "arbitrary"),
            vmem_limit_bytes=_vmem_limit(blocks, [], 4 * blocks[0])),
        name="modulation",
    )(c_all, w_mod, b_mod.reshape(DEPTH, 1, n_out))


def _ffn_kernel(x_ref, mod_ref, wg_ref, wu_ref, wd_ref, lng_ref, lnb_ref, o_ref, *, j0, sub):
    shift = mod_ref[0, j0:j0 + 1, :]
    scale = mod_ref[0, j0 + 1:j0 + 2, :]
    gate = mod_ref[0, j0 + 2:j0 + 3, :]
    sizes = [sub] * (x_ref.shape[0] // sub)
    starts = [s * sub for s in range(len(sizes))]
    nsub = len(sizes)

    def gate_up(s):
        x = x_ref[starts[s]:starts[s] + sizes[s], :]
        xin = (x * (1.0 + scale) + shift).astype(BF16)
        return (jnp.dot(xin, wg_ref[...], preferred_element_type=F32),
                jnp.dot(xin, wu_ref[...], preferred_element_type=F32))

    pending = gate_up(0)
    for s in range(nsub):
        g, u = pending
        if s + 1 < nsub:
            pending = gate_up(s + 1)
        rows = slice(starts[s], starts[s] + sizes[s])
        a = (g * jax.nn.sigmoid(g) * u).astype(BF16)
        y = jnp.dot(a, wd_ref[...], preferred_element_type=F32)
        z = DEEPNORM_ALPHA * x_ref[rows, :] + (0.5 * gate) * y
        o_ref[rows, :] = _layer_norm(z, lng_ref[...], lnb_ref[...])


def _ffn(x, mod, j0, which, wg, wu, wd, ln_g, ln_b, *, nb, length, tm=1024, sub=256):
    nt = length // tm
    tok = _nbytes((tm, D_MODEL), F32)
    weights = [_nbytes(w.shape[2:], BF16) for w in (wg, wu, wd)]
    temps = 3 * _nbytes((tm, D_FF), F32) + 4 * tok
    wspec = lambda w: pl.BlockSpec((None, None) + w.shape[2:], lambda b, t: which + (0, 0),
                                   pipeline_mode=pl.Buffered(1))
    return pl.pallas_call(
        functools.partial(_ffn_kernel, j0=j0, sub=sub),
        grid=(nb, nt),
        in_specs=[
            _tok_spec(tm, D_MODEL, nt),
            _mod_spec(),
            wspec(wg), wspec(wu), wspec(wd),
            _const_spec((1, D_MODEL)), _const_spec((1, D_MODEL)),
        ],
        out_specs=_tok_spec(tm, D_MODEL, nt),
        out_shape=jax.ShapeDtypeStruct((nb * length, D_MODEL), F32),
        compiler_params=pltpu.CompilerParams(
            dimension_semantics=("arbitrary", "arbitrary"),
            vmem_limit_bytes=_vmem_limit([tok, tok], weights, temps)),
        name="ffn",
    )(x, mod, wg, wu, wd, ln_g, ln_b)


def _dup_halves(z, lane):
    zr = pltpu.roll(z, HEAD_DIM, 1)
    lo = lane < HEAD_DIM
    return jnp.where(lo, z, zr), jnp.where(lo, zr, z)


def _attn_inproj_kernel(*refs, rope):
    if rope:
        x_ref, mod_ref, w_ref, cos_ref, sin_ref, q_ref, k_ref, v_ref, u_ref = refs
    else:
        x_ref, mod_ref, w_ref, q_ref, k_ref, v_ref, u_ref = refs
    x = x_ref[...]
    xin = (x * (1.0 + mod_ref[0, 4:5, :]) + mod_ref[0, 3:4, :]).astype(BF16)
    p = jnp.dot(xin, w_ref[...], preferred_element_type=F32)
    lane = lax.broadcasted_iota(jnp.int32, (x.shape[0], V7X_LANES), 1)
    first_half = (lane & (HEAD_DIM - 1)) < HEAD_DIM // 2

    def rot(z):
        if not rope:
            return z
        zr = jnp.where(first_half, pltpu.roll(z, V7X_LANES - HEAD_DIM // 2, 1),
                       pltpu.roll(z, HEAD_DIM // 2, 1))
        return z * cos_ref[...] + zr * sin_ref[...]

    for c in range(ATT_WIDTH // V7X_LANES):
        sl = slice(c * V7X_LANES, (c + 1) * V7X_LANES)
        q_ref[:, sl] = (rot(p[:, sl]) * (ATT_SCALE * LOG2E)).astype(BF16)
    k0, k1 = _dup_halves(rot(p[:, ATT_WIDTH:ATT_WIDTH + KV_WIDTH]), lane)
    k_ref[...] = jnp.concatenate([k0, k1], axis=-1).T.astype(BF16)
    v = p[:, ATT_WIDTH + KV_WIDTH:ATT_WIDTH + 2 * KV_WIDTH]
    vr = pltpu.roll(v, HEAD_DIM, 1)
    lo = lane < HEAD_DIM
    slabs = (jnp.where(lo, v, 1.0), jnp.where(lo, 1.0, vr), jnp.where(lo, vr, 1.0), jnp.where(lo, 1.0, v))
    for i, slab in enumerate(slabs):
        v_ref[:, i * V7X_LANES:(i + 1) * V7X_LANES] = slab.astype(BF16)
    u_ref[...] = p[:, ATT_WIDTH + 2 * KV_WIDTH:]


def _attn_inproj(x, mod, w_in, cos, sin, *, nb, length, tm=2048):
    nt = length // tm
    rope = cos is not None
    tok = _nbytes((tm, D_MODEL), F32)
    outs = [_nbytes((tm, ATT_WIDTH), BF16), _nbytes((tm, KEY_SLAB_W + VAL_SLAB_W), BF16),
            _nbytes((tm, POOL_WIDTH), F32)]
    in_specs = [_tok_spec(tm, D_MODEL, nt), _mod_spec(), _const_spec(w_in.shape)]
    args = [x, mod, w_in]
    if rope:
        in_specs += [pl.BlockSpec((tm, V7X_LANES), lambda b, t: (t, 0))] * 2
        args += [cos, sin]
    rows = nb * length
    return pl.pallas_call(
        functools.partial(_attn_inproj_kernel, rope=rope),
        grid=(nb, nt),
        in_specs=in_specs,
        out_specs=[
            _tok_spec(tm, ATT_WIDTH, nt),
            pl.BlockSpec((KEY_SLAB_W, tm), lambda b, t: (0, b * nt + t)),
            _tok_spec(tm, VAL_SLAB_W, nt),
            _tok_spec(tm, POOL_WIDTH, nt),
        ],
        out_shape=[
            jax.ShapeDtypeStruct((rows, ATT_WIDTH), BF16),
            jax.ShapeDtypeStruct((KEY_SLAB_W, rows), BF16),
            jax.ShapeDtypeStruct((rows, VAL_SLAB_W), BF16),
            jax.ShapeDtypeStruct((rows, POOL_WIDTH), F32),
        ],
        compiler_params=pltpu.CompilerParams(
            dimension_semantics=("arbitrary", "arbitrary"),
            vmem_limit_bytes=_vmem_limit([tok] + outs, [_nbytes(w_in.shape, BF16)],
                                         3 * _nbytes((tm, MIX_AB_IN), F32))),
        name="attn_inproj",
    )(*args)


def _attend_scores(q_blk, keys, kh):
    lane = lax.broadcasted_iota(jnp.int32, (BLOCK, V7X_LANES), 1)
    lo = lane < HEAD_DIM
    zero = jnp.zeros((BLOCK, V7X_LANES), BF16)
    parts = []
    for g in range(ATT_GROUPS):
        c = (kh * ATT_GROUPS + g) // 2
        qc = q_blk[:, c * V7X_LANES:(c + 1) * V7X_LANES]
        parts.append(jnp.where(lo if g % 2 == 0 else jnp.logical_not(lo), qc, zero))
    qs = jnp.concatenate(parts, axis=0)
    return [jnp.dot(qs, k, preferred_element_type=F32) for k in keys]


def _attend_finish(scores, values_even, values_odd, segments, sink_ref, kh):
    lane = lax.broadcasted_iota(jnp.int32, (BLOCK, V7X_LANES), 1)
    lo = lane < HEAD_DIM
    probs = ([[] for _ in scores], [[] for _ in scores])
    sink_terms = []
    for g in range(ATT_GROUPS):
        rows = slice(g * BLOCK, (g + 1) * BLOCK)
        segs = [[s[rows, a:b] if bias is None else s[rows, a:b] + bias for a, b, bias in seg]
                for s, seg in zip(scores, segments)]
        sink = sink_ref[kh * ATT_GROUPS + g] * LOG2E
        chunks = [x[:, c:c + V7X_LANES] for sl in segs for x in sl
                  for c in range(0, x.shape[1], V7X_LANES)]
        m = functools.reduce(jnp.maximum, chunks).max(axis=-1, keepdims=True)
        m = jnp.maximum(m, sink)
        sink_terms.append(jnp.exp2(sink - m))
        for i, sl in enumerate(segs):
            e = [jnp.exp2(x - m).astype(BF16) for x in sl]
            probs[g % 2][i].append(e[0] if len(e) == 1 else jnp.concatenate(e, axis=-1))

    def weighted_values(ps, vals):
        o = None
        for p, v in zip(ps, vals):
            t = jnp.dot(jnp.concatenate(p, axis=0), v, preferred_element_type=F32)
            o = t if o is None else o + t
        return o

    o_par = (weighted_values(probs[0], values_even), weighted_values(probs[1], values_odd))
    outs = []
    for g in range(ATT_GROUPS):
        o = o_par[g % 2][(g // 2) * BLOCK:(g // 2 + 1) * BLOCK]
        rinv = 1.0 / (o + sink_terms[g])
        outs.append(o * pltpu.roll(rinv, HEAD_DIM, 1))
    return jnp.where(lo, outs[0], outs[1]), jnp.where(lo, outs[2], outs[3])


def _pool_sums(ubuf_ref, nblk, band_ref):
    u_hi, u_lo = _split_bf16(ubuf_ref[...])
    sums = {}
    for j in range(nblk):
        win = slice(j * BLOCK, (j + 2) * BLOCK)
        for gi in range(len(POOL_WINDOWS)):
            cols = slice(gi * POOL_GROUP_W, (gi + 1) * POOL_GROUP_W)
            band = band_ref[gi]
            sums[j, gi] = (jnp.dot(band, u_hi[win, cols], preferred_element_type=F32)
                           + jnp.dot(band, u_lo[win, cols], preferred_element_type=F32))
    return sums


def _pool_finish(sums, ubuf_ref, blocks, pos0, length, wpool_ref, pscale_ref, mix_scr):
    for j in blocks:
        pos = pos0 + j * BLOCK + lax.broadcasted_iota(jnp.int32, (BLOCK, 1), 0)
        rows = slice(POOL_HALO + j * BLOCK, POOL_HALO + (j + 1) * BLOCK)
        for gi, w in enumerate(POOL_WINDOWS):
            r = w // 2
            cols = slice(gi * POOL_GROUP_W, (gi + 1) * POOL_GROUP_W)
            cnt = (jnp.minimum(pos + r, length - 1) - jnp.maximum(pos - r, 0) + 1).astype(F32)
            d = sums[j, gi] / cnt - ubuf_ref[rows, cols]
            y = jnp.dot(d.astype(BF16), wpool_ref[gi], preferred_element_type=F32)
            mix_scr[j * BLOCK:(j + 1) * BLOCK, ATT_WIDTH + gi * POOL_GROUP_W:
                    ATT_WIDTH + (gi + 1) * POOL_GROUP_W] = (y * pscale_ref[:, cols]).astype(BF16)


def _split_bf16(x):
    hi = x.astype(BF16)
    return hi, (x - hi.astype(F32)).astype(BF16)


def _pool_bands():
    i = jnp.arange(BLOCK)[:, None]
    c = jnp.arange(2 * BLOCK)[None, :]
    return jnp.stack([(jnp.abs(c - POOL_HALO - i) <= w // 2) for w in POOL_WINDOWS]).astype(BF16)


def _attn_out(mix_scr, rows, wout_ref, h_ref, gate, lng_ref, lnb_ref, o_ref):
    y = jnp.dot(mix_scr[rows, :], wout_ref[...], preferred_element_type=F32)
    z = DEEPNORM_ALPHA * h_ref[rows, :] + gate * y
    o_ref[rows, :] = _layer_norm(z, lng_ref[...], lnb_ref[...])


def _attn_lat_kernel(sink_ref, q_ref, kc_ref, kp_ref, kn_ref, vc_ref, vp_ref, vn_ref,
                     kctx_ref, vctx_ref, uc_ref, up_ref, un_ref, h_ref, mod_ref, bias_ref, band_ref,
                     wpool_ref, pscale_ref, wout_ref, lng_ref, lnb_ref, o_ref,
                     kbuf, vbuf, ubuf, mix_scr, *, tq, length):
    t = pl.program_id(1)
    last = pl.num_programs(1) - 1
    nblk = tq // BLOCK
    kbuf[:, 0:BLOCK] = kp_ref[...]
    kbuf[:, BLOCK:BLOCK + tq] = kc_ref[...]
    kbuf[:, BLOCK + tq:] = kn_ref[...]
    vbuf[0:BLOCK] = vp_ref[...]
    vbuf[BLOCK:BLOCK + tq] = vc_ref[...]
    vbuf[BLOCK + tq:] = vn_ref[...]
    ubuf[0:POOL_HALO] = jnp.where(t == 0, 0.0, up_ref[...])
    ubuf[POOL_HALO:POOL_HALO + tq] = uc_ref[...]
    ubuf[POOL_HALO + tq:2 * POOL_HALO + tq] = jnp.where(t == last, 0.0, un_ref[...])
    ubuf[2 * POOL_HALO + tq:] = jnp.zeros((2 * BLOCK - 2 * POOL_HALO, POOL_WIDTH), F32)
    lanes = lambda i: slice(i * V7X_LANES, (i + 1) * V7X_LANES)
    gate = mod_ref[0, 5:6, :]
    units = [(j, kh) for j in range(nblk) for kh in range(ATT_KV_HEADS)]

    def scores_of(j, kh):
        win = slice(j * BLOCK, j * BLOCK + 3 * BLOCK)
        return _attend_scores(q_ref[j * BLOCK:(j + 1) * BLOCK, :],
                              [kbuf[lanes(kh), win], kctx_ref[lanes(kh), :]], kh)

    sums = _pool_sums(ubuf, nblk, band_ref)
    pending = [scores_of(*u) for u in units[:SCORE_LOOKAHEAD]]
    _pool_finish(sums, ubuf, range(nblk), t * tq, length, wpool_ref, pscale_ref, mix_scr)
    for n, (j, kh) in enumerate(units):
        scores = pending.pop(0)
        if n + SCORE_LOOKAHEAD < len(units):
            pending.append(scores_of(*units[n + SCORE_LOOKAHEAD]))
        blk = t * nblk + j
        bidx = jnp.where(blk == 0, 0, jnp.where(blk == length // BLOCK - 1, 2, 1))
        bias = bias_ref[bidx]
        segments = [[(0, BLOCK, bias[:, :BLOCK]), (BLOCK, 2 * BLOCK, None),
                     (2 * BLOCK, 3 * BLOCK, bias[:, BLOCK:])], [(0, CTX_LEN, None)]]
        win = slice(j * BLOCK, j * BLOCK + 3 * BLOCK)
        rows = slice(j * BLOCK, (j + 1) * BLOCK)
        c0, c1 = _attend_finish(
            scores, [vbuf[win, lanes(2 * kh)], vctx_ref[:, lanes(2 * kh)]],
            [vbuf[win, lanes(2 * kh + 1)], vctx_ref[:, lanes(2 * kh + 1)]],
            segments, sink_ref, kh)
        mix_scr[rows, lanes(2 * kh)] = c0.astype(BF16)
        mix_scr[rows, lanes(2 * kh + 1)] = c1.astype(BF16)
        per_epilogue = EPILOGUE_BLOCKS * ATT_KV_HEADS
        if n > 0 and n % per_epilogue == 0:
            e = n // per_epilogue - 1
            done = slice(e * EPILOGUE_BLOCKS * BLOCK, (e + 1) * EPILOGUE_BLOCKS * BLOCK)
            _attn_out(mix_scr, done, wout_ref, h_ref, gate, lng_ref, lnb_ref, o_ref)
    _attn_out(mix_scr, slice(tq - EPILOGUE_BLOCKS * BLOCK, tq), wout_ref, h_ref, gate, lng_ref,
              lnb_ref, o_ref)


def _attn_ctx_kernel(sink_ref, q_ref, kctx_ref, vctx_ref, uc_ref, h_ref, mod_ref, band_ref,
                     wpool_ref, pscale_ref, wout_ref, lng_ref, lnb_ref, o_ref,
                     ubuf, mix_scr, *, tq):
    ubuf[0:POOL_HALO] = jnp.zeros((POOL_HALO, POOL_WIDTH), F32)
    ubuf[POOL_HALO:POOL_HALO + tq] = uc_ref[...]
    ubuf[POOL_HALO + tq:] = jnp.zeros((2 * BLOCK - POOL_HALO, POOL_WIDTH), F32)
    lanes = lambda i: slice(i * V7X_LANES, (i + 1) * V7X_LANES)
    nblk = tq // BLOCK
    units = [(j, kh) for j in range(nblk) for kh in range(ATT_KV_HEADS)]
    scores_of = lambda j, kh: _attend_scores(q_ref[j * BLOCK:(j + 1) * BLOCK, :],
                                             [kctx_ref[lanes(kh), :]], kh)
    sums = _pool_sums(ubuf, nblk, band_ref)
    pending = [scores_of(*u) for u in units[:SCORE_LOOKAHEAD]]
    _pool_finish(sums, ubuf, range(nblk), 0, tq, wpool_ref, pscale_ref, mix_scr)
    for n, (j, kh) in enumerate(units):
        scores = pending.pop(0)
        if n + SCORE_LOOKAHEAD < len(units):
            pending.append(scores_of(*units[n + SCORE_LOOKAHEAD]))
        rows = slice(j * BLOCK, (j + 1) * BLOCK)
        c0, c1 = _attend_finish(scores, [vctx_ref[:, lanes(2 * kh)]],
                                [vctx_ref[:, lanes(2 * kh + 1)]], [[(0, CTX_LEN, None)]],
                                sink_ref, kh)
        mix_scr[rows, lanes(2 * kh)] = c0.astype(BF16)
        mix_scr[rows, lanes(2 * kh + 1)] = c1.astype(BF16)
    _attn_out(mix_scr, slice(0, tq), wout_ref, h_ref, mod_ref[0, 5:6, :], lng_ref, lnb_ref, o_ref)


def _window_bias():
    qi = jnp.arange(BLOCK)[:, None]
    kc = jnp.arange(3 * BLOCK)[None, :]
    band = jnp.abs(qi + BLOCK - kc) <= WINDOW
    first = band & (kc >= BLOCK)
    final = band & (kc < 2 * BLOCK)
    full = jnp.where(jnp.stack([first, band, final]), 0.0, NEG_INF).astype(F32)
    return jnp.concatenate([full[:, :, :BLOCK], full[:, :, 2 * BLOCK:]], axis=-1)


def _smem_spec():
    return pl.BlockSpec(memory_space=pltpu.SMEM)


def _attn_lat(sink, q, k2, v2, kctx, vctx, u, h, mod, wpool, pscale, wout, ln_g, ln_b, *, tq=1024):
    nt = SEQ // tq
    bpt = tq // BLOCK
    bps = SEQ // BLOCK
    hpt = tq // POOL_HALO
    hps = SEQ // POOL_HALO
    cur = lambda w: pl.BlockSpec((tq, w), lambda b, t: (b * nt + t, 0))
    prev_blk = lambda rows, w, per_tile, per_samp: pl.BlockSpec(
        (rows, w), lambda b, t: (b * per_samp + jnp.maximum(t * per_tile - 1, 0), 0))
    next_blk = lambda rows, w, per_tile, per_samp: pl.BlockSpec(
        (rows, w), lambda b, t: (b * per_samp + jnp.minimum((t + 1) * per_tile, per_samp - 1), 0))
    ctx_blk = lambda w: pl.BlockSpec((CTX_LEN, w), lambda b, t: (b, 0))
    kv_specs = lambda w: [cur(w), prev_blk(BLOCK, w, bpt, bps), next_blk(BLOCK, w, bpt, bps)]
    key_specs = [
        pl.BlockSpec((KEY_SLAB_W, tq), lambda b, t: (0, b * nt + t)),
        pl.BlockSpec((KEY_SLAB_W, BLOCK), lambda b, t: (0, b * bps + jnp.maximum(t * bpt - 1, 0))),
        pl.BlockSpec((KEY_SLAB_W, BLOCK),
                     lambda b, t: (0, b * bps + jnp.minimum((t + 1) * bpt, bps - 1))),
    ]
    bias = _window_bias()
    bands = _pool_bands()
    tok = _nbytes((tq, D_MODEL), F32)
    kv_rows = tq + 2 * BLOCK
    u_rows = tq + 2 * BLOCK
    pipelined = [_nbytes((tq, ATT_WIDTH), BF16), 2 * _nbytes((kv_rows, KEY_SLAB_W + VAL_SLAB_W), BF16),
                 _nbytes((tq + 2 * POOL_HALO, POOL_WIDTH), F32), 2 * tok]
    resident = [_nbytes(bias.shape, F32), _nbytes(bands.shape, BF16), _nbytes(wpool.shape, BF16),
                _nbytes(wout.shape, BF16)]
    scratch = [_nbytes((kv_rows, KEY_SLAB_W + VAL_SLAB_W), BF16),
               _nbytes((u_rows, POOL_WIDTH), F32), _nbytes((tq, D_MODEL), BF16)]
    return pl.pallas_call(
        functools.partial(_attn_lat_kernel, tq=tq, length=SEQ),
        grid=(BATCH, nt),
        in_specs=[_smem_spec(), cur(ATT_WIDTH)] + key_specs + kv_specs(VAL_SLAB_W) + [
            pl.BlockSpec((KEY_SLAB_W, CTX_LEN), lambda b, t: (0, b)), ctx_blk(VAL_SLAB_W),
            cur(POOL_WIDTH), prev_blk(POOL_HALO, POOL_WIDTH, hpt, hps),
            next_blk(POOL_HALO, POOL_WIDTH, hpt, hps),
            cur(D_MODEL), _mod_spec(),
            _const_spec(bias.shape), _const_spec(bands.shape),
            _const_spec(wpool.shape), _const_spec((1, POOL_WIDTH)),
            _const_spec(wout.shape), _const_spec((1, D_MODEL)), _const_spec((1, D_MODEL)),
        ],
        out_specs=cur(D_MODEL),
        out_shape=jax.ShapeDtypeStruct((BATCH * SEQ, D_MODEL), F32),
        scratch_shapes=[
            pltpu.VMEM((KEY_SLAB_W, kv_rows), BF16), pltpu.VMEM((kv_rows, VAL_SLAB_W), BF16),
            pltpu.VMEM((u_rows, POOL_WIDTH), F32), pltpu.VMEM((tq, D_MODEL), BF16),
        ],
        compiler_params=pltpu.CompilerParams(
            dimension_semantics=("arbitrary", "arbitrary"),
            vmem_limit_bytes=_vmem_limit(pipelined, resident + scratch, 8 * tok)),
        name="attn_latent",
    )(sink, q, k2, k2, k2, v2, v2, v2, kctx, vctx, u, u, u, h, mod, bias, bands, wpool, pscale, wout,
      ln_g, ln_b)


def _attn_ctx(sink, q, kctx, vctx, u, h, mod, wpool, pscale, wout, ln_g, ln_b):
    tq = CTX_LEN
    blk = lambda w: pl.BlockSpec((tq, w), lambda b, t: (b, 0))
    bands = _pool_bands()
    tok = _nbytes((tq, D_MODEL), F32)
    u_rows = tq + 2 * BLOCK
    pipelined = [_nbytes((tq, ATT_WIDTH), BF16), _nbytes((tq, KEY_SLAB_W + VAL_SLAB_W), BF16),
                 _nbytes((tq, POOL_WIDTH), F32), 2 * tok]
    resident = [_nbytes(bands.shape, BF16), _nbytes(wpool.shape, BF16), _nbytes(wout.shape, BF16)]
    scratch = [_nbytes((u_rows, POOL_WIDTH), F32), _nbytes((tq, D_MODEL), BF16)]
    return pl.pallas_call(
        functools.partial(_attn_ctx_kernel, tq=tq),
        grid=(BATCH, 1),
        in_specs=[
            _smem_spec(),
            blk(ATT_WIDTH), pl.BlockSpec((KEY_SLAB_W, tq), lambda b, t: (0, b)),
            blk(VAL_SLAB_W), blk(POOL_WIDTH), blk(D_MODEL),
            pl.BlockSpec((1, N_MOD, D_MODEL), lambda b, t: (0, 0, 0)),
            _const_spec(bands.shape), _const_spec(wpool.shape), _const_spec((1, POOL_WIDTH)),
            _const_spec(wout.shape), _const_spec((1, D_MODEL)), _const_spec((1, D_MODEL)),
        ],
        out_specs=blk(D_MODEL),
        out_shape=jax.ShapeDtypeStruct((BATCH * CTX_LEN, D_MODEL), F32),
        scratch_shapes=[
            pltpu.VMEM((u_rows, POOL_WIDTH), F32), pltpu.VMEM((tq, D_MODEL), BF16),
        ],
        compiler_params=pltpu.CompilerParams(
            dimension_semantics=("arbitrary", "arbitrary"),
            vmem_limit_bytes=_vmem_limit(pipelined, resident + scratch, 8 * tok)),
        name="attn_context",
    )(sink, q, kctx, vctx, u, h, mod, bands, wpool, pscale, wout, ln_g, ln_b)


def _row_perm(to_time_major):
    n = BATCH * PERM_STEPS
    r = lax.broadcasted_iota(jnp.int32, (n, n), 0)
    c = lax.broadcasted_iota(jnp.int32, (n, n), 1)
    if to_time_major:
        hit = ((r // BATCH) == (c % PERM_STEPS)) & ((r % BATCH) == (c // PERM_STEPS))
    else:
        hit = ((r // PERM_STEPS) == (c % BATCH)) & ((r % PERM_STEPS) == (c // BATCH))
    return jnp.where(hit, 1.0, 0.0).astype(BF16)


def _lru_time_major_input(x_ref, scale_ref, shift_ref, tt):
    xin = (x_ref[...] * (1.0 + scale_ref[...]) + shift_ref[...]).astype(BF16)
    perm = _row_perm(to_time_major=True)
    parts = []
    for tau in range(tt // PERM_STEPS):
        sub = xin[:, tau * PERM_STEPS:(tau + 1) * PERM_STEPS, :].reshape(BATCH * PERM_STEPS, D_MODEL)
        parts.append(jnp.dot(perm, sub, preferred_element_type=F32).astype(BF16))
    return jnp.concatenate(parts, axis=0)


def _lru_in_fwd_kernel(*refs, tt, with_gate):
    (x_ref, scale_ref, shift_ref, w_ref, h0_ref, cw_ref, cb_ref, wrg_ref, ba_ref, bx_ref,
     lam_ref) = refs[:11]
    outs = refs[11:]
    if with_gate:
        gate_ref, uconv_ref, s_ref, hfin_ref, ring, ulast, h_scr, a_scr, b_scr = outs
    else:
        uconv_ref, s_ref, hfin_ref, ring, ulast, h_scr, a_scr, b_scr = outs
    i = pl.program_id(0)
    ntiles = pl.num_programs(0) - LRU_SCAN_LAG

    @pl.when(i == 0)
    def _():
        ring[...] = jnp.zeros(ring.shape, F32)
        ulast[...] = jnp.zeros(ulast.shape, F32)
        h_scr[...] = h0_ref[...]

    xt = _lru_time_major_input(x_ref, scale_ref, shift_ref, tt)
    slot_new = lax.rem(i, LRU_SCAN_LAG + 1)
    slot_cur = lax.rem(i + 1, LRU_SCAN_LAG + 1)
    slot_nxt = lax.rem(i + 2, LRU_SCAN_LAG + 1)

    def project_chunk(c):
        cols = slice(c * LRU_PROJ_CHUNK, (c + 1) * LRU_PROJ_CHUNK)
        val = jnp.dot(xt, w_ref[:, cols], preferred_element_type=F32)
        val = val.reshape(tt, BATCH, LRU_PROJ_CHUNK)
        if with_gate and c < LRU_WIDTH // LRU_PROJ_CHUNK:
            gate_ref[:, :, cols] = val.astype(BF16)
        else:
            ucols = slice(cols.start % LRU_WIDTH, cols.start % LRU_WIDTH + LRU_PROJ_CHUNK)
            ring[slot_new, :, :, ucols] = val

    nchunks = w_ref.shape[1] // LRU_PROJ_CHUNK
    ahead = nchunks - (LRU_BLOCKS - 2)
    for c in range(ahead):
        project_chunk(c)

    cur = ring[slot_cur]
    nxt = jnp.where(i == ntiles + LRU_SCAN_LAG - 1, 0.0, ring[slot_nxt, 0:CONV_W - 1 - CONV_LEFT])
    u = _lru_conv(cur, ulast[...], nxt, cw_ref, cb_ref, tt)
    uconv_ref[...] = u
    _lru_coeffs(u, wrg_ref, ba_ref, bx_ref, lam_ref, a_scr, b_scr, tt,
                interleave=[functools.partial(project_chunk, c) for c in range(max(ahead, 0), nchunks)])
    ulast[...] = cur[tt - CONV_LEFT:tt]

    h_prev = h_scr[...]
    h = h_prev
    for k in range(tt):
        h = a_scr[k] * h + b_scr[k]
        s_ref[k] = h
    h = jnp.where(i >= LRU_SCAN_LAG, h, h_prev)
    h_scr[...] = h
    hfin_ref[...] = h


def _lru_in_fwd(x3, scale, shift, w, h0, conv_w, conv_b, wrg, ba, bx, lam, *, with_gate, tt=32):
    length = x3.shape[1]
    ntiles = length // tt
    blk = _nbytes((tt, BATCH, LRU_WIDTH), F32)
    proj_tile = lambda i: jnp.minimum(i, ntiles - 1)
    scan_tile = lambda i: jnp.maximum(i - LRU_SCAN_LAG, 0)
    tm_spec = lambda tile: pl.BlockSpec((tt, BATCH, LRU_WIDTH), lambda i: (tile(i), 0, 0))
    out_specs = [tm_spec(scan_tile), tm_spec(scan_tile),
                 pl.BlockSpec((BATCH, LRU_WIDTH), lambda i: (0, 0))]
    out_shape = [jax.ShapeDtypeStruct((length, BATCH, LRU_WIDTH), F32)] * 2 + [
        jax.ShapeDtypeStruct((BATCH, LRU_WIDTH), F32)]
    outs = [blk, blk]
    if with_gate:
        out_specs = [tm_spec(proj_tile)] + out_specs
        out_shape = [jax.ShapeDtypeStruct((length, BATCH, LRU_WIDTH), BF16)] + out_shape
        outs.append(blk // 2)
    res = pl.pallas_call(
        functools.partial(_lru_in_fwd_kernel, tt=tt, with_gate=with_gate),
        grid=(ntiles + LRU_SCAN_LAG,),
        in_specs=[pl.BlockSpec((BATCH, tt, D_MODEL), lambda i: (0, proj_tile(i), 0)),
                  _const_spec(scale.shape), _const_spec(shift.shape), _const_spec(w.shape),
                  _const_spec((BATCH, LRU_WIDTH)),
                  _const_spec((CONV_W, LRU_WIDTH)), _const_spec((1, LRU_WIDTH)),
                  _const_spec(wrg.shape),
                  _const_spec((1, LRU_WIDTH)), _const_spec((1, LRU_WIDTH)), _const_spec((1, LRU_WIDTH))],
        out_specs=out_specs,
        out_shape=out_shape,
        scratch_shapes=[
            pltpu.VMEM((LRU_SCAN_LAG + 1, tt, BATCH, LRU_WIDTH), F32),
            pltpu.VMEM((CONV_LEFT, BATCH, LRU_WIDTH), F32),
            pltpu.VMEM((BATCH, LRU_WIDTH), F32),
            pltpu.VMEM((tt, BATCH, LRU_WIDTH), F32), pltpu.VMEM((tt, BATCH, LRU_WIDTH), F32),
        ],
        compiler_params=pltpu.CompilerParams(
            dimension_semantics=("arbitrary",),
            vmem_limit_bytes=_vmem_limit(
                [blk] + outs, [_nbytes(w.shape, BF16), _nbytes(wrg.shape, BF16),
                               (LRU_SCAN_LAG + 3) * blk], 8 * blk)),
        name="lru_in_fwd",
    )(x3, scale, shift, w, h0, conv_w, conv_b, wrg, ba, bx, lam)
    return res if with_gate else (None,) + tuple(res)


def _lru_conv(cur, prev, nxt, cw_ref, cb_ref, tt):
    ext = jnp.concatenate([prev, cur, nxt], axis=0)
    u = cb_ref[...].reshape(1, 1, LRU_WIDTH)
    for tap in range(CONV_W):
        u = u + ext[tap:tap + tt] * cw_ref[tap:tap + 1, :].reshape(1, 1, LRU_WIDTH)
    return u


def _lru_coeffs(u, wrg_ref, ba_ref, bx_ref, lam_ref, a_scr, b_scr, tt, interleave=()):
    u2 = u.reshape(tt * BATCH, LRU_WIDTH)
    ub16 = u2.astype(BF16)
    col = lambda blk: slice(blk * LRU_BLOCK_W, (blk + 1) * LRU_BLOCK_W)
    lam = lam_ref[...]
    half_decay2 = (-0.5 * LRU_C * LOG2E) * (jnp.maximum(-lam, 0.0) + jnp.log1p(jnp.exp(-jnp.abs(lam))))
    half_ba = 0.5 * ba_ref[...]
    half_bx = 0.5 * bx_ref[...]
    rgs = []
    for blk in range(LRU_BLOCKS + 1):
        if blk < LRU_BLOCKS:
            rgs.append(jnp.dot(ub16[:, col(blk)], wrg_ref[blk], preferred_element_type=F32))
            if blk < len(interleave):
                interleave[blk]()
        if blk == 0:
            continue
        blk -= 1
        rg = rgs[blk]
        cols = col(blk)
        tr = jnp.tanh(rg[:, :LRU_BLOCK_W] + half_ba[:, cols])
        tg = jnp.tanh(rg[:, LRU_BLOCK_W:] + half_bx[:, cols])
        hd = half_decay2[:, cols]
        a = jnp.exp2(hd + hd * tr)
        x = 1.0 - a * a
        hu = 0.5 * u2[:, cols]
        b = jnp.where(x > 0.0, x * lax.rsqrt(x), 0.0) * (hu + hu * tg)
        a_scr[:, :, cols] = a.reshape(tt, BATCH, LRU_BLOCK_W)
        b_scr[:, :, cols] = b.reshape(tt, BATCH, LRU_BLOCK_W)


def _lru_bwd_state_kernel(u_ref, wrg_ref, ba_ref, bx_ref, lam_ref, hfin_ref, a_scr, b_scr, *, tt):
    @pl.when(pl.program_id(0) == 0)
    def _():
        hfin_ref[...] = jnp.zeros(hfin_ref.shape, F32)

    _lru_coeffs(u_ref[...], wrg_ref, ba_ref, bx_ref, lam_ref, a_scr, b_scr, tt)
    step = lambda k, h: a_scr[tt - 1 - k] * h + b_scr[tt - 1 - k]
    hfin_ref[...] = lax.fori_loop(0, tt, step, hfin_ref[...], unroll=4)


def _lru_bwd_state(u_conv, wrg, ba, bx, lam, *, tt=32):
    nsteps = u_conv.shape[0] // tt
    blk = _nbytes((tt, BATCH, LRU_WIDTH), F32)
    return pl.pallas_call(
        functools.partial(_lru_bwd_state_kernel, tt=tt),
        grid=(nsteps,),
        in_specs=[
            pl.BlockSpec((tt, BATCH, LRU_WIDTH), lambda i: (nsteps - 1 - i, 0, 0)),
            _const_spec(wrg.shape),
            _const_spec((1, LRU_WIDTH)), _const_spec((1, LRU_WIDTH)), _const_spec((1, LRU_WIDTH)),
        ],
        out_specs=pl.BlockSpec((BATCH, LRU_WIDTH), lambda i: (0, 0)),
        out_shape=jax.ShapeDtypeStruct((BATCH, LRU_WIDTH), F32),
        scratch_shapes=[pltpu.VMEM((tt, BATCH, LRU_WIDTH), F32)] * 2,
        compiler_params=pltpu.CompilerParams(
            dimension_semantics=("arbitrary",),
            vmem_limit_bytes=_vmem_limit([blk], [2 * blk, _nbytes(wrg.shape, BF16)], 6 * blk)),
        name="lru_bwd_state",
    )(u_conv, wrg, ba, bx, lam)


def _lru_bwd_out_kernel(u_ref, h0_ref, wrg_ref, ba_ref, bx_ref, lam_ref, sf_ref, gate_ref, h_ref,
                        g2_ref, w_ref, lng_ref, lnb_ref, o_ref,
                        h_scr, a_scr, b_scr, sb_ring, out_scr, *, tt):
    i = pl.program_id(0)
    ntiles = pl.num_programs(0) - 1

    @pl.when(i == 0)
    def _():
        h_scr[...] = h0_ref[...]
        sb_ring[...] = jnp.zeros(sb_ring.shape, F32)

    y = sb_ring[lax.rem(i + 1, 2)] + sf_ref[...]
    z = (_gelu_tanh(gate_ref[...].astype(F32)) * y).astype(BF16)
    perm = _row_perm(to_time_major=False)
    parts = []
    for tau in range(tt // PERM_STEPS):
        sub = z[tau * PERM_STEPS:(tau + 1) * PERM_STEPS].reshape(PERM_STEPS * BATCH, LRU_WIDTH)
        zb = jnp.dot(perm, sub, preferred_element_type=F32).astype(BF16)
        parts.append(zb.reshape(BATCH, PERM_STEPS, LRU_WIDTH))
    zb = jnp.concatenate(parts, axis=1).reshape(BATCH * tt, LRU_WIDTH)

    def project_chunk(c):
        cols = slice(c * LRU_PROJ_CHUNK, (c + 1) * LRU_PROJ_CHUNK)
        val = jnp.dot(zb, w_ref[:, cols], preferred_element_type=F32)
        out_scr[:, :, cols] = val.reshape(BATCH, tt, LRU_PROJ_CHUNK)

    _lru_coeffs(u_ref[...], wrg_ref, ba_ref, bx_ref, lam_ref, a_scr, b_scr, tt,
                interleave=[functools.partial(project_chunk, c)
                            for c in range(D_MODEL // LRU_PROJ_CHUNK)])
    slot = lax.rem(i, 2)

    h = h_scr[...]
    for idx in reversed(range(tt)):
        h = a_scr[idx] * h + b_scr[idx]
        sb_ring[slot, idx] = h
    h_scr[...] = h
    res = DEEPNORM_ALPHA * h_ref[...] + g2_ref[...] * out_scr[...]
    o_ref[...] = _layer_norm(res, lng_ref[...].reshape(1, 1, D_MODEL), lnb_ref[...].reshape(1, 1, D_MODEL))


def _lru_bwd_out(u_conv, h0, wrg, ba, bx, lam, s_f, gate_t, h3, g2, w_out, ln_g, ln_b, *, tt=32):
    ntiles = SEQ // tt
    scan_tile = lambda i: jnp.maximum(ntiles - 1 - i, 0)
    out_tile = lambda i: jnp.minimum(ntiles - i, ntiles - 1)
    tm_blk = lambda: pl.BlockSpec((tt, BATCH, LRU_WIDTH), lambda i: (out_tile(i), 0, 0))
    bm_blk = lambda: pl.BlockSpec((BATCH, tt, D_MODEL), lambda i: (0, out_tile(i), 0))
    blk = _nbytes((tt, BATCH, LRU_WIDTH), F32)
    return pl.pallas_call(
        functools.partial(_lru_bwd_out_kernel, tt=tt),
        grid=(ntiles + 1,),
        in_specs=[
            pl.BlockSpec((tt, BATCH, LRU_WIDTH), lambda i: (scan_tile(i), 0, 0)),
            _const_spec((BATCH, LRU_WIDTH)), _const_spec(wrg.shape),
            _const_spec((1, LRU_WIDTH)), _const_spec((1, LRU_WIDTH)), _const_spec((1, LRU_WIDTH)),
            tm_blk(), tm_blk(), bm_blk(),
            _const_spec((BATCH, 1, D_MODEL)), _const_spec(w_out.shape),
            _const_spec((1, D_MODEL)), _const_spec((1, D_MODEL))],
        out_specs=bm_blk(),
        out_shape=jax.ShapeDtypeStruct((BATCH, SEQ, D_MODEL), F32),
        scratch_shapes=[
            pltpu.VMEM((BATCH, LRU_WIDTH), F32),
            pltpu.VMEM((tt, BATCH, LRU_WIDTH), F32), pltpu.VMEM((tt, BATCH, LRU_WIDTH), F32),
            pltpu.VMEM((2, tt, BATCH, LRU_WIDTH), F32), pltpu.VMEM((BATCH, tt, D_MODEL), F32),
        ],
        compiler_params=pltpu.CompilerParams(
            dimension_semantics=("arbitrary",),
            vmem_limit_bytes=_vmem_limit(
                [blk] * 5, [5 * blk, _nbytes(w_out.shape, BF16), _nbytes(wrg.shape, BF16)], 8 * blk)),
        name="lru_bwd_out",
    )(u_conv, h0, wrg, ba, bx, lam, s_f, gate_t, h3, g2, w_out, ln_g, ln_b)


def _rope_tables():
    rows = SEQ // GRID_W
    row = jnp.repeat(jnp.arange(rows, dtype=F32), GRID_W)
    col = jnp.tile(jnp.arange(GRID_W, dtype=F32), rows)
    inv = ROPE_THETA ** (-jnp.arange(ROPE_FREQS, dtype=F32) / ROPE_FREQS)
    ang = jnp.concatenate([row[:, None] * inv, col[:, None] * inv], axis=-1)
    cos, sin = jnp.cos(ang), jnp.sin(ang)
    cos_t = jnp.tile(cos, (1, V7X_LANES // (HEAD_DIM // 2)))
    sin_t = jnp.tile(jnp.concatenate([-sin, sin], axis=-1), (1, V7X_LANES // HEAD_DIM))
    return cos_t, sin_t


def kernel(x, c, ctx, c_ctx, w_mod, b_mod, ln_g, ln_b, ffn_w_gate, ffn_w_up, ffn_w_down, mix_ab_w_in, attn_sink, pool_w, pool_scale, mix_ab_w_out, lru_w_in, lru_conv_w, lru_conv_b, lru_wa, lru_ba, lru_wx, lru_bx, lru_lambda, lru_w_out):
    assert x.shape == (BATCH, SEQ, D_MODEL) and ctx.shape == (BATCH, CTX_LEN, D_MODEL)
    c_all = jnp.concatenate(
        [c, c_ctx[None, :], jnp.zeros((MOD_ROWS - BATCH - 1, D_MODEL), F32)], axis=0)
    mod_all = _modulation(c_all, w_mod, b_mod)
    mod_lat = mod_all[:, :BATCH].reshape(DEPTH, BATCH, N_MOD, D_MODEL)
    mod_ctx = mod_all[:, BATCH:BATCH + 1].reshape(DEPTH, 1, N_MOD, D_MODEL)

    wg = ffn_w_gate.astype(BF16)
    wu = ffn_w_up.astype(BF16)
    wd = ffn_w_down.astype(BF16)
    row = lambda v: v.reshape(1, -1)
    n_ctx = BATCH * CTX_LEN

    h = x.reshape(BATCH * SEQ, D_MODEL)
    hc = ctx.reshape(n_ctx, D_MODEL)

    l = 0
    ml, mc = mod_lat[l], mod_ctx[l]
    ffn1 = ((l, 0), wg, wu, wd, row(ln_g[l, 0]), row(ln_b[l, 0]))
    ffn2 = ((l, 1), wg, wu, wd, row(ln_g[l, 2]), row(ln_b[l, 2]))
    h = _ffn(h, ml, 0, *ffn1, nb=BATCH, length=SEQ)
    hc = _ffn(hc, mc, 0, *ffn1, nb=1, length=n_ctx)
    w_in = mix_ab_w_in[0].astype(BF16)
    cos_t, sin_t = _rope_tables()
    q, k2, v2, u = _attn_inproj(h, ml, w_in, cos_t, sin_t, nb=BATCH, length=SEQ)
    q_c, k2_c, v2_c, u_c = _attn_inproj(hc, mc, w_in, None, None, nb=1, length=n_ctx)
    mix_args = (pool_w[0].astype(BF16), row(pool_scale[0]), mix_ab_w_out[0].astype(BF16),
                row(ln_g[l, 1]), row(ln_b[l, 1]))
    h = _attn_lat(attn_sink[0], q, k2, v2, k2_c, v2_c, u, h, ml, *mix_args)
    hc = _attn_ctx(attn_sink[0], q_c, k2_c, v2_c, u_c, hc, mc, *mix_args)
    h = _ffn(h, ml, 6, *ffn2, nb=BATCH, length=SEQ)
    hc = _ffn(hc, mc, 6, *ffn2, nb=1, length=n_ctx)

    l = 1
    ml, mc = mod_lat[l], mod_ctx[l]
    ffn1 = ((l, 0), wg, wu, wd, row(ln_g[l, 0]), row(ln_b[l, 0]))
    ffn2 = ((l, 1), wg, wu, wd, row(ln_g[l, 2]), row(ln_b[l, 2]))
    h = _ffn(h, ml, 0, *ffn1, nb=BATCH, length=SEQ)
    hc = _ffn(hc, mc, 0, *ffn1, nb=1, length=n_ctx)
    w_in = lru_w_in[0].astype(BF16)
    h3 = h.reshape(BATCH, SEQ, D_MODEL)
    wrg = (0.5 * jnp.concatenate([lru_wa[0], lru_wx[0]], axis=-1)).astype(BF16)
    conv_w, conv_b = lru_conv_w[0], row(lru_conv_b[0])
    zeros = jnp.zeros((BATCH, LRU_WIDTH), F32)
    dir_args = [(wrg[d], row(lru_ba[0, d]), row(lru_bx[0, d]), row(lru_lambda[0, d]))
                for d in range(2)]
    _, uc_t, _, hf_ctx = _lru_in_fwd(hc.reshape(BATCH, CTX_LEN, D_MODEL), mc[:, 4:5, :], mc[:, 3:4, :],
                                     w_in[:, LRU_WIDTH:], zeros, conv_w, conv_b, *dir_args[0],
                                     with_gate=False)
    gate_t, u_t, s_f, _ = _lru_in_fwd(h3, ml[:, 4:5, :], ml[:, 3:4, :], w_in, hf_ctx, conv_w, conv_b,
                                      *dir_args[0], with_gate=True)
    hb_ctx = _lru_bwd_state(uc_t, *dir_args[1])
    h3 = _lru_bwd_out(u_t, hb_ctx, *dir_args[1], s_f, gate_t, h3, ml[:, 5:6, :],
                      lru_w_out[0].astype(BF16), row(ln_g[l, 1]), row(ln_b[l, 1]))
    h = _ffn(h3.reshape(BATCH * SEQ, D_MODEL), ml, 6, *ffn2, nb=BATCH, length=SEQ)
    return h.reshape(BATCH, SEQ, D_MODEL)
```

```python
import functools

import jax
import jax.numpy as jnp
from jax import lax
from jax.experimental import pallas as pl
from jax.experimental.pallas import tpu as pltpu

D_MODEL = 1024
BATCH = 16
SEQ = 2048
DEPTH = 2
GRID_W = 64
CTX_LEN = 256
HEAD_DIM = 64
ATT_HEADS = 8
ATT_KV_HEADS = 2
ATT_GROUPS = ATT_HEADS // ATT_KV_HEADS
ATT_WIDTH = ATT_HEADS * HEAD_DIM
KV_WIDTH = ATT_KV_HEADS * HEAD_DIM
WINDOW = 128
BLOCK = 128
ATT_SCALE = HEAD_DIM ** -0.5
LOG2E = 1.4426950408889634
ROPE_THETA = 10000.0
ROPE_FREQS = HEAD_DIM // 4
POOL_WINDOWS = (2, 4, 8, 16)
POOL_WIDTH = D_MODEL // 2
POOL_GROUP_W = POOL_WIDTH // len(POOL_WINDOWS)
MIX_AB_IN = ATT_WIDTH + 2 * KV_WIDTH + POOL_WIDTH
LRU_WIDTH = D_MODEL
LRU_BLOCKS = 8
LRU_BLOCK_W = LRU_WIDTH // LRU_BLOCKS
LRU_C = 8.0
CONV_W = 4
CONV_LEFT = (CONV_W - 1) // 2
D_FF = 2816
N_MOD = 9
LN_EPS = 1e-5
NEG_INF = -1e30
DEEPNORM_ALPHA = (2 * DEPTH) ** 0.25

V7X_LANES = 128
V7X_SUBLANES = 8
V7X_VMEM_BYTES = 64 * 1024 * 1024
V7X_VMEM_USABLE_BYTES = 60000 * 1024

F32 = jnp.float32
BF16 = jnp.bfloat16

POOL_HALO = V7X_SUBLANES
PERM_STEPS = 16
KEY_SLAB_W = ATT_KV_HEADS * V7X_LANES
VAL_SLAB_W = 2 * ATT_KV_HEADS * V7X_LANES
LRU_PROJ_CHUNK = 256
LRU_SCAN_LAG = 2
SCORE_LOOKAHEAD = 2
EPILOGUE_BLOCKS = 2
MOD_ROWS = 24


def _nbytes(shape, dtype):
    n = 1
    for s in shape:
        n *= s
    return n * jnp.dtype(dtype).itemsize


def _vmem_limit(pipelined, resident, temporaries):
    est = 2 * sum(pipelined) + sum(resident) + temporaries
    return int(min(V7X_VMEM_USABLE_BYTES, max(est * 5 // 4, 16 * 1024 * 1024)))


def _const_spec(shape):
    nd = len(shape)
    return pl.BlockSpec(shape, lambda *_: (0,) * nd, pipeline_mode=pl.Buffered(1))


def _tok_spec(tm, width, nt):
    return pl.BlockSpec((tm, width), lambda b, t: (b * nt + t, 0))


def _mod_spec():
    return pl.BlockSpec((1, N_MOD, D_MODEL), lambda b, t: (b, 0, 0))


def _layer_norm(z, g, b):
    mu = jnp.mean(z, axis=-1, keepdims=True)
    zc = z - mu
    var = jnp.mean(zc * zc, axis=-1, keepdims=True)
    return zc * lax.rsqrt(var + LN_EPS) * g + b


def _gelu_tanh(x):
    c = 0.7978845608028654
    half = 0.5 * x
    return half + half * jnp.tanh(x * (c + (c * 0.044715) * (x * x)))


def _mod_kernel(c_ref, w_ref, b_ref, o_ref):
    c = c_ref[...]
    a_hi, a_lo = _split_bf16(c * jax.nn.sigmoid(c))
    w_hi, w_lo = _split_bf16(w_ref[0])
    acc = jnp.dot(a_hi, w_lo, preferred_element_type=F32) + jnp.dot(a_lo, w_hi, preferred_element_type=F32)
    o_ref[0] = acc + jnp.dot(a_hi, w_hi, preferred_element_type=F32) + b_ref[0]


def _modulation(c_all, w_mod, b_mod):
    tn = 2304
    n_out = N_MOD * D_MODEL
    blocks = [_nbytes((1, D_MODEL, tn), F32), _nbytes((1, MOD_ROWS, tn), F32)]
    return pl.pallas_call(
        _mod_kernel,
        grid=(DEPTH, n_out // tn),
        in_specs=[
            pl.BlockSpec((MOD_ROWS, D_MODEL), lambda l, j: (0, 0)),
            pl.BlockSpec((1, D_MODEL, tn), lambda l, j: (l, 0, j)),
            pl.BlockSpec((1, 1, tn), lambda l, j: (l, 0, j)),
        ],
        out_specs=pl.BlockSpec((1, MOD_ROWS, tn), lambda l, j: (l, 0, j)),
        out_shape=jax.ShapeDtypeStruct((DEPTH, MOD_ROWS, n_out), F32),
        compiler_params=pltpu.CompilerParams(
            dimension_semantics=("arbitrary", "arbitrary"),
            vmem_limit_bytes=_vmem_limit(blocks, [], 4 * blocks[0])),
        name="modulation",
    )(c_all, w_mod, b_mod.reshape(DEPTH, 1, n_out))


def _ffn_kernel(x_ref, mod_ref, wg_ref, wu_ref, wd_ref, lng_ref, lnb_ref, o_ref, *, j0, sub):
    shift = mod_ref[0, j0:j0 + 1, :]
    scale = mod_ref[0, j0 + 1:j0 + 2, :]
    gate = mod_ref[0, j0 + 2:j0 + 3, :]
    sizes = [sub] * (x_ref.shape[0] // sub)
    starts = [s * sub for s in range(len(sizes))]
    nsub = len(sizes)

    def gate_up(s):
        x = x_ref[starts[s]:starts[s] + sizes[s], :]
        xin = (x * (1.0 + scale) + shift).astype(BF16)
        return (jnp.dot(xin, wg_ref[...], preferred_element_type=F32),
                jnp.dot(xin, wu_ref[...], preferred_element_type=F32))

    pending = gate_up(0)
    for s in range(nsub):
        g, u = pending
        if s + 1 < nsub:
            pending = gate_up(s + 1)
        rows = slice(starts[s], starts[s] + sizes[s])
        a = (g * jax.nn.sigmoid(g) * u).astype(BF16)
        y = jnp.dot(a, wd_ref[...], preferred_element_type=F32)
        z = DEEPNORM_ALPHA * x_ref[rows, :] + (0.5 * gate) * y
        o_ref[rows, :] = _layer_norm(z, lng_ref[...], lnb_ref[...])


def _ffn(x, mod, j0, which, wg, wu, wd, ln_g, ln_b, *, nb, length, tm=1024, sub=256):
    nt = length // tm
    tok = _nbytes((tm, D_MODEL), F32)
    weights = [_nbytes(w.shape[2:], BF16) for w in (wg, wu, wd)]
    temps = 3 * _nbytes((tm, D_FF), F32) + 4 * tok
    wspec = lambda w: pl.BlockSpec((None, None) + w.shape[2:], lambda b, t: which + (0, 0),
                                   pipeline_mode=pl.Buffered(1))
    return pl.pallas_call(
        functools.partial(_ffn_kernel, j0=j0, sub=sub),
        grid=(nb, nt),
        in_specs=[
            _tok_spec(tm, D_MODEL, nt),
            _mod_spec(),
            wspec(wg), wspec(wu), wspec(wd),
            _const_spec((1, D_MODEL)), _const_spec((1, D_MODEL)),
        ],
        out_specs=_tok_spec(tm, D_MODEL, nt),
        out_shape=jax.ShapeDtypeStruct((nb * length, D_MODEL), F32),
        compiler_params=pltpu.CompilerParams(
            dimension_semantics=("arbitrary", "arbitrary"),
            vmem_limit_bytes=_vmem_limit([tok, tok], weights, temps)),
        name="ffn",
    )(x, mod, wg, wu, wd, ln_g, ln_b)


def _dup_halves(z, lane):
    zr = pltpu.roll(z, HEAD_DIM, 1)
    lo = lane < HEAD_DIM
    return jnp.where(lo, z, zr), jnp.where(lo, zr, z)


def _attn_inproj_kernel(*refs, rope):
    if rope:
        x_ref, mod_ref, w_ref, cos_ref, sin_ref, q_ref, k_ref, v_ref, u_ref = refs
    else:
        x_ref, mod_ref, w_ref, q_ref, k_ref, v_ref, u_ref = refs
    x = x_ref[...]
    xin = (x * (1.0 + mod_ref[0, 4:5, :]) + mod_ref[0, 3:4, :]).astype(BF16)
    p = jnp.dot(xin, w_ref[...], preferred_element_type=F32)
    lane = lax.broadcasted_iota(jnp.int32, (x.shape[0], V7X_LANES), 1)
    first_half = (lane & (HEAD_DIM - 1)) < HEAD_DIM // 2

    def rot(z):
        if not rope:
            return z
        zr = jnp.where(first_half, pltpu.roll(z, V7X_LANES - HEAD_DIM // 2, 1),
                       pltpu.roll(z, HEAD_DIM // 2, 1))
        return z * cos_ref[...] + zr * sin_ref[...]

    for c in range(ATT_WIDTH // V7X_LANES):
        sl = slice(c * V7X_LANES, (c + 1) * V7X_LANES)
        q_ref[:, sl] = (rot(p[:, sl]) * (ATT_SCALE * LOG2E)).astype(BF16)
    k0, k1 = _dup_halves(rot(p[:, ATT_WIDTH:ATT_WIDTH + KV_WIDTH]), lane)
    k_ref[...] = jnp.concatenate([k0, k1], axis=-1).T.astype(BF16)
    v = p[:, ATT_WIDTH + KV_WIDTH:ATT_WIDTH + 2 * KV_WIDTH]
    vr = pltpu.roll(v, HEAD_DIM, 1)
    lo = lane < HEAD_DIM
    slabs = (jnp.where(lo, v, 1.0), jnp.where(lo, 1.0, vr), jnp.where(lo, vr, 1.0), jnp.where(lo, 1.0, v))
    for i, slab in enumerate(slabs):
        v_ref[:, i * V7X_LANES:(i + 1) * V7X_LANES] = slab.astype(BF16)
    u_ref[...] = p[:, ATT_WIDTH + 2 * KV_WIDTH:]


def _attn_inproj(x, mod, w_in, cos, sin, *, nb, length, tm=1024):
    nt = length // tm
    rope = cos is not None
    tok = _nbytes((tm, D_MODEL), F32)
    outs = [_nbytes((tm, ATT_WIDTH), BF16), _nbytes((tm, KEY_SLAB_W + VAL_SLAB_W), BF16),
            _nbytes((tm, POOL_WIDTH), F32)]
    in_specs = [_tok_spec(tm, D_MODEL, nt), _mod_spec(), _const_spec(w_in.shape)]
    args = [x, mod, w_in]
    if rope:
        in_specs += [pl.BlockSpec((tm, V7X_LANES), lambda b, t: (t, 0))] * 2
        args += [cos, sin]
    rows = nb * length
    return pl.pallas_call(
        functools.partial(_attn_inproj_kernel, rope=rope),
        grid=(nb, nt),
        in_specs=in_specs,
        out_specs=[
            _tok_spec(tm, ATT_WIDTH, nt),
            pl.BlockSpec((KEY_SLAB_W, tm), lambda b, t: (0, b * nt + t)),
            _tok_spec(tm, VAL_SLAB_W, nt),
            _tok_spec(tm, POOL_WIDTH, nt),
        ],
        out_shape=[
            jax.ShapeDtypeStruct((rows, ATT_WIDTH), BF16),
            jax.ShapeDtypeStruct((KEY_SLAB_W, rows), BF16),
            jax.ShapeDtypeStruct((rows, VAL_SLAB_W), BF16),
            jax.ShapeDtypeStruct((rows, POOL_WIDTH), F32),
        ],
        compiler_params=pltpu.CompilerParams(
            dimension_semantics=("arbitrary", "arbitrary"),
            vmem_limit_bytes=_vmem_limit([tok] + outs, [_nbytes(w_in.shape, BF16)],
                                         3 * _nbytes((tm, MIX_AB_IN), F32))),
        name="attn_inproj",
    )(*args)


def _attend_scores(q_blk, keys, kh):
    lane = lax.broadcasted_iota(jnp.int32, (BLOCK, V7X_LANES), 1)
    lo = lane < HEAD_DIM
    zero = jnp.zeros((BLOCK, V7X_LANES), BF16)
    parts = []
    for g in range(ATT_GROUPS):
        c = (kh * ATT_GROUPS + g) // 2
        qc = q_blk[:, c * V7X_LANES:(c + 1) * V7X_LANES]
        parts.append(jnp.where(lo if g % 2 == 0 else jnp.logical_not(lo), qc, zero))
    qs = jnp.concatenate(parts, axis=0)
    return [jnp.dot(qs, k, preferred_element_type=F32) for k in keys]


def _attend_finish(scores, values_even, values_odd, segments, sink_ref, kh):
    lane = lax.broadcasted_iota(jnp.int32, (BLOCK, V7X_LANES), 1)
    lo = lane < HEAD_DIM
    probs = ([[] for _ in scores], [[] for _ in scores])
    sink_terms = []
    for g in range(ATT_GROUPS):
        rows = slice(g * BLOCK, (g + 1) * BLOCK)
        segs = [[s[rows, a:b] if bias is None else s[rows, a:b] + bias for a, b, bias in seg]
                for s, seg in zip(scores, segments)]
        sink = sink_ref[kh * ATT_GROUPS + g] * LOG2E
        chunks = [x[:, c:c + V7X_LANES] for sl in segs for x in sl
                  for c in range(0, x.shape[1], V7X_LANES)]
        m = functools.reduce(jnp.maximum, chunks).max(axis=-1, keepdims=True)
        m = jnp.maximum(m, sink)
        sink_terms.append(jnp.exp2(sink - m))
        for i, sl in enumerate(segs):
            e = [jnp.exp2(x - m).astype(BF16) for x in sl]
            probs[g % 2][i].append(e[0] if len(e) == 1 else jnp.concatenate(e, axis=-1))

    def weighted_values(ps, vals):
        o = None
        for p, v in zip(ps, vals):
            t = jnp.dot(jnp.concatenate(p, axis=0), v, preferred_element_type=F32)
            o = t if o is None else o + t
        return o

    o_par = (weighted_values(probs[0], values_even), weighted_values(probs[1], values_odd))
    outs = []
    for g in range(ATT_GROUPS):
        o = o_par[g % 2][(g // 2) * BLOCK:(g // 2 + 1) * BLOCK]
        rinv = 1.0 / (o + sink_terms[g])
        outs.append(o * pltpu.roll(rinv, HEAD_DIM, 1))
    return jnp.where(lo, outs[0], outs[1]), jnp.where(lo, outs[2], outs[3])


def _pool_sums(ubuf_ref, nblk, band_ref):
    u_hi, u_lo = _split_bf16(ubuf_ref[...])
    sums = {}
    for j in range(nblk):
        win = slice(j * BLOCK, (j + 2) * BLOCK)
        for gi in range(len(POOL_WINDOWS)):
            cols = slice(gi * POOL_GROUP_W, (gi + 1) * POOL_GROUP_W)
            band = band_ref[gi]
            sums[j, gi] = (jnp.dot(band, u_hi[win, cols], preferred_element_type=F32)
                           + jnp.dot(band, u_lo[win, cols], preferred_element_type=F32))
    return sums


def _pool_finish(sums, ubuf_ref, blocks, pos0, length, wpool_ref, pscale_ref, mix_scr):
    for j in blocks:
        pos = pos0 + j * BLOCK + lax.broadcasted_iota(jnp.int32, (BLOCK, 1), 0)
        rows = slice(POOL_HALO + j * BLOCK, POOL_HALO + (j + 1) * BLOCK)
        for gi, w in enumerate(POOL_WINDOWS):
            r = w // 2
            cols = slice(gi * POOL_GROUP_W, (gi + 1) * POOL_GROUP_W)
            cnt = (jnp.minimum(pos + r, length - 1) - jnp.maximum(pos - r, 0) + 1).astype(F32)
            d = sums[j, gi] / cnt - ubuf_ref[rows, cols]
            y = jnp.dot(d.astype(BF16), wpool_ref[gi], preferred_element_type=F32)
            mix_scr[j * BLOCK:(j + 1) * BLOCK, ATT_WIDTH + gi * POOL_GROUP_W:
                    ATT_WIDTH + (gi + 1) * POOL_GROUP_W] = (y * pscale_ref[:, cols]).astype(BF16)


def _split_bf16(x):
    hi = x.astype(BF16)
    return hi, (x - hi.astype(F32)).astype(BF16)


def _pool_bands():
    i = jnp.arange(BLOCK)[:, None]
    c = jnp.arange(2 * BLOCK)[None, :]
    return jnp.stack([(jnp.abs(c - POOL_HALO - i) <= w // 2) for w in POOL_WINDOWS]).astype(BF16)


def _attn_out(mix_scr, rows, wout_ref, h_ref, gate, lng_ref, lnb_ref, o_ref):
    y = jnp.dot(mix_scr[rows, :], wout_ref[...], preferred_element_type=F32)
    z = DEEPNORM_ALPHA * h_ref[rows, :] + gate * y
    o_ref[rows, :] = _layer_norm(z, lng_ref[...], lnb_ref[...])


def _attn_lat_kernel(sink_ref, q_ref, kc_ref, kp_ref, kn_ref, vc_ref, vp_ref, vn_ref,
                     kctx_ref, vctx_ref, uc_ref, up_ref, un_ref, h_ref, mod_ref, bias_ref, band_ref,
                     wpool_ref, pscale_ref, wout_ref, lng_ref, lnb_ref, o_ref,
                     kbuf, vbuf, ubuf, mix_scr, *, tq, length):
    t = pl.program_id(1)
    last = pl.num_programs(1) - 1
    nblk = tq // BLOCK
    kbuf[:, 0:BLOCK] = kp_ref[...]
    kbuf[:, BLOCK:BLOCK + tq] = kc_ref[...]
    kbuf[:, BLOCK + tq:] = kn_ref[...]
    vbuf[0:BLOCK] = vp_ref[...]
    vbuf[BLOCK:BLOCK + tq] = vc_ref[...]
    vbuf[BLOCK + tq:] = vn_ref[...]
    ubuf[0:POOL_HALO] = jnp.where(t == 0, 0.0, up_ref[...])
    ubuf[POOL_HALO:POOL_HALO + tq] = uc_ref[...]
    ubuf[POOL_HALO + tq:2 * POOL_HALO + tq] = jnp.where(t == last, 0.0, un_ref[...])
    ubuf[2 * POOL_HALO + tq:] = jnp.zeros((2 * BLOCK - 2 * POOL_HALO, POOL_WIDTH), F32)
    lanes = lambda i: slice(i * V7X_LANES, (i + 1) * V7X_LANES)
    gate = mod_ref[0, 5:6, :]
    units = [(j, kh) for j in range(nblk) for kh in range(ATT_KV_HEADS)]

    def scores_of(j, kh):
        win = slice(j * BLOCK, j * BLOCK + 3 * BLOCK)
        return _attend_scores(q_ref[j * BLOCK:(j + 1) * BLOCK, :],
                              [kbuf[lanes(kh), win], kctx_ref[lanes(kh), :]], kh)

    sums = _pool_sums(ubuf, nblk, band_ref)
    pending = [scores_of(*u) for u in units[:SCORE_LOOKAHEAD]]
    _pool_finish(sums, ubuf, range(nblk), t * tq, length, wpool_ref, pscale_ref, mix_scr)
    for n, (j, kh) in enumerate(units):
        scores = pending.pop(0)
        if n + SCORE_LOOKAHEAD < len(units):
            pending.append(scores_of(*units[n + SCORE_LOOKAHEAD]))
        blk = t * nblk + j
        bidx = jnp.where(blk == 0, 0, jnp.where(blk == length // BLOCK - 1, 2, 1))
        bias = bias_ref[bidx]
        segments = [[(0, BLOCK, bias[:, :BLOCK]), (BLOCK, 2 * BLOCK, None),
                     (2 * BLOCK, 3 * BLOCK, bias[:, BLOCK:])], [(0, CTX_LEN, None)]]
        win = slice(j * BLOCK, j * BLOCK + 3 * BLOCK)
        rows = slice(j * BLOCK, (j + 1) * BLOCK)
        c0, c1 = _attend_finish(
            scores, [vbuf[win, lanes(2 * kh)], vctx_ref[:, lanes(2 * kh)]],
            [vbuf[win, lanes(2 * kh + 1)], vctx_ref[:, lanes(2 * kh + 1)]],
            segments, sink_ref, kh)
        mix_scr[rows, lanes(2 * kh)] = c0.astype(BF16)
        mix_scr[rows, lanes(2 * kh + 1)] = c1.astype(BF16)
        per_epilogue = EPILOGUE_BLOCKS * ATT_KV_HEADS
        if n > 0 and n % per_epilogue == 0:
            e = n // per_epilogue - 1
            done = slice(e * EPILOGUE_BLOCKS * BLOCK, (e + 1) * EPILOGUE_BLOCKS * BLOCK)
            _attn_out(mix_scr, done, wout_ref, h_ref, gate, lng_ref, lnb_ref, o_ref)
    _attn_out(mix_scr, slice(tq - EPILOGUE_BLOCKS * BLOCK, tq), wout_ref, h_ref, gate, lng_ref,
              lnb_ref, o_ref)


def _attn_ctx_kernel(sink_ref, q_ref, kctx_ref, vctx_ref, uc_ref, h_ref, mod_ref, band_ref,
                     wpool_ref, pscale_ref, wout_ref, lng_ref, lnb_ref, o_ref,
                     ubuf, mix_scr, *, tq, ns):
    lanes = lambda i: slice(i * V7X_LANES, (i + 1) * V7X_LANES)
    nblk = tq // BLOCK
    toks = lambda s: slice(s * tq, (s + 1) * tq)
    for s in range(ns):
        ubuf[s, 0:POOL_HALO] = jnp.zeros((POOL_HALO, POOL_WIDTH), F32)
        ubuf[s, POOL_HALO:POOL_HALO + tq] = uc_ref[toks(s), :]
        ubuf[s, POOL_HALO + tq:] = jnp.zeros((2 * BLOCK - POOL_HALO, POOL_WIDTH), F32)
    units = [(s, j, kh) for s in range(ns) for j in range(nblk) for kh in range(ATT_KV_HEADS)]
    scores_of = lambda s, j, kh: _attend_scores(
        q_ref[s * tq + j * BLOCK:s * tq + (j + 1) * BLOCK, :], [kctx_ref[lanes(kh), toks(s)]], kh)
    sums = [_pool_sums(ubuf.at[s], nblk, band_ref) for s in range(ns)]
    pending = [scores_of(*u) for u in units[:SCORE_LOOKAHEAD]]
    for s in range(ns):
        _pool_finish(sums[s], ubuf.at[s], range(nblk), 0, tq, wpool_ref, pscale_ref,
                     mix_scr.at[pl.ds(s * tq, tq)])
    for n, (s, j, kh) in enumerate(units):
        scores = pending.pop(0)
        if n + SCORE_LOOKAHEAD < len(units):
            pending.append(scores_of(*units[n + SCORE_LOOKAHEAD]))
        rows = slice(s * tq + j * BLOCK, s * tq + (j + 1) * BLOCK)
        c0, c1 = _attend_finish(scores, [vctx_ref[toks(s), lanes(2 * kh)]],
                                [vctx_ref[toks(s), lanes(2 * kh + 1)]], [[(0, CTX_LEN, None)]],
                                sink_ref, kh)
        mix_scr[rows, lanes(2 * kh)] = c0.astype(BF16)
        mix_scr[rows, lanes(2 * kh + 1)] = c1.astype(BF16)
    _attn_out(mix_scr, slice(0, ns * tq), wout_ref, h_ref, mod_ref[0, 5:6, :], lng_ref, lnb_ref, o_ref)


def _window_bias():
    qi = jnp.arange(BLOCK)[:, None]
    kc = jnp.arange(3 * BLOCK)[None, :]
    band = jnp.abs(qi + BLOCK - kc) <= WINDOW
    first = band & (kc >= BLOCK)
    final = band & (kc < 2 * BLOCK)
    full = jnp.where(jnp.stack([first, band, final]), 0.0, NEG_INF).astype(F32)
    return jnp.concatenate([full[:, :, :BLOCK], full[:, :, 2 * BLOCK:]], axis=-1)


def _smem_spec():
    return pl.BlockSpec(memory_space=pltpu.SMEM)


def _attn_lat(sink, q, k2, v2, kctx, vctx, u, h, mod, wpool, pscale, wout, ln_g, ln_b, *, tq=512):
    nt = SEQ // tq
    bpt = tq // BLOCK
    bps = SEQ // BLOCK
    hpt = tq // POOL_HALO
    hps = SEQ // POOL_HALO
    cur = lambda w: pl.BlockSpec((tq, w), lambda b, t: (b * nt + t, 0))
    prev_blk = lambda rows, w, per_tile, per_samp: pl.BlockSpec(
        (rows, w), lambda b, t: (b * per_samp + jnp.maximum(t * per_tile - 1, 0), 0))
    next_blk = lambda rows, w, per_tile, per_samp: pl.BlockSpec(
        (rows, w), lambda b, t: (b * per_samp + jnp.minimum((t + 1) * per_tile, per_samp - 1), 0))
    ctx_blk = lambda w: pl.BlockSpec((CTX_LEN, w), lambda b, t: (b, 0))
    kv_specs = lambda w: [cur(w), prev_blk(BLOCK, w, bpt, bps), next_blk(BLOCK, w, bpt, bps)]
    key_specs = [
        pl.BlockSpec((KEY_SLAB_W, tq), lambda b, t: (0, b * nt + t)),
        pl.BlockSpec((KEY_SLAB_W, BLOCK), lambda b, t: (0, b * bps + jnp.maximum(t * bpt - 1, 0))),
        pl.BlockSpec((KEY_SLAB_W, BLOCK),
                     lambda b, t: (0, b * bps + jnp.minimum((t + 1) * bpt, bps - 1))),
    ]
    bias = _window_bias()
    bands = _pool_bands()
    tok = _nbytes((tq, D_MODEL), F32)
    kv_rows = tq + 2 * BLOCK
    u_rows = tq + 2 * BLOCK
    pipelined = [_nbytes((tq, ATT_WIDTH), BF16), 2 * _nbytes((kv_rows, KEY_SLAB_W + VAL_SLAB_W), BF16),
                 _nbytes((tq + 2 * POOL_HALO, POOL_WIDTH), F32), 2 * tok]
    resident = [_nbytes(bias.shape, F32), _nbytes(bands.shape, BF16), _nbytes(wpool.shape, BF16),
                _nbytes(wout.shape, BF16)]
    scratch = [_nbytes((kv_rows, KEY_SLAB_W + VAL_SLAB_W), BF16),
               _nbytes((u_rows, POOL_WIDTH), F32), _nbytes((tq, D_MODEL), BF16)]
    return pl.pallas_call(
        functools.partial(_attn_lat_kernel, tq=tq, length=SEQ),
        grid=(BATCH, nt),
        in_specs=[_smem_spec(), cur(ATT_WIDTH)] + key_specs + kv_specs(VAL_SLAB_W) + [
            pl.BlockSpec((KEY_SLAB_W, CTX_LEN), lambda b, t: (0, b)), ctx_blk(VAL_SLAB_W),
            cur(POOL_WIDTH), prev_blk(POOL_HALO, POOL_WIDTH, hpt, hps),
            next_blk(POOL_HALO, POOL_WIDTH, hpt, hps),
            cur(D_MODEL), _mod_spec(),
            _const_spec(bias.shape), _const_spec(bands.shape),
            _const_spec(wpool.shape), _const_spec((1, POOL_WIDTH)),
            _const_spec(wout.shape), _const_spec((1, D_MODEL)), _const_spec((1, D_MODEL)),
        ],
        out_specs=cur(D_MODEL),
        out_shape=jax.ShapeDtypeStruct((BATCH * SEQ, D_MODEL), F32),
        scratch_shapes=[
            pltpu.VMEM((KEY_SLAB_W, kv_rows), BF16), pltpu.VMEM((kv_rows, VAL_SLAB_W), BF16),
            pltpu.VMEM((u_rows, POOL_WIDTH), F32), pltpu.VMEM((tq, D_MODEL), BF16),
        ],
        compiler_params=pltpu.CompilerParams(
            dimension_semantics=("arbitrary", "arbitrary"),
            vmem_limit_bytes=_vmem_limit(pipelined, resident + scratch, 8 * tok)),
        name="attn_latent",
    )(sink, q, k2, k2, k2, v2, v2, v2, kctx, vctx, u, u, u, h, mod, bias, bands, wpool, pscale, wout,
      ln_g, ln_b)


def _attn_ctx(sink, q, kctx, vctx, u, h, mod, wpool, pscale, wout, ln_g, ln_b, *, ns=2):
    tq = CTX_LEN
    rows = ns * tq
    blk = lambda w: pl.BlockSpec((rows, w), lambda b, t: (b, 0))
    bands = _pool_bands()
    tok = _nbytes((rows, D_MODEL), F32)
    u_rows = tq + 2 * BLOCK
    pipelined = [_nbytes((rows, ATT_WIDTH), BF16), _nbytes((rows, KEY_SLAB_W + VAL_SLAB_W), BF16),
                 _nbytes((rows, POOL_WIDTH), F32), 2 * tok]
    resident = [_nbytes(bands.shape, BF16), _nbytes(wpool.shape, BF16), _nbytes(wout.shape, BF16)]
    scratch = [_nbytes((ns, u_rows, POOL_WIDTH), F32), _nbytes((rows, D_MODEL), BF16)]
    return pl.pallas_call(
        functools.partial(_attn_ctx_kernel, tq=tq, ns=ns),
        grid=(BATCH // ns, 1),
        in_specs=[
            _smem_spec(),
            blk(ATT_WIDTH), pl.BlockSpec((KEY_SLAB_W, rows), lambda b, t: (0, b)),
            blk(VAL_SLAB_W), blk(POOL_WIDTH), blk(D_MODEL),
            pl.BlockSpec((1, N_MOD, D_MODEL), lambda b, t: (0, 0, 0)),
            _const_spec(bands.shape), _const_spec(wpool.shape), _const_spec((1, POOL_WIDTH)),
            _const_spec(wout.shape), _const_spec((1, D_MODEL)), _const_spec((1, D_MODEL)),
        ],
        out_specs=blk(D_MODEL),
        out_shape=jax.ShapeDtypeStruct((BATCH * CTX_LEN, D_MODEL), F32),
        scratch_shapes=[
            pltpu.VMEM((ns, u_rows, POOL_WIDTH), F32), pltpu.VMEM((rows, D_MODEL), BF16),
        ],
        compiler_params=pltpu.CompilerParams(
            dimension_semantics=("arbitrary", "arbitrary"),
            vmem_limit_bytes=_vmem_limit(pipelined, resident + scratch, 8 * tok)),
        name="attn_context",
    )(sink, q, kctx, vctx, u, h, mod, bands, wpool, pscale, wout, ln_g, ln_b)


def _row_perm(to_time_major):
    n = BATCH * PERM_STEPS
    r = lax.broadcasted_iota(jnp.int32, (n, n), 0)
    c = lax.broadcasted_iota(jnp.int32, (n, n), 1)
    if to_time_major:
        hit = ((r // BATCH) == (c % PERM_STEPS)) & ((r % BATCH) == (c // PERM_STEPS))
    else:
        hit = ((r // PERM_STEPS) == (c % BATCH)) & ((r % PERM_STEPS) == (c // BATCH))
    return jnp.where(hit, 1.0, 0.0).astype(BF16)


def _lru_time_major_input(x_ref, scale_ref, shift_ref, tt):
    xin = (x_ref[...] * (1.0 + scale_ref[...]) + shift_ref[...]).astype(BF16)
    perm = _row_perm(to_time_major=True)
    parts = []
    for tau in range(tt // PERM_STEPS):
        sub = xin[:, tau * PERM_STEPS:(tau + 1) * PERM_STEPS, :].reshape(BATCH * PERM_STEPS, D_MODEL)
        parts.append(jnp.dot(perm, sub, preferred_element_type=F32).astype(BF16))
    return jnp.concatenate(parts, axis=0)


def _lru_in_fwd_kernel(*refs, tt, with_gate):
    (x_ref, scale_ref, shift_ref, w_ref, h0_ref, cw_ref, cb_ref, wrg_ref, ba_ref, bx_ref,
     lam_ref) = refs[:11]
    outs = refs[11:]
    if with_gate:
        gate_ref, uconv_ref, s_ref, hfin_ref, ring, ulast, h_scr, a_scr, b_scr = outs
    else:
        uconv_ref, s_ref, hfin_ref, ring, ulast, h_scr, a_scr, b_scr = outs
    i = pl.program_id(0)
    ntiles = pl.num_programs(0) - LRU_SCAN_LAG

    @pl.when(i == 0)
    def _():
        ring[...] = jnp.zeros(ring.shape, F32)
        ulast[...] = jnp.zeros(ulast.shape, F32)
        h_scr[...] = h0_ref[...]

    xt = _lru_time_major_input(x_ref, scale_ref, shift_ref, tt)
    slot_new = lax.rem(i, LRU_SCAN_LAG + 1)
    slot_cur = lax.rem(i + 1, LRU_SCAN_LAG + 1)
    slot_nxt = lax.rem(i + 2, LRU_SCAN_LAG + 1)

    def project_chunk(c):
        cols = slice(c * LRU_PROJ_CHUNK, (c + 1) * LRU_PROJ_CHUNK)
        val = jnp.dot(xt, w_ref[:, cols], preferred_element_type=F32)
        val = val.reshape(tt, BATCH, LRU_PROJ_CHUNK)
        if with_gate and c < LRU_WIDTH // LRU_PROJ_CHUNK:
            gate_ref[:, :, cols] = val.astype(BF16)
        else:
            ucols = slice(cols.start % LRU_WIDTH, cols.start % LRU_WIDTH + LRU_PROJ_CHUNK)
            ring[slot_new, :, :, ucols] = val

    nchunks = w_ref.shape[1] // LRU_PROJ_CHUNK
    ahead = nchunks - (LRU_BLOCKS - 2)
    for c in range(ahead):
        project_chunk(c)

    cur = ring[slot_cur]
    nxt = jnp.where(i == ntiles + LRU_SCAN_LAG - 1, 0.0, ring[slot_nxt, 0:CONV_W - 1 - CONV_LEFT])
    u = _lru_conv(cur, ulast[...], nxt, cw_ref, cb_ref, tt)
    uconv_ref[...] = u
    _lru_coeffs(u, wrg_ref, ba_ref, bx_ref, lam_ref, a_scr, b_scr, tt,
                interleave=[functools.partial(project_chunk, c) for c in range(max(ahead, 0), nchunks)])
    ulast[...] = cur[tt - CONV_LEFT:tt]

    h_prev = h_scr[...]
    h = h_prev
    for k in range(tt):
        h = a_scr[k] * h + b_scr[k]
        s_ref[k] = h
    h = jnp.where(i >= LRU_SCAN_LAG, h, h_prev)
    h_scr[...] = h
    hfin_ref[...] = h


def _lru_in_fwd(x3, scale, shift, w, h0, conv_w, conv_b, wrg, ba, bx, lam, *, with_gate, tt=32):
    length = x3.shape[1]
    ntiles = length // tt
    blk = _nbytes((tt, BATCH, LRU_WIDTH), F32)
    proj_tile = lambda i: jnp.minimum(i, ntiles - 1)
    scan_tile = lambda i: jnp.maximum(i - LRU_SCAN_LAG, 0)
    tm_spec = lambda tile: pl.BlockSpec((tt, BATCH, LRU_WIDTH), lambda i: (tile(i), 0, 0))
    out_specs = [tm_spec(scan_tile), tm_spec(scan_tile),
                 pl.BlockSpec((BATCH, LRU_WIDTH), lambda i: (0, 0))]
    out_shape = [jax.ShapeDtypeStruct((length, BATCH, LRU_WIDTH), F32)] * 2 + [
        jax.ShapeDtypeStruct((BATCH, LRU_WIDTH), F32)]
    outs = [blk, blk]
    if with_gate:
        out_specs = [tm_spec(proj_tile)] + out_specs
        out_shape = [jax.ShapeDtypeStruct((length, BATCH, LRU_WIDTH), BF16)] + out_shape
        outs.append(blk // 2)
    res = pl.pallas_call(
        functools.partial(_lru_in_fwd_kernel, tt=tt, with_gate=with_gate),
        grid=(ntiles + LRU_SCAN_LAG,),
        in_specs=[pl.BlockSpec((BATCH, tt, D_MODEL), lambda i: (0, proj_tile(i), 0)),
                  _const_spec(scale.shape), _const_spec(shift.shape), _const_spec(w.shape),
                  _const_spec((BATCH, LRU_WIDTH)),
                  _const_spec((CONV_W, LRU_WIDTH)), _const_spec((1, LRU_WIDTH)),
                  _const_spec(wrg.shape),
                  _const_spec((1, LRU_WIDTH)), _const_spec((1, LRU_WIDTH)), _const_spec((1, LRU_WIDTH))],
        out_specs=out_specs,
        out_shape=out_shape,
        scratch_shapes=[
            pltpu.VMEM((LRU_SCAN_LAG + 1, tt, BATCH, LRU_WIDTH), F32),
            pltpu.VMEM((CONV_LEFT, BATCH, LRU_WIDTH), F32),
            pltpu.VMEM((BATCH, LRU_WIDTH), F32),
            pltpu.VMEM((tt, BATCH, LRU_WIDTH), F32), pltpu.VMEM((tt, BATCH, LRU_WIDTH), F32),
        ],
        compiler_params=pltpu.CompilerParams(
            dimension_semantics=("arbitrary",),
            vmem_limit_bytes=_vmem_limit(
                [blk] + outs, [_nbytes(w.shape, BF16), _nbytes(wrg.shape, BF16),
                               (LRU_SCAN_LAG + 3) * blk], 8 * blk)),
        name="lru_in_fwd",
    )(x3, scale, shift, w, h0, conv_w, conv_b, wrg, ba, bx, lam)
    return res if with_gate else (None,) + tuple(res)


def _lru_conv(cur, prev, nxt, cw_ref, cb_ref, tt):
    ext = jnp.concatenate([prev, cur, nxt], axis=0)
    u = cb_ref[...].reshape(1, 1, LRU_WIDTH)
    for tap in range(CONV_W):
        u = u + ext[tap:tap + tt] * cw_ref[tap:tap + 1, :].reshape(1, 1, LRU_WIDTH)
    return u


def _lru_coeffs(u, wrg_ref, ba_ref, bx_ref, lam_ref, a_scr, b_scr, tt, interleave=()):
    u2 = u.reshape(tt * BATCH, LRU_WIDTH)
    ub16 = u2.astype(BF16)
    col = lambda blk: slice(blk * LRU_BLOCK_W, (blk + 1) * LRU_BLOCK_W)
    lam = lam_ref[...]
    half_decay2 = (-0.5 * LRU_C * LOG2E) * (jnp.maximum(-lam, 0.0) + jnp.log1p(jnp.exp(-jnp.abs(lam))))
    half_ba = 0.5 * ba_ref[...]
    half_bx = 0.5 * bx_ref[...]
    rgs = []
    for blk in range(LRU_BLOCKS + 1):
        if blk < LRU_BLOCKS:
            rgs.append(jnp.dot(ub16[:, col(blk)], wrg_ref[blk], preferred_element_type=F32))
            if blk < len(interleave):
                interleave[blk]()
        if blk == 0:
            continue
        blk -= 1
        rg = rgs[blk]
        cols = col(blk)
        tr = jnp.tanh(rg[:, :LRU_BLOCK_W] + half_ba[:, cols])
        tg = jnp.tanh(rg[:, LRU_BLOCK_W:] + half_bx[:, cols])
        hd = half_decay2[:, cols]
        a = jnp.exp2(hd + hd * tr)
        x = 1.0 - a * a
        hu = 0.5 * u2[:, cols]
        b = jnp.where(x > 0.0, x * lax.rsqrt(x), 0.0) * (hu + hu * tg)
        a_scr[:, :, cols] = a.reshape(tt, BATCH, LRU_BLOCK_W)
        b_scr[:, :, cols] = b.reshape(tt, BATCH, LRU_BLOCK_W)


def _lru_bwd_state_kernel(u_ref, wrg_ref, ba_ref, bx_ref, lam_ref, hfin_ref, a_scr, b_scr, *, tt):
    @pl.when(pl.program_id(0) == 0)
    def _():
        hfin_ref[...] = jnp.zeros(hfin_ref.shape, F32)

    _lru_coeffs(u_ref[...], wrg_ref, ba_ref, bx_ref, lam_ref, a_scr, b_scr, tt)
    step = lambda k, h: a_scr[tt - 1 - k] * h + b_scr[tt - 1 - k]
    hfin_ref[...] = lax.fori_loop(0, tt, step, hfin_ref[...], unroll=4)


def _lru_bwd_state(u_conv, wrg, ba, bx, lam, *, tt=32):
    nsteps = u_conv.shape[0] // tt
    blk = _nbytes((tt, BATCH, LRU_WIDTH), F32)
    return pl.pallas_call(
        functools.partial(_lru_bwd_state_kernel, tt=tt),
        grid=(nsteps,),
        in_specs=[
            pl.BlockSpec((tt, BATCH, LRU_WIDTH), lambda i: (nsteps - 1 - i, 0, 0)),
            _const_spec(wrg.shape),
            _const_spec((1, LRU_WIDTH)), _const_spec((1, LRU_WIDTH)), _const_spec((1, LRU_WIDTH)),
        ],
        out_specs=pl.BlockSpec((BATCH, LRU_WIDTH), lambda i: (0, 0)),
        out_shape=jax.ShapeDtypeStruct((BATCH, LRU_WIDTH), F32),
        scratch_shapes=[pltpu.VMEM((tt, BATCH, LRU_WIDTH), F32)] * 2,
        compiler_params=pltpu.CompilerParams(
            dimension_semantics=("arbitrary",),
            vmem_limit_bytes=_vmem_limit([blk], [2 * blk, _nbytes(wrg.shape, BF16)], 6 * blk)),
        name="lru_bwd_state",
    )(u_conv, wrg, ba, bx, lam)


def _lru_bwd_out_kernel(u_ref, h0_ref, wrg_ref, ba_ref, bx_ref, lam_ref, sf_ref, gate_ref, h_ref,
                        g2_ref, w_ref, lng_ref, lnb_ref, o_ref,
                        h_scr, a_scr, b_scr, sb_ring, out_scr, *, tt):
    i = pl.program_id(0)
    ntiles = pl.num_programs(0) - 1

    @pl.when(i == 0)
    def _():
        h_scr[...] = h0_ref[...]
        sb_ring[...] = jnp.zeros(sb_ring.shape, F32)

    y = sb_ring[lax.rem(i + 1, 2)] + sf_ref[...]
    z = (_gelu_tanh(gate_ref[...].astype(F32)) * y).astype(BF16)
    perm = _row_perm(to_time_major=False)
    parts = []
    for tau in range(tt // PERM_STEPS):
        sub = z[tau * PERM_STEPS:(tau + 1) * PERM_STEPS].reshape(PERM_STEPS * BATCH, LRU_WIDTH)
        zb = jnp.dot(perm, sub, preferred_element_type=F32).astype(BF16)
        parts.append(zb.reshape(BATCH, PERM_STEPS, LRU_WIDTH))
    zb = jnp.concatenate(parts, axis=1).reshape(BATCH * tt, LRU_WIDTH)

    def project_chunk(c):
        cols = slice(c * LRU_PROJ_CHUNK, (c + 1) * LRU_PROJ_CHUNK)
        val = jnp.dot(zb, w_ref[:, cols], preferred_element_type=F32)
        out_scr[:, :, cols] = val.reshape(BATCH, tt, LRU_PROJ_CHUNK)

    _lru_coeffs(u_ref[...], wrg_ref, ba_ref, bx_ref, lam_ref, a_scr, b_scr, tt,
                interleave=[functools.partial(project_chunk, c)
                            for c in range(D_MODEL // LRU_PROJ_CHUNK)])
    slot = lax.rem(i, 2)

    h = h_scr[...]
    for idx in reversed(range(tt)):
        h = a_scr[idx] * h + b_scr[idx]
        sb_ring[slot, idx] = h
    h_scr[...] = h
    res = DEEPNORM_ALPHA * h_ref[...] + g2_ref[...] * out_scr[...]
    o_ref[...] = _layer_norm(res, lng_ref[...].reshape(1, 1, D_MODEL), lnb_ref[...].reshape(1, 1, D_MODEL))


def _lru_bwd_out(u_conv, h0, wrg, ba, bx, lam, s_f, gate_t, h3, g2, w_out, ln_g, ln_b, *, tt=32):
    ntiles = SEQ // tt
    scan_tile = lambda i: jnp.maximum(ntiles - 1 - i, 0)
    out_tile = lambda i: jnp.minimum(ntiles - i, ntiles - 1)
    tm_blk = lambda: pl.BlockSpec((tt, BATCH, LRU_WIDTH), lambda i: (out_tile(i), 0, 0))
    bm_blk = lambda: pl.BlockSpec((BATCH, tt, D_MODEL), lambda i: (0, out_tile(i), 0))
    blk = _nbytes((tt, BATCH, LRU_WIDTH), F32)
    return pl.pallas_call(
        functools.partial(_lru_bwd_out_kernel, tt=tt),
        grid=(ntiles + 1,),
        in_specs=[
            pl.BlockSpec((tt, BATCH, LRU_WIDTH), lambda i: (scan_tile(i), 0, 0)),
            _const_spec((BATCH, LRU_WIDTH)), _const_spec(wrg.shape),
            _const_spec((1, LRU_WIDTH)), _const_spec((1, LRU_WIDTH)), _const_spec((1, LRU_WIDTH)),
            tm_blk(), tm_blk(), bm_blk(),
            _const_spec((BATCH, 1, D_MODEL)), _const_spec(w_out.shape),
            _const_spec((1, D_MODEL)), _const_spec((1, D_MODEL))],
        out_specs=bm_blk(),
        out_shape=jax.ShapeDtypeStruct((BATCH, SEQ, D_MODEL), F32),
        scratch_shapes=[
            pltpu.VMEM((BATCH, LRU_WIDTH), F32),
            pltpu.VMEM((tt, BATCH, LRU_WIDTH), F32), pltpu.VMEM((tt, BATCH, LRU_WIDTH), F32),
            pltpu.VMEM((2, tt, BATCH, LRU_WIDTH), F32), pltpu.VMEM((BATCH, tt, D_MODEL), F32),
        ],
        compiler_params=pltpu.CompilerParams(
            dimension_semantics=("arbitrary",),
            vmem_limit_bytes=_vmem_limit(
                [blk] * 5, [5 * blk, _nbytes(w_out.shape, BF16), _nbytes(wrg.shape, BF16)], 8 * blk)),
        name="lru_bwd_out",
    )(u_conv, h0, wrg, ba, bx, lam, s_f, gate_t, h3, g2, w_out, ln_g, ln_b)


def _rope_tables():
    rows = SEQ // GRID_W
    row = jnp.repeat(jnp.arange(rows, dtype=F32), GRID_W)
    col = jnp.tile(jnp.arange(GRID_W, dtype=F32), rows)
    inv = ROPE_THETA ** (-jnp.arange(ROPE_FREQS, dtype=F32) / ROPE_FREQS)
    ang = jnp.concatenate([row[:, None] * inv, col[:, None] * inv], axis=-1)
    cos, sin = jnp.cos(ang), jnp.sin(ang)
    cos_t = jnp.tile(cos, (1, V7X_LANES // (HEAD_DIM // 2)))
    sin_t = jnp.tile(jnp.concatenate([-sin, sin], axis=-1), (1, V7X_LANES // HEAD_DIM))
    return cos_t, sin_t


def kernel(x, c, ctx, c_ctx, w_mod, b_mod, ln_g, ln_b, ffn_w_gate, ffn_w_up, ffn_w_down, mix_ab_w_in, attn_sink, pool_w, pool_scale, mix_ab_w_out, lru_w_in, lru_conv_w, lru_conv_b, lru_wa, lru_ba, lru_wx, lru_bx, lru_lambda, lru_w_out):
    assert x.shape == (BATCH, SEQ, D_MODEL) and ctx.shape == (BATCH, CTX_LEN, D_MODEL)
    c_all = jnp.concatenate(
        [c, c_ctx[None, :], jnp.zeros((MOD_ROWS - BATCH - 1, D_MODEL), F32)], axis=0)
    mod_all = _modulation(c_all, w_mod, b_mod)
    mod_lat = mod_all[:, :BATCH].reshape(DEPTH, BATCH, N_MOD, D_MODEL)
    mod_ctx = mod_all[:, BATCH:BATCH + 1].reshape(DEPTH, 1, N_MOD, D_MODEL)

    wg = ffn_w_gate.astype(BF16)
    wu = ffn_w_up.astype(BF16)
    wd = ffn_w_down.astype(BF16)
    row = lambda v: v.reshape(1, -1)
    n_ctx = BATCH * CTX_LEN

    h = x.reshape(BATCH * SEQ, D_MODEL)
    hc = ctx.reshape(n_ctx, D_MODEL)

    l = 0
    ml, mc = mod_lat[l], mod_ctx[l]
    ffn1 = ((l, 0), wg, wu, wd, row(ln_g[l, 0]), row(ln_b[l, 0]))
    ffn2 = ((l, 1), wg, wu, wd, row(ln_g[l, 2]), row(ln_b[l, 2]))
    h = _ffn(h, ml, 0, *ffn1, nb=BATCH, length=SEQ)
    hc = _ffn(hc, mc, 0, *ffn1, nb=1, length=n_ctx)
    w_in = mix_ab_w_in[0].astype(BF16)
    cos_t, sin_t = _rope_tables()
    q, k2, v2, u = _attn_inproj(h, ml, w_in, cos_t, sin_t, nb=BATCH, length=SEQ)
    q_c, k2_c, v2_c, u_c = _attn_inproj(hc, mc, w_in, None, None, nb=1, length=n_ctx)
    mix_args = (pool_w[0].astype(BF16), row(pool_scale[0]), mix_ab_w_out[0].astype(BF16),
                row(ln_g[l, 1]), row(ln_b[l, 1]))
    h = _attn_lat(attn_sink[0], q, k2, v2, k2_c, v2_c, u, h, ml, *mix_args)
    hc = _attn_ctx(attn_sink[0], q_c, k2_c, v2_c, u_c, hc, mc, *mix_args)
    h = _ffn(h, ml, 6, *ffn2, nb=BATCH, length=SEQ)
    hc = _ffn(hc, mc, 6, *ffn2, nb=1, length=n_ctx)

    l = 1
    ml, mc = mod_lat[l], mod_ctx[l]
    ffn1 = ((l, 0), wg, wu, wd, row(ln_g[l, 0]), row(ln_b[l, 0]))
    ffn2 = ((l, 1), wg, wu, wd, row(ln_g[l, 2]), row(ln_b[l, 2]))
    h = _ffn(h, ml, 0, *ffn1, nb=BATCH, length=SEQ)
    hc = _ffn(hc, mc, 0, *ffn1, nb=1, length=n_ctx)
    w_in = lru_w_in[0].astype(BF16)
    h3 = h.reshape(BATCH, SEQ, D_MODEL)
    wrg = (0.5 * jnp.concatenate([lru_wa[0], lru_wx[0]], axis=-1)).astype(BF16)
    conv_w, conv_b = lru_conv_w[0], row(lru_conv_b[0])
    zeros = jnp.zeros((BATCH, LRU_WIDTH), F32)
    dir_args = [(wrg[d], row(lru_ba[0, d]), row(lru_bx[0, d]), row(lru_lambda[0, d]))
                for d in range(2)]
    _, uc_t, _, hf_ctx = _lru_in_fwd(hc.reshape(BATCH, CTX_LEN, D_MODEL), mc[:, 4:5, :], mc[:, 3:4, :],
                                     w_in[:, LRU_WIDTH:], zeros, conv_w, conv_b, *dir_args[0],
                                     with_gate=False)
    gate_t, u_t, s_f, _ = _lru_in_fwd(h3, ml[:, 4:5, :], ml[:, 3:4, :], w_in, hf_ctx, conv_w, conv_b,
                                      *dir_args[0], with_gate=True)
    hb_ctx = _lru_bwd_state(uc_t, *dir_args[1])
    h3 = _lru_bwd_out(u_t, hb_ctx, *dir_args[1], s_f, gate_t, h3, ml[:, 5:6, :],
                      lru_w_out[0].astype(BF16), row(ln_g[l, 1]), row(ln_b[l, 1]))
    h = _ffn(h3.reshape(BATCH * SEQ, D_MODEL), ml, 6, *ffn2, nb=BATCH, length=SEQ)
    return h.reshape(BATCH, SEQ, D_MODEL)
```
